```python
import numpy as np
import jax
import jax.numpy as jnp
from jax import lax

D_MODEL = 1024
BATCH = 16
SEQ = 2048
DEPTH = 4

GRID_W = 64
CTX_LEN = 256
NORM_EPS = 1e-6
N_MOD = 6

RET_HEADS = 4
RET_DK = 128
RET_DV = 256
RET_QK = RET_HEADS * RET_DK
RET_V = RET_HEADS * RET_DV
RET_CHUNK = 128
ROPE_BASE = 10000.0

CONV_W = 512
CONV_K = 3

POOL_WINDOWS = (2, 4, 8, 16)
POOL_GROUPS = 4
POOL_GDIM = 128
POOL_W = POOL_GROUPS * POOL_GDIM

N_BRANCH = 3
IN_SPLITS = (RET_QK, RET_QK, RET_V, RET_V, CONV_W, CONV_W, CONV_W, POOL_W, N_BRANCH * D_MODEL)
IN_WIDTH = 2 * RET_QK + 2 * RET_V + 3 * CONV_W + POOL_W + N_BRANCH * D_MODEL

N_GROUPS = 4
EXP_PER_GROUP = 4
N_EXPERTS = N_GROUPS * EXP_PER_GROUP
TOP_K_INNER = 2
D_FF_EXPERT = 512

kernel_name = 'hybrid_retention_conv_pool_hmoe_dit'


def rmsnorm(x, g):
    xf = x.astype(jnp.float32)
    y = xf * lax.rsqrt(jnp.mean(xf * xf, axis=-1, keepdims=True) + NORM_EPS)
    return (y * g.astype(jnp.float32)).astype(x.dtype)


def split_cols(p):
    offsets = np.cumsum(IN_SPLITS)[:-1].tolist()
    return jnp.split(p, offsets, axis=-1)


def to_heads(a, head_dim):
    b, n, w = a.shape
    return a.astype(jnp.float32).reshape(b, n, w // head_dim, head_dim)


def axial_rope(n_tokens):
    rows = n_tokens // GRID_W
    row = jnp.repeat(jnp.arange(rows, dtype=jnp.float32), GRID_W)
    col = jnp.tile(jnp.arange(GRID_W, dtype=jnp.float32), rows)
    n_freq = RET_DK // 4
    inv_freq = ROPE_BASE ** (-jnp.arange(n_freq, dtype=jnp.float32) / n_freq)
    ang = jnp.concatenate([row[:, None] * inv_freq[None, :], col[:, None] * inv_freq[None, :]], axis=-1)
    return jnp.cos(ang), jnp.sin(ang)


def apply_rope(x, cos, sin):
    x1, x2 = jnp.split(x, 2, axis=-1)
    c = cos[None, :, None, :]
    s = sin[None, :, None, :]
    return jnp.concatenate([x1 * c - x2 * s, x1 * s + x2 * c], axis=-1)


def retention_dir(q, k, v, log_gamma, state0):
    b, n_tok, h, dk = q.shape
    dv = v.shape[-1]
    C = RET_CHUNK
    n = n_tok // C
    qc = q.reshape(b, n, C, h, dk)
    kc = k.reshape(b, n, C, h, dk)
    vc = v.reshape(b, n, C, h, dv)
    j = jnp.arange(C, dtype=jnp.float32)
    rel = j[:, None] - j[None, :]
    intra_decay = jnp.where(rel[None] >= 0.0,
                            jnp.exp(log_gamma[:, None, None] * jnp.maximum(rel, 0.0)[None]), 0.0)
    scores = jnp.einsum('bnihd,bnjhd->bnhij', qc, kc) * intra_decay[None, None]
    intra = jnp.einsum('bnhij,bnjhe->bnihe', scores, vc)
    q_decay = jnp.exp(log_gamma[None, :] * (j[:, None] + 1.0))
    k_decay = jnp.exp(log_gamma[None, :] * (C - 1.0 - j[:, None]))
    chunk_decay = jnp.exp(log_gamma * C)[None, :, None, None]
    kv = jnp.einsum('bnjhd,jh,bnjhe->nbhde', kc, k_decay, vc)

    def step(state, inp):
        q_i, kv_i = inp
        out_i = jnp.einsum('bihd,bhde->bihe', q_i, state)
        return state * chunk_decay + kv_i, out_i

    _, inter = lax.scan(step, state0, (jnp.moveaxis(qc, 1, 0), kv))
    inter = jnp.moveaxis(inter, 0, 1) * q_decay[None, None, :, :, None]
    return (intra + inter).reshape(b, n_tok, h, dv)


def ctx_final_states(k, v, log_gamma):
    n_tok = k.shape[1]
    m = jnp.arange(n_tok, dtype=jnp.float32)
    w_fwd = jnp.exp(log_gamma[0][:, None] * (n_tok - 1.0 - m)[None, :])
    w_bwd = jnp.exp(log_gamma[1][:, None] * m[None, :])
    s_fwd = jnp.einsum('blhd,hl,blhe->bhde', k, w_fwd, v)
    s_bwd = jnp.einsum('blhd,hl,blhe->bhde', k, w_bwd, v)
    return s_fwd, s_bwd


def head_norm(y):
    mu = jnp.mean(y, axis=-1, keepdims=True)
    yc = y - mu
    return yc * lax.rsqrt(jnp.mean(yc * yc, axis=-1, keepdims=True) + NORM_EPS)


def short_conv(u, w):
    up = jnp.pad(u, ((0, 0), (1, 1), (0, 0)))
    return w[0] * up[:, :-2] + w[1] * up[:, 1:-1] + w[2] * up[:, 2:]


def multiscale_pool(u, w_group, scale):
    b, n_tok, width = u.shape
    uf = u.astype(jnp.float32)
    cs = jnp.concatenate([jnp.zeros((b, 1, width), jnp.float32), jnp.cumsum(uf, axis=1)], axis=1)
    t = jnp.arange(n_tok)
    outs = []
    for gi, win in enumerate(POOL_WINDOWS):
        sl = slice(gi * POOL_GDIM, (gi + 1) * POOL_GDIM)
        lo = jnp.clip(t - win // 2, 0, n_tok)
        hi = jnp.clip(t + win - win // 2, 0, n_tok)
        cnt = (hi - lo).astype(jnp.float32)
        mean = (cs[:, hi, sl] - cs[:, lo, sl]) / cnt[None, :, None]
        outs.append(mean - uf[:, :, sl])
    pooled = jnp.stack(outs, axis=2)
    mixed = jnp.einsum('blgc,gcd->blgd', pooled, w_group.astype(jnp.float32)).reshape(b, n_tok, width)
    return (mixed * scale.astype(jnp.float32)).astype(u.dtype)


def mix_stream(parts, rope, log_gamma, state_f, state_b, conv_w, pool_w, pool_scale,
               w_ret_out, w_conv_out, w_pool_out, w_o):
    q_in, k_in, v_in, g_in, conv_b, conv_c, conv_x, pool_in, gate_in = parts
    dt = q_in.dtype
    b, n_tok, _ = q_in.shape
    q = to_heads(q_in, RET_DK)
    k = to_heads(k_in, RET_DK) * RET_DK ** -0.5
    v = to_heads(v_in, RET_DV)
    if rope is not None:
        q = apply_rope(q, rope[0], rope[1])
        k = apply_rope(k, rope[0], rope[1])
    flip = lambda a: jnp.flip(a, axis=1)
    o_fwd = retention_dir(q, k, v, log_gamma[0], state_f)
    o_bwd = flip(retention_dir(flip(q), flip(k), flip(v), log_gamma[1], state_b))
    y_ret = head_norm(o_fwd + o_bwd).reshape(b, n_tok, RET_V).astype(dt) * jax.nn.silu(g_in)
    y_conv = conv_b * short_conv(conv_c * conv_x, conv_w)
    y_pool = multiscale_pool(pool_in, pool_w, pool_scale)
    g_ret, g_conv, g_pool = jnp.split(jax.nn.sigmoid(gate_in.astype(jnp.float32)).astype(dt), N_BRANCH, axis=-1)
    merged = g_ret * (y_ret @ w_ret_out) + g_conv * (y_conv @ w_conv_out) + g_pool * (y_pool @ w_pool_out)
    return merged @ w_o


def token_mixers(h, hc, rope, w_in, ret_decay, conv_w, pool_w, pool_scale,
                 w_ret_out, w_conv_out, w_pool_out, w_o, need_ctx):
    log_gamma = jax.nn.log_sigmoid(ret_decay.astype(jnp.float32))
    if need_ctx:
        pc = split_cols(hc @ w_in)
        k_c_in, v_c_in = pc[1], pc[2]
    else:
        k_c_in, v_c_in = jnp.split(hc @ w_in[:, RET_QK:2 * RET_QK + RET_V], [RET_QK], axis=-1)
    k_c = to_heads(k_c_in, RET_DK) * RET_DK ** -0.5
    v_c = to_heads(v_c_in, RET_DV)
    s_fwd, s_bwd = ctx_final_states(k_c, v_c, log_gamma)
    y = mix_stream(split_cols(h @ w_in), rope, log_gamma, s_fwd, s_bwd, conv_w, pool_w, pool_scale,
                   w_ret_out, w_conv_out, w_pool_out, w_o)
    if need_ctx:
        zeros = jnp.zeros_like(s_fwd)
        yc = mix_stream(pc, None, log_gamma, zeros, zeros, conv_w, pool_w, pool_scale,
                        w_ret_out, w_conv_out, w_pool_out, w_o)
    else:
        yc = None
    return y, yc


def hier_moe(h, w_rg, b_rg, w_re, b_re, w1, w3, w2):
    b, n_tok, d = h.shape
    t = h.reshape(-1, d)
    tf = t.astype(jnp.float32)
    g_prob = jax.nn.softmax(tf @ w_rg.astype(jnp.float32) + b_rg.astype(jnp.float32), axis=-1)
    g_top, g_idx = lax.top_k(g_prob, 1)
    e_logits = (tf @ w_re.astype(jnp.float32) + b_re.astype(jnp.float32)).reshape(-1, N_GROUPS, EXP_PER_GROUP)
    e_in_group = jnp.take_along_axis(e_logits, g_idx[:, :, None], axis=1)[:, 0]
    e_top, e_loc = lax.top_k(e_in_group, TOP_K_INNER)
    e_w = jax.nn.softmax(e_top, axis=-1) * g_top
    e_idx = g_idx * EXP_PER_GROUP + e_loc
    combine = jnp.sum(jax.nn.one_hot(e_idx, N_EXPERTS, dtype=jnp.float32) * e_w[..., None], axis=1).astype(t.dtype)
    y = jnp.zeros_like(t)
    for e in range(N_EXPERTS):
        a = jax.nn.silu(t @ w1[e]) * (t @ w3[e])
        y = y + combine[:, e:e + 1] * (a @ w2[e])
    return y.reshape(b, n_tok, d)


def setup_inputs(seed: int = 0) -> dict:
    key = jax.random.key(seed)
    ks = jax.random.split(key, 26)
    f32 = jnp.float32

    def nrm(k, shape, scale):
        return jax.random.normal(k, shape, f32) * scale

    D = D_MODEL
    decay_base = jnp.log(2.0 ** jnp.arange(5, 5 + RET_HEADS, dtype=f32) - 1.0)
    return {
        'x': nrm(ks[0], (BATCH, SEQ, D), 1.0),
        'c': nrm(ks[1], (BATCH, D), 1.0),
        'ctx': nrm(ks[2], (BATCH, CTX_LEN, D), 1.0),
        'c_ctx': nrm(ks[3], (D,), 1.0),
        'w_ada': nrm(ks[4], (DEPTH, D, N_MOD * D), 0.5 * D ** -0.5),
        'b_ada': nrm(ks[5], (DEPTH, N_MOD * D), 0.02),
        'norm1': 1.0 + nrm(ks[6], (DEPTH, D), 0.05),
        'norm2': 1.0 + nrm(ks[7], (DEPTH, D), 0.05),
        'w_in': nrm(ks[8], (DEPTH, D, IN_WIDTH), D ** -0.5),
        'ret_decay': decay_base[None, None, :] + nrm(ks[9], (DEPTH, 2, RET_HEADS), 0.1),
        'conv_w': nrm(ks[10], (DEPTH, CONV_K, CONV_W), CONV_K ** -0.5),
        'pool_w': nrm(ks[11], (DEPTH, POOL_GROUPS, POOL_GDIM, POOL_GDIM), POOL_GDIM ** -0.5),
        'pool_scale': 1.0 + nrm(ks[12], (DEPTH, POOL_W), 0.1),
        'w_ret_out': nrm(ks[13], (DEPTH, RET_V, D), RET_V ** -0.5),
        'w_conv_out': nrm(ks[14], (DEPTH, CONV_W, D), CONV_W ** -0.5),
        'w_pool_out': nrm(ks[15], (DEPTH, POOL_W, D), POOL_W ** -0.5),
        'w_o': nrm(ks[16], (DEPTH, D, D), D ** -0.5),
        'w_rg': nrm(ks[17], (DEPTH, D, N_GROUPS), D ** -0.5),
        'b_rg': nrm(ks[18], (DEPTH, N_GROUPS), 0.01),
        'w_re': nrm(ks[19], (DEPTH, D, N_EXPERTS), D ** -0.5),
        'b_re': nrm(ks[20], (DEPTH, N_EXPERTS), 0.01),
        'w1': nrm(ks[21], (DEPTH, N_EXPERTS, D, D_FF_EXPERT), D ** -0.5),
        'w3': nrm(ks[22], (DEPTH, N_EXPERTS, D, D_FF_EXPERT), D ** -0.5),
        'w2': nrm(ks[23], (DEPTH, N_EXPERTS, D_FF_EXPERT, D), D_FF_EXPERT ** -0.5),
        'final_norm': 1.0 + nrm(ks[24], (D,), 0.05),
    }


def reference(x, c, ctx, c_ctx, w_ada, b_ada, norm1, norm2, w_in, ret_decay, conv_w, pool_w, pool_scale,
              w_ret_out, w_conv_out, w_pool_out, w_o, w_rg, b_rg, w_re, b_re, w1, w3, w2, final_norm):
    rope = axial_rope(x.shape[1])
    xc = ctx
    silu_c = jax.nn.silu(c)
    silu_cc = jax.nn.silu(c_ctx)
    n_ctx = ctx.shape[1]
    for l in range(DEPTH):
        last = l == DEPTH - 1
        mod = (silu_c @ w_ada[l] + b_ada[l])[:, None, :]
        mod_c = (silu_cc @ w_ada[l] + b_ada[l])[None, None, :]
        sh1, sc1, gt1, sh2, sc2, gt2 = jnp.split(mod, N_MOD, axis=-1)
        csh1, csc1, cgt1, csh2, csc2, cgt2 = jnp.split(mod_c, N_MOD, axis=-1)
        h = rmsnorm(x, norm1[l]) * (1.0 + sc1) + sh1
        hc = rmsnorm(xc, norm1[l]) * (1.0 + csc1) + csh1
        y, yc = token_mixers(h, hc, rope, w_in[l], ret_decay[l], conv_w[l], pool_w[l], pool_scale[l],
                             w_ret_out[l], w_conv_out[l], w_pool_out[l], w_o[l], not last)
        x = x + gt1 * y
        h = rmsnorm(x, norm2[l]) * (1.0 + sc2) + sh2
        if last:
            x = x + gt2 * hier_moe(h, w_rg[l], b_rg[l], w_re[l], b_re[l], w1[l], w3[l], w2[l])
        else:
            xc = xc + cgt1 * yc
            hc = rmsnorm(xc, norm2[l]) * (1.0 + csc2) + csh2
            f = hier_moe(jnp.concatenate([hc, h], axis=1), w_rg[l], b_rg[l], w_re[l], b_re[l], w1[l], w3[l], w2[l])
            xc = xc + cgt2 * f[:, :n_ctx]
            x = x + gt2 * f[:, n_ctx:]
    return rmsnorm(x, final_norm)
```

```python
import functools

import numpy as np
import jax
import jax.numpy as jnp
from jax import lax
from jax.experimental import pallas as pl
from jax.experimental.pallas import tpu as pltpu

F32 = jnp.float32
MM_DTYPE = jnp.bfloat16
ACT_DTYPE = jnp.bfloat16

NORM_EPS = 1e-6
GRID_W = 64
ROPE_BASE = 10000.0
N_MOD = 6

RET_HEADS = 4
RET_DK = 128
RET_DV = 256
RET_QK = RET_HEADS * RET_DK
RET_V = RET_HEADS * RET_DV
RET_CHUNK = 256

CONV_W = 512
POOL_WINDOWS = (2, 4, 8, 16)
POOL_GROUPS = 4
POOL_GDIM = 128
POOL_W = POOL_GROUPS * POOL_GDIM
POOL_HALO = 8

N_GROUPS = 4
EXP_PER_GROUP = 4
N_EXPERTS = N_GROUPS * EXP_PER_GROUP
D_FF = 512
PAIR_LO = (0, 0, 0, 1, 1, 2)
PAIR_HI = (1, 2, 3, 2, 3, 3)
N_PAIRS = len(PAIR_LO)
N_BUCKETS = N_GROUPS * N_PAIRS

LANES = 128
META_BUCKET, META_WLO, META_WHI = 0, 2, 3

OFF_Q = 0
OFF_K = OFF_Q + RET_QK
OFF_V = OFF_K + RET_QK
OFF_G = OFF_V + RET_V
OFF_CB = OFF_G + RET_V
OFF_CC = OFF_CB + CONV_W
OFF_CX = OFF_CC + CONV_W
OFF_PI = OFF_CX + CONV_W
OFF_GATE = OFF_PI + POOL_W

VMEM_LIMIT = 56 * 1024 * 1024


def _cparams(sem):
    return pltpu.CompilerParams(dimension_semantics=sem, vmem_limit_bytes=VMEM_LIMIT)


def _split_bf16(a):
    hi = a.astype(jnp.bfloat16)
    lo = (a - hi.astype(F32)).astype(jnp.bfloat16)
    return hi, lo


def _dot(a, b):
    return jnp.dot(a, b, preferred_element_type=F32)


def _dot3(a, b):
    ah, al = _split_bf16(a)
    bh, bl = _split_bf16(b)
    return _dot(ah, bh) + _dot(ah, bl) + _dot(al, bh)


def _mod_row(i, tile, n_lat, seq, batch):
    return jnp.where(i < n_lat // tile, (i * tile) // seq, batch)


def _ada_kernel(c_ref, w_ref, b_ref, o_ref):
    cv = c_ref[...]
    s = cv * jax.nn.sigmoid(cv)
    o_ref[0] = _dot3(s, w_ref[0]) + b_ref[0]


def _ada(cond, w_ada, b_ada):
    depth, d, width = w_ada.shape
    rows = cond.shape[0]
    tn = 512
    return pl.pallas_call(
        _ada_kernel,
        grid=(depth, width // tn),
        in_specs=[pl.BlockSpec((rows, d), lambda l, j: (0, 0)),
                  pl.BlockSpec((1, d, tn), lambda l, j: (l, 0, j)),
                  pl.BlockSpec((1, 1, tn), lambda l, j: (l, 0, j))],
        out_specs=pl.BlockSpec((1, rows, tn), lambda l, j: (l, 0, j)),
        out_shape=jax.ShapeDtypeStruct((depth, rows, width), F32),
        compiler_params=_cparams(("parallel", "parallel")),
    )(cond, w_ada, b_ada.reshape(depth, 1, width))


def _rms_mod(x, g, sc, sh):
    ms = jnp.mean(x * x, axis=-1, keepdims=True)
    return x * lax.rsqrt(ms + NORM_EPS) * g * (1.0 + sc) + sh


def _inproj_kernel(x_ref, g_ref, sc_ref, sh_ref, w_ref, o_ref, h_scr):
    @pl.when(pl.program_id(1) == 0)
    def _():
        h_scr[...] = _rms_mod(x_ref[...], g_ref[...], sc_ref[0], sh_ref[0]).astype(h_scr.dtype)

    o_ref[...] = _dot(h_scr[...], w_ref[...]).astype(o_ref.dtype)


def _inproj(xs, norm_g, mod_l, w, *, tm, tn, n_lat, seq, batch):
    n, d = xs.shape
    width = w.shape[1]
    row = functools.partial(_mod_row, tile=tm, n_lat=n_lat, seq=seq, batch=batch)
    return pl.pallas_call(
        _inproj_kernel,
        grid=(n // tm, width // tn),
        in_specs=[pl.BlockSpec((tm, d), lambda i, j: (i, 0)),
                  pl.BlockSpec((1, d), lambda i, j: (0, 0)),
                  pl.BlockSpec((1, 1, d), lambda i, j: (row(i), 0, 1)),
                  pl.BlockSpec((1, 1, d), lambda i, j: (row(i), 0, 0)),
                  pl.BlockSpec((d, tn), lambda i, j: (0, j))],
        out_specs=pl.BlockSpec((tm, tn), lambda i, j: (i, j)),
        out_shape=jax.ShapeDtypeStruct((n, width), ACT_DTYPE),
        scratch_shapes=[pltpu.VMEM((tm, d), MM_DTYPE)],
        compiler_params=_cparams(("parallel", "arbitrary")),
    )(xs, norm_g, mod_l, mod_l, w)


def _ret_kernel(lg_ref, *refs, seq, n_ctx, use_rope):
    if n_ctx:
        q_ref, k_ref, v_ref, g_ref, kc_ref, vc_ref, cos_ref, sin_ref, o_ref, kt_scr, sb_scr = refs
    else:
        q_ref, k_ref, v_ref, g_ref, o_ref, kt_scr = refs
    C = RET_CHUNK
    n_chunk = seq // C
    head = pl.program_id(1)
    lgf = lg_ref[0, head]
    lgb = lg_ref[1, head]

    q = q_ref[...].astype(F32)
    k = k_ref[...].astype(F32)
    if use_rope:
        cos = cos_ref[...]
        sin = sin_ref[...]
        q = q * cos + pltpu.roll(q, RET_DK // 2, 1) * sin
        k = k * cos + pltpu.roll(k, RET_DK // 2, 1) * sin
    q = q * (RET_DK ** -0.5)
    kt_scr[...] = k.T

    ri = lax.broadcasted_iota(jnp.int32, (C, C), 0)
    ci = lax.broadcasted_iota(jnp.int32, (C, C), 1)
    rel = (ri - ci).astype(F32)
    dmask = jnp.where(rel > 0.0, jnp.exp(lgf * jnp.maximum(rel, 0.0)),
                      jnp.where(rel < 0.0, jnp.exp(lgb * jnp.maximum(-rel, 0.0)), 2.0))
    icol = lax.broadcasted_iota(jnp.int32, (C, 1), 0).astype(F32)
    jrow = lax.broadcasted_iota(jnp.int32, (1, C), 1).astype(F32)
    qdec_f = jnp.exp(lgf * (icol + 1.0))
    qdec_b = jnp.exp(lgb * (C - icol))
    kdec_f = jnp.exp(lgf * (C - 1.0 - jrow))
    kdec_b = jnp.exp(lgb * jrow)
    zrow = jnp.zeros((1, RET_DV), F32)
    cdec_f = jnp.exp(zrow + lgf * C)
    cdec_b = jnp.exp(zrow + lgb * C)

    def chunk(ref, n):
        return ref[n * C:(n + 1) * C, :]

    def kt_chunk(n):
        return kt_scr[:, n * C:(n + 1) * C]

    if n_ctx:
        kct = kc_ref[...].astype(F32).T
        vcx = vc_ref[...].astype(MM_DTYPE)
        mrow = lax.broadcasted_iota(jnp.int32, (1, n_ctx), 1).astype(F32)
        s_f = _dot((kct * jnp.exp(lgf * (n_ctx - 1.0 - mrow))).astype(MM_DTYPE), vcx)
        s_b = _dot((kct * jnp.exp(lgb * mrow)).astype(MM_DTYPE), vcx)
        sb_scr[n_chunk - 1] = s_b
        for n in range(n_chunk - 1, 0, -1):
            s_b = s_b * cdec_b + _dot((kt_chunk(n) * kdec_b).astype(MM_DTYPE), chunk(v_ref, n).astype(MM_DTYPE))
            sb_scr[n - 1] = s_b

    for n in range(n_chunk):
        qn = q[n * C:(n + 1) * C, :]
        ktn = kt_chunk(n)
        vn = chunk(v_ref, n).astype(MM_DTYPE)
        s = _dot(qn.astype(MM_DTYPE), ktn.astype(MM_DTYPE)) * dmask
        o = _dot(s.astype(MM_DTYPE), vn)
        if n_ctx:
            qcat = jnp.concatenate([qn * qdec_f, qn * qdec_b], axis=1).astype(MM_DTYPE)
            scat = jnp.concatenate([s_f, sb_scr[n]], axis=0).astype(MM_DTYPE)
            o = o + _dot(qcat, scat)
            if n + 1 < n_chunk:
                s_f = s_f * cdec_f + _dot((ktn * kdec_f).astype(MM_DTYPE), vn)
        mu = jnp.mean(o, axis=-1, keepdims=True)
        oc = o - mu
        yn = oc * lax.rsqrt(jnp.mean(oc * oc, axis=-1, keepdims=True) + NORM_EPS)
        gn = chunk(g_ref, n).astype(F32)
        o_ref[n * C:(n + 1) * C, :] = (yn * (gn * jax.nn.sigmoid(gn))).astype(o_ref.dtype)


def _retention_latent(p, log_gamma, cos, sin, *, batch, seq, n_ctx, n_lat):
    n = p.shape[0]
    cb = n_lat // n_ctx
    kq, kv = OFF_K // RET_DK, OFF_V // RET_DV
    gs = pltpu.PrefetchScalarGridSpec(
        num_scalar_prefetch=1,
        grid=(batch, RET_HEADS),
        in_specs=[pl.BlockSpec((seq, RET_DK), lambda b, h, lg: (b, h)),
                  pl.BlockSpec((seq, RET_DK), lambda b, h, lg: (b, kq + h)),
                  pl.BlockSpec((seq, RET_DV), lambda b, h, lg: (b, kv + h)),
                  pl.BlockSpec((seq, RET_DV), lambda b, h, lg: (b, OFF_G // RET_DV + h)),
                  pl.BlockSpec((n_ctx, RET_DK), lambda b, h, lg: (cb + b, kq + h)),
                  pl.BlockSpec((n_ctx, RET_DV), lambda b, h, lg: (cb + b, kv + h)),
                  pl.BlockSpec((seq, RET_DK), lambda b, h, lg: (0, 0)),
                  pl.BlockSpec((seq, RET_DK), lambda b, h, lg: (0, 0))],
        out_specs=pl.BlockSpec((seq, RET_DV), lambda b, h, lg: (b, h)),
        scratch_shapes=[pltpu.VMEM((RET_DK, seq), F32),
                        pltpu.VMEM((seq // RET_CHUNK, RET_DK, RET_DV), F32)])
    return pl.pallas_call(
        functools.partial(_ret_kernel, seq=seq, n_ctx=n_ctx, use_rope=True),
        grid_spec=gs,
        out_shape=jax.ShapeDtypeStruct((n, RET_V), ACT_DTYPE),
        compiler_params=_cparams(("parallel", "parallel")),
    )(log_gamma, p, p, p, p, p, p, cos, sin)


def _retention_ctx(p, log_gamma, y_ret, *, batch, n_ctx, n_lat):
    cb = n_lat // n_ctx
    kq, kv = OFF_K // RET_DK, OFF_V // RET_DV
    gs = pltpu.PrefetchScalarGridSpec(
        num_scalar_prefetch=1,
        grid=(batch, RET_HEADS),
        in_specs=[pl.BlockSpec((n_ctx, RET_DK), lambda b, h, lg: (cb + b, h)),
                  pl.BlockSpec((n_ctx, RET_DK), lambda b, h, lg: (cb + b, kq + h)),
                  pl.BlockSpec((n_ctx, RET_DV), lambda b, h, lg: (cb + b, kv + h)),
                  pl.BlockSpec((n_ctx, RET_DV), lambda b, h, lg: (cb + b, OFF_G // RET_DV + h)),
                  pl.BlockSpec(memory_space=pl.ANY)],
        out_specs=pl.BlockSpec((n_ctx, RET_DV), lambda b, h, lg: (cb + b, h)),
        scratch_shapes=[pltpu.VMEM((RET_DK, n_ctx), F32)])

    def body(lg_ref, q_ref, k_ref, v_ref, g_ref, alias_ref, o_ref, kt_scr):
        del alias_ref
        _ret_kernel(lg_ref, q_ref, k_ref, v_ref, g_ref, o_ref, kt_scr, seq=n_ctx, n_ctx=0, use_rope=False)

    return pl.pallas_call(
        body,
        grid_spec=gs,
        out_shape=jax.ShapeDtypeStruct(y_ret.shape, y_ret.dtype),
        input_output_aliases={5: 0},
        compiler_params=_cparams(("parallel", "parallel")),
    )(log_gamma, p, p, p, p, y_ret)


def _convpool_kernel(*refs, seq, aliased):
    if aliased:
        cb_ref, cc_ref, cx_ref, pi_ref, cw_ref, pw_ref, ps_ref, _, _, yc_ref, yp_ref = refs
    else:
        cb_ref, cc_ref, cx_ref, pi_ref, cw_ref, pw_ref, ps_ref, yc_ref, yp_ref = refs
    grp = pl.program_id(1)
    t = lax.broadcasted_iota(jnp.int32, (seq, 1), 0)

    u = cc_ref[...].astype(F32) * cx_ref[...].astype(F32)
    u_prev = jnp.where(t == 0, 0.0, pltpu.roll(u, 1, 0))
    u_next = jnp.where(t == seq - 1, 0.0, pltpu.roll(u, seq - 1, 0))
    w = cw_ref[...]
    conv = w[0:1, :] * u_prev + w[1:2, :] * u + w[2:3, :] * u_next
    yc_ref[...] = (cb_ref[...].astype(F32) * conv).astype(yc_ref.dtype)

    p = pi_ref[...].astype(F32)
    halo = jnp.zeros((POOL_HALO, p.shape[1]), F32)
    ext = jnp.concatenate([halo, p, halo], axis=0)
    length = seq + 2 * POOL_HALO

    def at(a, k):
        return pltpu.roll(a, (-k) % length, 0)

    a2 = at(ext, -1) + ext
    a4 = at(a2, -1) + at(a2, 1)
    a8 = at(a4, -2) + at(a4, 2)
    a16 = at(a8, -4) + at(a8, 4)
    wsum = jnp.where(grp == 0, a2, jnp.where(grp == 1, a4, jnp.where(grp == 2, a8, a16)))
    wsum = wsum[POOL_HALO:POOL_HALO + seq, :]
    half = jnp.left_shift(1, grp)
    lo = jnp.clip(t - half, 0, seq)
    hi = jnp.clip(t + half, 0, seq)
    cnt = (hi - lo).astype(F32)
    pooled = wsum / cnt - p
    mixed = _dot(pooled.astype(MM_DTYPE), pw_ref[0]) * ps_ref[...]
    yp_ref[...] = mixed.astype(yp_ref.dtype)


def _convpool(p, conv_w, pool_w, pool_scale, prev, *, n_seq, seq, row0):
    n = p.shape[0]
    g128 = lambda off: off // POOL_GDIM
    col = lambda off: (lambda b, g: (row0 + b, g128(off) + g))
    in_specs = [pl.BlockSpec((seq, POOL_GDIM), col(OFF_CB)),
                pl.BlockSpec((seq, POOL_GDIM), col(OFF_CC)),
                pl.BlockSpec((seq, POOL_GDIM), col(OFF_CX)),
                pl.BlockSpec((seq, POOL_GDIM), col(OFF_PI)),
                pl.BlockSpec((conv_w.shape[0], POOL_GDIM), lambda b, g: (0, g)),
                pl.BlockSpec((1, POOL_GDIM, POOL_GDIM), lambda b, g: (g, 0, 0)),
                pl.BlockSpec((1, POOL_GDIM), lambda b, g: (0, g))]
    args = [p, p, p, p, conv_w, pool_w, pool_scale]
    aliases = {}
    if prev is not None:
        in_specs += [pl.BlockSpec(memory_space=pl.ANY), pl.BlockSpec(memory_space=pl.ANY)]
        args += list(prev)
        aliases = {7: 0, 8: 1}
    out_spec = pl.BlockSpec((seq, POOL_GDIM), lambda b, g: (row0 + b, g))
    return pl.pallas_call(
        functools.partial(_convpool_kernel, seq=seq, aliased=prev is not None),
        grid=(n_seq, POOL_GROUPS),
        in_specs=in_specs,
        out_specs=[out_spec, out_spec],
        out_shape=[jax.ShapeDtypeStruct((n, CONV_W), ACT_DTYPE), jax.ShapeDtypeStruct((n, POOL_W), ACT_DTYPE)],
        input_output_aliases=aliases,
        compiler_params=_cparams(("parallel", "parallel")),
    )(*args)


def _route(logits):
    lane = lax.broadcasted_iota(jnp.int32, logits.shape, 1)
    neg = -jnp.inf
    is_g = lane < N_GROUPS
    gl = jnp.where(is_g, logits, neg)
    gmax = jnp.max(gl, axis=-1, keepdims=True)
    gidx = jnp.min(jnp.where(gl == gmax, lane, LANES), axis=-1, keepdims=True)
    gtop = 1.0 / jnp.sum(jnp.where(is_g, jnp.exp(gl - gmax), 0.0), axis=-1, keepdims=True)
    base = N_GROUPS + EXP_PER_GROUP * gidx
    el = jnp.where((lane >= base) & (lane < base + EXP_PER_GROUP), logits, neg)
    m1 = jnp.max(el, axis=-1, keepdims=True)
    i1 = jnp.min(jnp.where(el == m1, lane, LANES), axis=-1, keepdims=True)
    el2 = jnp.where(lane == i1, neg, el)
    m2 = jnp.max(el2, axis=-1, keepdims=True)
    i2 = jnp.min(jnp.where(el2 == m2, lane, LANES), axis=-1, keepdims=True)
    e2 = jnp.exp(m2 - m1)
    w1 = gtop / (1.0 + e2)
    w2 = gtop * e2 / (1.0 + e2)
    first_lower = i1 < i2
    lo = jnp.minimum(i1, i2) - base
    hi = jnp.maximum(i1, i2) - base
    pair = lo * 3 - jnp.right_shift(lo * (lo - 1), 1) + hi - lo - 1
    bucket = (gidx * N_PAIRS + pair).astype(F32)
    wlo = jnp.where(first_lower, w1, w2)
    whi = jnp.where(first_lower, w2, w1)
    return jnp.where(lane == META_BUCKET, bucket,
                     jnp.where(lane == META_WLO, wlo, jnp.where(lane == META_WHI, whi, 0.0)))


def _outproj_kernel(x_ref, yr_ref, yc_ref, yp_ref, g0_ref, g1_ref, g2_ref, gt_ref, sc_ref, sh_ref, ng_ref,
                    wr_ref, wc_ref, wp_ref, wo_ref, wrt_ref, brt_ref, xo_ref, h_ref):
    d = x_ref.shape[1]
    sig = lambda r: jax.nn.sigmoid(r[...].astype(F32))
    merged = (sig(g0_ref) * _dot(yr_ref[...], wr_ref[...])
              + sig(g1_ref) * _dot(yc_ref[...], wc_ref[...])
              + sig(g2_ref) * _dot(yp_ref[...], wp_ref[...]))
    y = _dot(merged.astype(MM_DTYPE), wo_ref[...])
    x = x_ref[...] + gt_ref[0] * y
    xo_ref[...] = x
    h = _rms_mod(x, ng_ref[...], sc_ref[0], sh_ref[0])
    h_ref[:, :d] = h
    h_ref[:, d:] = _route(_dot3(h, wrt_ref[...]) + brt_ref[...])


def _outproj(xs, p, y_ret, y_conv, y_pool, mod_l, norm_g, w_ret, w_conv, w_pool, w_o, w_router, b_router,
             *, tm, n_rows, n_lat, seq, batch):
    n, d = xs.shape
    row = functools.partial(_mod_row, tile=tm, n_lat=n_lat, seq=seq, batch=batch)
    gate = lambda k: pl.BlockSpec((tm, d), lambda i: (i, OFF_GATE // d + k))
    mod = lambda k: pl.BlockSpec((1, 1, d), lambda i: (row(i), 0, k))
    full = lambda a: pl.BlockSpec(a.shape, lambda i: (0,) * a.ndim)
    return pl.pallas_call(
        _outproj_kernel,
        grid=(n_rows // tm,),
        in_specs=[pl.BlockSpec((tm, d), lambda i: (i, 0)),
                  pl.BlockSpec((tm, RET_V), lambda i: (i, 0)),
                  pl.BlockSpec((tm, CONV_W), lambda i: (i, 0)),
                  pl.BlockSpec((tm, POOL_W), lambda i: (i, 0)),
                  gate(0), gate(1), gate(2),
                  mod(2), mod(4), mod(3),
                  full(norm_g), full(w_ret), full(w_conv), full(w_pool), full(w_o), full(w_router), full(b_router)],
        out_specs=[pl.BlockSpec((tm, d), lambda i: (i, 0)),
                   pl.BlockSpec((tm, d + LANES), lambda i: (i, 0))],
        out_shape=[jax.ShapeDtypeStruct((n, d), F32), jax.ShapeDtypeStruct((n, d + LANES), F32)],
        input_output_aliases={0: 0},
        compiler_params=_cparams(("parallel",)),
    )(xs, y_ret, y_conv, y_pool, p, p, p, mod_l, mod_l, mod_l, norm_g, w_ret, w_conv, w_pool, w_o, w_router, b_router)


def _pos_kernel(meta_ref, pos_ref, cnt_ref, carry, offs, *, tile):
    phase = pl.program_id(0)
    i = pl.program_id(1)
    shift = tile.bit_length() - 1
    assert tile == 1 << shift
    tr = meta_ref.shape[0]
    lane = lax.broadcasted_iota(jnp.int32, (tr, LANES), 1)
    bucket = meta_ref[:, META_BUCKET:META_BUCKET + 1].astype(jnp.int32)
    onehot = lane == bucket
    ones = jnp.where(onehot, 1.0, 0.0)
    tile_cnt = jnp.sum(ones, axis=0, keepdims=True)

    @pl.when((phase == 0) & (i == 0))
    def _():
        carry[...] = jnp.zeros_like(carry)

    @pl.when(phase == 0)
    def _():
        carry[...] = carry[...] + tile_cnt

    @pl.when((phase == 1) & (i == 0))
    def _():
        cnt = carry[...].astype(jnp.int32)
        cnt_ref[...] = jnp.broadcast_to(cnt, cnt_ref.shape)
        padded = jnp.left_shift(jnp.right_shift(cnt + (tile - 1), shift), shift)
        lane1 = lax.broadcasted_iota(jnp.int32, (1, LANES), 1)
        run = padded
        k = 1
        while k < LANES:
            run = run + jnp.where(lane1 >= k, pltpu.roll(run, k, 1), 0)
            k *= 2
        offs[...] = (run - padded).astype(F32)
        carry[...] = jnp.zeros_like(carry)

    @pl.when(phase == 1)
    def _():
        r = lax.broadcasted_iota(jnp.int32, (tr, tr), 0)
        c = lax.broadcasted_iota(jnp.int32, (tr, tr), 1)
        tri = jnp.where(r >= c, 1.0, 0.0).astype(jnp.bfloat16)
        incl = _dot(tri, ones.astype(jnp.bfloat16))
        slot = incl - 1.0 + carry[...] + offs[...]
        pos_ref[...] = jnp.sum(jnp.where(onehot, slot, 0.0), axis=-1, keepdims=True).astype(jnp.int32)
        carry[...] = carry[...] + tile_cnt


def _positions(h_ext, *, n_rows, tr, tile, d):
    n = h_ext.shape[0]
    return pl.pallas_call(
        functools.partial(_pos_kernel, tile=tile),
        grid=(2, n_rows // tr),
        in_specs=[pl.BlockSpec((tr, LANES), lambda ph, i: (i, d // LANES))],
        out_specs=[pl.BlockSpec((tr, 1), lambda ph, i: (i * ph, 0)),
                   pl.BlockSpec((8, LANES), lambda ph, i: (0, 0))],
        out_shape=[jax.ShapeDtypeStruct((n, 1), jnp.int32), jax.ShapeDtypeStruct((8, LANES), jnp.int32)],
        scratch_shapes=[pltpu.VMEM((1, LANES), F32), pltpu.VMEM((1, LANES), F32)],
        compiler_params=_cparams(("arbitrary", "arbitrary")),
    )(h_ext)


def _scatter_kernel(pos_ref, h_ref, hs_ref, sem):
    i = pl.program_id(0)
    ts = h_ref.shape[0]

    def row_copy(r):
        return pltpu.make_async_copy(h_ref.at[pl.ds(r, 1)], hs_ref.at[pl.ds(pos_ref[i * ts + r], 1)], sem)

    def start(r, carry):
        row_copy(r).start()
        return carry

    def wait(r, carry):
        row_copy(r).wait()
        return carry

    lax.fori_loop(0, ts, start, 0)
    lax.fori_loop(0, ts, wait, 0)


def _scatter_rows(pos, h_ext, *, n_rows, ts, n_sorted):
    width = h_ext.shape[1]
    gs = pltpu.PrefetchScalarGridSpec(
        num_scalar_prefetch=1,
        grid=(n_rows // ts,),
        in_specs=[pl.BlockSpec((ts, width), lambda i, pos: (i, 0))],
        out_specs=pl.BlockSpec(memory_space=pl.ANY),
        scratch_shapes=[pltpu.SemaphoreType.DMA(())])
    return pl.pallas_call(
        _scatter_kernel,
        grid_spec=gs,
        out_shape=jax.ShapeDtypeStruct((n_sorted, width), h_ext.dtype),
        compiler_params=_cparams(("arbitrary",)),
    )(pos, h_ext)


def _moe_kernel(elo_ref, ehi_ref, valid_ref, hs_ref, w1l_ref, w1h_ref, w3l_ref, w3h_ref, w2l_ref, w2h_ref, ys_ref):
    s = pl.program_id(0)
    valid = valid_ref[s]
    tm = hs_ref.shape[0]
    d = ys_ref.shape[1]

    @pl.when(valid > 0)
    def _():
        keep = lax.broadcasted_iota(jnp.int32, (tm, 1), 0) < valid
        h = jnp.where(keep, hs_ref[:, :d], 0.0).astype(MM_DTYPE)
        meta = hs_ref[:, d:]
        wlo = jnp.where(keep, meta[:, META_WLO:META_WLO + 1], 0.0)
        whi = jnp.where(keep, meta[:, META_WHI:META_WHI + 1], 0.0)

        def expert(w1, w3, w2):
            a = _dot(h, w1[0])
            a = a * jax.nn.sigmoid(a) * _dot(h, w3[0])
            return _dot(a.astype(MM_DTYPE), w2[0])

        ys_ref[...] = wlo * expert(w1l_ref, w3l_ref, w2l_ref) + whi * expert(w1h_ref, w3h_ref, w2h_ref)


def _moe(elo, ehi, valid, hs, w1, w3, w2, *, tm, d):
    n_sorted, width = hs.shape
    up = lambda sel: pl.BlockSpec((1, d, D_FF), lambda s, elo, ehi, valid: ((elo, ehi)[sel][s], 0, 0))
    down = lambda sel: pl.BlockSpec((1, D_FF, d), lambda s, elo, ehi, valid: ((elo, ehi)[sel][s], 0, 0))
    gs = pltpu.PrefetchScalarGridSpec(
        num_scalar_prefetch=3,
        grid=(n_sorted // tm,),
        in_specs=[pl.BlockSpec((tm, width), lambda s, elo, ehi, valid: (s, 0)),
                  up(0), up(1), up(0), up(1), down(0), down(1)],
        out_specs=pl.BlockSpec((tm, d), lambda s, elo, ehi, valid: (s, 0)))
    return pl.pallas_call(
        _moe_kernel,
        grid_spec=gs,
        out_shape=jax.ShapeDtypeStruct((n_sorted, d), F32),
        compiler_params=_cparams(("arbitrary",)),
    )(elo, ehi, valid, hs, w1, w1, w3, w3, w2, w2)


def _gather_kernel(pos_ref, ys_ref, x_ref, gt_ref, fg_ref, o_ref, buf, sem, *, final):
    i = pl.program_id(0)
    tg = x_ref.shape[0]

    def row_copy(r):
        return pltpu.make_async_copy(ys_ref.at[pl.ds(pos_ref[i * tg + r], 1)], buf.at[pl.ds(r, 1)], sem)

    def start(r, carry):
        row_copy(r).start()
        return carry

    def wait(r, carry):
        row_copy(r).wait()
        return carry

    lax.fori_loop(0, tg, start, 0)
    lax.fori_loop(0, tg, wait, 0)
    x = x_ref[...] + gt_ref[0] * buf[...]
    if final:
        x = x * lax.rsqrt(jnp.mean(x * x, axis=-1, keepdims=True) + NORM_EPS) * fg_ref[...]
    o_ref[...] = x


def _gather_rows(pos, ys, xs, mod_l, final_g, *, tg, n_rows, n_lat, seq, batch, final):
    n, d = xs.shape
    row = functools.partial(_mod_row, tile=tg, n_lat=n_lat, seq=seq, batch=batch)
    gs = pltpu.PrefetchScalarGridSpec(
        num_scalar_prefetch=1,
        grid=(n_rows // tg,),
        in_specs=[pl.BlockSpec(memory_space=pl.ANY),
                  pl.BlockSpec((tg, d), lambda i, pos: (i, 0)),
                  pl.BlockSpec((1, 1, d), lambda i, pos: (row(i), 0, 5)),
                  pl.BlockSpec((1, d), lambda i, pos: (0, 0))],
        out_specs=pl.BlockSpec((tg, d), lambda i, pos: (i, 0)),
        scratch_shapes=[pltpu.VMEM((tg, d), F32), pltpu.SemaphoreType.DMA(())])
    return pl.pallas_call(
        functools.partial(_gather_kernel, final=final),
        grid_spec=gs,
        out_shape=jax.ShapeDtypeStruct((n_rows if final else n, d), F32),
        input_output_aliases={} if final else {2: 0},
        compiler_params=_cparams(("arbitrary",)),
    )(pos, ys, xs, mod_l, final_g)


def _rope_tables(seq):
    rows = seq // GRID_W
    row = jnp.repeat(jnp.arange(rows, dtype=F32), GRID_W)
    col = jnp.tile(jnp.arange(GRID_W, dtype=F32), rows)
    n_freq = RET_DK // 4
    inv_freq = ROPE_BASE ** (-jnp.arange(n_freq, dtype=F32) / n_freq)
    ang = jnp.concatenate([row[:, None] * inv_freq[None, :], col[:, None] * inv_freq[None, :]], axis=-1)
    cos, sin = jnp.cos(ang), jnp.sin(ang)
    return jnp.concatenate([cos, cos], axis=-1), jnp.concatenate([-sin, sin], axis=-1)


def _tile_tables(counts, *, tm, n_tiles):
    cnt = counts[0, :N_BUCKETS]
    padded = ((cnt + tm - 1) // tm) * tm
    ends = jnp.cumsum(padded)
    starts = ends - padded
    tile_row = jnp.arange(n_tiles, dtype=jnp.int32) * tm
    tb = jnp.sum((tile_row[:, None] >= ends[None, :]).astype(jnp.int32), axis=1)
    used = tb < N_BUCKETS
    tbc = jnp.minimum(tb, N_BUCKETS - 1)
    valid = jnp.where(used, jnp.clip(starts[tbc] + cnt[tbc] - tile_row, 0, tm), 0).astype(jnp.int32)
    last = jnp.maximum(jnp.sum(used.astype(jnp.int32)) - 1, 0)
    tbc = jnp.where(used, tbc, tbc[last])
    grp, pair = tbc // N_PAIRS, tbc % N_PAIRS
    elo = grp * EXP_PER_GROUP + jnp.asarray(PAIR_LO, jnp.int32)[pair]
    ehi = grp * EXP_PER_GROUP + jnp.asarray(PAIR_HI, jnp.int32)[pair]
    return elo.astype(jnp.int32), ehi.astype(jnp.int32), valid


def _pick_tile(n, want, *also):
    t = want
    while n % t or any(a % t for a in also):
        t //= 2
    return t


def kernel(x, c, ctx, c_ctx, w_ada, b_ada, norm1, norm2, w_in, ret_decay, conv_w, pool_w, pool_scale, w_ret_out,
           w_conv_out, w_pool_out, w_o, w_rg, b_rg, w_re, b_re, w1, w3, w2, final_norm):
    batch, seq, d = x.shape
    n_ctx = ctx.shape[1]
    depth = w_ada.shape[0]
    n_lat, n_c = batch * seq, batch * n_ctx
    n = n_lat + n_c
    assert POOL_WINDOWS == (2, 4, 8, 16) and POOL_HALO >= max(POOL_WINDOWS) // 2
    assert seq % RET_CHUNK == 0 and n_ctx == RET_CHUNK and seq % GRID_W == 0 and n_lat % n_ctx == 0

    tm_in = _pick_tile(seq, 1024, n_c)
    tm_out = _pick_tile(seq, 512, n_c)
    tm_moe = 256
    tr = _pick_tile(seq, 512, n_c)
    ts = _pick_tile(seq, 512, n_c)

    xs = jnp.concatenate([x.reshape(n_lat, d), ctx.reshape(n_c, d)], axis=0)
    mod_rows = -(-(batch + 1) // 8) * 8
    cond = jnp.zeros((mod_rows, d), F32).at[:batch].set(c).at[batch].set(c_ctx)
    mod = _ada(cond, w_ada, b_ada).reshape(depth, mod_rows, 1, N_MOD * d)
    cos, sin = _rope_tables(seq)
    log_gamma = jax.nn.log_sigmoid(ret_decay.astype(F32))

    mm = lambda a: a.astype(MM_DTYPE)
    pad_router = lambda a: jnp.pad(a, [(0, 0)] * (a.ndim - 1) + [(0, LANES - a.shape[-1])])
    for l in range(depth):
        last = l == depth - 1
        rows = n_lat if last else n
        mod_l = mod[l]
        p = _inproj(xs, norm1[l][None], mod_l, mm(w_in[l]), tm=tm_in, tn=1024, n_lat=n_lat, seq=seq, batch=batch)
        y_ret = _retention_latent(p, log_gamma[l], cos, sin, batch=batch, seq=seq, n_ctx=n_ctx, n_lat=n_lat)
        y_conv, y_pool = _convpool(p, conv_w[l], mm(pool_w[l]), pool_scale[l][None], None,
                                   n_seq=batch, seq=seq, row0=0)
        if not last:
            y_ret = _retention_ctx(p, log_gamma[l], y_ret, batch=batch, n_ctx=n_ctx, n_lat=n_lat)
            y_conv, y_pool = _convpool(p, conv_w[l], mm(pool_w[l]), pool_scale[l][None], (y_conv, y_pool),
                                       n_seq=batch, seq=n_ctx, row0=n_lat // n_ctx)
        w_router = pad_router(jnp.concatenate([w_rg[l], w_re[l]], axis=-1))
        b_router = pad_router(jnp.concatenate([b_rg[l], b_re[l]], axis=-1)[None])
        xs, h_ext = _outproj(xs, p, y_ret, y_conv, y_pool, mod_l, norm2[l][None], mm(w_ret_out[l]), mm(w_conv_out[l]),
                             mm(w_pool_out[l]), mm(w_o[l]), w_router, b_router,
                             tm=tm_out, n_rows=rows, n_lat=n_lat, seq=seq, batch=batch)
        pos, counts = _positions(h_ext, n_rows=rows, tr=tr, tile=tm_moe, d=d)
        pos = pos.reshape(-1)
        n_sorted = rows + N_BUCKETS * tm_moe
        elo, ehi, valid = _tile_tables(counts, tm=tm_moe, n_tiles=n_sorted // tm_moe)
        hs = _scatter_rows(pos, h_ext, n_rows=rows, ts=ts, n_sorted=n_sorted)
        ys = _moe(elo, ehi, valid, hs, mm(w1[l]), mm(w3[l]), mm(w2[l]), tm=tm_moe, d=d)
        xs = _gather_rows(pos, ys, xs, mod_l, final_norm[None], tg=ts, n_rows=rows, n_lat=n_lat, seq=seq,
                          batch=batch, final=last)
    return xs.reshape(batch, seq, d)
```

```python
import functools

import numpy as np
import jax
import jax.numpy as jnp
from jax import lax
from jax.experimental import pallas as pl
from jax.experimental.pallas import tpu as pltpu

F32 = jnp.float32
MM_DTYPE = jnp.bfloat16
ACT_DTYPE = jnp.bfloat16

NORM_EPS = 1e-6
GRID_W = 64
ROPE_BASE = 10000.0
N_MOD = 6

RET_HEADS = 4
RET_DK = 128
RET_DV = 256
RET_QK = RET_HEADS * RET_DK
RET_V = RET_HEADS * RET_DV
RET_CHUNK = 256

CONV_W = 512
POOL_WINDOWS = (2, 4, 8, 16)
POOL_GROUPS = 4
POOL_GDIM = 128
POOL_W = POOL_GROUPS * POOL_GDIM
POOL_HALO = 8

N_GROUPS = 4
EXP_PER_GROUP = 4
N_EXPERTS = N_GROUPS * EXP_PER_GROUP
D_FF = 512
PAIR_LO = (0, 0, 0, 1, 1, 2)
PAIR_HI = (1, 2, 3, 2, 3, 3)
N_PAIRS = len(PAIR_LO)
N_BUCKETS = N_GROUPS * N_PAIRS

LANES = 128
META_TOKEN, META_WLO, META_WHI = 0, 2, 3
ROUTER_LO_LANE = 32

OFF_Q = 0
OFF_K = OFF_Q + RET_QK
OFF_V = OFF_K + RET_QK
OFF_G = OFF_V + RET_V
OFF_CB = OFF_G + RET_V
OFF_CC = OFF_CB + CONV_W
OFF_CX = OFF_CC + CONV_W
OFF_PI = OFF_CX + CONV_W
OFF_GATE = OFF_PI + POOL_W

VMEM_LIMIT = 56 * 1024 * 1024


def _cparams(sem):
    return pltpu.CompilerParams(dimension_semantics=sem, vmem_limit_bytes=VMEM_LIMIT)


def _split_bf16(a):
    hi = a.astype(jnp.bfloat16)
    lo = (a - hi.astype(F32)).astype(jnp.bfloat16)
    return hi, lo


def _dot(a, b):
    return jnp.dot(a, b, preferred_element_type=F32)


def _dot3(a, b):
    ah, al = _split_bf16(a)
    bh, bl = _split_bf16(b)
    return _dot(ah, bh) + _dot(ah, bl) + _dot(al, bh)


def _mod_row(i, tile, n_lat, seq, batch):
    return jnp.where(i < n_lat // tile, (i * tile) // seq, batch)


def _ada_kernel(c_ref, w_ref, b_ref, o_ref):
    cv = c_ref[...]
    s = cv * jax.nn.sigmoid(cv)
    o_ref[0] = _dot3(s, w_ref[0]) + b_ref[0]


def _ada(cond, w_ada, b_ada):
    depth, d, width = w_ada.shape
    rows = cond.shape[0]
    tn = 512
    return pl.pallas_call(
        _ada_kernel,
        grid=(depth, width // tn),
        in_specs=[pl.BlockSpec((rows, d), lambda l, j: (0, 0)),
                  pl.BlockSpec((1, d, tn), lambda l, j: (l, 0, j)),
                  pl.BlockSpec((1, 1, tn), lambda l, j: (l, 0, j))],
        out_specs=pl.BlockSpec((1, rows, tn), lambda l, j: (l, 0, j)),
        out_shape=jax.ShapeDtypeStruct((depth, rows, width), F32),
        compiler_params=_cparams(("parallel", "parallel")),
    )(cond, w_ada, b_ada.reshape(depth, 1, width))


def _rms_mod(x, g, sc, sh):
    ms = jnp.mean(x * x, axis=-1, keepdims=True)
    return x * lax.rsqrt(ms + NORM_EPS) * g * (1.0 + sc) + sh


def _inproj_kernel(*refs, residual):
    if residual:
        x_ref, f_ref, gt_ref, g_ref, sc_ref, sh_ref, w_ref, o_ref, xo_ref, h_scr = refs
    else:
        x_ref, g_ref, sc_ref, sh_ref, w_ref, o_ref, h_scr = refs

    @pl.when(pl.program_id(1) == 0)
    def _():
        x = x_ref[...]
        if residual:
            x = x + gt_ref[0] * f_ref[...]
            xo_ref[...] = x
        h_scr[...] = _rms_mod(x, g_ref[...], sc_ref[0], sh_ref[0]).astype(h_scr.dtype)

    o_ref[...] = _dot(h_scr[...], w_ref[...]).astype(o_ref.dtype)


def _inproj(xs, prev, norm_g, mod_l, w, *, tm, tn, n_lat, seq, batch):
    n, d = xs.shape
    width = w.shape[1]
    row = functools.partial(_mod_row, tile=tm, n_lat=n_lat, seq=seq, batch=batch)
    mod = lambda k: pl.BlockSpec((1, 1, d), lambda i, j: (row(i), 0, k))
    x_spec = pl.BlockSpec((tm, d), lambda i, j: (i, 0))
    p_spec = pl.BlockSpec((tm, tn), lambda i, j: (i, j))
    p_shape = jax.ShapeDtypeStruct((n, width), ACT_DTYPE)
    tail_specs = [pl.BlockSpec((1, d), lambda i, j: (0, 0)), mod(1), mod(0),
                  pl.BlockSpec((d, tn), lambda i, j: (0, j))]
    common = dict(grid=(n // tm, width // tn), scratch_shapes=[pltpu.VMEM((tm, d), MM_DTYPE)],
                  compiler_params=_cparams(("parallel", "arbitrary")))
    if prev is None:
        p = pl.pallas_call(functools.partial(_inproj_kernel, residual=False),
                           in_specs=[x_spec] + tail_specs, out_specs=p_spec, out_shape=p_shape,
                           **common)(xs, norm_g, mod_l, mod_l, w)
        return p, xs
    f, mod_prev = prev
    return pl.pallas_call(functools.partial(_inproj_kernel, residual=True),
                          in_specs=[x_spec, x_spec, mod(5)] + tail_specs,
                          out_specs=[p_spec, x_spec],
                          out_shape=[p_shape, jax.ShapeDtypeStruct((n, d), F32)],
                          input_output_aliases={0: 1},
                          **common)(xs, f, mod_prev, norm_g, mod_l, mod_l, w)


def _ret_kernel(lg_ref, *refs, seq, n_ctx, use_rope):
    if n_ctx:
        q_ref, k_ref, v_ref, g_ref, kc_ref, vc_ref, cos_ref, sin_ref, o_ref, kt_scr, sb_scr = refs
    else:
        q_ref, k_ref, v_ref, g_ref, o_ref, kt_scr = refs
    C = RET_CHUNK
    n_chunk = seq // C
    head = pl.program_id(1)
    lgf = lg_ref[0, head]
    lgb = lg_ref[1, head]

    q = q_ref[...].astype(F32)
    k = k_ref[...].astype(F32)
    if use_rope:
        cos = cos_ref[...]
        sin = sin_ref[...]
        q = q * cos + pltpu.roll(q, RET_DK // 2, 1) * sin
        k = k * cos + pltpu.roll(k, RET_DK // 2, 1) * sin
    q = q * (RET_DK ** -0.5)
    kt_scr[...] = k.T

    ri = lax.broadcasted_iota(jnp.int32, (C, C), 0)
    ci = lax.broadcasted_iota(jnp.int32, (C, C), 1)
    rel = (ri - ci).astype(F32)
    dmask = jnp.where(rel > 0.0, jnp.exp(lgf * jnp.maximum(rel, 0.0)),
                      jnp.where(rel < 0.0, jnp.exp(lgb * jnp.maximum(-rel, 0.0)), 2.0))
    icol = lax.broadcasted_iota(jnp.int32, (C, 1), 0).astype(F32)
    jrow = lax.broadcasted_iota(jnp.int32, (1, C), 1).astype(F32)
    qdec_f = jnp.exp(lgf * (icol + 1.0))
    qdec_b = jnp.exp(lgb * (C - icol))
    kdec_f = jnp.exp(lgf * (C - 1.0 - jrow))
    kdec_b = jnp.exp(lgb * jrow)
    zrow = jnp.zeros((1, RET_DV), F32)
    cdec_f = jnp.exp(zrow + lgf * C)
    cdec_b = jnp.exp(zrow + lgb * C)

    def chunk(ref, n):
        return ref[n * C:(n + 1) * C, :]

    def kt_chunk(n):
        return kt_scr[:, n * C:(n + 1) * C]

    if n_ctx:
        kct = kc_ref[...].astype(F32).T
        vcx = vc_ref[...].astype(MM_DTYPE)
        mrow = lax.broadcasted_iota(jnp.int32, (1, n_ctx), 1).astype(F32)
        s_f = _dot((kct * jnp.exp(lgf * (n_ctx - 1.0 - mrow))).astype(MM_DTYPE), vcx)
        s_b = _dot((kct * jnp.exp(lgb * mrow)).astype(MM_DTYPE), vcx)
        sb_scr[n_chunk - 1] = s_b
        for n in range(n_chunk - 1, 0, -1):
            s_b = s_b * cdec_b + _dot((kt_chunk(n) * kdec_b).astype(MM_DTYPE), chunk(v_ref, n).astype(MM_DTYPE))
            sb_scr[n - 1] = s_b

    for n in range(n_chunk):
        qn = q[n * C:(n + 1) * C, :]
        ktn = kt_chunk(n)
        vn = chunk(v_ref, n).astype(MM_DTYPE)
        s = _dot(qn.astype(MM_DTYPE), ktn.astype(MM_DTYPE)) * dmask
        o = _dot(s.astype(MM_DTYPE), vn)
        if n_ctx:
            qcat = jnp.concatenate([qn * qdec_f, qn * qdec_b], axis=1).astype(MM_DTYPE)
            scat = jnp.concatenate([s_f, sb_scr[n]], axis=0).astype(MM_DTYPE)
            o = o + _dot(qcat, scat)
            if n + 1 < n_chunk:
                s_f = s_f * cdec_f + _dot((ktn * kdec_f).astype(MM_DTYPE), vn)
        mu = jnp.mean(o, axis=-1, keepdims=True)
        oc = o - mu
        yn = oc * lax.rsqrt(jnp.mean(oc * oc, axis=-1, keepdims=True) + NORM_EPS)
        gn = chunk(g_ref, n).astype(F32)
        o_ref[n * C:(n + 1) * C, :] = (yn * (gn * jax.nn.sigmoid(gn))).astype(o_ref.dtype)


def _retention_latent(p, log_gamma, cos, sin, *, batch, seq, n_ctx, n_lat):
    n = p.shape[0]
    cb = n_lat // n_ctx
    kq, kv = OFF_K // RET_DK, OFF_V // RET_DV
    gs = pltpu.PrefetchScalarGridSpec(
        num_scalar_prefetch=1,
        grid=(batch, RET_HEADS),
        in_specs=[pl.BlockSpec((seq, RET_DK), lambda b, h, lg: (b, h)),
                  pl.BlockSpec((seq, RET_DK), lambda b, h, lg: (b, kq + h)),
                  pl.BlockSpec((seq, RET_DV), lambda b, h, lg: (b, kv + h)),
                  pl.BlockSpec((seq, RET_DV), lambda b, h, lg: (b, OFF_G // RET_DV + h)),
                  pl.BlockSpec((n_ctx, RET_DK), lambda b, h, lg: (cb + b, kq + h)),
                  pl.BlockSpec((n_ctx, RET_DV), lambda b, h, lg: (cb + b, kv + h)),
                  pl.BlockSpec((seq, RET_DK), lambda b, h, lg: (0, 0)),
                  pl.BlockSpec((seq, RET_DK), lambda b, h, lg: (0, 0))],
        out_specs=pl.BlockSpec((seq, RET_DV), lambda b, h, lg: (b, h)),
        scratch_shapes=[pltpu.VMEM((RET_DK, seq), F32),
                        pltpu.VMEM((seq // RET_CHUNK, RET_DK, RET_DV), F32)])
    return pl.pallas_call(
        functools.partial(_ret_kernel, seq=seq, n_ctx=n_ctx, use_rope=True),
        grid_spec=gs,
        out_shape=jax.ShapeDtypeStruct((n, RET_V), ACT_DTYPE),
        compiler_params=_cparams(("parallel", "parallel")),
    )(log_gamma, p, p, p, p, p, p, cos, sin)


def _retention_ctx(p, log_gamma, y_ret, *, batch, n_ctx, n_lat):
    cb = n_lat // n_ctx
    kq, kv = OFF_K // RET_DK, OFF_V // RET_DV
    gs = pltpu.PrefetchScalarGridSpec(
        num_scalar_prefetch=1,
        grid=(batch, RET_HEADS),
        in_specs=[pl.BlockSpec((n_ctx, RET_DK), lambda b, h, lg: (cb + b, h)),
                  pl.BlockSpec((n_ctx, RET_DK), lambda b, h, lg: (cb + b, kq + h)),
                  pl.BlockSpec((n_ctx, RET_DV), lambda b, h, lg: (cb + b, kv + h)),
                  pl.BlockSpec((n_ctx, RET_DV), lambda b, h, lg: (cb + b, OFF_G // RET_DV + h)),
                  pl.BlockSpec(memory_space=pl.ANY)],
        out_specs=pl.BlockSpec((n_ctx, RET_DV), lambda b, h, lg: (cb + b, h)),
        scratch_shapes=[pltpu.VMEM((RET_DK, n_ctx), F32)])

    def body(lg_ref, q_ref, k_ref, v_ref, g_ref, alias_ref, o_ref, kt_scr):
        del alias_ref
        _ret_kernel(lg_ref, q_ref, k_ref, v_ref, g_ref, o_ref, kt_scr, seq=n_ctx, n_ctx=0, use_rope=False)

    return pl.pallas_call(
        body,
        grid_spec=gs,
        out_shape=jax.ShapeDtypeStruct(y_ret.shape, y_ret.dtype),
        input_output_aliases={5: 0},
        compiler_params=_cparams(("parallel", "parallel")),
    )(log_gamma, p, p, p, p, y_ret)


def _convpool_kernel(*refs, seq, aliased):
    if aliased:
        cb_ref, cc_ref, cx_ref, pi_ref, cw_ref, pw_ref, ps_ref, _, _, yc_ref, yp_ref = refs
    else:
        cb_ref, cc_ref, cx_ref, pi_ref, cw_ref, pw_ref, ps_ref, yc_ref, yp_ref = refs
    grp = pl.program_id(1)
    t = lax.broadcasted_iota(jnp.int32, (seq, 1), 0)

    u = cc_ref[...].astype(F32) * cx_ref[...].astype(F32)
    u_prev = jnp.where(t == 0, 0.0, pltpu.roll(u, 1, 0))
    u_next = jnp.where(t == seq - 1, 0.0, pltpu.roll(u, seq - 1, 0))
    w = cw_ref[...]
    conv = w[0:1, :] * u_prev + w[1:2, :] * u + w[2:3, :] * u_next
    yc_ref[...] = (cb_ref[...].astype(F32) * conv).astype(yc_ref.dtype)

    p = pi_ref[...].astype(F32)
    halo = jnp.zeros((POOL_HALO, p.shape[1]), F32)
    ext = jnp.concatenate([halo, p, halo], axis=0)
    length = seq + 2 * POOL_HALO

    def at(a, k):
        return pltpu.roll(a, (-k) % length, 0)

    a2 = at(ext, -1) + ext
    a4 = at(a2, -1) + at(a2, 1)
    a8 = at(a4, -2) + at(a4, 2)
    a16 = at(a8, -4) + at(a8, 4)
    wsum = jnp.where(grp == 0, a2, jnp.where(grp == 1, a4, jnp.where(grp == 2, a8, a16)))
    wsum = wsum[POOL_HALO:POOL_HALO + seq, :]
    half = jnp.left_shift(1, grp)
    lo = jnp.clip(t - half, 0, seq)
    hi = jnp.clip(t + half, 0, seq)
    cnt = (hi - lo).astype(F32)
    pooled = wsum / cnt - p
    mixed = _dot(pooled.astype(MM_DTYPE), pw_ref[0]) * ps_ref[...]
    yp_ref[...] = mixed.astype(yp_ref.dtype)


def _convpool(p, conv_w, pool_w, pool_scale, prev, *, n_seq, seq, row0):
    n = p.shape[0]
    g128 = lambda off: off // POOL_GDIM
    col = lambda off: (lambda b, g: (row0 + b, g128(off) + g))
    in_specs = [pl.BlockSpec((seq, POOL_GDIM), col(OFF_CB)),
                pl.BlockSpec((seq, POOL_GDIM), col(OFF_CC)),
                pl.BlockSpec((seq, POOL_GDIM), col(OFF_CX)),
                pl.BlockSpec((seq, POOL_GDIM), col(OFF_PI)),
                pl.BlockSpec((conv_w.shape[0], POOL_GDIM), lambda b, g: (0, g)),
                pl.BlockSpec((1, POOL_GDIM, POOL_GDIM), lambda b, g: (g, 0, 0)),
                pl.BlockSpec((1, POOL_GDIM), lambda b, g: (0, g))]
    args = [p, p, p, p, conv_w, pool_w, pool_scale]
    aliases = {}
    if prev is not None:
        in_specs += [pl.BlockSpec(memory_space=pl.ANY), pl.BlockSpec(memory_space=pl.ANY)]
        args += list(prev)
        aliases = {7: 0, 8: 1}
    out_spec = pl.BlockSpec((seq, POOL_GDIM), lambda b, g: (row0 + b, g))
    return pl.pallas_call(
        functools.partial(_convpool_kernel, seq=seq, aliased=prev is not None),
        grid=(n_seq, POOL_GROUPS),
        in_specs=in_specs,
        out_specs=[out_spec, out_spec],
        out_shape=[jax.ShapeDtypeStruct((n, CONV_W), ACT_DTYPE), jax.ShapeDtypeStruct((n, POOL_W), ACT_DTYPE)],
        input_output_aliases=aliases,
        compiler_params=_cparams(("parallel", "parallel")),
    )(*args)


def _route(logits):
    lane = lax.broadcasted_iota(jnp.int32, logits.shape, 1)
    neg = -jnp.inf
    is_g = lane < N_GROUPS
    gl = jnp.where(is_g, logits, neg)
    gmax = jnp.max(gl, axis=-1, keepdims=True)
    gidx = jnp.min(jnp.where(gl == gmax, lane, LANES), axis=-1, keepdims=True)
    gtop = 1.0 / jnp.sum(jnp.where(is_g, jnp.exp(gl - gmax), 0.0), axis=-1, keepdims=True)
    base = N_GROUPS + EXP_PER_GROUP * gidx
    el = jnp.where((lane >= base) & (lane < base + EXP_PER_GROUP), logits, neg)
    m1 = jnp.max(el, axis=-1, keepdims=True)
    i1 = jnp.min(jnp.where(el == m1, lane, LANES), axis=-1, keepdims=True)
    el2 = jnp.where(lane == i1, neg, el)
    m2 = jnp.max(el2, axis=-1, keepdims=True)
    i2 = jnp.min(jnp.where(el2 == m2, lane, LANES), axis=-1, keepdims=True)
    e2 = jnp.exp(m2 - m1)
    w1 = gtop / (1.0 + e2)
    w2 = gtop * e2 / (1.0 + e2)
    first_lower = i1 < i2
    lo = jnp.minimum(i1, i2) - base
    hi = jnp.maximum(i1, i2) - base
    pair = lo * 3 - jnp.right_shift(lo * (lo - 1), 1) + hi - lo - 1
    bucket = gidx * N_PAIRS + pair
    wlo = jnp.where(first_lower, w1, w2)
    whi = jnp.where(first_lower, w2, w1)
    return bucket, wlo, whi


def _column_to_lanes(col):
    return jnp.broadcast_to(col, (col.shape[0], LANES)).T[0:8, :]


def _outproj_kernel(x_ref, yr_ref, yc_ref, yp_ref, g0_ref, g1_ref, g2_ref, gt_ref, sc_ref, sh_ref, ng_ref,
                    wr_ref, wc_ref, wp_ref, wo_ref, wrt_ref, brt_ref,
                    xo_ref, hs_ref, cnt_ref,
                    h_scr, pos_vmem, pos_smem, carry, row_sem, pos_sem, *, cap):
    i = pl.program_id(0)
    n_steps = pl.num_programs(0)
    slot = lax.rem(i, 2)
    tm, d = x_ref.shape

    def wait_rows(s):
        pltpu.make_async_copy(h_scr.at[s], hs_ref.at[pl.ds(0, tm)], row_sem.at[s]).wait()

    @pl.when(i == 0)
    def _():
        carry[...] = jnp.zeros_like(carry)

    @pl.when(i >= 2)
    def _():
        wait_rows(slot)

    sig = lambda r: jax.nn.sigmoid(r[...].astype(F32))
    merged = (sig(g0_ref) * _dot(yr_ref[...], wr_ref[...])
              + sig(g1_ref) * _dot(yc_ref[...], wc_ref[...])
              + sig(g2_ref) * _dot(yp_ref[...], wp_ref[...]))
    y = _dot(merged.astype(MM_DTYPE), wo_ref[...])
    x = x_ref[...] + gt_ref[0] * y
    xo_ref[...] = x
    h = _rms_mod(x, ng_ref[...], sc_ref[0], sh_ref[0])

    hh, hl = _split_bf16(h)
    s2 = _dot(hh, wrt_ref[...]) + _dot(hl, wrt_ref[...])
    logits = s2 + pltpu.roll(s2, LANES - ROUTER_LO_LANE, 1) + brt_ref[...]
    bucket, wlo, whi = _route(logits)

    lane = lax.broadcasted_iota(jnp.int32, (tm, LANES), 1)
    onehot = lane == bucket
    ones = jnp.where(onehot, 1.0, 0.0)
    r = lax.broadcasted_iota(jnp.int32, (tm, tm), 0)
    c = lax.broadcasted_iota(jnp.int32, (tm, tm), 1)
    tri = jnp.where(r >= c, 1.0, 0.0).astype(jnp.bfloat16)
    incl = _dot(tri, ones.astype(jnp.bfloat16))
    rank = jnp.sum(jnp.where(onehot, incl - 1.0 + carry[...], 0.0), axis=-1, keepdims=True)
    carry[...] = carry[...] + jnp.sum(ones, axis=0, keepdims=True)
    pos = bucket.astype(F32) * float(cap) + rank
    token = (i * tm + lax.broadcasted_iota(jnp.int32, (tm, 1), 0)).astype(F32)

    h_scr[slot, :, :d] = h
    h_scr[slot, :, d:] = jnp.where(lane == META_TOKEN, token,
                                   jnp.where(lane == META_WLO, wlo, jnp.where(lane == META_WHI, whi, 0.0)))
    pos_vmem[...] = _column_to_lanes(pos).astype(jnp.int32)
    to_smem = pltpu.make_async_copy(pos_vmem.at[pl.ds(0, 1)], pos_smem, pos_sem)
    to_smem.start()
    to_smem.wait()

    def send(rr, carry_):
        pltpu.make_async_copy(h_scr.at[slot, pl.ds(rr, 1)], hs_ref.at[pl.ds(pos_smem[0, rr], 1)],
                              row_sem.at[slot]).start()
        return carry_

    lax.fori_loop(0, tm, send, 0)

    @pl.when(i == n_steps - 1)
    def _():
        cnt_ref[...] = jnp.broadcast_to(carry[...].astype(jnp.int32), cnt_ref.shape)
        wait_rows(slot)

    @pl.when((i == n_steps - 1) & (i >= 1))
    def _():
        wait_rows(1 - slot)


def _outproj(xs, p, y_ret, y_conv, y_pool, mod_l, norm_g, w_ret, w_conv, w_pool, w_o, w_router, b_router,
             *, tm, n_rows, n_lat, seq, batch):
    n, d = xs.shape
    cap = n_rows
    assert N_BUCKETS * cap < 2 ** 24
    width = d + LANES
    row = functools.partial(_mod_row, tile=tm, n_lat=n_lat, seq=seq, batch=batch)
    gate = lambda k: pl.BlockSpec((tm, d), lambda i: (i, OFF_GATE // d + k))
    mod = lambda k: pl.BlockSpec((1, 1, d), lambda i: (row(i), 0, k))
    full = lambda a: pl.BlockSpec(a.shape, lambda i: (0,) * a.ndim)
    return pl.pallas_call(
        functools.partial(_outproj_kernel, cap=cap),
        grid=(n_rows // tm,),
        in_specs=[pl.BlockSpec((tm, d), lambda i: (i, 0)),
                  pl.BlockSpec((tm, RET_V), lambda i: (i, 0)),
                  pl.BlockSpec((tm, CONV_W), lambda i: (i, 0)),
                  pl.BlockSpec((tm, POOL_W), lambda i: (i, 0)),
                  gate(0), gate(1), gate(2),
                  mod(2), mod(4), mod(3),
                  full(norm_g), full(w_ret), full(w_conv), full(w_pool), full(w_o), full(w_router), full(b_router)],
        out_specs=[pl.BlockSpec((tm, d), lambda i: (i, 0)),
                   pl.BlockSpec(memory_space=pl.ANY),
                   pl.BlockSpec((8, LANES), lambda i: (0, 0))],
        out_shape=[jax.ShapeDtypeStruct((n, d), F32),
                   jax.ShapeDtypeStruct((N_BUCKETS * cap, width), F32),
                   jax.ShapeDtypeStruct((8, LANES), jnp.int32)],
        scratch_shapes=[pltpu.VMEM((2, tm, width), F32),
                        pltpu.VMEM((8, tm), jnp.int32),
                        pltpu.SMEM((1, tm), jnp.int32),
                        pltpu.VMEM((1, LANES), F32),
                        pltpu.SemaphoreType.DMA((2,)),
                        pltpu.SemaphoreType.DMA(())],
        input_output_aliases={0: 0},
        compiler_params=_cparams(("arbitrary",)),
    )(xs, y_ret, y_conv, y_pool, p, p, p, mod_l, mod_l, mod_l, norm_g, w_ret, w_conv, w_pool, w_o, w_router, b_router)


def _moe_kernel(blk_ref, elo_ref, ehi_ref, valid_ref, hs_ref, w1l_ref, w1h_ref, w3l_ref, w3h_ref, w2l_ref, w2h_ref,
                ys_ref, y_scr, tok_vmem, tok_smem, row_sem, tok_sem):
    del blk_ref
    w = pl.program_id(0)
    n_steps = pl.num_programs(0)
    slot = lax.rem(w, 2)
    valid = valid_ref[w]
    tm = hs_ref.shape[0]
    d = ys_ref.shape[1]

    def wait_rows(s, count):
        bit = 1
        while bit <= tm:
            @pl.when((count & bit) != 0)
            def _(bit=bit):
                pltpu.make_async_copy(y_scr.at[s, pl.ds(0, bit)], ys_ref.at[pl.ds(0, bit)], row_sem.at[s]).wait()
            bit *= 2

    @pl.when(w >= 2)
    def _():
        wait_rows(slot, valid_ref[jnp.maximum(w - 2, 0)])

    @pl.when(valid > 0)
    def _():
        keep = lax.broadcasted_iota(jnp.int32, (tm, 1), 0) < valid
        meta = hs_ref[:, d:]
        token = jnp.where(keep, meta[:, META_TOKEN:META_TOKEN + 1], 0.0)
        tok_vmem[...] = _column_to_lanes(token).astype(jnp.int32)
        to_smem = pltpu.make_async_copy(tok_vmem.at[pl.ds(0, 1)], tok_smem, tok_sem)
        to_smem.start()

        h = jnp.where(keep, hs_ref[:, :d], 0.0).astype(MM_DTYPE)
        wlo = jnp.where(keep, meta[:, META_WLO:META_WLO + 1], 0.0)
        whi = jnp.where(keep, meta[:, META_WHI:META_WHI + 1], 0.0)

        def expert(w1, w3, w2):
            a = _dot(h, w1[0])
            a = a * jax.nn.sigmoid(a) * _dot(h, w3[0])
            return _dot(a.astype(MM_DTYPE), w2[0])

        y_scr[slot] = wlo * expert(w1l_ref, w3l_ref, w2l_ref) + whi * expert(w1h_ref, w3h_ref, w2h_ref)
        to_smem.wait()

        def send(r, carry_):
            pltpu.make_async_copy(y_scr.at[slot, pl.ds(r, 1)], ys_ref.at[pl.ds(tok_smem[0, r], 1)],
                                  row_sem.at[slot]).start()
            return carry_

        lax.fori_loop(0, valid, send, 0)

    @pl.when(w == n_steps - 1)
    def _():
        wait_rows(slot, valid)

    @pl.when((w == n_steps - 1) & (w >= 1))
    def _():
        wait_rows(1 - slot, valid_ref[jnp.maximum(w - 1, 0)])


def _moe(blk, elo, ehi, valid, hs, w1, w3, w2, *, tm, d, n_rows):
    width = hs.shape[1]
    n_work = blk.shape[0]
    up = lambda sel: pl.BlockSpec((1, d, D_FF), lambda s, blk, elo, ehi, valid: ((elo, ehi)[sel][s], 0, 0))
    down = lambda sel: pl.BlockSpec((1, D_FF, d), lambda s, blk, elo, ehi, valid: ((elo, ehi)[sel][s], 0, 0))
    gs = pltpu.PrefetchScalarGridSpec(
        num_scalar_prefetch=4,
        grid=(n_work,),
        in_specs=[pl.BlockSpec((tm, width), lambda s, blk, elo, ehi, valid: (blk[s], 0)),
                  up(0), up(1), up(0), up(1), down(0), down(1)],
        out_specs=pl.BlockSpec(memory_space=pl.ANY),
        scratch_shapes=[pltpu.VMEM((2, tm, d), F32),
                        pltpu.VMEM((8, tm), jnp.int32),
                        pltpu.SMEM((1, tm), jnp.int32),
                        pltpu.SemaphoreType.DMA((2,)),
                        pltpu.SemaphoreType.DMA(())])
    return pl.pallas_call(
        _moe_kernel,
        grid_spec=gs,
        out_shape=jax.ShapeDtypeStruct((n_rows, d), F32),
        compiler_params=_cparams(("arbitrary",)),
    )(blk, elo, ehi, valid, hs, w1, w1, w3, w3, w2, w2)


def _final_kernel(x_ref, f_ref, gt_ref, g_ref, o_ref):
    x = x_ref[...] + gt_ref[0] * f_ref[...]
    o_ref[...] = x * lax.rsqrt(jnp.mean(x * x, axis=-1, keepdims=True) + NORM_EPS) * g_ref[...]


def _final(xs, f, mod_l, final_g, *, tm, n_rows, n_lat, seq, batch):
    d = xs.shape[1]
    row = functools.partial(_mod_row, tile=tm, n_lat=n_lat, seq=seq, batch=batch)
    x_spec = pl.BlockSpec((tm, d), lambda i: (i, 0))
    return pl.pallas_call(
        _final_kernel,
        grid=(n_rows // tm,),
        in_specs=[x_spec, x_spec,
                  pl.BlockSpec((1, 1, d), lambda i: (row(i), 0, 5)),
                  pl.BlockSpec((1, d), lambda i: (0, 0))],
        out_specs=x_spec,
        out_shape=jax.ShapeDtypeStruct((n_rows, d), F32),
        compiler_params=_cparams(("parallel",)),
    )(xs, f, mod_l, final_g)


def _rope_tables(seq):
    rows = seq // GRID_W
    row = jnp.repeat(jnp.arange(rows, dtype=F32), GRID_W)
    col = jnp.tile(jnp.arange(GRID_W, dtype=F32), rows)
    n_freq = RET_DK // 4
    inv_freq = ROPE_BASE ** (-jnp.arange(n_freq, dtype=F32) / n_freq)
    ang = jnp.concatenate([row[:, None] * inv_freq[None, :], col[:, None] * inv_freq[None, :]], axis=-1)
    cos, sin = jnp.cos(ang), jnp.sin(ang)
    return jnp.concatenate([cos, cos], axis=-1), jnp.concatenate([-sin, sin], axis=-1)


def _work_tables(counts, *, tm, cap, n_work):
    cnt = counts[0, :N_BUCKETS]
    tiles = (cnt + tm - 1) // tm
    ends = jnp.cumsum(tiles)
    starts = ends - tiles
    item = jnp.arange(n_work, dtype=jnp.int32)
    used = item < ends[-1]
    ref_item = jnp.minimum(item, jnp.maximum(ends[-1] - 1, 0))
    bkt = jnp.minimum(jnp.sum((ref_item[:, None] >= ends[None, :]).astype(jnp.int32), axis=1), N_BUCKETS - 1)
    j = ref_item - starts[bkt]
    valid = jnp.where(used, jnp.clip(cnt[bkt] - j * tm, 0, tm), 0)
    blk = bkt * (cap // tm) + j
    grp, pair = bkt // N_PAIRS, bkt % N_PAIRS
    elo = grp * EXP_PER_GROUP + jnp.asarray(PAIR_LO, jnp.int32)[pair]
    ehi = grp * EXP_PER_GROUP + jnp.asarray(PAIR_HI, jnp.int32)[pair]
    i32 = lambda a: a.astype(jnp.int32)
    return i32(blk), i32(elo), i32(ehi), i32(valid)


def _router_weights(w_rg, b_rg, w_re, b_re):
    w = jnp.concatenate([w_rg, w_re], axis=-1).astype(F32)
    n_out = w.shape[1]
    assert n_out <= ROUTER_LO_LANE and ROUTER_LO_LANE + n_out <= LANES
    hi = w.astype(jnp.bfloat16)
    lo = (w - hi.astype(F32)).astype(jnp.bfloat16)
    packed = jnp.zeros((w.shape[0], LANES), jnp.bfloat16)
    packed = packed.at[:, :n_out].set(hi).at[:, ROUTER_LO_LANE:ROUTER_LO_LANE + n_out].set(lo)
    bias = jnp.zeros((1, LANES), F32).at[0, :n_out].set(jnp.concatenate([b_rg, b_re]).astype(F32))
    return packed, bias


def _pick_tile(n, want, *also):
    t = want
    while n % t or any(a % t for a in also):
        t //= 2
    return t


def kernel(x, c, ctx, c_ctx, w_ada, b_ada, norm1, norm2, w_in, ret_decay, conv_w, pool_w, pool_scale, w_ret_out,
           w_conv_out, w_pool_out, w_o, w_rg, b_rg, w_re, b_re, w1, w3, w2, final_norm):
    batch, seq, d = x.shape
    n_ctx = ctx.shape[1]
    depth = w_ada.shape[0]
    n_lat, n_c = batch * seq, batch * n_ctx
    n = n_lat + n_c
    assert POOL_WINDOWS == (2, 4, 8, 16) and POOL_HALO >= max(POOL_WINDOWS) // 2
    assert seq % RET_CHUNK == 0 and n_ctx == RET_CHUNK and seq % GRID_W == 0 and n_lat % n_ctx == 0

    tm_in = _pick_tile(seq, 1024, n_c)
    tm_out = _pick_tile(seq, 512, n_c)
    tm_moe = 256
    tm_fin = _pick_tile(seq, 512)

    xs = jnp.concatenate([x.reshape(n_lat, d), ctx.reshape(n_c, d)], axis=0)
    mod_rows = -(-(batch + 1) // 8) * 8
    cond = jnp.zeros((mod_rows, d), F32).at[:batch].set(c).at[batch].set(c_ctx)
    mod = _ada(cond, w_ada, b_ada).reshape(depth, mod_rows, 1, N_MOD * d)
    cos, sin = _rope_tables(seq)
    log_gamma = jax.nn.log_sigmoid(ret_decay.astype(F32))

    mm = lambda a: a.astype(MM_DTYPE)
    prev = None
    for l in range(depth):
        last = l == depth - 1
        rows = n_lat if last else n
        mod_l = mod[l]
        p, xs = _inproj(xs, prev, norm1[l][None], mod_l, mm(w_in[l]), tm=tm_in, tn=1024, n_lat=n_lat, seq=seq,
                        batch=batch)
        y_ret = _retention_latent(p, log_gamma[l], cos, sin, batch=batch, seq=seq, n_ctx=n_ctx, n_lat=n_lat)
        y_conv, y_pool = _convpool(p, conv_w[l], mm(pool_w[l]), pool_scale[l][None], None,
                                   n_seq=batch, seq=seq, row0=0)
        if not last:
            y_ret = _retention_ctx(p, log_gamma[l], y_ret, batch=batch, n_ctx=n_ctx, n_lat=n_lat)
            y_conv, y_pool = _convpool(p, conv_w[l], mm(pool_w[l]), pool_scale[l][None], (y_conv, y_pool),
                                       n_seq=batch, seq=n_ctx, row0=n_lat // n_ctx)
        w_router, b_router = _router_weights(w_rg[l], b_rg[l], w_re[l], b_re[l])
        xs, hs, counts = _outproj(xs, p, y_ret, y_conv, y_pool, mod_l, norm2[l][None], mm(w_ret_out[l]),
                                  mm(w_conv_out[l]), mm(w_pool_out[l]), mm(w_o[l]), w_router, b_router,
                                  tm=tm_out, n_rows=rows, n_lat=n_lat, seq=seq, batch=batch)
        blk, elo, ehi, valid = _work_tables(counts, tm=tm_moe, cap=rows, n_work=rows // tm_moe + N_BUCKETS)
        ys = _moe(blk, elo, ehi, valid, hs, mm(w1[l]), mm(w3[l]), mm(w2[l]), tm=tm_moe, d=d, n_rows=rows)
        prev = (ys, mod_l)
    out = _final(xs, prev[0], prev[1], final_norm[None], tm=tm_fin, n_rows=n_lat, n_lat=n_lat, seq=seq, batch=batch)
    return out.reshape(batch, seq, d)
```

```python
import functools

import numpy as np
import jax
import jax.numpy as jnp
from jax import lax
from jax.experimental import pallas as pl
from jax.experimental.pallas import tpu as pltpu

F32 = jnp.float32
MM_DTYPE = jnp.bfloat16
ACT_DTYPE = jnp.bfloat16

NORM_EPS = 1e-6
GRID_W = 64
ROPE_BASE = 10000.0
N_MOD = 6

RET_HEADS = 4
RET_DK = 128
RET_DV = 256
RET_QK = RET_HEADS * RET_DK
RET_V = RET_HEADS * RET_DV
RET_CHUNK = 256

CONV_W = 512
POOL_WINDOWS = (2, 4, 8, 16)
POOL_GROUPS = 4
POOL_GDIM = 128
POOL_W = POOL_GROUPS * POOL_GDIM
POOL_HALO = 8

N_GROUPS = 4
EXP_PER_GROUP = 4
N_EXPERTS = N_GROUPS * EXP_PER_GROUP
D_FF = 512
PAIR_LO = (0, 0, 0, 1, 1, 2)
PAIR_HI = (1, 2, 3, 2, 3, 3)
N_PAIRS = len(PAIR_LO)
N_BUCKETS = N_GROUPS * N_PAIRS

LANES = 128
N_PIECES = 3
ROW_ALIGN = 8
ALIGN_SHIFT = 3
TAB_COUNT, TAB_LOCAL, TAB_GLOBAL, TAB_TOTAL, TAB_ROWS = 0, 1, 2, 3, 4
ROUTER_LO_LANE = 32

OFF_Q = 0
OFF_K = OFF_Q + RET_QK
OFF_V = OFF_K + RET_QK
OFF_G = OFF_V + RET_V
OFF_CB = OFF_G + RET_V
OFF_CC = OFF_CB + CONV_W
OFF_CX = OFF_CC + CONV_W
OFF_PI = OFF_CX + CONV_W
OFF_GATE = OFF_PI + POOL_W

VMEM_LIMIT = 56 * 1024 * 1024


def _cparams(sem):
    return pltpu.CompilerParams(dimension_semantics=sem, vmem_limit_bytes=VMEM_LIMIT)


def _split_bf16(a):
    hi = a.astype(jnp.bfloat16)
    lo = (a - hi.astype(F32)).astype(jnp.bfloat16)
    return hi, lo


def _dot(a, b):
    return jnp.dot(a, b, preferred_element_type=F32)


def _dot3(a, b):
    ah, al = _split_bf16(a)
    bh, bl = _split_bf16(b)
    return _dot(ah, bh) + _dot(ah, bl) + _dot(al, bh)


def _mod_row(i, tile, n_lat, seq, batch):
    return jnp.where(i < n_lat // tile, (i * tile) // seq, batch)


def _ada_kernel(c_ref, w_ref, b_ref, o_ref):
    cv = c_ref[...]
    s = cv * jax.nn.sigmoid(cv)
    o_ref[0] = _dot3(s, w_ref[0]) + b_ref[0]


def _ada(cond, w_ada, b_ada):
    depth, d, width = w_ada.shape
    rows = cond.shape[0]
    tn = 512
    return pl.pallas_call(
        _ada_kernel,
        grid=(depth, width // tn),
        in_specs=[pl.BlockSpec((rows, d), lambda l, j: (0, 0)),
                  pl.BlockSpec((1, d, tn), lambda l, j: (l, 0, j)),
                  pl.BlockSpec((1, 1, tn), lambda l, j: (l, 0, j))],
        out_specs=pl.BlockSpec((1, rows, tn), lambda l, j: (l, 0, j)),
        out_shape=jax.ShapeDtypeStruct((depth, rows, width), F32),
        compiler_params=_cparams(("parallel", "parallel")),
    )(cond, w_ada, b_ada.reshape(depth, 1, width))


def _rms_mod(x, g, sc, sh):
    ms = jnp.mean(x * x, axis=-1, keepdims=True)
    return x * lax.rsqrt(ms + NORM_EPS) * g * (1.0 + sc) + sh


def _inproj_kernel(x_ref, g_ref, sc_ref, sh_ref, w_ref, o_ref, h_scr):
    @pl.when(pl.program_id(1) == 0)
    def _():
        h_scr[...] = _rms_mod(x_ref[...], g_ref[...], sc_ref[0], sh_ref[0]).astype(h_scr.dtype)

    o_ref[...] = _dot(h_scr[...], w_ref[...]).astype(o_ref.dtype)


def _inproj_residual_kernel(tab_ref, x_ref, ys_ref, lpos_ref, gt_ref, g_ref, sc_ref, sh_ref, w_ref,
                            o_ref, xo_ref, h_scr, stage, sem, *, n_sub, tile_rows, cap):
    @pl.when(pl.program_id(1) == 0)
    def _():
        f = _expert_residual(tab_ref, ys_ref, lpos_ref, stage, sem, pl.program_id(0) * n_sub,
                             n_sub=n_sub, tile_rows=tile_rows, cap=cap)
        x = x_ref[...] + gt_ref[0] * f
        xo_ref[...] = x
        h_scr[...] = _rms_mod(x, g_ref[...], sc_ref[0], sh_ref[0]).astype(h_scr.dtype)

    o_ref[...] = _dot(h_scr[...], w_ref[...]).astype(o_ref.dtype)


def _inproj(xs, prev, norm_g, mod_l, w, *, tm, tn, tile_rows, n_lat, seq, batch):
    n, d = xs.shape
    width = w.shape[1]
    row = functools.partial(_mod_row, tile=tm, n_lat=n_lat, seq=seq, batch=batch)
    mod = lambda k: pl.BlockSpec((1, 1, d), lambda i, j, *_: (row(i), 0, k))
    x_spec = pl.BlockSpec((tm, d), lambda i, j, *_: (i, 0))
    p_spec = pl.BlockSpec((tm, tn), lambda i, j, *_: (i, j))
    p_shape = jax.ShapeDtypeStruct((n, width), ACT_DTYPE)
    tail_specs = [pl.BlockSpec((1, d), lambda i, j, *_: (0, 0)), mod(1), mod(0),
                  pl.BlockSpec((d, tn), lambda i, j, *_: (0, j))]
    grid = (n // tm, width // tn)
    h_scratch = pltpu.VMEM((tm, d), MM_DTYPE)
    if prev is None:
        p = pl.pallas_call(_inproj_kernel, grid=grid, in_specs=[x_spec] + tail_specs, out_specs=p_spec,
                           out_shape=p_shape, scratch_shapes=[h_scratch],
                           compiler_params=_cparams(("parallel", "arbitrary")))(xs, norm_g, mod_l, mod_l, w)
        return p, xs
    ys, tab, lpos, mod_prev, cap = prev
    gs = pltpu.PrefetchScalarGridSpec(
        num_scalar_prefetch=1,
        grid=grid,
        in_specs=[x_spec, pl.BlockSpec(memory_space=pl.ANY), pl.BlockSpec((tm, 1), lambda i, j, *_: (i, 0)),
                  mod(5)] + tail_specs,
        out_specs=[p_spec, x_spec],
        scratch_shapes=[h_scratch, pltpu.VMEM((tm // tile_rows * _sorted_rows(tile_rows), d), F32),
                        pltpu.SemaphoreType.DMA(())])
    return pl.pallas_call(
        functools.partial(_inproj_residual_kernel, n_sub=tm // tile_rows, tile_rows=tile_rows, cap=cap),
        grid_spec=gs,
        out_shape=[p_shape, jax.ShapeDtypeStruct((n, d), F32)],
        input_output_aliases={1: 1},
        compiler_params=_cparams(("arbitrary", "arbitrary")),
    )(tab, xs, ys, lpos, mod_prev, norm_g, mod_l, mod_l, w)


def _ret_kernel(lg_ref, *refs, seq, n_ctx, use_rope):
    if n_ctx:
        q_ref, k_ref, v_ref, g_ref, kc_ref, vc_ref, cos_ref, sin_ref, o_ref, kt_scr, sb_scr = refs
    else:
        q_ref, k_ref, v_ref, g_ref, o_ref, kt_scr = refs
    C = RET_CHUNK
    n_chunk = seq // C
    head = pl.program_id(1)
    lgf = lg_ref[0, head]
    lgb = lg_ref[1, head]

    q = q_ref[...].astype(F32)
    k = k_ref[...].astype(F32)
    if use_rope:
        cos = cos_ref[...]
        sin = sin_ref[...]
        q = q * cos + pltpu.roll(q, RET_DK // 2, 1) * sin
        k = k * cos + pltpu.roll(k, RET_DK // 2, 1) * sin
    q = q * (RET_DK ** -0.5)
    kt_scr[...] = k.T

    ri = lax.broadcasted_iota(jnp.int32, (C, C), 0)
    ci = lax.broadcasted_iota(jnp.int32, (C, C), 1)
    rel = (ri - ci).astype(F32)
    dmask = jnp.where(rel > 0.0, jnp.exp(lgf * jnp.maximum(rel, 0.0)),
                      jnp.where(rel < 0.0, jnp.exp(lgb * jnp.maximum(-rel, 0.0)), 2.0))
    icol = lax.broadcasted_iota(jnp.int32, (C, 1), 0).astype(F32)
    jrow = lax.broadcasted_iota(jnp.int32, (1, C), 1).astype(F32)
    qdec_f = jnp.exp(lgf * (icol + 1.0))
    qdec_b = jnp.exp(lgb * (C - icol))
    kdec_f = jnp.exp(lgf * (C - 1.0 - jrow))
    kdec_b = jnp.exp(lgb * jrow)
    zrow = jnp.zeros((1, RET_DV), F32)
    cdec_f = jnp.exp(zrow + lgf * C)
    cdec_b = jnp.exp(zrow + lgb * C)

    def chunk(ref, n):
        return ref[n * C:(n + 1) * C, :]

    def kt_chunk(n):
        return kt_scr[:, n * C:(n + 1) * C]

    if n_ctx:
        kct = kc_ref[...].astype(F32).T
        vcx = vc_ref[...].astype(MM_DTYPE)
        mrow = lax.broadcasted_iota(jnp.int32, (1, n_ctx), 1).astype(F32)
        s_f = _dot((kct * jnp.exp(lgf * (n_ctx - 1.0 - mrow))).astype(MM_DTYPE), vcx)
        s_b = _dot((kct * jnp.exp(lgb * mrow)).astype(MM_DTYPE), vcx)
        sb_scr[n_chunk - 1] = s_b
        for n in range(n_chunk - 1, 0, -1):
            s_b = s_b * cdec_b + _dot((kt_chunk(n) * kdec_b).astype(MM_DTYPE), chunk(v_ref, n).astype(MM_DTYPE))
            sb_scr[n - 1] = s_b

    for n in range(n_chunk):
        qn = q[n * C:(n + 1) * C, :]
        ktn = kt_chunk(n)
        vn = chunk(v_ref, n).astype(MM_DTYPE)
        s = _dot(qn.astype(MM_DTYPE), ktn.astype(MM_DTYPE)) * dmask
        o = _dot(s.astype(MM_DTYPE), vn)
        if n_ctx:
            qcat = jnp.concatenate([qn * qdec_f, qn * qdec_b], axis=1).astype(MM_DTYPE)
            scat = jnp.concatenate([s_f, sb_scr[n]], axis=0).astype(MM_DTYPE)
            o = o + _dot(qcat, scat)
            if n + 1 < n_chunk:
                s_f = s_f * cdec_f + _dot((ktn * kdec_f).astype(MM_DTYPE), vn)
        mu = jnp.mean(o, axis=-1, keepdims=True)
        oc = o - mu
        yn = oc * lax.rsqrt(jnp.mean(oc * oc, axis=-1, keepdims=True) + NORM_EPS)
        gn = chunk(g_ref, n).astype(F32)
        o_ref[n * C:(n + 1) * C, :] = (yn * (gn * jax.nn.sigmoid(gn))).astype(o_ref.dtype)


def _retention_latent(p, log_gamma, cos, sin, *, batch, seq, n_ctx, n_lat):
    n = p.shape[0]
    cb = n_lat // n_ctx
    kq, kv = OFF_K // RET_DK, OFF_V // RET_DV
    gs = pltpu.PrefetchScalarGridSpec(
        num_scalar_prefetch=1,
        grid=(batch, RET_HEADS),
        in_specs=[pl.BlockSpec((seq, RET_DK), lambda b, h, lg: (b, h)),
                  pl.BlockSpec((seq, RET_DK), lambda b, h, lg: (b, kq + h)),
                  pl.BlockSpec((seq, RET_DV), lambda b, h, lg: (b, kv + h)),
                  pl.BlockSpec((seq, RET_DV), lambda b, h, lg: (b, OFF_G // RET_DV + h)),
                  pl.BlockSpec((n_ctx, RET_DK), lambda b, h, lg: (cb + b, kq + h)),
                  pl.BlockSpec((n_ctx, RET_DV), lambda b, h, lg: (cb + b, kv + h)),
                  pl.BlockSpec((seq, RET_DK), lambda b, h, lg: (0, 0)),
                  pl.BlockSpec((seq, RET_DK), lambda b, h, lg: (0, 0))],
        out_specs=pl.BlockSpec((seq, RET_DV), lambda b, h, lg: (b, h)),
        scratch_shapes=[pltpu.VMEM((RET_DK, seq), F32),
                        pltpu.VMEM((seq // RET_CHUNK, RET_DK, RET_DV), F32)])
    return pl.pallas_call(
        functools.partial(_ret_kernel, seq=seq, n_ctx=n_ctx, use_rope=True),
        grid_spec=gs,
        out_shape=jax.ShapeDtypeStruct((n, RET_V), ACT_DTYPE),
        compiler_params=_cparams(("parallel", "parallel")),
    )(log_gamma, p, p, p, p, p, p, cos, sin)


def _retention_ctx(p, log_gamma, y_ret, *, batch, n_ctx, n_lat):
    cb = n_lat // n_ctx
    kq, kv = OFF_K // RET_DK, OFF_V // RET_DV
    gs = pltpu.PrefetchScalarGridSpec(
        num_scalar_prefetch=1,
        grid=(batch, RET_HEADS),
        in_specs=[pl.BlockSpec((n_ctx, RET_DK), lambda b, h, lg: (cb + b, h)),
                  pl.BlockSpec((n_ctx, RET_DK), lambda b, h, lg: (cb + b, kq + h)),
                  pl.BlockSpec((n_ctx, RET_DV), lambda b, h, lg: (cb + b, kv + h)),
                  pl.BlockSpec((n_ctx, RET_DV), lambda b, h, lg: (cb + b, OFF_G // RET_DV + h)),
                  pl.BlockSpec(memory_space=pl.ANY)],
        out_specs=pl.BlockSpec((n_ctx, RET_DV), lambda b, h, lg: (cb + b, h)),
        scratch_shapes=[pltpu.VMEM((RET_DK, n_ctx), F32)])

    def body(lg_ref, q_ref, k_ref, v_ref, g_ref, alias_ref, o_ref, kt_scr):
        del alias_ref
        _ret_kernel(lg_ref, q_ref, k_ref, v_ref, g_ref, o_ref, kt_scr, seq=n_ctx, n_ctx=0, use_rope=False)

    return pl.pallas_call(
        body,
        grid_spec=gs,
        out_shape=jax.ShapeDtypeStruct(y_ret.shape, y_ret.dtype),
        input_output_aliases={5: 0},
        compiler_params=_cparams(("parallel", "parallel")),
    )(log_gamma, p, p, p, p, y_ret)


def _convpool_kernel(*refs, seq, aliased):
    if aliased:
        cb_ref, cc_ref, cx_ref, pi_ref, cw_ref, pw_ref, ps_ref, _, _, yc_ref, yp_ref = refs
    else:
        cb_ref, cc_ref, cx_ref, pi_ref, cw_ref, pw_ref, ps_ref, yc_ref, yp_ref = refs
    grp = pl.program_id(1)
    t = lax.broadcasted_iota(jnp.int32, (seq, 1), 0)

    u = cc_ref[...].astype(F32) * cx_ref[...].astype(F32)
    u_prev = jnp.where(t == 0, 0.0, pltpu.roll(u, 1, 0))
    u_next = jnp.where(t == seq - 1, 0.0, pltpu.roll(u, seq - 1, 0))
    w = cw_ref[...]
    conv = w[0:1, :] * u_prev + w[1:2, :] * u + w[2:3, :] * u_next
    yc_ref[...] = (cb_ref[...].astype(F32) * conv).astype(yc_ref.dtype)

    p = pi_ref[...].astype(F32)
    halo = jnp.zeros((POOL_HALO, p.shape[1]), F32)
    ext = jnp.concatenate([halo, p, halo], axis=0)
    length = seq + 2 * POOL_HALO

    def at(a, k):
        return pltpu.roll(a, (-k) % length, 0)

    a2 = at(ext, -1) + ext
    a4 = at(a2, -1) + at(a2, 1)
    a8 = at(a4, -2) + at(a4, 2)
    a16 = at(a8, -4) + at(a8, 4)
    wsum = jnp.where(grp == 0, a2, jnp.where(grp == 1, a4, jnp.where(grp == 2, a8, a16)))
    wsum = wsum[POOL_HALO:POOL_HALO + seq, :]
    half = jnp.left_shift(1, grp)
    lo = jnp.clip(t - half, 0, seq)
    hi = jnp.clip(t + half, 0, seq)
    cnt = (hi - lo).astype(F32)
    pooled = wsum / cnt - p
    mixed = _dot(pooled.astype(MM_DTYPE), pw_ref[0]) * ps_ref[...]
    yp_ref[...] = mixed.astype(yp_ref.dtype)


def _convpool(p, conv_w, pool_w, pool_scale, prev, *, n_seq, seq, row0):
    n = p.shape[0]
    g128 = lambda off: off // POOL_GDIM
    col = lambda off: (lambda b, g: (row0 + b, g128(off) + g))
    in_specs = [pl.BlockSpec((seq, POOL_GDIM), col(OFF_CB)),
                pl.BlockSpec((seq, POOL_GDIM), col(OFF_CC)),
                pl.BlockSpec((seq, POOL_GDIM), col(OFF_CX)),
                pl.BlockSpec((seq, POOL_GDIM), col(OFF_PI)),
                pl.BlockSpec((conv_w.shape[0], POOL_GDIM), lambda b, g: (0, g)),
                pl.BlockSpec((1, POOL_GDIM, POOL_GDIM), lambda b, g: (g, 0, 0)),
                pl.BlockSpec((1, POOL_GDIM), lambda b, g: (0, g))]
    args = [p, p, p, p, conv_w, pool_w, pool_scale]
    aliases = {}
    if prev is not None:
        in_specs += [pl.BlockSpec(memory_space=pl.ANY), pl.BlockSpec(memory_space=pl.ANY)]
        args += list(prev)
        aliases = {7: 0, 8: 1}
    out_spec = pl.BlockSpec((seq, POOL_GDIM), lambda b, g: (row0 + b, g))
    return pl.pallas_call(
        functools.partial(_convpool_kernel, seq=seq, aliased=prev is not None),
        grid=(n_seq, POOL_GROUPS),
        in_specs=in_specs,
        out_specs=[out_spec, out_spec],
        out_shape=[jax.ShapeDtypeStruct((n, CONV_W), ACT_DTYPE), jax.ShapeDtypeStruct((n, POOL_W), ACT_DTYPE)],
        input_output_aliases=aliases,
        compiler_params=_cparams(("parallel", "parallel")),
    )(*args)


def _route(logits):
    lane = lax.broadcasted_iota(jnp.int32, logits.shape, 1)
    neg = -jnp.inf
    is_g = lane < N_GROUPS
    gl = jnp.where(is_g, logits, neg)
    gmax = jnp.max(gl, axis=-1, keepdims=True)
    gidx = jnp.min(jnp.where(gl == gmax, lane, LANES), axis=-1, keepdims=True)
    gtop = 1.0 / jnp.sum(jnp.where(is_g, jnp.exp(gl - gmax), 0.0), axis=-1, keepdims=True)
    base = N_GROUPS + EXP_PER_GROUP * gidx
    el = jnp.where((lane >= base) & (lane < base + EXP_PER_GROUP), logits, neg)
    m1 = jnp.max(el, axis=-1, keepdims=True)
    i1 = jnp.min(jnp.where(el == m1, lane, LANES), axis=-1, keepdims=True)
    el2 = jnp.where(lane == i1, neg, el)
    m2 = jnp.max(el2, axis=-1, keepdims=True)
    i2 = jnp.min(jnp.where(el2 == m2, lane, LANES), axis=-1, keepdims=True)
    e2 = jnp.exp(m2 - m1)
    w1 = gtop / (1.0 + e2)
    w2 = gtop * e2 / (1.0 + e2)
    first_lower = i1 < i2
    lo = jnp.minimum(i1, i2) - base
    hi = jnp.maximum(i1, i2) - base
    pair = lo * 3 - jnp.right_shift(lo * (lo - 1), 1) + hi - lo - 1
    bucket = gidx * N_PAIRS + pair
    wlo = jnp.where(first_lower, w1, w2)
    whi = jnp.where(first_lower, w2, w1)
    return bucket, wlo, whi


def _column_to_lanes(col):
    return jnp.broadcast_to(col, (col.shape[0], LANES)).T[0:8, :]


def _bucket_run_copies(read_run, max_rows, make_copy):
    def per_bucket(b, carry_):
        n, local, glob = read_run(b)
        off = jnp.int32(0)
        size = 1 << (max_rows.bit_length() - 1)
        while size >= ROW_ALIGN:
            take = n & size

            @pl.when(take != 0)
            def _(size=size, off=off):
                make_copy(pl.multiple_of(local + off, ROW_ALIGN), pl.multiple_of(glob + off, ROW_ALIGN), size).start()

            off = off + take
            size //= 2
        return carry_

    lax.fori_loop(0, N_BUCKETS, per_bucket, 0)


def _wait_rows(total, max_rows, make_copy):
    size = 1 << (max_rows.bit_length() - 1)
    while size >= ROW_ALIGN:
        @pl.when((total & size) != 0)
        def _(size=size):
            make_copy(size).wait()
        size //= 2


def _exact_bf16_pieces(w):
    a = w.astype(jnp.bfloat16).astype(F32)
    b = (w - a).astype(jnp.bfloat16).astype(F32)
    c = (w - a - b).astype(jnp.bfloat16).astype(F32)
    return a, b, c


def _outproj_kernel(x_ref, yr_ref, yc_ref, yp_ref, g0_ref, g1_ref, g2_ref, gt_ref, sc_ref, sh_ref, ng_ref,
                    wr_ref, wc_ref, wp_ref, wo_ref, wrt_ref, brt_ref,
                    xo_ref, hs_ref, cnt_ref, tab_ref, lpos_ref,
                    h_scr, tab_vmem, tab_smem, sent_smem, carry, row_sem, tab_sem, *, cap):
    i = pl.program_id(0)
    n_steps = pl.num_programs(0)
    slot = lax.rem(i, 2)
    tm, d = x_ref.shape
    ts = h_scr.shape[1]

    def wait_rows(s):
        _wait_rows(sent_smem[s], ts, lambda size: pltpu.make_async_copy(
            h_scr.at[s, pl.ds(0, size)], hs_ref.at[pl.ds(0, size)], row_sem.at[s]))

    @pl.when(i == 0)
    def _():
        carry[...] = jnp.zeros_like(carry)

    @pl.when(i >= 2)
    def _():
        wait_rows(slot)

    sig = lambda r: jax.nn.sigmoid(r[...].astype(F32))
    merged = (sig(g0_ref) * _dot(yr_ref[...], wr_ref[...])
              + sig(g1_ref) * _dot(yc_ref[...], wc_ref[...])
              + sig(g2_ref) * _dot(yp_ref[...], wp_ref[...]))
    y = _dot(merged.astype(MM_DTYPE), wo_ref[...])
    x = x_ref[...] + gt_ref[0] * y
    xo_ref[...] = x
    h = _rms_mod(x, ng_ref[...], sc_ref[0], sh_ref[0])

    hh, hl = _split_bf16(h)
    s2 = _dot(hh, wrt_ref[...]) + _dot(hl, wrt_ref[...])
    logits = s2 + pltpu.roll(s2, LANES - ROUTER_LO_LANE, 1) + brt_ref[...]
    bucket, wlo, whi = _route(logits)

    lane = lax.broadcasted_iota(jnp.int32, (tm, LANES), 1)
    onehot = lane == bucket
    ones = jnp.where(onehot, 1.0, 0.0)
    r = lax.broadcasted_iota(jnp.int32, (tm, tm), 0)
    c = lax.broadcasted_iota(jnp.int32, (tm, tm), 1)
    tri = jnp.where(r >= c, 1.0, 0.0).astype(jnp.bfloat16)
    incl = _dot(tri, ones.astype(jnp.bfloat16))
    tile_cnt = jnp.sum(ones, axis=0, keepdims=True).astype(jnp.int32)
    run_len = jnp.left_shift(jnp.right_shift(tile_cnt + (ROW_ALIGN - 1), ALIGN_SHIFT), ALIGN_SHIFT)
    lane1 = lax.broadcasted_iota(jnp.int32, (1, LANES), 1)
    run = run_len
    k = 1
    while k < LANES:
        run = run + jnp.where(lane1 >= k, pltpu.roll(run, k, 1), 0)
        k *= 2
    local_start = run - run_len
    total = run[:, LANES - 1:LANES]
    lpos = jnp.sum(jnp.where(onehot, incl - 1.0 + local_start.astype(F32), 0.0), axis=-1, keepdims=True)
    lpos_ref[...] = lpos.astype(jnp.int32)
    srow = lax.broadcasted_iota(jnp.int32, (8, LANES), 0)
    tab = jnp.where(srow == TAB_COUNT, run_len,
                    jnp.where(srow == TAB_LOCAL, local_start,
                              jnp.where(srow == TAB_GLOBAL, carry[...], jnp.where(srow == TAB_TOTAL, total, 0))))
    carry[...] = carry[...] + run_len
    tab_ref[0] = tab
    tab_vmem[...] = tab
    to_smem = pltpu.make_async_copy(tab_vmem, tab_smem, tab_sem)
    to_smem.start()

    lpos_row = _column_to_lanes(lpos).astype(jnp.int32)[0:1, :]
    srt = lax.broadcasted_iota(jnp.int32, (ts, tm), 0)
    perm = jnp.where(srt == lpos_row, 1.0, 0.0).astype(MM_DTYPE)
    pieces = _exact_bf16_pieces(wlo) + _exact_bf16_pieces(whi)
    meta = jnp.zeros((tm, LANES), F32)
    for k, piece in enumerate(pieces):
        meta = jnp.where(lane == k, piece, meta)
    h_scr[slot, :, :d] = _dot(perm, h.astype(MM_DTYPE))
    h_scr[slot, :, d:] = _dot(perm, meta.astype(MM_DTYPE))
    to_smem.wait()
    sent_smem[slot] = tab_smem[TAB_TOTAL, 0]

    _bucket_run_copies(
        lambda b: (tab_smem[TAB_COUNT, b], tab_smem[TAB_LOCAL, b], b * cap + tab_smem[TAB_GLOBAL, b]), tm,
        lambda local, glob, size: pltpu.make_async_copy(h_scr.at[slot, pl.ds(local, size)],
                                                        hs_ref.at[pl.ds(glob, size)], row_sem.at[slot]))

    @pl.when(i == n_steps - 1)
    def _():
        cnt_ref[...] = jnp.broadcast_to(carry[...].astype(jnp.int32), cnt_ref.shape)
        wait_rows(slot)

    @pl.when((i == n_steps - 1) & (i >= 1))
    def _():
        wait_rows(1 - slot)


def _sorted_rows(tm):
    return tm + N_BUCKETS * ROW_ALIGN


def _outproj(xs, p, y_ret, y_conv, y_pool, mod_l, norm_g, w_ret, w_conv, w_pool, w_o, w_router, b_router,
             *, tm, cap, n_rows, n_lat, seq, batch):
    n, d = xs.shape
    assert cap >= n_rows + (n_rows // tm) * ROW_ALIGN and cap % ROW_ALIGN == 0
    width = d + LANES
    ts = _sorted_rows(tm)
    row = functools.partial(_mod_row, tile=tm, n_lat=n_lat, seq=seq, batch=batch)
    gate = lambda k: pl.BlockSpec((tm, d), lambda i: (i, OFF_GATE // d + k))
    mod = lambda k: pl.BlockSpec((1, 1, d), lambda i: (row(i), 0, k))
    full = lambda a: pl.BlockSpec(a.shape, lambda i: (0,) * a.ndim)
    return pl.pallas_call(
        functools.partial(_outproj_kernel, cap=cap),
        grid=(n_rows // tm,),
        in_specs=[pl.BlockSpec((tm, d), lambda i: (i, 0)),
                  pl.BlockSpec((tm, RET_V), lambda i: (i, 0)),
                  pl.BlockSpec((tm, CONV_W), lambda i: (i, 0)),
                  pl.BlockSpec((tm, POOL_W), lambda i: (i, 0)),
                  gate(0), gate(1), gate(2),
                  mod(2), mod(4), mod(3),
                  full(norm_g), full(w_ret), full(w_conv), full(w_pool), full(w_o), full(w_router), full(b_router)],
        out_specs=[pl.BlockSpec((tm, d), lambda i: (i, 0)),
                   pl.BlockSpec(memory_space=pl.ANY),
                   pl.BlockSpec((8, LANES), lambda i: (0, 0)),
                   pl.BlockSpec((1, 8, LANES), lambda i: (i, 0, 0)),
                   pl.BlockSpec((tm, 1), lambda i: (i, 0))],
        out_shape=[jax.ShapeDtypeStruct((n, d), F32),
                   jax.ShapeDtypeStruct((N_BUCKETS * cap, width), F32),
                   jax.ShapeDtypeStruct((8, LANES), jnp.int32),
                   jax.ShapeDtypeStruct((n_rows // tm, 8, LANES), jnp.int32),
                   jax.ShapeDtypeStruct((n_rows, 1), jnp.int32)],
        scratch_shapes=[pltpu.VMEM((2, ts, width), F32),
                        pltpu.VMEM((8, LANES), jnp.int32),
                        pltpu.SMEM((8, LANES), jnp.int32),
                        pltpu.SMEM((2,), jnp.int32),
                        pltpu.VMEM((1, LANES), jnp.int32),
                        pltpu.SemaphoreType.DMA((2,)),
                        pltpu.SemaphoreType.DMA(())],
        input_output_aliases={0: 0},
        compiler_params=_cparams(("arbitrary",)),
    )(xs, y_ret, y_conv, y_pool, p, p, p, mod_l, mod_l, mod_l, norm_g, w_ret, w_conv, w_pool, w_o, w_router, b_router)


def _moe_kernel(blk_ref, elo_ref, ehi_ref, valid_ref, hs_ref, w1l_ref, w1h_ref, w3l_ref, w3h_ref, w2l_ref, w2h_ref,
                ys_ref):
    del blk_ref, elo_ref, ehi_ref
    valid = valid_ref[pl.program_id(0)]
    tm, d = ys_ref.shape

    @pl.when(valid > 0)
    def _():
        keep = lax.broadcasted_iota(jnp.int32, (tm, 1), 0) < valid
        meta = jnp.where(keep, hs_ref[:, d:], 0.0)
        h = jnp.where(keep, hs_ref[:, :d], 0.0).astype(MM_DTYPE)
        wlo = meta[:, 0:1] + meta[:, 1:2] + meta[:, 2:3]
        whi = meta[:, 3:4] + meta[:, 4:5] + meta[:, 5:6]

        def expert(w1, w3, w2):
            a = _dot(h, w1[0])
            a = a * jax.nn.sigmoid(a) * _dot(h, w3[0])
            return _dot(a.astype(MM_DTYPE), w2[0])

        y = wlo * expert(w1l_ref, w3l_ref, w2l_ref) + whi * expert(w1h_ref, w3h_ref, w2h_ref)
        ys_ref[...] = y.astype(ACT_DTYPE).astype(F32)


def _moe(blk, elo, ehi, valid, hs, w1, w3, w2, *, tm, d):
    n_sorted, width = hs.shape
    n_work = blk.shape[0]
    up = lambda sel: pl.BlockSpec((1, d, D_FF), lambda s, blk, elo, ehi, valid: ((elo, ehi)[sel][s], 0, 0))
    down = lambda sel: pl.BlockSpec((1, D_FF, d), lambda s, blk, elo, ehi, valid: ((elo, ehi)[sel][s], 0, 0))
    gs = pltpu.PrefetchScalarGridSpec(
        num_scalar_prefetch=4,
        grid=(n_work,),
        in_specs=[pl.BlockSpec((tm, width), lambda s, blk, elo, ehi, valid: (blk[s], 0)),
                  up(0), up(1), up(0), up(1), down(0), down(1)],
        out_specs=pl.BlockSpec((tm, d), lambda s, blk, elo, ehi, valid: (blk[s], 0)))
    return pl.pallas_call(
        _moe_kernel,
        grid_spec=gs,
        out_shape=jax.ShapeDtypeStruct((n_sorted, d), F32),
        compiler_params=_cparams(("arbitrary",)),
    )(blk, elo, ehi, valid, hs, w1, w1, w3, w3, w2, w2)


def _expert_residual(tab_ref, ys_ref, lpos_ref, stage, sem, first_tile, *, n_sub, tile_rows, cap):
    ts = _sorted_rows(tile_rows)
    entry = lambda s, row, lane: tab_ref[(first_tile + s) * (TAB_ROWS * LANES) + row * LANES + lane]
    for s in range(n_sub):
        _bucket_run_copies(
            lambda b, s=s: (entry(s, TAB_COUNT, b), entry(s, TAB_LOCAL, b), b * cap + entry(s, TAB_GLOBAL, b)),
            tile_rows,
            lambda local, glob, size, s=s: pltpu.make_async_copy(
                ys_ref.at[pl.ds(glob, size)], stage.at[pl.ds(s * ts + local, size)], sem))
    for s in range(n_sub):
        _wait_rows(entry(s, TAB_TOTAL, 0), ts, lambda size, s=s: pltpu.make_async_copy(
            ys_ref.at[pl.ds(0, size)], stage.at[pl.ds(s * ts, size)], sem))
    parts = []
    for s in range(n_sub):
        lpos = lpos_ref[s * tile_rows:(s + 1) * tile_rows, :]
        unperm = jnp.where(lpos == lax.broadcasted_iota(jnp.int32, (tile_rows, ts), 1), 1.0, 0.0).astype(MM_DTYPE)
        filled = lax.broadcasted_iota(jnp.int32, (ts, 1), 0) < entry(s, TAB_TOTAL, 0)
        rows = jnp.where(filled, stage[s * ts:(s + 1) * ts, :], 0.0)
        parts.append(_dot(unperm, rows.astype(MM_DTYPE)))
    return parts[0] if n_sub == 1 else jnp.concatenate(parts, axis=0)


def _final_kernel(tab_ref, x_ref, ys_ref, lpos_ref, gt_ref, g_ref, o_ref, stage, sem, *, cap):
    tm = x_ref.shape[0]
    f = _expert_residual(tab_ref, ys_ref, lpos_ref, stage, sem, pl.program_id(0), n_sub=1, tile_rows=tm, cap=cap)
    x = x_ref[...] + gt_ref[0] * f
    o_ref[...] = x * lax.rsqrt(jnp.mean(x * x, axis=-1, keepdims=True) + NORM_EPS) * g_ref[...]


def _final(xs, moe_out, final_g, *, tm, n_rows, n_lat, seq, batch):
    ys, tab, lpos, mod_l, cap = moe_out
    d = xs.shape[1]
    row = functools.partial(_mod_row, tile=tm, n_lat=n_lat, seq=seq, batch=batch)
    x_spec = pl.BlockSpec((tm, d), lambda i, tab: (i, 0))
    gs = pltpu.PrefetchScalarGridSpec(
        num_scalar_prefetch=1,
        grid=(n_rows // tm,),
        in_specs=[x_spec,
                  pl.BlockSpec(memory_space=pl.ANY),
                  pl.BlockSpec((tm, 1), lambda i, tab: (i, 0)),
                  pl.BlockSpec((1, 1, d), lambda i, tab: (row(i), 0, 5)),
                  pl.BlockSpec((1, d), lambda i, tab: (0, 0))],
        out_specs=x_spec,
        scratch_shapes=[pltpu.VMEM((_sorted_rows(tm), d), F32), pltpu.SemaphoreType.DMA(())])
    return pl.pallas_call(
        functools.partial(_final_kernel, cap=cap),
        grid_spec=gs,
        out_shape=jax.ShapeDtypeStruct((n_rows, d), F32),
        compiler_params=_cparams(("arbitrary",)),
    )(tab, xs, ys, lpos, mod_l, final_g)


def _rope_tables(seq):
    rows = seq // GRID_W
    row = jnp.repeat(jnp.arange(rows, dtype=F32), GRID_W)
    col = jnp.tile(jnp.arange(GRID_W, dtype=F32), rows)
    n_freq = RET_DK // 4
    inv_freq = ROPE_BASE ** (-jnp.arange(n_freq, dtype=F32) / n_freq)
    ang = jnp.concatenate([row[:, None] * inv_freq[None, :], col[:, None] * inv_freq[None, :]], axis=-1)
    cos, sin = jnp.cos(ang), jnp.sin(ang)
    return jnp.concatenate([cos, cos], axis=-1), jnp.concatenate([-sin, sin], axis=-1)


def _work_tables(counts, *, tm, cap, n_work):
    cnt = counts[0, :N_BUCKETS]
    tiles = (cnt + tm - 1) // tm
    ends = jnp.cumsum(tiles)
    starts = ends - tiles
    item = jnp.arange(n_work, dtype=jnp.int32)
    used = item < ends[-1]
    ref_item = jnp.minimum(item, jnp.maximum(ends[-1] - 1, 0))
    bkt = jnp.minimum(jnp.sum((ref_item[:, None] >= ends[None, :]).astype(jnp.int32), axis=1), N_BUCKETS - 1)
    j = ref_item - starts[bkt]
    valid = jnp.where(used, jnp.clip(cnt[bkt] - j * tm, 0, tm), 0)
    blk = bkt * (cap // tm) + j
    grp, pair = bkt // N_PAIRS, bkt % N_PAIRS
    elo = grp * EXP_PER_GROUP + jnp.asarray(PAIR_LO, jnp.int32)[pair]
    ehi = grp * EXP_PER_GROUP + jnp.asarray(PAIR_HI, jnp.int32)[pair]
    i32 = lambda a: a.astype(jnp.int32)
    return i32(blk), i32(elo), i32(ehi), i32(valid)


def _router_weights(w_rg, b_rg, w_re, b_re):
    w = jnp.concatenate([w_rg, w_re], axis=-1).astype(F32)
    n_out = w.shape[1]
    assert n_out <= ROUTER_LO_LANE and ROUTER_LO_LANE + n_out <= LANES
    hi = w.astype(jnp.bfloat16)
    lo = (w - hi.astype(F32)).astype(jnp.bfloat16)
    packed = jnp.zeros((w.shape[0], LANES), jnp.bfloat16)
    packed = packed.at[:, :n_out].set(hi).at[:, ROUTER_LO_LANE:ROUTER_LO_LANE + n_out].set(lo)
    bias = jnp.zeros((1, LANES), F32).at[0, :n_out].set(jnp.concatenate([b_rg, b_re]).astype(F32))
    return packed, bias


def _pick_tile(n, want, *also):
    t = want
    while n % t or any(a % t for a in also):
        t //= 2
    return t


def kernel(x, c, ctx, c_ctx, w_ada, b_ada, norm1, norm2, w_in, ret_decay, conv_w, pool_w, pool_scale, w_ret_out,
           w_conv_out, w_pool_out, w_o, w_rg, b_rg, w_re, b_re, w1, w3, w2, final_norm):
    batch, seq, d = x.shape
    n_ctx = ctx.shape[1]
    depth = w_ada.shape[0]
    n_lat, n_c = batch * seq, batch * n_ctx
    n = n_lat + n_c
    assert POOL_WINDOWS == (2, 4, 8, 16) and POOL_HALO >= max(POOL_WINDOWS) // 2
    assert seq % RET_CHUNK == 0 and n_ctx == RET_CHUNK and seq % GRID_W == 0 and n_lat % n_ctx == 0

    tm_in = _pick_tile(seq, 1024, n_c)
    tm_out = _pick_tile(seq, 512, n_c)
    tm_moe = 256
    assert tm_in % tm_out == 0

    xs = jnp.concatenate([x.reshape(n_lat, d), ctx.reshape(n_c, d)], axis=0)
    mod_rows = -(-(batch + 1) // 8) * 8
    cond = jnp.zeros((mod_rows, d), F32).at[:batch].set(c).at[batch].set(c_ctx)
    mod = _ada(cond, w_ada, b_ada).reshape(depth, mod_rows, 1, N_MOD * d)
    cos, sin = _rope_tables(seq)
    log_gamma = jax.nn.log_sigmoid(ret_decay.astype(F32))

    mm = lambda a: a.astype(MM_DTYPE)
    prev = None
    for l in range(depth):
        last = l == depth - 1
        rows = n_lat if last else n
        mod_l = mod[l]
        p, xs = _inproj(xs, prev, norm1[l][None], mod_l, mm(w_in[l]), tm=tm_in, tn=1024, tile_rows=tm_out,
                        n_lat=n_lat, seq=seq, batch=batch)
        y_ret = _retention_latent(p, log_gamma[l], cos, sin, batch=batch, seq=seq, n_ctx=n_ctx, n_lat=n_lat)
        y_conv, y_pool = _convpool(p, conv_w[l], mm(pool_w[l]), pool_scale[l][None], None,
                                   n_seq=batch, seq=seq, row0=0)
        if not last:
            y_ret = _retention_ctx(p, log_gamma[l], y_ret, batch=batch, n_ctx=n_ctx, n_lat=n_lat)
            y_conv, y_pool = _convpool(p, conv_w[l], mm(pool_w[l]), pool_scale[l][None], (y_conv, y_pool),
                                       n_seq=batch, seq=n_ctx, row0=n_lat // n_ctx)
        w_router, b_router = _router_weights(w_rg[l], b_rg[l], w_re[l], b_re[l])
        n_tiles = rows // tm_out
        cap = -(-(rows + n_tiles * ROW_ALIGN) // tm_moe) * tm_moe
        max_sorted = rows + n_tiles * N_BUCKETS * (ROW_ALIGN - 1)
        xs, hs, counts, tab, lpos = _outproj(xs, p, y_ret, y_conv, y_pool, mod_l, norm2[l][None], mm(w_ret_out[l]),
                                             mm(w_conv_out[l]), mm(w_pool_out[l]), mm(w_o[l]), w_router, b_router,
                                             tm=tm_out, cap=cap, n_rows=rows, n_lat=n_lat, seq=seq, batch=batch)
        blk, elo, ehi, valid = _work_tables(counts, tm=tm_moe, cap=cap, n_work=-(-max_sorted // tm_moe) + N_BUCKETS)
        ys = _moe(blk, elo, ehi, valid, hs, mm(w1[l]), mm(w3[l]), mm(w2[l]), tm=tm_moe, d=d)
        prev = (ys, tab[:, :TAB_ROWS, :].reshape(-1), lpos, mod_l, cap)
    out = _final(xs, prev, final_norm[None], tm=tm_out, n_rows=n_lat, n_lat=n_lat, seq=seq, batch=batch)
    return out.reshape(batch, seq, d)
```

```python
import functools

import numpy as np
import jax
import jax.numpy as jnp
from jax import lax
from jax.experimental import pallas as pl
from jax.experimental.pallas import tpu as pltpu

F32 = jnp.float32
MM_DTYPE = jnp.bfloat16
ACT_DTYPE = jnp.bfloat16

NORM_EPS = 1e-6
GRID_W = 64
ROPE_BASE = 10000.0
N_MOD = 6

RET_HEADS = 4
RET_DK = 128
RET_DV = 256
RET_QK = RET_HEADS * RET_DK
RET_V = RET_HEADS * RET_DV
RET_CHUNK = 256

CONV_W = 512
POOL_WINDOWS = (2, 4, 8, 16)
POOL_GROUPS = 4
POOL_GDIM = 128
POOL_W = POOL_GROUPS * POOL_GDIM
POOL_HALO = 8

N_GROUPS = 4
EXP_PER_GROUP = 4
N_EXPERTS = N_GROUPS * EXP_PER_GROUP
D_FF = 512
PAIR_LO = (0, 0, 0, 1, 1, 2)
PAIR_HI = (1, 2, 3, 2, 3, 3)
N_PAIRS = len(PAIR_LO)
N_BUCKETS = N_GROUPS * N_PAIRS

LANES = 128
N_PIECES = 3
ROW_ALIGN = 8
ALIGN_SHIFT = 3
TAB_COUNT, TAB_LOCAL, TAB_GLOBAL, TAB_TOTAL, TAB_ROWS = 0, 1, 2, 3, 4
ROUTER_LO_LANE = 32

OFF_Q = 0
OFF_K = OFF_Q + RET_QK
OFF_V = OFF_K + RET_QK
OFF_G = OFF_V + RET_V
OFF_CB = OFF_G + RET_V
OFF_CC = OFF_CB + CONV_W
OFF_CX = OFF_CC + CONV_W
OFF_PI = OFF_CX + CONV_W
OFF_GATE = OFF_PI + POOL_W

VMEM_LIMIT = 56 * 1024 * 1024


def _cparams(sem):
    return pltpu.CompilerParams(dimension_semantics=sem, vmem_limit_bytes=VMEM_LIMIT)


def _split_bf16(a):
    hi = a.astype(jnp.bfloat16)
    lo = (a - hi.astype(F32)).astype(jnp.bfloat16)
    return hi, lo


def _dot(a, b):
    return jnp.dot(a, b, preferred_element_type=F32)


def _dot3(a, b):
    ah, al = _split_bf16(a)
    bh, bl = _split_bf16(b)
    return _dot(ah, bh) + _dot(ah, bl) + _dot(al, bh)


def _mod_row(i, tile, n_lat, seq, batch):
    return jnp.where(i < n_lat // tile, (i * tile) // seq, batch)


def _ada_kernel(c_ref, w_ref, b_ref, o_ref):
    cv = c_ref[...]
    s = cv * jax.nn.sigmoid(cv)
    o_ref[0] = _dot3(s, w_ref[0]) + b_ref[0]


def _ada(cond, w_ada, b_ada):
    depth, d, width = w_ada.shape
    rows = cond.shape[0]
    tn = 512
    return pl.pallas_call(
        _ada_kernel,
        grid=(depth, width // tn),
        in_specs=[pl.BlockSpec((rows, d), lambda l, j: (0, 0)),
                  pl.BlockSpec((1, d, tn), lambda l, j: (l, 0, j)),
                  pl.BlockSpec((1, 1, tn), lambda l, j: (l, 0, j))],
        out_specs=pl.BlockSpec((1, rows, tn), lambda l, j: (l, 0, j)),
        out_shape=jax.ShapeDtypeStruct((depth, rows, width), F32),
        compiler_params=_cparams(("parallel", "parallel")),
    )(cond, w_ada, b_ada.reshape(depth, 1, width))


def _rms_mod(x, g, sc, sh):
    ms = jnp.mean(x * x, axis=-1, keepdims=True)
    return x * lax.rsqrt(ms + NORM_EPS) * g * (1.0 + sc) + sh


def _project_columns(h_scr, w_ref, o_ref, tn):
    for j in range(w_ref.shape[1] // tn):
        cols = slice(j * tn, (j + 1) * tn)
        o_ref[:, cols] = _dot(h_scr[...], w_ref[:, cols]).astype(o_ref.dtype)


def _inproj_kernel(x_ref, g_ref, sc_ref, sh_ref, w_ref, o_ref, h_scr, *, tn):
    h_scr[...] = _rms_mod(x_ref[...], g_ref[...], sc_ref[0], sh_ref[0]).astype(h_scr.dtype)
    _project_columns(h_scr, w_ref, o_ref, tn)


def _inproj_residual_kernel(tab_ref, x_ref, ys_ref, lpos_ref, gt_ref, g_ref, sc_ref, sh_ref, w_ref,
                            o_ref, xo_ref, h_scr, stage, sems, *, tn, cap):
    f = _expert_residual(tab_ref, ys_ref, lpos_ref, stage, sems, tile_rows=x_ref.shape[0], cap=cap)
    x = x_ref[...] + gt_ref[0] * f
    xo_ref[...] = x
    h_scr[...] = _rms_mod(x, g_ref[...], sc_ref[0], sh_ref[0]).astype(h_scr.dtype)
    _project_columns(h_scr, w_ref, o_ref, tn)


def _inproj(xs, prev, norm_g, mod_l, w, *, tm, tn, n_lat, seq, batch):
    n, d = xs.shape
    width = w.shape[1]
    row = functools.partial(_mod_row, tile=tm, n_lat=n_lat, seq=seq, batch=batch)
    mod = lambda k: pl.BlockSpec((1, 1, d), lambda i, *_: (row(i), 0, k))
    x_spec = pl.BlockSpec((tm, d), lambda i, *_: (i, 0))
    p_spec = pl.BlockSpec((tm, width), lambda i, *_: (i, 0))
    p_shape = jax.ShapeDtypeStruct((n, width), ACT_DTYPE)
    tail_specs = [pl.BlockSpec((1, d), lambda i, *_: (0, 0)), mod(1), mod(0),
                  pl.BlockSpec((d, width), lambda i, *_: (0, 0), pipeline_mode=pl.Buffered(1))]
    grid = (n // tm,)
    h_scratch = pltpu.VMEM((tm, d), MM_DTYPE)
    if prev is None:
        p = pl.pallas_call(functools.partial(_inproj_kernel, tn=tn), grid=grid, in_specs=[x_spec] + tail_specs,
                           out_specs=p_spec, out_shape=p_shape, scratch_shapes=[h_scratch],
                           compiler_params=_cparams(("parallel",)))(xs, norm_g, mod_l, mod_l, w)
        return p, xs
    ys, tab, lpos, mod_prev, cap = prev
    gs = pltpu.PrefetchScalarGridSpec(
        num_scalar_prefetch=1,
        grid=grid,
        in_specs=[x_spec, pl.BlockSpec(memory_space=pl.ANY), pl.BlockSpec((tm, 1), lambda i, *_: (i, 0)),
                  mod(5)] + tail_specs,
        out_specs=[p_spec, x_spec],
        scratch_shapes=[h_scratch, pltpu.VMEM((2, _sorted_rows(tm), d), F32), pltpu.SemaphoreType.DMA((2,))])
    return pl.pallas_call(
        functools.partial(_inproj_residual_kernel, tn=tn, cap=cap),
        grid_spec=gs,
        out_shape=[p_shape, jax.ShapeDtypeStruct((n, d), F32)],
        input_output_aliases={1: 1},
        compiler_params=_cparams(("arbitrary",)),
    )(tab, xs, ys, lpos, mod_prev, norm_g, mod_l, mod_l, w)


def _ret_kernel(lg_ref, *refs, seq, n_ctx, use_rope):
    if n_ctx:
        q_ref, k_ref, v_ref, g_ref, kc_ref, vc_ref, cos_ref, sin_ref, o_ref, kt_scr, sb_scr = refs
    else:
        q_ref, k_ref, v_ref, g_ref, o_ref, kt_scr = refs
    C = RET_CHUNK
    n_chunk = seq // C
    head = pl.program_id(1)
    lgf = lg_ref[0, head]
    lgb = lg_ref[1, head]

    q = q_ref[...].astype(F32)
    k = k_ref[...].astype(F32)
    if use_rope:
        cos = cos_ref[...]
        sin = sin_ref[...]
        q = q * cos + pltpu.roll(q, RET_DK // 2, 1) * sin
        k = k * cos + pltpu.roll(k, RET_DK // 2, 1) * sin
    q = q * (RET_DK ** -0.5)
    kt_scr[...] = k.T

    ri = lax.broadcasted_iota(jnp.int32, (C, C), 0)
    ci = lax.broadcasted_iota(jnp.int32, (C, C), 1)
    rel = (ri - ci).astype(F32)
    dmask = jnp.where(rel > 0.0, jnp.exp(lgf * jnp.maximum(rel, 0.0)),
                      jnp.where(rel < 0.0, jnp.exp(lgb * jnp.maximum(-rel, 0.0)), 2.0))
    icol = lax.broadcasted_iota(jnp.int32, (C, 1), 0).astype(F32)
    jrow = lax.broadcasted_iota(jnp.int32, (1, C), 1).astype(F32)
    qdec_f = jnp.exp(lgf * (icol + 1.0))
    qdec_b = jnp.exp(lgb * (C - icol))
    kdec_f = jnp.exp(lgf * (C - 1.0 - jrow))
    kdec_b = jnp.exp(lgb * jrow)
    zrow = jnp.zeros((1, RET_DV), F32)
    cdec_f = jnp.exp(zrow + lgf * C)
    cdec_b = jnp.exp(zrow + lgb * C)

    def chunk(ref, n):
        return ref[n * C:(n + 1) * C, :]

    def kt_chunk(n):
        return kt_scr[:, n * C:(n + 1) * C]

    if n_ctx:
        kct = kc_ref[...].astype(F32).T
        vcx = vc_ref[...].astype(MM_DTYPE)
        mrow = lax.broadcasted_iota(jnp.int32, (1, n_ctx), 1).astype(F32)
        s_f = _dot((kct * jnp.exp(lgf * (n_ctx - 1.0 - mrow))).astype(MM_DTYPE), vcx)
        s_b = _dot((kct * jnp.exp(lgb * mrow)).astype(MM_DTYPE), vcx)
        sb_scr[n_chunk - 1] = s_b
        for n in range(n_chunk - 1, 0, -1):
            s_b = s_b * cdec_b + _dot((kt_chunk(n) * kdec_b).astype(MM_DTYPE), chunk(v_ref, n).astype(MM_DTYPE))
            sb_scr[n - 1] = s_b

    for n in range(n_chunk):
        qn = q[n * C:(n + 1) * C, :]
        ktn = kt_chunk(n)
        vn = chunk(v_ref, n).astype(MM_DTYPE)
        s = _dot(qn.astype(MM_DTYPE), ktn.astype(MM_DTYPE)) * dmask
        o = _dot(s.astype(MM_DTYPE), vn)
        if n_ctx:
            qcat = jnp.concatenate([qn * qdec_f, qn * qdec_b], axis=1).astype(MM_DTYPE)
            scat = jnp.concatenate([s_f, sb_scr[n]], axis=0).astype(MM_DTYPE)
            o = o + _dot(qcat, scat)
            if n + 1 < n_chunk:
                s_f = s_f * cdec_f + _dot((ktn * kdec_f).astype(MM_DTYPE), vn)
        mu = jnp.mean(o, axis=-1, keepdims=True)
        oc = o - mu
        yn = oc * lax.rsqrt(jnp.mean(oc * oc, axis=-1, keepdims=True) + NORM_EPS)
        gn = chunk(g_ref, n).astype(F32)
        o_ref[n * C:(n + 1) * C, :] = (yn * (gn * jax.nn.sigmoid(gn))).astype(o_ref.dtype)


def _retention_latent(p, log_gamma, cos, sin, *, batch, seq, n_ctx, n_lat):
    n = p.shape[0]
    cb = n_lat // n_ctx
    kq, kv = OFF_K // RET_DK, OFF_V // RET_DV
    gs = pltpu.PrefetchScalarGridSpec(
        num_scalar_prefetch=1,
        grid=(batch, RET_HEADS),
        in_specs=[pl.BlockSpec((seq, RET_DK), lambda b, h, lg: (b, h)),
                  pl.BlockSpec((seq, RET_DK), lambda b, h, lg: (b, kq + h)),
                  pl.BlockSpec((seq, RET_DV), lambda b, h, lg: (b, kv + h)),
                  pl.BlockSpec((seq, RET_DV), lambda b, h, lg: (b, OFF_G // RET_DV + h)),
                  pl.BlockSpec((n_ctx, RET_DK), lambda b, h, lg: (cb + b, kq + h)),
                  pl.BlockSpec((n_ctx, RET_DV), lambda b, h, lg: (cb + b, kv + h)),
                  pl.BlockSpec((seq, RET_DK), lambda b, h, lg: (0, 0)),
                  pl.BlockSpec((seq, RET_DK), lambda b, h, lg: (0, 0))],
        out_specs=pl.BlockSpec((seq, RET_DV), lambda b, h, lg: (b, h)),
        scratch_shapes=[pltpu.VMEM((RET_DK, seq), F32),
                        pltpu.VMEM((seq // RET_CHUNK, RET_DK, RET_DV), F32)])
    return pl.pallas_call(
        functools.partial(_ret_kernel, seq=seq, n_ctx=n_ctx, use_rope=True),
        grid_spec=gs,
        out_shape=jax.ShapeDtypeStruct((n, RET_V), ACT_DTYPE),
        compiler_params=_cparams(("parallel", "parallel")),
    )(log_gamma, p, p, p, p, p, p, cos, sin)


def _retention_ctx(p, log_gamma, y_ret, *, batch, n_ctx, n_lat):
    cb = n_lat // n_ctx
    kq, kv = OFF_K // RET_DK, OFF_V // RET_DV
    gs = pltpu.PrefetchScalarGridSpec(
        num_scalar_prefetch=1,
        grid=(batch, RET_HEADS),
        in_specs=[pl.BlockSpec((n_ctx, RET_DK), lambda b, h, lg: (cb + b, h)),
                  pl.BlockSpec((n_ctx, RET_DK), lambda b, h, lg: (cb + b, kq + h)),
                  pl.BlockSpec((n_ctx, RET_DV), lambda b, h, lg: (cb + b, kv + h)),
                  pl.BlockSpec((n_ctx, RET_DV), lambda b, h, lg: (cb + b, OFF_G // RET_DV + h)),
                  pl.BlockSpec(memory_space=pl.ANY)],
        out_specs=pl.BlockSpec((n_ctx, RET_DV), lambda b, h, lg: (cb + b, h)),
        scratch_shapes=[pltpu.VMEM((RET_DK, n_ctx), F32)])

    def body(lg_ref, q_ref, k_ref, v_ref, g_ref, alias_ref, o_ref, kt_scr):
        del alias_ref
        _ret_kernel(lg_ref, q_ref, k_ref, v_ref, g_ref, o_ref, kt_scr, seq=n_ctx, n_ctx=0, use_rope=False)

    return pl.pallas_call(
        body,
        grid_spec=gs,
        out_shape=jax.ShapeDtypeStruct(y_ret.shape, y_ret.dtype),
        input_output_aliases={5: 0},
        compiler_params=_cparams(("parallel", "parallel")),
    )(log_gamma, p, p, p, p, y_ret)


def _convpool_kernel(*refs, seq, aliased):
    if aliased:
        cb_ref, cc_ref, cx_ref, pi_ref, cw_ref, pw_ref, ps_ref, _, _, yc_ref, yp_ref = refs
    else:
        cb_ref, cc_ref, cx_ref, pi_ref, cw_ref, pw_ref, ps_ref, yc_ref, yp_ref = refs
    grp = pl.program_id(1)
    t = lax.broadcasted_iota(jnp.int32, (seq, 1), 0)

    u = cc_ref[...].astype(F32) * cx_ref[...].astype(F32)
    u_prev = jnp.where(t == 0, 0.0, pltpu.roll(u, 1, 0))
    u_next = jnp.where(t == seq - 1, 0.0, pltpu.roll(u, seq - 1, 0))
    w = cw_ref[...]
    conv = w[0:1, :] * u_prev + w[1:2, :] * u + w[2:3, :] * u_next
    yc_ref[...] = (cb_ref[...].astype(F32) * conv).astype(yc_ref.dtype)

    p = pi_ref[...].astype(F32)
    halo = jnp.zeros((POOL_HALO, p.shape[1]), F32)
    ext = jnp.concatenate([halo, p, halo], axis=0)
    length = seq + 2 * POOL_HALO

    def at(a, k):
        return pltpu.roll(a, (-k) % length, 0)

    a2 = at(ext, -1) + ext
    a4 = at(a2, -1) + at(a2, 1)
    a8 = at(a4, -2) + at(a4, 2)
    a16 = at(a8, -4) + at(a8, 4)
    wsum = jnp.where(grp == 0, a2, jnp.where(grp == 1, a4, jnp.where(grp == 2, a8, a16)))
    wsum = wsum[POOL_HALO:POOL_HALO + seq, :]
    half = jnp.left_shift(1, grp)
    lo = jnp.clip(t - half, 0, seq)
    hi = jnp.clip(t + half, 0, seq)
    cnt = (hi - lo).astype(F32)
    pooled = wsum / cnt - p
    mixed = _dot(pooled.astype(MM_DTYPE), pw_ref[0]) * ps_ref[...]
    yp_ref[...] = mixed.astype(yp_ref.dtype)


def _convpool(p, conv_w, pool_w, pool_scale, prev, *, n_seq, seq, row0):
    n = p.shape[0]
    g128 = lambda off: off // POOL_GDIM
    col = lambda off: (lambda b, g: (row0 + b, g128(off) + g))
    in_specs = [pl.BlockSpec((seq, POOL_GDIM), col(OFF_CB)),
                pl.BlockSpec((seq, POOL_GDIM), col(OFF_CC)),
                pl.BlockSpec((seq, POOL_GDIM), col(OFF_CX)),
                pl.BlockSpec((seq, POOL_GDIM), col(OFF_PI)),
                pl.BlockSpec((conv_w.shape[0], POOL_GDIM), lambda b, g: (0, g)),
                pl.BlockSpec((1, POOL_GDIM, POOL_GDIM), lambda b, g: (g, 0, 0)),
                pl.BlockSpec((1, POOL_GDIM), lambda b, g: (0, g))]
    args = [p, p, p, p, conv_w, pool_w, pool_scale]
    aliases = {}
    if prev is not None:
        in_specs += [pl.BlockSpec(memory_space=pl.ANY), pl.BlockSpec(memory_space=pl.ANY)]
        args += list(prev)
        aliases = {7: 0, 8: 1}
    out_spec = pl.BlockSpec((seq, POOL_GDIM), lambda b, g: (row0 + b, g))
    return pl.pallas_call(
        functools.partial(_convpool_kernel, seq=seq, aliased=prev is not None),
        grid=(n_seq, POOL_GROUPS),
        in_specs=in_specs,
        out_specs=[out_spec, out_spec],
        out_shape=[jax.ShapeDtypeStruct((n, CONV_W), ACT_DTYPE), jax.ShapeDtypeStruct((n, POOL_W), ACT_DTYPE)],
        input_output_aliases=aliases,
        compiler_params=_cparams(("parallel", "parallel")),
    )(*args)


def _route(logits):
    lane = lax.broadcasted_iota(jnp.int32, logits.shape, 1)
    neg = -jnp.inf
    is_g = lane < N_GROUPS
    gl = jnp.where(is_g, logits, neg)
    gmax = jnp.max(gl, axis=-1, keepdims=True)
    gidx = jnp.min(jnp.where(gl == gmax, lane, LANES), axis=-1, keepdims=True)
    gtop = 1.0 / jnp.sum(jnp.where(is_g, jnp.exp(gl - gmax), 0.0), axis=-1, keepdims=True)
    base = N_GROUPS + EXP_PER_GROUP * gidx
    el = jnp.where((lane >= base) & (lane < base + EXP_PER_GROUP), logits, neg)
    m1 = jnp.max(el, axis=-1, keepdims=True)
    i1 = jnp.min(jnp.where(el == m1, lane, LANES), axis=-1, keepdims=True)
    el2 = jnp.where(lane == i1, neg, el)
    m2 = jnp.max(el2, axis=-1, keepdims=True)
    i2 = jnp.min(jnp.where(el2 == m2, lane, LANES), axis=-1, keepdims=True)
    e2 = jnp.exp(m2 - m1)
    w1 = gtop / (1.0 + e2)
    w2 = gtop * e2 / (1.0 + e2)
    first_lower = i1 < i2
    lo = jnp.minimum(i1, i2) - base
    hi = jnp.maximum(i1, i2) - base
    pair = lo * 3 - jnp.right_shift(lo * (lo - 1), 1) + hi - lo - 1
    bucket = gidx * N_PAIRS + pair
    wlo = jnp.where(first_lower, w1, w2)
    whi = jnp.where(first_lower, w2, w1)
    return bucket, wlo, whi


def _column_to_lanes(col):
    return jnp.broadcast_to(col, (col.shape[0], LANES)).T[0:8, :]


def _bucket_run_copies(read_run, max_rows, make_copy):
    def per_bucket(b, carry_):
        n, local, glob = read_run(b)
        off = jnp.int32(0)
        size = 1 << (max_rows.bit_length() - 1)
        while size >= ROW_ALIGN:
            take = n & size

            @pl.when(take != 0)
            def _(size=size, off=off):
                make_copy(pl.multiple_of(local + off, ROW_ALIGN), pl.multiple_of(glob + off, ROW_ALIGN), size).start()

            off = off + take
            size //= 2
        return carry_

    lax.fori_loop(0, N_BUCKETS, per_bucket, 0)


def _wait_rows(total, max_rows, make_copy):
    size = 1 << (max_rows.bit_length() - 1)
    while size >= ROW_ALIGN:
        @pl.when((total & size) != 0)
        def _(size=size):
            make_copy(size).wait()
        size //= 2


def _exact_bf16_pieces(w):
    a = w.astype(jnp.bfloat16).astype(F32)
    b = (w - a).astype(jnp.bfloat16).astype(F32)
    c = (w - a - b).astype(jnp.bfloat16).astype(F32)
    return a, b, c


def _outproj_kernel(x_ref, yr_ref, yc_ref, yp_ref, g0_ref, g1_ref, g2_ref, gt_ref, sc_ref, sh_ref, ng_ref,
                    wr_ref, wc_ref, wp_ref, wo_ref, wrt_ref, brt_ref,
                    xo_ref, hs_ref, cnt_ref, tab_ref, lpos_ref,
                    h_scr, tab_vmem, tab_smem, sent_smem, carry, row_sem, tab_sem, *, cap):
    i = pl.program_id(0)
    n_steps = pl.num_programs(0)
    slot = lax.rem(i, 2)
    tm, d = x_ref.shape
    ts = h_scr.shape[1]

    def wait_rows(s):
        _wait_rows(sent_smem[s], ts, lambda size: pltpu.make_async_copy(
            h_scr.at[s, pl.ds(0, size)], hs_ref.at[pl.ds(0, size)], row_sem.at[s]))

    @pl.when(i == 0)
    def _():
        carry[...] = jnp.zeros_like(carry)

    @pl.when(i >= 2)
    def _():
        wait_rows(slot)

    sig = lambda r: jax.nn.sigmoid(r[...].astype(F32))
    merged = (sig(g0_ref) * _dot(yr_ref[...], wr_ref[...])
              + sig(g1_ref) * _dot(yc_ref[...], wc_ref[...])
              + sig(g2_ref) * _dot(yp_ref[...], wp_ref[...]))
    y = _dot(merged.astype(MM_DTYPE), wo_ref[...])
    x = x_ref[...] + gt_ref[0] * y
    xo_ref[...] = x
    h = _rms_mod(x, ng_ref[...], sc_ref[0], sh_ref[0])

    hh, hl = _split_bf16(h)
    s2 = _dot(hh, wrt_ref[...]) + _dot(hl, wrt_ref[...])
    logits = s2 + pltpu.roll(s2, LANES - ROUTER_LO_LANE, 1) + brt_ref[...]
    bucket, wlo, whi = _route(logits)

    lane = lax.broadcasted_iota(jnp.int32, (tm, LANES), 1)
    onehot = lane == bucket
    ones = jnp.where(onehot, 1.0, 0.0)
    r = lax.broadcasted_iota(jnp.int32, (tm, tm), 0)
    c = lax.broadcasted_iota(jnp.int32, (tm, tm), 1)
    tri = jnp.where(r >= c, 1.0, 0.0).astype(jnp.bfloat16)
    incl = _dot(tri, ones.astype(jnp.bfloat16))
    tile_cnt = jnp.sum(ones, axis=0, keepdims=True).astype(jnp.int32)
    run_len = jnp.left_shift(jnp.right_shift(tile_cnt + (ROW_ALIGN - 1), ALIGN_SHIFT), ALIGN_SHIFT)
    lane1 = lax.broadcasted_iota(jnp.int32, (1, LANES), 1)
    run = run_len
    k = 1
    while k < LANES:
        run = run + jnp.where(lane1 >= k, pltpu.roll(run, k, 1), 0)
        k *= 2
    local_start = run - run_len
    total = run[:, LANES - 1:LANES]
    lpos = jnp.sum(jnp.where(onehot, incl - 1.0 + local_start.astype(F32), 0.0), axis=-1, keepdims=True)
    lpos_ref[...] = lpos.astype(jnp.int32)
    srow = lax.broadcasted_iota(jnp.int32, (8, LANES), 0)
    tab = jnp.where(srow == TAB_COUNT, run_len,
                    jnp.where(srow == TAB_LOCAL, local_start,
                              jnp.where(srow == TAB_GLOBAL, carry[...], jnp.where(srow == TAB_TOTAL, total, 0))))
    carry[...] = carry[...] + run_len
    tab_ref[0] = tab
    tab_vmem[...] = tab
    to_smem = pltpu.make_async_copy(tab_vmem, tab_smem, tab_sem)
    to_smem.start()

    lpos_row = _column_to_lanes(lpos).astype(jnp.int32)[0:1, :]
    srt = lax.broadcasted_iota(jnp.int32, (ts, tm), 0)
    perm = jnp.where(srt == lpos_row, 1.0, 0.0).astype(MM_DTYPE)
    pieces = _exact_bf16_pieces(wlo) + _exact_bf16_pieces(whi)
    meta = jnp.zeros((tm, LANES), F32)
    for k, piece in enumerate(pieces):
        meta = jnp.where(lane == k, piece, meta)
    h_scr[slot, :, :d] = _dot(perm, h.astype(MM_DTYPE))
    h_scr[slot, :, d:] = _dot(perm, meta.astype(MM_DTYPE))
    to_smem.wait()
    sent_smem[slot] = tab_smem[TAB_TOTAL, 0]

    _bucket_run_copies(
        lambda b: (tab_smem[TAB_COUNT, b], tab_smem[TAB_LOCAL, b], b * cap + tab_smem[TAB_GLOBAL, b]), tm,
        lambda local, glob, size: pltpu.make_async_copy(h_scr.at[slot, pl.ds(local, size)],
                                                        hs_ref.at[pl.ds(glob, size)], row_sem.at[slot]))

    @pl.when(i == n_steps - 1)
    def _():
        cnt_ref[...] = jnp.broadcast_to(carry[...].astype(jnp.int32), cnt_ref.shape)
        wait_rows(slot)

    @pl.when((i == n_steps - 1) & (i >= 1))
    def _():
        wait_rows(1 - slot)


def _sorted_rows(tm):
    return tm + N_BUCKETS * ROW_ALIGN


def _outproj(xs, p, y_ret, y_conv, y_pool, mod_l, norm_g, w_ret, w_conv, w_pool, w_o, w_router, b_router,
             *, tm, cap, n_rows, n_lat, seq, batch):
    n, d = xs.shape
    assert cap >= n_rows + (n_rows // tm) * ROW_ALIGN and cap % ROW_ALIGN == 0
    width = d + LANES
    ts = _sorted_rows(tm)
    row = functools.partial(_mod_row, tile=tm, n_lat=n_lat, seq=seq, batch=batch)
    gate = lambda k: pl.BlockSpec((tm, d), lambda i: (i, OFF_GATE // d + k))
    mod = lambda k: pl.BlockSpec((1, 1, d), lambda i: (row(i), 0, k))
    full = lambda a: pl.BlockSpec(a.shape, lambda i: (0,) * a.ndim)
    return pl.pallas_call(
        functools.partial(_outproj_kernel, cap=cap),
        grid=(n_rows // tm,),
        in_specs=[pl.BlockSpec((tm, d), lambda i: (i, 0)),
                  pl.BlockSpec((tm, RET_V), lambda i: (i, 0)),
                  pl.BlockSpec((tm, CONV_W), lambda i: (i, 0)),
                  pl.BlockSpec((tm, POOL_W), lambda i: (i, 0)),
                  gate(0), gate(1), gate(2),
                  mod(2), mod(4), mod(3),
                  full(norm_g), full(w_ret), full(w_conv), full(w_pool), full(w_o), full(w_router), full(b_router)],
        out_specs=[pl.BlockSpec((tm, d), lambda i: (i, 0)),
                   pl.BlockSpec(memory_space=pl.ANY),
                   pl.BlockSpec((8, LANES), lambda i: (0, 0)),
                   pl.BlockSpec((1, 8, LANES), lambda i: (i, 0, 0)),
                   pl.BlockSpec((tm, 1), lambda i: (i, 0))],
        out_shape=[jax.ShapeDtypeStruct((n, d), F32),
                   jax.ShapeDtypeStruct((N_BUCKETS * cap, width), F32),
                   jax.ShapeDtypeStruct((8, LANES), jnp.int32),
                   jax.ShapeDtypeStruct((n_rows // tm, 8, LANES), jnp.int32),
                   jax.ShapeDtypeStruct((n_rows, 1), jnp.int32)],
        scratch_shapes=[pltpu.VMEM((2, ts, width), F32),
                        pltpu.VMEM((8, LANES), jnp.int32),
                        pltpu.SMEM((8, LANES), jnp.int32),
                        pltpu.SMEM((2,), jnp.int32),
                        pltpu.VMEM((1, LANES), jnp.int32),
                        pltpu.SemaphoreType.DMA((2,)),
                        pltpu.SemaphoreType.DMA(())],
        input_output_aliases={0: 0},
        compiler_params=_cparams(("arbitrary",)),
    )(xs, y_ret, y_conv, y_pool, p, p, p, mod_l, mod_l, mod_l, norm_g, w_ret, w_conv, w_pool, w_o, w_router, b_router)


def _moe_kernel(blk_ref, elo_ref, ehi_ref, valid_ref, hs_ref, w1l_ref, w1h_ref, w3l_ref, w3h_ref, w2l_ref, w2h_ref,
                ys_ref):
    del blk_ref, elo_ref, ehi_ref
    valid = valid_ref[pl.program_id(0)]
    tm, d = ys_ref.shape

    @pl.when(valid > 0)
    def _():
        keep = lax.broadcasted_iota(jnp.int32, (tm, 1), 0) < valid
        meta = jnp.where(keep, hs_ref[:, d:], 0.0)
        h = jnp.where(keep, hs_ref[:, :d], 0.0).astype(MM_DTYPE)
        wlo = meta[:, 0:1] + meta[:, 1:2] + meta[:, 2:3]
        whi = meta[:, 3:4] + meta[:, 4:5] + meta[:, 5:6]

        def expert(w1, w3, w2):
            a = _dot(h, w1[0])
            a = a * jax.nn.sigmoid(a) * _dot(h, w3[0])
            return _dot(a.astype(MM_DTYPE), w2[0])

        y = wlo * expert(w1l_ref, w3l_ref, w2l_ref) + whi * expert(w1h_ref, w3h_ref, w2h_ref)
        ys_ref[...] = y.astype(ACT_DTYPE).astype(F32)


def _moe(blk, elo, ehi, valid, hs, w1, w3, w2, *, tm, d):
    n_sorted, width = hs.shape
    n_work = blk.shape[0]
    up = lambda sel: pl.BlockSpec((1, d, D_FF), lambda s, blk, elo, ehi, valid: ((elo, ehi)[sel][s], 0, 0))
    down = lambda sel: pl.BlockSpec((1, D_FF, d), lambda s, blk, elo, ehi, valid: ((elo, ehi)[sel][s], 0, 0))
    gs = pltpu.PrefetchScalarGridSpec(
        num_scalar_prefetch=4,
        grid=(n_work,),
        in_specs=[pl.BlockSpec((tm, width), lambda s, blk, elo, ehi, valid: (blk[s], 0)),
                  up(0), up(1), up(0), up(1), down(0), down(1)],
        out_specs=pl.BlockSpec((tm, d), lambda s, blk, elo, ehi, valid: (blk[s], 0)))
    return pl.pallas_call(
        _moe_kernel,
        grid_spec=gs,
        out_shape=jax.ShapeDtypeStruct((n_sorted, d), F32),
        compiler_params=_cparams(("arbitrary",)),
    )(blk, elo, ehi, valid, hs, w1, w1, w3, w3, w2, w2)


def _expert_residual(tab_ref, ys_ref, lpos_ref, stage, sems, *, tile_rows, cap):
    i = pl.program_id(0)
    slot = lax.rem(i, 2)
    ts = stage.shape[1]
    entry = lambda tile, row, lane: tab_ref[tile * (TAB_ROWS * LANES) + row * LANES + lane]

    def start(tile, s):
        _bucket_run_copies(
            lambda b: (entry(tile, TAB_COUNT, b), entry(tile, TAB_LOCAL, b), b * cap + entry(tile, TAB_GLOBAL, b)),
            tile_rows,
            lambda local, glob, size: pltpu.make_async_copy(
                ys_ref.at[pl.ds(glob, size)], stage.at[s, pl.ds(local, size)], sems.at[s]))

    @pl.when(i == 0)
    def _():
        start(i, slot)

    @pl.when(i + 1 < pl.num_programs(0))
    def _():
        start(i + 1, 1 - slot)

    total = entry(i, TAB_TOTAL, 0)
    _wait_rows(total, ts, lambda size: pltpu.make_async_copy(
        ys_ref.at[pl.ds(0, size)], stage.at[slot, pl.ds(0, size)], sems.at[slot]))
    lpos = lpos_ref[...]
    unperm = jnp.where(lpos == lax.broadcasted_iota(jnp.int32, (tile_rows, ts), 1), 1.0, 0.0).astype(MM_DTYPE)
    filled = lax.broadcasted_iota(jnp.int32, (ts, 1), 0) < total
    return _dot(unperm, jnp.where(filled, stage[slot], 0.0).astype(MM_DTYPE))


def _final_kernel(tab_ref, x_ref, ys_ref, lpos_ref, gt_ref, g_ref, o_ref, stage, sems, *, cap):
    f = _expert_residual(tab_ref, ys_ref, lpos_ref, stage, sems, tile_rows=x_ref.shape[0], cap=cap)
    x = x_ref[...] + gt_ref[0] * f
    o_ref[...] = x * lax.rsqrt(jnp.mean(x * x, axis=-1, keepdims=True) + NORM_EPS) * g_ref[...]


def _final(xs, moe_out, final_g, *, tm, n_rows, n_lat, seq, batch):
    ys, tab, lpos, mod_l, cap = moe_out
    d = xs.shape[1]
    row = functools.partial(_mod_row, tile=tm, n_lat=n_lat, seq=seq, batch=batch)
    x_spec = pl.BlockSpec((tm, d), lambda i, tab: (i, 0))
    gs = pltpu.PrefetchScalarGridSpec(
        num_scalar_prefetch=1,
        grid=(n_rows // tm,),
        in_specs=[x_spec,
                  pl.BlockSpec(memory_space=pl.ANY),
                  pl.BlockSpec((tm, 1), lambda i, tab: (i, 0)),
                  pl.BlockSpec((1, 1, d), lambda i, tab: (row(i), 0, 5)),
                  pl.BlockSpec((1, d), lambda i, tab: (0, 0))],
        out_specs=x_spec,
        scratch_shapes=[pltpu.VMEM((2, _sorted_rows(tm), d), F32), pltpu.SemaphoreType.DMA((2,))])
    return pl.pallas_call(
        functools.partial(_final_kernel, cap=cap),
        grid_spec=gs,
        out_shape=jax.ShapeDtypeStruct((n_rows, d), F32),
        compiler_params=_cparams(("arbitrary",)),
    )(tab, xs, ys, lpos, mod_l, final_g)


def _rope_tables(seq):
    rows = seq // GRID_W
    row = jnp.repeat(jnp.arange(rows, dtype=F32), GRID_W)
    col = jnp.tile(jnp.arange(GRID_W, dtype=F32), rows)
    n_freq = RET_DK // 4
    inv_freq = ROPE_BASE ** (-jnp.arange(n_freq, dtype=F32) / n_freq)
    ang = jnp.concatenate([row[:, None] * inv_freq[None, :], col[:, None] * inv_freq[None, :]], axis=-1)
    cos, sin = jnp.cos(ang), jnp.sin(ang)
    return jnp.concatenate([cos, cos], axis=-1), jnp.concatenate([-sin, sin], axis=-1)


def _work_tables(counts, *, tm, cap, n_work):
    cnt = counts[0, :N_BUCKETS]
    tiles = (cnt + tm - 1) // tm
    ends = jnp.cumsum(tiles)
    starts = ends - tiles
    item = jnp.arange(n_work, dtype=jnp.int32)
    used = item < ends[-1]
    ref_item = jnp.minimum(item, jnp.maximum(ends[-1] - 1, 0))
    bkt = jnp.minimum(jnp.sum((ref_item[:, None] >= ends[None, :]).astype(jnp.int32), axis=1), N_BUCKETS - 1)
    j = ref_item - starts[bkt]
    valid = jnp.where(used, jnp.clip(cnt[bkt] - j * tm, 0, tm), 0)
    blk = bkt * (cap // tm) + j
    grp, pair = bkt // N_PAIRS, bkt % N_PAIRS
    elo = grp * EXP_PER_GROUP + jnp.asarray(PAIR_LO, jnp.int32)[pair]
    ehi = grp * EXP_PER_GROUP + jnp.asarray(PAIR_HI, jnp.int32)[pair]
    i32 = lambda a: a.astype(jnp.int32)
    return i32(blk), i32(elo), i32(ehi), i32(valid)


def _router_weights(w_rg, b_rg, w_re, b_re):
    w = jnp.concatenate([w_rg, w_re], axis=-1).astype(F32)
    n_out = w.shape[1]
    assert n_out <= ROUTER_LO_LANE and ROUTER_LO_LANE + n_out <= LANES
    hi = w.astype(jnp.bfloat16)
    lo = (w - hi.astype(F32)).astype(jnp.bfloat16)
    packed = jnp.zeros((w.shape[0], LANES), jnp.bfloat16)
    packed = packed.at[:, :n_out].set(hi).at[:, ROUTER_LO_LANE:ROUTER_LO_LANE + n_out].set(lo)
    bias = jnp.zeros((1, LANES), F32).at[0, :n_out].set(jnp.concatenate([b_rg, b_re]).astype(F32))
    return packed, bias


def _pick_tile(n, want, *also):
    t = want
    while n % t or any(a % t for a in also):
        t //= 2
    return t


def kernel(x, c, ctx, c_ctx, w_ada, b_ada, norm1, norm2, w_in, ret_decay, conv_w, pool_w, pool_scale, w_ret_out,
           w_conv_out, w_pool_out, w_o, w_rg, b_rg, w_re, b_re, w1, w3, w2, final_norm):
    batch, seq, d = x.shape
    n_ctx = ctx.shape[1]
    depth = w_ada.shape[0]
    n_lat, n_c = batch * seq, batch * n_ctx
    n = n_lat + n_c
    assert POOL_WINDOWS == (2, 4, 8, 16) and POOL_HALO >= max(POOL_WINDOWS) // 2
    assert seq % RET_CHUNK == 0 and n_ctx == RET_CHUNK and seq % GRID_W == 0 and n_lat % n_ctx == 0

    tm_out = _pick_tile(seq, 512, n_c)
    tm_moe = 256

    xs = jnp.concatenate([x.reshape(n_lat, d), ctx.reshape(n_c, d)], axis=0)
    mod_rows = -(-(batch + 1) // 8) * 8
    cond = jnp.zeros((mod_rows, d), F32).at[:batch].set(c).at[batch].set(c_ctx)
    mod = _ada(cond, w_ada, b_ada).reshape(depth, mod_rows, 1, N_MOD * d)
    cos, sin = _rope_tables(seq)
    log_gamma = jax.nn.log_sigmoid(ret_decay.astype(F32))

    mm = lambda a: a.astype(MM_DTYPE)
    prev = None
    for l in range(depth):
        last = l == depth - 1
        rows = n_lat if last else n
        mod_l = mod[l]
        p, xs = _inproj(xs, prev, norm1[l][None], mod_l, mm(w_in[l]), tm=tm_out, tn=1024,
                        n_lat=n_lat, seq=seq, batch=batch)
        y_ret = _retention_latent(p, log_gamma[l], cos, sin, batch=batch, seq=seq, n_ctx=n_ctx, n_lat=n_lat)
        y_conv, y_pool = _convpool(p, conv_w[l], mm(pool_w[l]), pool_scale[l][None], None,
                                   n_seq=batch, seq=seq, row0=0)
        if not last:
            y_ret = _retention_ctx(p, log_gamma[l], y_ret, batch=batch, n_ctx=n_ctx, n_lat=n_lat)
            y_conv, y_pool = _convpool(p, conv_w[l], mm(pool_w[l]), pool_scale[l][None], (y_conv, y_pool),
                                       n_seq=batch, seq=n_ctx, row0=n_lat // n_ctx)
        w_router, b_router = _router_weights(w_rg[l], b_rg[l], w_re[l], b_re[l])
        n_tiles = rows // tm_out
        cap = -(-(rows + n_tiles * ROW_ALIGN) // tm_moe) * tm_moe
        max_sorted = rows + n_tiles * N_BUCKETS * (ROW_ALIGN - 1)
        xs, hs, counts, tab, lpos = _outproj(xs, p, y_ret, y_conv, y_pool, mod_l, norm2[l][None], mm(w_ret_out[l]),
                                             mm(w_conv_out[l]), mm(w_pool_out[l]), mm(w_o[l]), w_router, b_router,
                                             tm=tm_out, cap=cap, n_rows=rows, n_lat=n_lat, seq=seq, batch=batch)
        blk, elo, ehi, valid = _work_tables(counts, tm=tm_moe, cap=cap, n_work=-(-max_sorted // tm_moe) + N_BUCKETS)
        ys = _moe(blk, elo, ehi, valid, hs, mm(w1[l]), mm(w3[l]), mm(w2[l]), tm=tm_moe, d=d)
        prev = (ys, tab[:, :TAB_ROWS, :].reshape(-1), lpos, mod_l, cap)
    out = _final(xs, prev, final_norm[None], tm=tm_out, n_rows=n_lat, n_lat=n_lat, seq=seq, batch=batch)
    return out.reshape(batch, seq, d)
```

```python
import functools

import numpy as np
import jax
import jax.numpy as jnp
from jax import lax
from jax.experimental import pallas as pl
from jax.experimental.pallas import tpu as pltpu

F32 = jnp.float32
MM_DTYPE = jnp.bfloat16
ACT_DTYPE = jnp.bfloat16

NORM_EPS = 1e-6
GRID_W = 64
ROPE_BASE = 10000.0
N_MOD = 6

RET_HEADS = 4
RET_DK = 128
RET_DV = 256
RET_QK = RET_HEADS * RET_DK
RET_V = RET_HEADS * RET_DV
RET_CHUNK = 256

CONV_W = 512
POOL_WINDOWS = (2, 4, 8, 16)
POOL_GROUPS = 4
POOL_GDIM = 128
POOL_W = POOL_GROUPS * POOL_GDIM
POOL_BLOCK = 256
POOL_HALO = 16

N_GROUPS = 4
EXP_PER_GROUP = 4
N_EXPERTS = N_GROUPS * EXP_PER_GROUP
D_FF = 512
PAIR_LO = (0, 0, 0, 1, 1, 2)
PAIR_HI = (1, 2, 3, 2, 3, 3)
N_PAIRS = len(PAIR_LO)
N_BUCKETS = N_GROUPS * N_PAIRS

LANES = 128
OUT_SUBBLOCKS = 2
N_PIECES = 3
ROW_ALIGN = 8
ALIGN_SHIFT = 3
TAB_COUNT, TAB_LOCAL, TAB_GLOBAL, TAB_TOTAL, TAB_ROWS = 0, 1, 2, 3, 4
ROUTER_LO_ROW = 32

OFF_Q = 0
OFF_K = OFF_Q + RET_QK
OFF_V = OFF_K + RET_QK
OFF_G = OFF_V + RET_V
OFF_CB = OFF_G + RET_V
OFF_CC = OFF_CB + CONV_W
OFF_CX = OFF_CC + CONV_W
OFF_PI = OFF_CX + CONV_W
OFF_GATE = OFF_PI + POOL_W

VMEM_LIMIT = 56 * 1024 * 1024


def _cparams(sem):
    return pltpu.CompilerParams(dimension_semantics=sem, vmem_limit_bytes=VMEM_LIMIT)


def _split_bf16(a):
    hi = a.astype(jnp.bfloat16)
    lo = (a - hi.astype(F32)).astype(jnp.bfloat16)
    return hi, lo


def _dot(a, b):
    return jnp.dot(a, b, preferred_element_type=F32)


def _dot3(a, b):
    ah, al = _split_bf16(a)
    bh, bl = _split_bf16(b)
    return _dot(ah, bh) + _dot(ah, bl) + _dot(al, bh)


def _mod_row(i, tile, n_lat, seq, batch):
    return jnp.where(i < n_lat // tile, (i * tile) // seq, batch)


def _ada_kernel(c_ref, w_ref, b_ref, o_ref):
    cv = c_ref[...]
    s = cv * jax.nn.sigmoid(cv)
    o_ref[0] = _dot3(s, w_ref[0]) + b_ref[0]


def _ada(cond, w_ada, b_ada):
    depth, d, width = w_ada.shape
    rows = cond.shape[0]
    tn = 512
    return pl.pallas_call(
        _ada_kernel,
        grid=(depth, width // tn),
        in_specs=[pl.BlockSpec((rows, d), lambda l, j: (0, 0)),
                  pl.BlockSpec((1, d, tn), lambda l, j: (l, 0, j)),
                  pl.BlockSpec((1, 1, tn), lambda l, j: (l, 0, j))],
        out_specs=pl.BlockSpec((1, rows, tn), lambda l, j: (l, 0, j)),
        out_shape=jax.ShapeDtypeStruct((depth, rows, width), F32),
        compiler_params=_cparams(("parallel", "parallel")),
    )(cond, w_ada, b_ada.reshape(depth, 1, width))


def _rms_mod(x, g, sc, sh):
    ms = jnp.mean(x * x, axis=-1, keepdims=True)
    return x * lax.rsqrt(ms + NORM_EPS) * g * (1.0 + sc) + sh


def _project_columns(h_scr, w_ref, o_ref, tn):
    assert OFF_G % tn == 0 and OFF_CB % tn == 0 and OFF_GATE % tn == 0
    for j in range(w_ref.shape[1] // tn):
        cols = slice(j * tn, (j + 1) * tn)
        z = _dot(h_scr[...], w_ref[:, cols])
        if OFF_G <= j * tn < OFF_CB:
            z = z * jax.nn.sigmoid(z)
        elif j * tn >= OFF_GATE:
            z = jax.nn.sigmoid(z)
        o_ref[:, cols] = z.astype(o_ref.dtype)


def _inproj_kernel(x_ref, g_ref, sc_ref, sh_ref, w_ref, o_ref, h_scr, *, tn):
    h_scr[...] = _rms_mod(x_ref[...], g_ref[...], sc_ref[0], sh_ref[0]).astype(h_scr.dtype)
    _project_columns(h_scr, w_ref, o_ref, tn)


def _inproj_residual_kernel(tab_ref, x_ref, ys_ref, lpos_ref, gt_ref, g_ref, sc_ref, sh_ref, w_ref,
                            o_ref, xo_ref, h_scr, stage, sems, *, tn, cap):
    f = _expert_residual(tab_ref, ys_ref, lpos_ref, stage, sems, tile_rows=x_ref.shape[0], cap=cap)
    x = x_ref[...] + gt_ref[0] * f
    xo_ref[...] = x
    h_scr[...] = _rms_mod(x, g_ref[...], sc_ref[0], sh_ref[0]).astype(h_scr.dtype)
    _project_columns(h_scr, w_ref, o_ref, tn)


def _inproj(xs, prev, norm_g, mod_l, w, *, tm, tn, n_lat, seq, batch):
    n, d = xs.shape
    width = w.shape[1]
    row = functools.partial(_mod_row, tile=tm, n_lat=n_lat, seq=seq, batch=batch)
    mod = lambda k: pl.BlockSpec((1, 1, d), lambda i, *_: (row(i), 0, k))
    x_spec = pl.BlockSpec((tm, d), lambda i, *_: (i, 0))
    p_spec = pl.BlockSpec((tm, width), lambda i, *_: (i, 0))
    p_shape = jax.ShapeDtypeStruct((n, width), ACT_DTYPE)
    tail_specs = [pl.BlockSpec((1, d), lambda i, *_: (0, 0)), mod(1), mod(0),
                  pl.BlockSpec((d, width), lambda i, *_: (0, 0), pipeline_mode=pl.Buffered(1))]
    grid = (n // tm,)
    h_scratch = pltpu.VMEM((tm, d), MM_DTYPE)
    if prev is None:
        p = pl.pallas_call(functools.partial(_inproj_kernel, tn=tn), grid=grid, in_specs=[x_spec] + tail_specs,
                           out_specs=p_spec, out_shape=p_shape, scratch_shapes=[h_scratch],
                           compiler_params=_cparams(("parallel",)))(xs, norm_g, mod_l, mod_l, w)
        return p, xs
    ys, tab, lpos, mod_prev, cap = prev
    gs = pltpu.PrefetchScalarGridSpec(
        num_scalar_prefetch=1,
        grid=grid,
        in_specs=[x_spec, pl.BlockSpec(memory_space=pl.ANY), pl.BlockSpec((tm, 1), lambda i, *_: (i, 0)),
                  mod(5)] + tail_specs,
        out_specs=[p_spec, x_spec],
        scratch_shapes=[h_scratch, pltpu.VMEM((2, _sorted_rows(tm), d), F32), pltpu.SemaphoreType.DMA((2,))])
    return pl.pallas_call(
        functools.partial(_inproj_residual_kernel, tn=tn, cap=cap),
        grid_spec=gs,
        out_shape=[p_shape, jax.ShapeDtypeStruct((n, d), F32)],
        input_output_aliases={1: 1},
        compiler_params=_cparams(("arbitrary",)),
    )(tab, xs, ys, lpos, mod_prev, norm_g, mod_l, mod_l, w)


def _ret_kernel(lg_ref, *refs, seq, n_ctx, use_rope):
    if n_ctx:
        q_ref, k_ref, v_ref, g_ref, kc_ref, vc_ref, cos_ref, sin_ref, o_ref, kt_scr, sb_scr = refs
    else:
        q_ref, k_ref, v_ref, g_ref, o_ref, kt_scr = refs
    C = RET_CHUNK
    n_chunk = seq // C
    head = pl.program_id(1)
    lgf = lg_ref[0, head]
    lgb = lg_ref[1, head]

    q = q_ref[...].astype(F32)
    k = k_ref[...].astype(F32)
    if use_rope:
        cos = cos_ref[...]
        sin = sin_ref[...]
        q = q * cos + pltpu.roll(q, RET_DK // 2, 1) * sin
        k = k * cos + pltpu.roll(k, RET_DK // 2, 1) * sin
    q = q * (RET_DK ** -0.5)
    kt_scr[...] = k.T

    ri = lax.broadcasted_iota(jnp.int32, (C, C), 0)
    ci = lax.broadcasted_iota(jnp.int32, (C, C), 1)
    rel = (ri - ci).astype(F32)
    dmask = jnp.where(rel > 0.0, jnp.exp(lgf * jnp.maximum(rel, 0.0)),
                      jnp.where(rel < 0.0, jnp.exp(lgb * jnp.maximum(-rel, 0.0)), 2.0))
    icol = lax.broadcasted_iota(jnp.int32, (C, 1), 0).astype(F32)
    jrow = lax.broadcasted_iota(jnp.int32, (1, C), 1).astype(F32)
    qdec_f = jnp.exp(lgf * (icol + 1.0))
    qdec_b = jnp.exp(lgb * (C - icol))
    kdec_f = jnp.exp(lgf * (C - 1.0 - jrow))
    kdec_b = jnp.exp(lgb * jrow)
    zrow = jnp.zeros((1, RET_DV), F32)
    cdec_f = jnp.exp(zrow + lgf * C)
    cdec_b = jnp.exp(zrow + lgb * C)

    def chunk(ref, n):
        return ref[n * C:(n + 1) * C, :]

    def kt_chunk(n):
        return kt_scr[:, n * C:(n + 1) * C]

    if n_ctx:
        kct = kc_ref[...].astype(F32).T
        vcx = vc_ref[...].astype(MM_DTYPE)
        mrow = lax.broadcasted_iota(jnp.int32, (1, n_ctx), 1).astype(F32)
        s_f = _dot((kct * jnp.exp(lgf * (n_ctx - 1.0 - mrow))).astype(MM_DTYPE), vcx)
        s_b = _dot((kct * jnp.exp(lgb * mrow)).astype(MM_DTYPE), vcx)
        sb_scr[n_chunk - 1] = s_b
        for n in range(n_chunk - 1, 0, -1):
            s_b = s_b * cdec_b + _dot((kt_chunk(n) * kdec_b).astype(MM_DTYPE), chunk(v_ref, n).astype(MM_DTYPE))
            sb_scr[n - 1] = s_b

    for n in range(n_chunk):
        qn = q[n * C:(n + 1) * C, :]
        ktn = kt_chunk(n)
        vn = chunk(v_ref, n).astype(MM_DTYPE)
        s = _dot(qn.astype(MM_DTYPE), ktn.astype(MM_DTYPE)) * dmask
        o = _dot(s.astype(MM_DTYPE), vn)
        if n_ctx:
            qcat = jnp.concatenate([qn * qdec_f, qn * qdec_b], axis=1).astype(MM_DTYPE)
            scat = jnp.concatenate([s_f, sb_scr[n]], axis=0).astype(MM_DTYPE)
            o = o + _dot(qcat, scat)
            if n + 1 < n_chunk:
                s_f = s_f * cdec_f + _dot((ktn * kdec_f).astype(MM_DTYPE), vn)
        mu = jnp.mean(o, axis=-1, keepdims=True)
        oc = o - mu
        yn = oc * lax.rsqrt(jnp.mean(oc * oc, axis=-1, keepdims=True) + NORM_EPS)
        o_ref[n * C:(n + 1) * C, :] = (yn * chunk(g_ref, n).astype(F32)).astype(o_ref.dtype)


def _retention_latent(p, log_gamma, cos, sin, *, batch, seq, n_ctx, n_lat):
    n = p.shape[0]
    cb = n_lat // n_ctx
    kq, kv = OFF_K // RET_DK, OFF_V // RET_DV
    gs = pltpu.PrefetchScalarGridSpec(
        num_scalar_prefetch=1,
        grid=(batch, RET_HEADS),
        in_specs=[pl.BlockSpec((seq, RET_DK), lambda b, h, lg: (b, h)),
                  pl.BlockSpec((seq, RET_DK), lambda b, h, lg: (b, kq + h)),
                  pl.BlockSpec((seq, RET_DV), lambda b, h, lg: (b, kv + h)),
                  pl.BlockSpec((seq, RET_DV), lambda b, h, lg: (b, OFF_G // RET_DV + h)),
                  pl.BlockSpec((n_ctx, RET_DK), lambda b, h, lg: (cb + b, kq + h)),
                  pl.BlockSpec((n_ctx, RET_DV), lambda b, h, lg: (cb + b, kv + h)),
                  pl.BlockSpec((seq, RET_DK), lambda b, h, lg: (0, 0)),
                  pl.BlockSpec((seq, RET_DK), lambda b, h, lg: (0, 0))],
        out_specs=pl.BlockSpec((seq, RET_DV), lambda b, h, lg: (b, h)),
        scratch_shapes=[pltpu.VMEM((RET_DK, seq), F32),
                        pltpu.VMEM((seq // RET_CHUNK, RET_DK, RET_DV), F32)])
    return pl.pallas_call(
        functools.partial(_ret_kernel, seq=seq, n_ctx=n_ctx, use_rope=True),
        grid_spec=gs,
        out_shape=jax.ShapeDtypeStruct((n, RET_V), ACT_DTYPE),
        compiler_params=_cparams(("parallel", "parallel")),
    )(log_gamma, p, p, p, p, p, p, cos, sin)


def _retention_ctx(p, log_gamma, y_ret, *, batch, n_ctx, n_lat):
    cb = n_lat // n_ctx
    kq, kv = OFF_K // RET_DK, OFF_V // RET_DV
    gs = pltpu.PrefetchScalarGridSpec(
        num_scalar_prefetch=1,
        grid=(batch, RET_HEADS),
        in_specs=[pl.BlockSpec((n_ctx, RET_DK), lambda b, h, lg: (cb + b, h)),
                  pl.BlockSpec((n_ctx, RET_DK), lambda b, h, lg: (cb + b, kq + h)),
                  pl.BlockSpec((n_ctx, RET_DV), lambda b, h, lg: (cb + b, kv + h)),
                  pl.BlockSpec((n_ctx, RET_DV), lambda b, h, lg: (cb + b, OFF_G // RET_DV + h)),
                  pl.BlockSpec(memory_space=pl.ANY)],
        out_specs=pl.BlockSpec((n_ctx, RET_DV), lambda b, h, lg: (cb + b, h)),
        scratch_shapes=[pltpu.VMEM((RET_DK, n_ctx), F32)])

    def body(lg_ref, q_ref, k_ref, v_ref, g_ref, alias_ref, o_ref, kt_scr):
        del alias_ref
        _ret_kernel(lg_ref, q_ref, k_ref, v_ref, g_ref, o_ref, kt_scr, seq=n_ctx, n_ctx=0, use_rope=False)

    return pl.pallas_call(
        body,
        grid_spec=gs,
        out_shape=jax.ShapeDtypeStruct(y_ret.shape, y_ret.dtype),
        input_output_aliases={5: 0},
        compiler_params=_cparams(("parallel", "parallel")),
    )(log_gamma, p, p, p, p, y_ret)


def _convpool_kernel(*refs, seq, aliased):
    if aliased:
        cb_ref, cc_ref, cx_ref, pi_ref, cw_ref, pw_ref, ps_ref, _, _, yc_ref, yp_ref = refs
    else:
        cb_ref, cc_ref, cx_ref, pi_ref, cw_ref, pw_ref, ps_ref, yc_ref, yp_ref = refs
    grp = pl.program_id(1)
    pb, hb = POOL_BLOCK, POOL_HALO
    n_blk = seq // pb

    w = cw_ref[...]
    tl = lax.broadcasted_iota(jnp.int32, (pb, 1), 0)
    cxu = lambda a, b: cc_ref[a:b, :].astype(F32) * cx_ref[a:b, :].astype(F32)
    for blk in range(n_blk):
        r0 = blk * pb
        u = cxu(r0, r0 + pb)
        before = cxu(r0 - hb, r0)[hb - 1:hb, :] if blk > 0 else 0.0
        after = cxu(r0 + pb, r0 + pb + hb)[0:1, :] if blk + 1 < n_blk else 0.0
        u_prev = jnp.where(tl == 0, before, pltpu.roll(u, 1, 0))
        u_next = jnp.where(tl == pb - 1, after, pltpu.roll(u, pb - 1, 0))
        conv = w[0:1, :] * u_prev + w[1:2, :] * u + w[2:3, :] * u_next
        yc_ref[r0:r0 + pb, :] = (cb_ref[r0:r0 + pb, :].astype(F32) * conv).astype(yc_ref.dtype)

    half = jnp.left_shift(1, grp)

    def band(rows, cols, shift):
        dd = (lax.broadcasted_iota(jnp.int32, (rows, cols), 1) + shift
              - lax.broadcasted_iota(jnp.int32, (rows, cols), 0))
        return jnp.where((dd >= -half) & (dd < half), 1.0, 0.0).astype(MM_DTYPE)

    band_self = band(pb, pb, 0)
    band_prev = band(hb, hb, -hb)
    band_next = band(hb, hb, hb)
    for blk in range(n_blk):
        r0 = blk * pb
        p_blk = pi_ref[r0:r0 + pb, :]
        wsum = _dot(band_self, p_blk.astype(MM_DTYPE))
        top, mid, bot = wsum[:hb], wsum[hb:pb - hb], wsum[pb - hb:]
        if blk > 0:
            top = top + _dot(band_prev, pi_ref[r0 - hb:r0, :].astype(MM_DTYPE))
        if blk + 1 < n_blk:
            bot = bot + _dot(band_next, pi_ref[r0 + pb:r0 + pb + hb, :].astype(MM_DTYPE))
        wsum = jnp.concatenate([top, mid, bot], axis=0)
        tb = r0 + lax.broadcasted_iota(jnp.int32, (pb, 1), 0)
        cnt = (jnp.clip(tb + half, 0, seq) - jnp.clip(tb - half, 0, seq)).astype(F32)
        pooled = wsum / cnt - p_blk.astype(F32)
        mixed = _dot(pooled.astype(MM_DTYPE), pw_ref[0]) * ps_ref[...]
        yp_ref[r0:r0 + pb, :] = mixed.astype(yp_ref.dtype)


def _convpool(p, conv_w, pool_w, pool_scale, prev, *, n_seq, seq, row0):
    n = p.shape[0]
    g128 = lambda off: off // POOL_GDIM
    col = lambda off: (lambda b, g: (row0 + b, g128(off) + g))
    in_specs = [pl.BlockSpec((seq, POOL_GDIM), col(OFF_CB)),
                pl.BlockSpec((seq, POOL_GDIM), col(OFF_CC)),
                pl.BlockSpec((seq, POOL_GDIM), col(OFF_CX)),
                pl.BlockSpec((seq, POOL_GDIM), col(OFF_PI)),
                pl.BlockSpec((conv_w.shape[0], POOL_GDIM), lambda b, g: (0, g)),
                pl.BlockSpec((1, POOL_GDIM, POOL_GDIM), lambda b, g: (g, 0, 0)),
                pl.BlockSpec((1, POOL_GDIM), lambda b, g: (0, g))]
    args = [p, p, p, p, conv_w, pool_w, pool_scale]
    aliases = {}
    if prev is not None:
        in_specs += [pl.BlockSpec(memory_space=pl.ANY), pl.BlockSpec(memory_space=pl.ANY)]
        args += list(prev)
        aliases = {7: 0, 8: 1}
    out_spec = pl.BlockSpec((seq, POOL_GDIM), lambda b, g: (row0 + b, g))
    return pl.pallas_call(
        functools.partial(_convpool_kernel, seq=seq, aliased=prev is not None),
        grid=(n_seq, POOL_GROUPS),
        in_specs=in_specs,
        out_specs=[out_spec, out_spec],
        out_shape=[jax.ShapeDtypeStruct((n, CONV_W), ACT_DTYPE), jax.ShapeDtypeStruct((n, POOL_W), ACT_DTYPE)],
        input_output_aliases=aliases,
        compiler_params=_cparams(("parallel", "parallel")),
    )(*args)


def _first_max(vals):
    top = functools.reduce(jnp.maximum, vals)
    idx = jnp.full(top.shape, len(vals) - 1, jnp.int32)
    for k in range(len(vals) - 2, -1, -1):
        idx = jnp.where(vals[k] == top, k, idx)
    return top, idx


def _route(logits_t):
    row = lambda k: logits_t[k:k + 1, :]
    groups = [row(g) for g in range(N_GROUPS)]
    gmax, gidx = _first_max(groups)
    gtop = 1.0 / sum(jnp.exp(g - gmax) for g in groups)
    experts = []
    for k in range(EXP_PER_GROUP):
        e = row(N_GROUPS + (N_GROUPS - 1) * EXP_PER_GROUP + k)
        for g in range(N_GROUPS - 2, -1, -1):
            e = jnp.where(gidx == g, row(N_GROUPS + g * EXP_PER_GROUP + k), e)
        experts.append(e)
    m1, i1 = _first_max(experts)
    m2, i2 = _first_max([jnp.where(i1 == k, -jnp.inf, e) for k, e in enumerate(experts)])
    e2 = jnp.exp(m2 - m1)
    w1 = gtop / (1.0 + e2)
    w2 = gtop * e2 / (1.0 + e2)
    first_lower = i1 < i2
    lo = jnp.minimum(i1, i2)
    hi = jnp.maximum(i1, i2)
    pair = lo * 3 - jnp.right_shift(lo * (lo - 1), 1) + hi - lo - 1
    bucket = gidx * N_PAIRS + pair
    wlo = jnp.where(first_lower, w1, w2)
    whi = jnp.where(first_lower, w2, w1)
    return bucket, wlo, whi


def _column_to_lanes(col):
    return jnp.broadcast_to(col, (col.shape[0], LANES)).T[0:1, :]


def _row_to_column(row):
    return jnp.broadcast_to(row, (8, row.shape[1])).T[:, 0:1]


def _dot_nt(a, b):
    return lax.dot_general(a, b, (((1,), (1,)), ((), ())), preferred_element_type=F32)


def _bucket_run_copies(read_run, max_rows, make_copy):
    def per_bucket(b, carry_):
        n, local, glob = read_run(b)
        off = jnp.int32(0)
        size = 1 << (max_rows.bit_length() - 1)
        while size >= ROW_ALIGN:
            take = n & size

            @pl.when(take != 0)
            def _(size=size, off=off):
                make_copy(pl.multiple_of(local + off, ROW_ALIGN), pl.multiple_of(glob + off, ROW_ALIGN), size).start()

            off = off + take
            size //= 2
        return carry_

    lax.fori_loop(0, N_BUCKETS, per_bucket, 0)


def _wait_rows(total, max_rows, make_copy):
    size = 1 << (max_rows.bit_length() - 1)
    while size >= ROW_ALIGN:
        @pl.when((total & size) != 0)
        def _(size=size):
            make_copy(size).wait()
        size //= 2


def _exact_bf16_pieces(w):
    a = w.astype(jnp.bfloat16).astype(F32)
    b = (w - a).astype(jnp.bfloat16).astype(F32)
    c = (w - a - b).astype(jnp.bfloat16).astype(F32)
    return a, b, c


def _outproj_kernel(x_ref, yr_ref, yc_ref, yp_ref, g0_ref, g1_ref, g2_ref, gt_ref, sc_ref, sh_ref, ng_ref,
                    wr_ref, wc_ref, wp_ref, wo_ref, wrt_ref, brt_ref,
                    xo_ref, hs_ref, cnt_ref, tab_ref, lpos_ref,
                    h_scr, tab_vmem, tab_smem, sent_smem, carry, row_sem, tab_sem, *, cap):
    i = pl.program_id(0)
    n_steps = pl.num_programs(0)
    slot = lax.rem(i, 2)
    tm, d = x_ref.shape
    ts = h_scr.shape[1]

    def wait_rows(s):
        _wait_rows(sent_smem[s], ts, lambda size: pltpu.make_async_copy(
            h_scr.at[s, pl.ds(0, size)], hs_ref.at[pl.ds(0, size)], row_sem.at[s]))

    @pl.when(i == 0)
    def _():
        carry[...] = jnp.zeros_like(carry)

    @pl.when(i >= 2)
    def _():
        wait_rows(slot)

    def rows_to_routing(rows):
        gate = lambda r: r[rows, :].astype(F32)
        merged = (gate(g0_ref) * _dot(yr_ref[rows, :], wr_ref[...])
                  + gate(g1_ref) * _dot(yc_ref[rows, :], wc_ref[...])
                  + gate(g2_ref) * _dot(yp_ref[rows, :], wp_ref[...]))
        y = _dot(merged.astype(MM_DTYPE), wo_ref[...])
        x = x_ref[rows, :] + gt_ref[0] * y
        xo_ref[rows, :] = x
        h = _rms_mod(x, ng_ref[...], sc_ref[0], sh_ref[0])
        hh, hl = _split_bf16(h)
        s2 = _dot_nt(wrt_ref[...], hh) + _dot_nt(wrt_ref[...], hl)
        logits_t = s2[0:ROUTER_LO_ROW, :] + s2[ROUTER_LO_ROW:2 * ROUTER_LO_ROW, :] + brt_ref[...]
        return (h.astype(MM_DTYPE),) + _route(logits_t)

    sub = tm // OUT_SUBBLOCKS
    parts = [rows_to_routing(slice(a, a + sub)) for a in range(0, tm, sub)]
    h = jnp.concatenate([part[0] for part in parts], axis=0)
    bucket, wlo, whi = (jnp.concatenate([part[k] for part in parts], axis=1) for k in (1, 2, 3))

    brow = lax.broadcasted_iota(jnp.int32, (LANES, tm), 0)
    onehot = brow == bucket
    ones = jnp.where(onehot, 1.0, 0.0)
    r = lax.broadcasted_iota(jnp.int32, (tm, tm), 0)
    c = lax.broadcasted_iota(jnp.int32, (tm, tm), 1)
    tri = jnp.where(r <= c, 1.0, 0.0).astype(jnp.bfloat16)
    incl = _dot(ones.astype(jnp.bfloat16), tri)
    cnt_col = incl[:, tm - 1:tm].astype(jnp.int32)
    units_col = jnp.right_shift(cnt_col + (ROW_ALIGN - 1), ALIGN_SHIFT)
    below = jnp.where(lax.broadcasted_iota(jnp.int32, (LANES, LANES), 0)
                      > lax.broadcasted_iota(jnp.int32, (LANES, LANES), 1), 1.0, 0.0).astype(jnp.bfloat16)
    units_b = jnp.broadcast_to(units_col.astype(F32), (LANES, LANES)).astype(jnp.bfloat16)
    start_col = _dot(below, units_b)[:, 0:1] * float(ROW_ALIGN)
    lpos = jnp.sum(jnp.where(onehot, incl - 1.0 + start_col, 0.0), axis=0, keepdims=True)
    lpos_ref[...] = _row_to_column(lpos).astype(jnp.int32)
    run_col = (units_col * ROW_ALIGN).astype(F32)
    run_len = _column_to_lanes(run_col).astype(jnp.int32)
    local_start = _column_to_lanes(start_col).astype(jnp.int32)
    total = (start_col + run_col)[LANES - 1:LANES, :].astype(jnp.int32)
    srow = lax.broadcasted_iota(jnp.int32, (8, LANES), 0)
    tab = jnp.where(srow == TAB_COUNT, run_len,
                    jnp.where(srow == TAB_LOCAL, local_start,
                              jnp.where(srow == TAB_GLOBAL, carry[...], jnp.where(srow == TAB_TOTAL, total, 0))))
    carry[...] = carry[...] + run_len
    tab_ref[0] = tab
    tab_vmem[...] = tab
    to_smem = pltpu.make_async_copy(tab_vmem, tab_smem, tab_sem)
    to_smem.start()

    srt = lax.broadcasted_iota(jnp.int32, (ts, tm), 0)
    perm = jnp.where(srt == lpos.astype(jnp.int32), 1.0, 0.0).astype(MM_DTYPE)
    pieces = _exact_bf16_pieces(wlo) + _exact_bf16_pieces(whi)
    meta_t = jnp.zeros((LANES, tm), F32)
    for k, piece in enumerate(pieces):
        meta_t = jnp.where(brow == k, piece, meta_t)
    h_scr[slot, :, :d] = _dot(perm, h)
    h_scr[slot, :, d:] = _dot_nt(perm, meta_t.astype(MM_DTYPE))
    to_smem.wait()
    sent_smem[slot] = tab_smem[TAB_TOTAL, 0]

    _bucket_run_copies(
        lambda b: (tab_smem[TAB_COUNT, b], tab_smem[TAB_LOCAL, b], b * cap + tab_smem[TAB_GLOBAL, b]), tm,
        lambda local, glob, size: pltpu.make_async_copy(h_scr.at[slot, pl.ds(local, size)],
                                                        hs_ref.at[pl.ds(glob, size)], row_sem.at[slot]))

    @pl.when(i == n_steps - 1)
    def _():
        cnt_ref[...] = jnp.broadcast_to(carry[...].astype(jnp.int32), cnt_ref.shape)
        wait_rows(slot)

    @pl.when((i == n_steps - 1) & (i >= 1))
    def _():
        wait_rows(1 - slot)


def _sorted_rows(tm):
    return tm + N_BUCKETS * ROW_ALIGN


def _outproj(xs, p, y_ret, y_conv, y_pool, mod_l, norm_g, w_ret, w_conv, w_pool, w_o, w_router, b_router,
             *, tm, cap, n_rows, n_lat, seq, batch):
    n, d = xs.shape
    assert cap >= n_rows + (n_rows // tm) * ROW_ALIGN and cap % ROW_ALIGN == 0
    width = d + LANES
    ts = _sorted_rows(tm)
    row = functools.partial(_mod_row, tile=tm, n_lat=n_lat, seq=seq, batch=batch)
    gate = lambda k: pl.BlockSpec((tm, d), lambda i: (i, OFF_GATE // d + k))
    mod = lambda k: pl.BlockSpec((1, 1, d), lambda i: (row(i), 0, k))
    full = lambda a: pl.BlockSpec(a.shape, lambda i: (0,) * a.ndim)
    return pl.pallas_call(
        functools.partial(_outproj_kernel, cap=cap),
        grid=(n_rows // tm,),
        in_specs=[pl.BlockSpec((tm, d), lambda i: (i, 0)),
                  pl.BlockSpec((tm, RET_V), lambda i: (i, 0)),
                  pl.BlockSpec((tm, CONV_W), lambda i: (i, 0)),
                  pl.BlockSpec((tm, POOL_W), lambda i: (i, 0)),
                  gate(0), gate(1), gate(2),
                  mod(2), mod(4), mod(3),
                  full(norm_g), full(w_ret), full(w_conv), full(w_pool), full(w_o), full(w_router), full(b_router)],
        out_specs=[pl.BlockSpec((tm, d), lambda i: (i, 0)),
                   pl.BlockSpec(memory_space=pl.ANY),
                   pl.BlockSpec((8, LANES), lambda i: (0, 0)),
                   pl.BlockSpec((1, 8, LANES), lambda i: (i, 0, 0)),
                   pl.BlockSpec((tm, 1), lambda i: (i, 0))],
        out_shape=[jax.ShapeDtypeStruct((n, d), F32),
                   jax.ShapeDtypeStruct((N_BUCKETS * cap, width), F32),
                   jax.ShapeDtypeStruct((8, LANES), jnp.int32),
                   jax.ShapeDtypeStruct((n_rows // tm, 8, LANES), jnp.int32),
                   jax.ShapeDtypeStruct((n_rows, 1), jnp.int32)],
        scratch_shapes=[pltpu.VMEM((2, ts, width), F32),
                        pltpu.VMEM((8, LANES), jnp.int32),
                        pltpu.SMEM((8, LANES), jnp.int32),
                        pltpu.SMEM((2,), jnp.int32),
                        pltpu.VMEM((1, LANES), jnp.int32),
                        pltpu.SemaphoreType.DMA((2,)),
                        pltpu.SemaphoreType.DMA(())],
        input_output_aliases={0: 0},
        compiler_params=_cparams(("arbitrary",)),
    )(xs, y_ret, y_conv, y_pool, p, p, p, mod_l, mod_l, mod_l, norm_g, w_ret, w_conv, w_pool, w_o, w_router, b_router)


def _moe_kernel(blk_ref, elo_ref, ehi_ref, valid_ref, hs_ref, w1l_ref, w1h_ref, w3l_ref, w3h_ref, w2l_ref, w2h_ref,
                ys_ref):
    del blk_ref, elo_ref, ehi_ref
    valid = valid_ref[pl.program_id(0)]
    tm, d = ys_ref.shape

    @pl.when(valid > 0)
    def _():
        keep = lax.broadcasted_iota(jnp.int32, (tm, 1), 0) < valid
        meta = jnp.where(keep, hs_ref[:, d:], 0.0)
        h = jnp.where(keep, hs_ref[:, :d], 0.0).astype(MM_DTYPE)
        wlo = meta[:, 0:1] + meta[:, 1:2] + meta[:, 2:3]
        whi = meta[:, 3:4] + meta[:, 4:5] + meta[:, 5:6]

        def expert(w1, w3, w2):
            a = _dot(h, w1[0])
            a = a * jax.nn.sigmoid(a) * _dot(h, w3[0])
            return _dot(a.astype(MM_DTYPE), w2[0])

        y = wlo * expert(w1l_ref, w3l_ref, w2l_ref) + whi * expert(w1h_ref, w3h_ref, w2h_ref)
        ys_ref[...] = y.astype(ACT_DTYPE).astype(F32)


def _moe(blk, elo, ehi, valid, hs, w1, w3, w2, *, tm, d):
    n_sorted, width = hs.shape
    n_work = blk.shape[0]
    up = lambda sel: pl.BlockSpec((1, d, D_FF), lambda s, blk, elo, ehi, valid: ((elo, ehi)[sel][s], 0, 0))
    down = lambda sel: pl.BlockSpec((1, D_FF, d), lambda s, blk, elo, ehi, valid: ((elo, ehi)[sel][s], 0, 0))
    gs = pltpu.PrefetchScalarGridSpec(
        num_scalar_prefetch=4,
        grid=(n_work,),
        in_specs=[pl.BlockSpec((tm, width), lambda s, blk, elo, ehi, valid: (blk[s], 0)),
                  up(0), up(1), up(0), up(1), down(0), down(1)],
        out_specs=pl.BlockSpec((tm, d), lambda s, blk, elo, ehi, valid: (blk[s], 0)))
    return pl.pallas_call(
        _moe_kernel,
        grid_spec=gs,
        out_shape=jax.ShapeDtypeStruct((n_sorted, d), F32),
        compiler_params=_cparams(("arbitrary",)),
    )(blk, elo, ehi, valid, hs, w1, w1, w3, w3, w2, w2)


def _expert_residual(tab_ref, ys_ref, lpos_ref, stage, sems, *, tile_rows, cap):
    i = pl.program_id(0)
    slot = lax.rem(i, 2)
    ts = stage.shape[1]
    entry = lambda tile, row, lane: tab_ref[tile * (TAB_ROWS * LANES) + row * LANES + lane]

    def start(tile, s):
        _bucket_run_copies(
            lambda b: (entry(tile, TAB_COUNT, b), entry(tile, TAB_LOCAL, b), b * cap + entry(tile, TAB_GLOBAL, b)),
            tile_rows,
            lambda local, glob, size: pltpu.make_async_copy(
                ys_ref.at[pl.ds(glob, size)], stage.at[s, pl.ds(local, size)], sems.at[s]))

    @pl.when(i == 0)
    def _():
        start(i, slot)

    @pl.when(i + 1 < pl.num_programs(0))
    def _():
        start(i + 1, 1 - slot)

    total = entry(i, TAB_TOTAL, 0)
    _wait_rows(total, ts, lambda size: pltpu.make_async_copy(
        ys_ref.at[pl.ds(0, size)], stage.at[slot, pl.ds(0, size)], sems.at[slot]))
    lpos = lpos_ref[...]
    unperm = jnp.where(lpos == lax.broadcasted_iota(jnp.int32, (tile_rows, ts), 1), 1.0, 0.0).astype(MM_DTYPE)
    filled = lax.broadcasted_iota(jnp.int32, (ts, 1), 0) < total
    return _dot(unperm, jnp.where(filled, stage[slot], 0.0).astype(MM_DTYPE))


def _final_kernel(tab_ref, x_ref, ys_ref, lpos_ref, gt_ref, g_ref, o_ref, stage, sems, *, cap):
    f = _expert_residual(tab_ref, ys_ref, lpos_ref, stage, sems, tile_rows=x_ref.shape[0], cap=cap)
    x = x_ref[...] + gt_ref[0] * f
    o_ref[...] = x * lax.rsqrt(jnp.mean(x * x, axis=-1, keepdims=True) + NORM_EPS) * g_ref[...]


def _final(xs, moe_out, final_g, *, tm, n_rows, n_lat, seq, batch):
    ys, tab, lpos, mod_l, cap = moe_out
    d = xs.shape[1]
    row = functools.partial(_mod_row, tile=tm, n_lat=n_lat, seq=seq, batch=batch)
    x_spec = pl.BlockSpec((tm, d), lambda i, tab: (i, 0))
    gs = pltpu.PrefetchScalarGridSpec(
        num_scalar_prefetch=1,
        grid=(n_rows // tm,),
        in_specs=[x_spec,
                  pl.BlockSpec(memory_space=pl.ANY),
                  pl.BlockSpec((tm, 1), lambda i, tab: (i, 0)),
                  pl.BlockSpec((1, 1, d), lambda i, tab: (row(i), 0, 5)),
                  pl.BlockSpec((1, d), lambda i, tab: (0, 0))],
        out_specs=x_spec,
        scratch_shapes=[pltpu.VMEM((2, _sorted_rows(tm), d), F32), pltpu.SemaphoreType.DMA((2,))])
    return pl.pallas_call(
        functools.partial(_final_kernel, cap=cap),
        grid_spec=gs,
        out_shape=jax.ShapeDtypeStruct((n_rows, d), F32),
        compiler_params=_cparams(("arbitrary",)),
    )(tab, xs, ys, lpos, mod_l, final_g)


def _rope_tables(seq):
    rows = seq // GRID_W
    row = jnp.repeat(jnp.arange(rows, dtype=F32), GRID_W)
    col = jnp.tile(jnp.arange(GRID_W, dtype=F32), rows)
    n_freq = RET_DK // 4
    inv_freq = ROPE_BASE ** (-jnp.arange(n_freq, dtype=F32) / n_freq)
    ang = jnp.concatenate([row[:, None] * inv_freq[None, :], col[:, None] * inv_freq[None, :]], axis=-1)
    cos, sin = jnp.cos(ang), jnp.sin(ang)
    return jnp.concatenate([cos, cos], axis=-1), jnp.concatenate([-sin, sin], axis=-1)


def _work_tables(counts, *, tm, cap, n_work):
    cnt = counts[0, :N_BUCKETS]
    tiles = (cnt + tm - 1) // tm
    ends = jnp.cumsum(tiles)
    starts = ends - tiles
    item = jnp.arange(n_work, dtype=jnp.int32)
    used = item < ends[-1]
    ref_item = jnp.minimum(item, jnp.maximum(ends[-1] - 1, 0))
    bkt = jnp.minimum(jnp.sum((ref_item[:, None] >= ends[None, :]).astype(jnp.int32), axis=1), N_BUCKETS - 1)
    j = ref_item - starts[bkt]
    valid = jnp.where(used, jnp.clip(cnt[bkt] - j * tm, 0, tm), 0)
    blk = bkt * (cap // tm) + j
    grp, pair = bkt // N_PAIRS, bkt % N_PAIRS
    elo = grp * EXP_PER_GROUP + jnp.asarray(PAIR_LO, jnp.int32)[pair]
    ehi = grp * EXP_PER_GROUP + jnp.asarray(PAIR_HI, jnp.int32)[pair]
    i32 = lambda a: a.astype(jnp.int32)
    return i32(blk), i32(elo), i32(ehi), i32(valid)


def _router_weights(w_rg, b_rg, w_re, b_re):
    w = jnp.concatenate([w_rg, w_re], axis=-1).astype(F32).T
    n_out = w.shape[0]
    assert n_out <= ROUTER_LO_ROW and 2 * ROUTER_LO_ROW <= LANES and EXP_PER_GROUP == 4
    hi = w.astype(jnp.bfloat16)
    lo = (w - hi.astype(F32)).astype(jnp.bfloat16)
    packed = jnp.zeros((LANES, w.shape[1]), jnp.bfloat16)
    packed = packed.at[:n_out].set(hi).at[ROUTER_LO_ROW:ROUTER_LO_ROW + n_out].set(lo)
    bias = jnp.zeros((ROUTER_LO_ROW, 1), F32).at[:n_out, 0].set(jnp.concatenate([b_rg, b_re]).astype(F32))
    return packed, bias


def _pick_tile(n, want, *also):
    t = want
    while n % t or any(a % t for a in also):
        t //= 2
    return t


def kernel(x, c, ctx, c_ctx, w_ada, b_ada, norm1, norm2, w_in, ret_decay, conv_w, pool_w, pool_scale, w_ret_out,
           w_conv_out, w_pool_out, w_o, w_rg, b_rg, w_re, b_re, w1, w3, w2, final_norm):
    batch, seq, d = x.shape
    n_ctx = ctx.shape[1]
    depth = w_ada.shape[0]
    n_lat, n_c = batch * seq, batch * n_ctx
    n = n_lat + n_c
    assert POOL_WINDOWS == (2, 4, 8, 16) and POOL_HALO >= max(POOL_WINDOWS) // 2
    assert seq % RET_CHUNK == 0 and n_ctx == RET_CHUNK and seq % GRID_W == 0 and n_lat % n_ctx == 0

    tm_out = _pick_tile(seq, 512, n_c)
    tm_moe = 256

    xs = jnp.concatenate([x.reshape(n_lat, d), ctx.reshape(n_c, d)], axis=0)
    mod_rows = -(-(batch + 1) // 8) * 8
    cond = jnp.zeros((mod_rows, d), F32).at[:batch].set(c).at[batch].set(c_ctx)
    mod = _ada(cond, w_ada, b_ada).reshape(depth, mod_rows, 1, N_MOD * d)
    cos, sin = _rope_tables(seq)
    log_gamma = jax.nn.log_sigmoid(ret_decay.astype(F32))

    mm = lambda a: a.astype(MM_DTYPE)
    prev = None
    for l in range(depth):
        last = l == depth - 1
        rows = n_lat if last else n
        mod_l = mod[l]
        p, xs = _inproj(xs, prev, norm1[l][None], mod_l, mm(w_in[l]), tm=tm_out, tn=1024,
                        n_lat=n_lat, seq=seq, batch=batch)
        y_ret = _retention_latent(p, log_gamma[l], cos, sin, batch=batch, seq=seq, n_ctx=n_ctx, n_lat=n_lat)
        y_conv, y_pool = _convpool(p, conv_w[l], mm(pool_w[l]), pool_scale[l][None], None,
                                   n_seq=batch, seq=seq, row0=0)
        if not last:
            y_ret = _retention_ctx(p, log_gamma[l], y_ret, batch=batch, n_ctx=n_ctx, n_lat=n_lat)
            y_conv, y_pool = _convpool(p, conv_w[l], mm(pool_w[l]), pool_scale[l][None], (y_conv, y_pool),
                                       n_seq=batch, seq=n_ctx, row0=n_lat // n_ctx)
        w_router, b_router = _router_weights(w_rg[l], b_rg[l], w_re[l], b_re[l])
        n_tiles = rows // tm_out
        cap = -(-(rows + n_tiles * ROW_ALIGN) // tm_moe) * tm_moe
        max_sorted = rows + n_tiles * N_BUCKETS * (ROW_ALIGN - 1)
        xs, hs, counts, tab, lpos = _outproj(xs, p, y_ret, y_conv, y_pool, mod_l, norm2[l][None], mm(w_ret_out[l]),
                                             mm(w_conv_out[l]), mm(w_pool_out[l]), mm(w_o[l]), w_router, b_router,
                                             tm=tm_out, cap=cap, n_rows=rows, n_lat=n_lat, seq=seq, batch=batch)
        blk, elo, ehi, valid = _work_tables(counts, tm=tm_moe, cap=cap, n_work=-(-max_sorted // tm_moe) + N_BUCKETS)
        ys = _moe(blk, elo, ehi, valid, hs, mm(w1[l]), mm(w3[l]), mm(w2[l]), tm=tm_moe, d=d)
        prev = (ys, tab[:, :TAB_ROWS, :].reshape(-1), lpos, mod_l, cap)
    out = _final(xs, prev, final_norm[None], tm=tm_out, n_rows=n_lat, n_lat=n_lat, seq=seq, batch=batch)
    return out.reshape(batch, seq, d)
```

```python
import functools

import numpy as np
import jax
import jax.numpy as jnp
from jax import lax
from jax.experimental import pallas as pl
from jax.experimental.pallas import tpu as pltpu

F32 = jnp.float32
MM_DTYPE = jnp.bfloat16
ACT_DTYPE = jnp.bfloat16

NORM_EPS = 1e-6
GRID_W = 64
ROPE_BASE = 10000.0
N_MOD = 6

RET_HEADS = 4
RET_DK = 128
RET_DV = 256
RET_QK = RET_HEADS * RET_DK
RET_V = RET_HEADS * RET_DV
RET_CHUNK = 256

CONV_W = 512
POOL_WINDOWS = (2, 4, 8, 16)
POOL_GROUPS = 4
POOL_GDIM = 128
POOL_W = POOL_GROUPS * POOL_GDIM
POOL_BLOCK = 256
POOL_HALO = 16

N_GROUPS = 4
EXP_PER_GROUP = 4
N_EXPERTS = N_GROUPS * EXP_PER_GROUP
D_FF = 512
PAIR_LO = (0, 0, 0, 1, 1, 2)
PAIR_HI = (1, 2, 3, 2, 3, 3)
N_PAIRS = len(PAIR_LO)
N_BUCKETS = N_GROUPS * N_PAIRS

LANES = 128
OUT_SUBBLOCKS = 2
ROW_ALIGN = 8
ALIGN_SHIFT = 3
TAB_COUNT, TAB_LOCAL, TAB_GLOBAL, TAB_TOTAL, TAB_ROWS = 0, 1, 2, 3, 4
ROUTER_LO_ROW = 32

OFF_Q = 0
OFF_K = OFF_Q + RET_QK
OFF_V = OFF_K + RET_QK
OFF_G = OFF_V + RET_V
OFF_CB = OFF_G + RET_V
OFF_CC = OFF_CB + CONV_W
OFF_CX = OFF_CC + CONV_W
OFF_PI = OFF_CX + CONV_W
OFF_GATE = OFF_PI + POOL_W

VMEM_LIMIT = 56 * 1024 * 1024


def _cparams(sem):
    return pltpu.CompilerParams(dimension_semantics=sem, vmem_limit_bytes=VMEM_LIMIT)


def _split_bf16(a):
    hi = a.astype(jnp.bfloat16)
    lo = (a - hi.astype(F32)).astype(jnp.bfloat16)
    return hi, lo


def _dot(a, b):
    return jnp.dot(a, b, preferred_element_type=F32)


def _dot3(a, b):
    ah, al = _split_bf16(a)
    bh, bl = _split_bf16(b)
    return _dot(ah, bh) + _dot(ah, bl) + _dot(al, bh)


def _layer_spec(layer, block, index, **kw):
    return pl.BlockSpec((None,) + tuple(block), lambda *a: (layer,) + tuple(index(*a)), **kw)


def _mod_row(i, tile, n_lat, seq, batch):
    return jnp.where(i < n_lat // tile, (i * tile) // seq, batch)


def _ada_kernel(c_ref, w_ref, b_ref, o_ref):
    cv = c_ref[...]
    s = cv * jax.nn.sigmoid(cv)
    o_ref[0] = _dot3(s, w_ref[0]) + b_ref[0]


def _ada(cond, w_ada, b_ada):
    depth, d, width = w_ada.shape
    rows = cond.shape[0]
    tn = 512
    return pl.pallas_call(
        _ada_kernel,
        grid=(depth, width // tn),
        in_specs=[pl.BlockSpec((rows, d), lambda l, j: (0, 0)),
                  pl.BlockSpec((1, d, tn), lambda l, j: (l, 0, j)),
                  pl.BlockSpec((1, 1, tn), lambda l, j: (l, 0, j))],
        out_specs=pl.BlockSpec((1, rows, tn), lambda l, j: (l, 0, j)),
        out_shape=jax.ShapeDtypeStruct((depth, rows, width), F32),
        compiler_params=_cparams(("parallel", "parallel")),
    )(cond, w_ada, b_ada.reshape(depth, 1, width))


def _rms_mod(x, g, sc, sh):
    ms = jnp.mean(x * x, axis=-1, keepdims=True)
    return x * lax.rsqrt(ms + NORM_EPS) * g * (1.0 + sc) + sh


def _project_columns(h_scr, w_ref, o_ref, tn):
    for j in range(w_ref.shape[1] // tn):
        cols = slice(j * tn, (j + 1) * tn)
        o_ref[:, cols] = _dot(h_scr[...], w_ref[:, cols]).astype(o_ref.dtype)


def _inproj_kernel(x_ref, g_ref, sc_ref, sh_ref, w_ref, o_ref, h_scr, *, tn):
    h_scr[...] = _rms_mod(x_ref[...], g_ref[...], sc_ref[0], sh_ref[0]).astype(h_scr.dtype)
    _project_columns(h_scr, w_ref, o_ref, tn)


def _inproj_residual_kernel(tab_ref, x_ref, ys_ref, lpos_ref, gt_ref, g_ref, sc_ref, sh_ref, w_ref,
                            o_ref, xo_ref, h_scr, stage, sems, *, tn, cap):
    f = _expert_residual(tab_ref, ys_ref, lpos_ref, stage, sems, tile_rows=x_ref.shape[0], cap=cap)
    x = x_ref[...] + gt_ref[0] * f
    xo_ref[...] = x
    h_scr[...] = _rms_mod(x, g_ref[...], sc_ref[0], sh_ref[0]).astype(h_scr.dtype)
    _project_columns(h_scr, w_ref, o_ref, tn)


def _inproj(xs, prev, norm_g, mod, w, *, layer, tm, tn, n_lat, seq, batch):
    n, d = xs.shape
    width = w.shape[-1]
    row = functools.partial(_mod_row, tile=tm, n_lat=n_lat, seq=seq, batch=batch)
    mod_spec = lambda lyr, k: _layer_spec(lyr, (1, 1, d), lambda i, *_: (row(i), 0, k))
    x_spec = pl.BlockSpec((tm, d), lambda i, *_: (i, 0))
    p_spec = pl.BlockSpec((tm, width), lambda i, *_: (i, 0))
    p_shape = jax.ShapeDtypeStruct((n, width), ACT_DTYPE)
    tail_specs = [_layer_spec(layer, (1, d), lambda i, *_: (0, 0)),
                  mod_spec(layer, 1), mod_spec(layer, 0),
                  _layer_spec(layer, (d, width), lambda i, *_: (0, 0), pipeline_mode=pl.Buffered(1))]
    grid = (n // tm,)
    h_scratch = pltpu.VMEM((tm, d), MM_DTYPE)
    if prev is None:
        p = pl.pallas_call(functools.partial(_inproj_kernel, tn=tn), grid=grid, in_specs=[x_spec] + tail_specs,
                           out_specs=p_spec, out_shape=p_shape, scratch_shapes=[h_scratch],
                           compiler_params=_cparams(("parallel",)))(xs, norm_g, mod, mod, w)
        return p, xs
    ys, tab, lpos, cap = prev
    gs = pltpu.PrefetchScalarGridSpec(
        num_scalar_prefetch=1,
        grid=grid,
        in_specs=[x_spec, pl.BlockSpec(memory_space=pl.ANY), pl.BlockSpec((tm, 1), lambda i, *_: (i, 0)),
                  mod_spec(layer - 1, 5)] + tail_specs,
        out_specs=[p_spec, x_spec],
        scratch_shapes=[h_scratch, pltpu.VMEM((2, _sorted_rows(tm), d), F32), pltpu.SemaphoreType.DMA((2,))])
    return pl.pallas_call(
        functools.partial(_inproj_residual_kernel, tn=tn, cap=cap),
        grid_spec=gs,
        out_shape=[p_shape, jax.ShapeDtypeStruct((n, d), F32)],
        input_output_aliases={1: 1},
        compiler_params=_cparams(("arbitrary",)),
    )(tab, xs, ys, lpos, mod, norm_g, mod, mod, w)


def _ret_kernel(lg_ref, *refs, seq, n_ctx, use_rope):
    if n_ctx:
        q_ref, k_ref, v_ref, g_ref, kc_ref, vc_ref, cos_ref, sin_ref, o_ref, kt_scr, sb_scr = refs
    else:
        q_ref, k_ref, v_ref, g_ref, o_ref, kt_scr = refs
    C = RET_CHUNK
    n_chunk = seq // C
    head = pl.program_id(1)
    lgf = lg_ref[0, head]
    lgb = lg_ref[1, head]

    q = q_ref[...].astype(F32)
    k = k_ref[...].astype(F32)
    if use_rope:
        cos = cos_ref[...]
        sin = sin_ref[...]
        q = q * cos + pltpu.roll(q, RET_DK // 2, 1) * sin
        k = k * cos + pltpu.roll(k, RET_DK // 2, 1) * sin
    q = q * (RET_DK ** -0.5)
    kt_scr[...] = k.T

    ri = lax.broadcasted_iota(jnp.int32, (C, C), 0)
    ci = lax.broadcasted_iota(jnp.int32, (C, C), 1)
    rel = (ri - ci).astype(F32)
    dmask = jnp.where(rel > 0.0, jnp.exp(lgf * jnp.maximum(rel, 0.0)),
                      jnp.where(rel < 0.0, jnp.exp(lgb * jnp.maximum(-rel, 0.0)), 2.0))
    icol = lax.broadcasted_iota(jnp.int32, (C, 1), 0).astype(F32)
    jrow = lax.broadcasted_iota(jnp.int32, (1, C), 1).astype(F32)
    qdec_f = jnp.exp(lgf * (icol + 1.0))
    qdec_b = jnp.exp(lgb * (C - icol))
    kdec_f = jnp.exp(lgf * (C - 1.0 - jrow))
    kdec_b = jnp.exp(lgb * jrow)
    zrow = jnp.zeros((1, RET_DV), F32)
    cdec_f = jnp.exp(zrow + lgf * C)
    cdec_b = jnp.exp(zrow + lgb * C)

    def chunk(ref, n):
        return ref[n * C:(n + 1) * C, :]

    def kt_chunk(n):
        return kt_scr[:, n * C:(n + 1) * C]

    if n_ctx:
        kct = kc_ref[...].astype(F32).T
        vcx = vc_ref[...].astype(MM_DTYPE)
        mrow = lax.broadcasted_iota(jnp.int32, (1, n_ctx), 1).astype(F32)
        s_f = _dot((kct * jnp.exp(lgf * (n_ctx - 1.0 - mrow))).astype(MM_DTYPE), vcx)
        s_b = _dot((kct * jnp.exp(lgb * mrow)).astype(MM_DTYPE), vcx)
        sb_scr[n_chunk - 1] = s_b
        for n in range(n_chunk - 1, 0, -1):
            s_b = s_b * cdec_b + _dot((kt_chunk(n) * kdec_b).astype(MM_DTYPE), chunk(v_ref, n).astype(MM_DTYPE))
            sb_scr[n - 1] = s_b

    for n in range(n_chunk):
        qn = q[n * C:(n + 1) * C, :]
        ktn = kt_chunk(n)
        vn = chunk(v_ref, n).astype(MM_DTYPE)
        s = _dot(qn.astype(MM_DTYPE), ktn.astype(MM_DTYPE)) * dmask
        o = _dot(s.astype(MM_DTYPE), vn)
        if n_ctx:
            qcat = jnp.concatenate([qn * qdec_f, qn * qdec_b], axis=1).astype(MM_DTYPE)
            scat = jnp.concatenate([s_f, sb_scr[n]], axis=0).astype(MM_DTYPE)
            o = o + _dot(qcat, scat)
            if n + 1 < n_chunk:
                s_f = s_f * cdec_f + _dot((ktn * kdec_f).astype(MM_DTYPE), vn)
        mu = jnp.mean(o, axis=-1, keepdims=True)
        oc = o - mu
        yn = oc * lax.rsqrt(jnp.mean(oc * oc, axis=-1, keepdims=True) + NORM_EPS)
        gn = chunk(g_ref, n).astype(F32)
        o_ref[n * C:(n + 1) * C, :] = (yn * (gn * jax.nn.sigmoid(gn))).astype(o_ref.dtype)


def _retention_latent(p, log_gamma, cos, sin, *, batch, seq, n_ctx, n_lat):
    n = p.shape[0]
    cb = n_lat // n_ctx
    kq, kv = OFF_K // RET_DK, OFF_V // RET_DV
    gs = pltpu.PrefetchScalarGridSpec(
        num_scalar_prefetch=1,
        grid=(batch, RET_HEADS),
        in_specs=[pl.BlockSpec((seq, RET_DK), lambda b, h, lg: (b, h)),
                  pl.BlockSpec((seq, RET_DK), lambda b, h, lg: (b, kq + h)),
                  pl.BlockSpec((seq, RET_DV), lambda b, h, lg: (b, kv + h)),
                  pl.BlockSpec((seq, RET_DV), lambda b, h, lg: (b, OFF_G // RET_DV + h)),
                  pl.BlockSpec((n_ctx, RET_DK), lambda b, h, lg: (cb + b, kq + h)),
                  pl.BlockSpec((n_ctx, RET_DV), lambda b, h, lg: (cb + b, kv + h)),
                  pl.BlockSpec((seq, RET_DK), lambda b, h, lg: (0, 0)),
                  pl.BlockSpec((seq, RET_DK), lambda b, h, lg: (0, 0))],
        out_specs=pl.BlockSpec((seq, RET_DV), lambda b, h, lg: (b, h)),
        scratch_shapes=[pltpu.VMEM((RET_DK, seq), F32),
                        pltpu.VMEM((seq // RET_CHUNK, RET_DK, RET_DV), F32)])
    return pl.pallas_call(
        functools.partial(_ret_kernel, seq=seq, n_ctx=n_ctx, use_rope=True),
        grid_spec=gs,
        out_shape=jax.ShapeDtypeStruct((n, RET_V), ACT_DTYPE),
        compiler_params=_cparams(("parallel", "parallel")),
    )(log_gamma, p, p, p, p, p, p, cos, sin)


def _retention_ctx(p, log_gamma, y_ret, *, batch, n_ctx, n_lat):
    cb = n_lat // n_ctx
    kq, kv = OFF_K // RET_DK, OFF_V // RET_DV
    gs = pltpu.PrefetchScalarGridSpec(
        num_scalar_prefetch=1,
        grid=(batch, RET_HEADS),
        in_specs=[pl.BlockSpec((n_ctx, RET_DK), lambda b, h, lg: (cb + b, h)),
                  pl.BlockSpec((n_ctx, RET_DK), lambda b, h, lg: (cb + b, kq + h)),
                  pl.BlockSpec((n_ctx, RET_DV), lambda b, h, lg: (cb + b, kv + h)),
                  pl.BlockSpec((n_ctx, RET_DV), lambda b, h, lg: (cb + b, OFF_G // RET_DV + h)),
                  pl.BlockSpec(memory_space=pl.ANY)],
        out_specs=pl.BlockSpec((n_ctx, RET_DV), lambda b, h, lg: (cb + b, h)),
        scratch_shapes=[pltpu.VMEM((RET_DK, n_ctx), F32)])

    def body(lg_ref, q_ref, k_ref, v_ref, g_ref, alias_ref, o_ref, kt_scr):
        del alias_ref
        _ret_kernel(lg_ref, q_ref, k_ref, v_ref, g_ref, o_ref, kt_scr, seq=n_ctx, n_ctx=0, use_rope=False)

    return pl.pallas_call(
        body,
        grid_spec=gs,
        out_shape=jax.ShapeDtypeStruct(y_ret.shape, y_ret.dtype),
        input_output_aliases={5: 0},
        compiler_params=_cparams(("parallel", "parallel")),
    )(log_gamma, p, p, p, p, y_ret)


def _convpool_kernel(*refs, seq, aliased):
    if aliased:
        cb_ref, cc_ref, cx_ref, pi_ref, cw_ref, pw_ref, ps_ref, _, _, yc_ref, yp_ref = refs
    else:
        cb_ref, cc_ref, cx_ref, pi_ref, cw_ref, pw_ref, ps_ref, yc_ref, yp_ref = refs
    grp = pl.program_id(1)
    pb, hb = POOL_BLOCK, POOL_HALO
    n_blk = seq // pb

    w = cw_ref[...]
    tl = lax.broadcasted_iota(jnp.int32, (pb, 1), 0)
    cxu = lambda a, b: cc_ref[a:b, :].astype(F32) * cx_ref[a:b, :].astype(F32)
    for blk in range(n_blk):
        r0 = blk * pb
        u = cxu(r0, r0 + pb)
        before = cxu(r0 - hb, r0)[hb - 1:hb, :] if blk > 0 else 0.0
        after = cxu(r0 + pb, r0 + pb + hb)[0:1, :] if blk + 1 < n_blk else 0.0
        u_prev = jnp.where(tl == 0, before, pltpu.roll(u, 1, 0))
        u_next = jnp.where(tl == pb - 1, after, pltpu.roll(u, pb - 1, 0))
        conv = w[0:1, :] * u_prev + w[1:2, :] * u + w[2:3, :] * u_next
        yc_ref[r0:r0 + pb, :] = (cb_ref[r0:r0 + pb, :].astype(F32) * conv).astype(yc_ref.dtype)

    half = jnp.left_shift(1, grp)

    def band(rows, cols, shift):
        dd = (lax.broadcasted_iota(jnp.int32, (rows, cols), 1) + shift
              - lax.broadcasted_iota(jnp.int32, (rows, cols), 0))
        return jnp.where((dd >= -half) & (dd < half), 1.0, 0.0).astype(MM_DTYPE)

    band_self = band(pb, pb, 0)
    band_prev = band(hb, hb, -hb)
    band_next = band(hb, hb, hb)
    for blk in range(n_blk):
        r0 = blk * pb
        p_blk = pi_ref[r0:r0 + pb, :]
        wsum = _dot(band_self, p_blk.astype(MM_DTYPE))
        top, mid, bot = wsum[:hb], wsum[hb:pb - hb], wsum[pb - hb:]
        if blk > 0:
            top = top + _dot(band_prev, pi_ref[r0 - hb:r0, :].astype(MM_DTYPE))
        if blk + 1 < n_blk:
            bot = bot + _dot(band_next, pi_ref[r0 + pb:r0 + pb + hb, :].astype(MM_DTYPE))
        wsum = jnp.concatenate([top, mid, bot], axis=0)
        tb = r0 + lax.broadcasted_iota(jnp.int32, (pb, 1), 0)
        cnt = (jnp.clip(tb + half, 0, seq) - jnp.clip(tb - half, 0, seq)).astype(F32)
        pooled = wsum / cnt - p_blk.astype(F32)
        mixed = _dot(pooled.astype(MM_DTYPE), pw_ref[0]) * ps_ref[...]
        yp_ref[r0:r0 + pb, :] = mixed.astype(yp_ref.dtype)


def _convpool(p, conv_w, pool_w, pool_scale, prev, *, layer, n_seq, seq, row0):
    n = p.shape[0]
    g128 = lambda off: off // POOL_GDIM
    col = lambda off: (lambda b, g: (row0 + b, g128(off) + g))
    in_specs = [pl.BlockSpec((seq, POOL_GDIM), col(OFF_CB)),
                pl.BlockSpec((seq, POOL_GDIM), col(OFF_CC)),
                pl.BlockSpec((seq, POOL_GDIM), col(OFF_CX)),
                pl.BlockSpec((seq, POOL_GDIM), col(OFF_PI)),
                _layer_spec(layer, (conv_w.shape[1], POOL_GDIM), lambda b, g: (0, g)),
                _layer_spec(layer, (1, POOL_GDIM, POOL_GDIM), lambda b, g: (g, 0, 0)),
                _layer_spec(layer, (1, POOL_GDIM), lambda b, g: (0, g))]
    args = [p, p, p, p, conv_w, pool_w, pool_scale]
    aliases = {}
    if prev is not None:
        in_specs += [pl.BlockSpec(memory_space=pl.ANY), pl.BlockSpec(memory_space=pl.ANY)]
        args += list(prev)
        aliases = {7: 0, 8: 1}
    out_spec = pl.BlockSpec((seq, POOL_GDIM), lambda b, g: (row0 + b, g))
    return pl.pallas_call(
        functools.partial(_convpool_kernel, seq=seq, aliased=prev is not None),
        grid=(n_seq, POOL_GROUPS),
        in_specs=in_specs,
        out_specs=[out_spec, out_spec],
        out_shape=[jax.ShapeDtypeStruct((n, CONV_W), ACT_DTYPE), jax.ShapeDtypeStruct((n, POOL_W), ACT_DTYPE)],
        input_output_aliases=aliases,
        compiler_params=_cparams(("parallel", "parallel")),
    )(*args)


def _first_max(vals):
    top = functools.reduce(jnp.maximum, vals)
    idx = jnp.full(top.shape, len(vals) - 1, jnp.int32)
    for k in range(len(vals) - 2, -1, -1):
        idx = jnp.where(vals[k] == top, k, idx)
    return top, idx


def _route(logits_t):
    row = lambda k: logits_t[k:k + 1, :]
    groups = [row(g) for g in range(N_GROUPS)]
    gmax, gidx = _first_max(groups)
    gtop = 1.0 / sum(jnp.exp(g - gmax) for g in groups)
    experts = []
    for k in range(EXP_PER_GROUP):
        e = row(N_GROUPS + (N_GROUPS - 1) * EXP_PER_GROUP + k)
        for g in range(N_GROUPS - 2, -1, -1):
            e = jnp.where(gidx == g, row(N_GROUPS + g * EXP_PER_GROUP + k), e)
        experts.append(e)
    m1, i1 = _first_max(experts)
    m2, i2 = _first_max([jnp.where(i1 == k, -jnp.inf, e) for k, e in enumerate(experts)])
    e2 = jnp.exp(m2 - m1)
    w1 = gtop / (1.0 + e2)
    w2 = gtop * e2 / (1.0 + e2)
    first_lower = i1 < i2
    lo = jnp.minimum(i1, i2)
    hi = jnp.maximum(i1, i2)
    pair = lo * 3 - jnp.right_shift(lo * (lo - 1), 1) + hi - lo - 1
    bucket = gidx * N_PAIRS + pair
    wlo = jnp.where(first_lower, w1, w2)
    whi = jnp.where(first_lower, w2, w1)
    return bucket, wlo, whi


def _column_to_lanes(col):
    return jnp.broadcast_to(col, (col.shape[0], LANES)).T[0:1, :]


def _row_to_column(row):
    return jnp.broadcast_to(row, (8, row.shape[1])).T[:, 0:1]


def _dot_nt(a, b):
    return lax.dot_general(a, b, (((1,), (1,)), ((), ())), preferred_element_type=F32)


def _bucket_run_copies(read_run, max_rows, make_copy):
    def per_bucket(b, carry_):
        n, local, glob = read_run(b)
        off = jnp.int32(0)
        size = 1 << (max_rows.bit_length() - 1)
        while size >= ROW_ALIGN:
            take = n & size

            @pl.when(take != 0)
            def _(size=size, off=off):
                make_copy(pl.multiple_of(local + off, ROW_ALIGN), pl.multiple_of(glob + off, ROW_ALIGN), size).start()

            off = off + take
            size //= 2
        return carry_

    lax.fori_loop(0, N_BUCKETS, per_bucket, 0)


def _wait_rows(total, max_rows, make_copy):
    size = 1 << (max_rows.bit_length() - 1)
    while size >= ROW_ALIGN:
        @pl.when((total & size) != 0)
        def _(size=size):
            make_copy(size).wait()
        size //= 2


def _exact_bf16_pieces(w):
    a = w.astype(jnp.bfloat16).astype(F32)
    b = (w - a).astype(jnp.bfloat16).astype(F32)
    c = (w - a - b).astype(jnp.bfloat16).astype(F32)
    return a, b, c


def _outproj_kernel(x_ref, yr_ref, yc_ref, yp_ref, g0_ref, g1_ref, g2_ref, gt_ref, sc_ref, sh_ref, ng_ref,
                    wr_ref, wc_ref, wp_ref, wo_ref, wrt_ref, brt_ref,
                    xo_ref, hs_ref, cnt_ref, tab_ref, lpos_ref,
                    h_scr, tab_vmem, tab_smem, sent_smem, carry, row_sem, tab_sem, *, cap):
    i = pl.program_id(0)
    n_steps = pl.num_programs(0)
    slot = lax.rem(i, 2)
    tm, d = x_ref.shape
    ts = h_scr.shape[1]

    def wait_rows(s):
        _wait_rows(sent_smem[s], ts, lambda size: pltpu.make_async_copy(
            h_scr.at[s, pl.ds(0, size)], hs_ref.at[pl.ds(0, size)], row_sem.at[s]))

    @pl.when(i == 0)
    def _():
        carry[...] = jnp.zeros_like(carry)

    @pl.when(i >= 2)
    def _():
        wait_rows(slot)

    def rows_to_routing(rows):
        gate = lambda r: jax.nn.sigmoid(r[rows, :].astype(F32))
        merged = (gate(g0_ref) * _dot(yr_ref[rows, :], wr_ref[...])
                  + gate(g1_ref) * _dot(yc_ref[rows, :], wc_ref[...])
                  + gate(g2_ref) * _dot(yp_ref[rows, :], wp_ref[...]))
        y = _dot(merged.astype(MM_DTYPE), wo_ref[...])
        x = x_ref[rows, :] + gt_ref[0] * y
        xo_ref[rows, :] = x
        h = _rms_mod(x, ng_ref[...], sc_ref[0], sh_ref[0])
        hh, hl = _split_bf16(h)
        s2 = _dot_nt(wrt_ref[...], hh) + _dot_nt(wrt_ref[...], hl)
        logits_t = s2[0:ROUTER_LO_ROW, :] + s2[ROUTER_LO_ROW:2 * ROUTER_LO_ROW, :] + brt_ref[...]
        return (h.astype(MM_DTYPE),) + _route(logits_t)

    sub = tm // OUT_SUBBLOCKS
    parts = [rows_to_routing(slice(a, a + sub)) for a in range(0, tm, sub)]
    h = jnp.concatenate([part[0] for part in parts], axis=0)
    bucket, wlo, whi = (jnp.concatenate([part[k] for part in parts], axis=1) for k in (1, 2, 3))

    brow = lax.broadcasted_iota(jnp.int32, (LANES, tm), 0)
    onehot = brow == bucket
    ones = jnp.where(onehot, 1.0, 0.0)
    r = lax.broadcasted_iota(jnp.int32, (tm, tm), 0)
    c = lax.broadcasted_iota(jnp.int32, (tm, tm), 1)
    tri = jnp.where(r <= c, 1.0, 0.0).astype(jnp.bfloat16)
    incl = _dot(ones.astype(jnp.bfloat16), tri)
    cnt_col = incl[:, tm - 1:tm].astype(jnp.int32)
    units_col = jnp.right_shift(cnt_col + (ROW_ALIGN - 1), ALIGN_SHIFT)
    below = jnp.where(lax.broadcasted_iota(jnp.int32, (LANES, LANES), 0)
                      > lax.broadcasted_iota(jnp.int32, (LANES, LANES), 1), 1.0, 0.0).astype(jnp.bfloat16)
    units_b = jnp.broadcast_to(units_col.astype(F32), (LANES, LANES)).astype(jnp.bfloat16)
    start_col = _dot(below, units_b)[:, 0:1] * float(ROW_ALIGN)
    lpos = jnp.sum(jnp.where(onehot, incl - 1.0 + start_col, 0.0), axis=0, keepdims=True)
    lpos_ref[...] = _row_to_column(lpos).astype(jnp.int32)
    run_col = (units_col * ROW_ALIGN).astype(F32)
    run_len = _column_to_lanes(run_col).astype(jnp.int32)
    local_start = _column_to_lanes(start_col).astype(jnp.int32)
    total = (start_col + run_col)[LANES - 1:LANES, :].astype(jnp.int32)
    srow = lax.broadcasted_iota(jnp.int32, (8, LANES), 0)
    tab = jnp.where(srow == TAB_COUNT, run_len,
                    jnp.where(srow == TAB_LOCAL, local_start,
                              jnp.where(srow == TAB_GLOBAL, carry[...], jnp.where(srow == TAB_TOTAL, total, 0))))
    carry[...] = carry[...] + run_len
    tab_ref[0] = tab
    tab_vmem[...] = tab
    to_smem = pltpu.make_async_copy(tab_vmem, tab_smem, tab_sem)
    to_smem.start()

    srt = lax.broadcasted_iota(jnp.int32, (ts, tm), 0)
    perm = jnp.where(srt == lpos.astype(jnp.int32), 1.0, 0.0).astype(MM_DTYPE)
    pieces = _exact_bf16_pieces(wlo) + _exact_bf16_pieces(whi)
    meta_t = jnp.zeros((LANES, tm), F32)
    for k, piece in enumerate(pieces):
        meta_t = jnp.where(brow == k, piece, meta_t)
    h_scr[slot, :, :d] = _dot(perm, h)
    h_scr[slot, :, d:] = _dot_nt(perm, meta_t.astype(MM_DTYPE))
    to_smem.wait()
    sent_smem[slot] = tab_smem[TAB_TOTAL, 0]

    _bucket_run_copies(
        lambda b: (tab_smem[TAB_COUNT, b], tab_smem[TAB_LOCAL, b], b * cap + tab_smem[TAB_GLOBAL, b]), tm,
        lambda local, glob, size: pltpu.make_async_copy(h_scr.at[slot, pl.ds(local, size)],
                                                        hs_ref.at[pl.ds(glob, size)], row_sem.at[slot]))

    @pl.when(i == n_steps - 1)
    def _():
        cnt_ref[...] = jnp.broadcast_to(carry[...].astype(jnp.int32), cnt_ref.shape)
        wait_rows(slot)

    @pl.when((i == n_steps - 1) & (i >= 1))
    def _():
        wait_rows(1 - slot)


def _sorted_rows(tm):
    return tm + N_BUCKETS * ROW_ALIGN


def _outproj(xs, p, y_ret, y_conv, y_pool, mod, norm_g, w_ret, w_conv, w_pool, w_o, w_router, b_router,
             *, layer, tm, cap, n_rows, n_lat, seq, batch):
    n, d = xs.shape
    assert cap >= n_rows + (n_rows // tm) * ROW_ALIGN and cap % ROW_ALIGN == 0
    width = d + LANES
    ts = _sorted_rows(tm)
    row = functools.partial(_mod_row, tile=tm, n_lat=n_lat, seq=seq, batch=batch)
    gate = lambda k: pl.BlockSpec((tm, d), lambda i: (i, OFF_GATE // d + k))
    mod_spec = lambda k: _layer_spec(layer, (1, 1, d), lambda i: (row(i), 0, k))
    full = lambda a: _layer_spec(layer, a.shape[1:], lambda i: (0,) * (a.ndim - 1))
    return pl.pallas_call(
        functools.partial(_outproj_kernel, cap=cap),
        grid=(n_rows // tm,),
        in_specs=[pl.BlockSpec((tm, d), lambda i: (i, 0)),
                  pl.BlockSpec((tm, RET_V), lambda i: (i, 0)),
                  pl.BlockSpec((tm, CONV_W), lambda i: (i, 0)),
                  pl.BlockSpec((tm, POOL_W), lambda i: (i, 0)),
                  gate(0), gate(1), gate(2),
                  mod_spec(2), mod_spec(4), mod_spec(3),
                  full(norm_g), full(w_ret), full(w_conv), full(w_pool), full(w_o), full(w_router), full(b_router)],
        out_specs=[pl.BlockSpec((tm, d), lambda i: (i, 0)),
                   pl.BlockSpec(memory_space=pl.ANY),
                   pl.BlockSpec((8, LANES), lambda i: (0, 0)),
                   pl.BlockSpec((1, 8, LANES), lambda i: (i, 0, 0)),
                   pl.BlockSpec((tm, 1), lambda i: (i, 0))],
        out_shape=[jax.ShapeDtypeStruct((n, d), F32),
                   jax.ShapeDtypeStruct((N_BUCKETS * cap, width), F32),
                   jax.ShapeDtypeStruct((8, LANES), jnp.int32),
                   jax.ShapeDtypeStruct((n_rows // tm, 8, LANES), jnp.int32),
                   jax.ShapeDtypeStruct((n_rows, 1), jnp.int32)],
        scratch_shapes=[pltpu.VMEM((2, ts, width), F32),
                        pltpu.VMEM((8, LANES), jnp.int32),
                        pltpu.SMEM((8, LANES), jnp.int32),
                        pltpu.SMEM((2,), jnp.int32),
                        pltpu.VMEM((1, LANES), jnp.int32),
                        pltpu.SemaphoreType.DMA((2,)),
                        pltpu.SemaphoreType.DMA(())],
        input_output_aliases={0: 0},
        compiler_params=_cparams(("arbitrary",)),
    )(xs, y_ret, y_conv, y_pool, p, p, p, mod, mod, mod, norm_g, w_ret, w_conv, w_pool, w_o, w_router, b_router)


def _moe_kernel(blk_ref, elo_ref, ehi_ref, valid_ref, hs_ref, w1l_ref, w1h_ref, w3l_ref, w3h_ref, w2l_ref, w2h_ref,
                ys_ref):
    del blk_ref, elo_ref, ehi_ref
    valid = valid_ref[pl.program_id(0)]
    tm, d = ys_ref.shape

    @pl.when(valid > 0)
    def _():
        keep = lax.broadcasted_iota(jnp.int32, (tm, 1), 0) < valid
        meta = jnp.where(keep, hs_ref[:, d:], 0.0)
        h = jnp.where(keep, hs_ref[:, :d], 0.0).astype(MM_DTYPE)
        wlo = meta[:, 0:1] + meta[:, 1:2] + meta[:, 2:3]
        whi = meta[:, 3:4] + meta[:, 4:5] + meta[:, 5:6]

        def expert(w1, w3, w2):
            a = _dot(h, w1[0])
            a = a * jax.nn.sigmoid(a) * _dot(h, w3[0])
            return _dot(a.astype(MM_DTYPE), w2[0])

        y = wlo * expert(w1l_ref, w3l_ref, w2l_ref) + whi * expert(w1h_ref, w3h_ref, w2h_ref)
        ys_ref[...] = y.astype(ACT_DTYPE).astype(F32)


def _moe(blk, elo, ehi, valid, hs, w1, w3, w2, *, layer, tm, d):
    n_sorted, width = hs.shape
    n_work = blk.shape[0]
    up = lambda sel: _layer_spec(layer, (1, d, D_FF), lambda s, blk, elo, ehi, valid: ((elo, ehi)[sel][s], 0, 0))
    down = lambda sel: _layer_spec(layer, (1, D_FF, d), lambda s, blk, elo, ehi, valid: ((elo, ehi)[sel][s], 0, 0))
    gs = pltpu.PrefetchScalarGridSpec(
        num_scalar_prefetch=4,
        grid=(n_work,),
        in_specs=[pl.BlockSpec((tm, width), lambda s, blk, elo, ehi, valid: (blk[s], 0)),
                  up(0), up(1), up(0), up(1), down(0), down(1)],
        out_specs=pl.BlockSpec((tm, d), lambda s, blk, elo, ehi, valid: (blk[s], 0)))
    return pl.pallas_call(
        _moe_kernel,
        grid_spec=gs,
        out_shape=jax.ShapeDtypeStruct((n_sorted, d), F32),
        compiler_params=_cparams(("arbitrary",)),
    )(blk, elo, ehi, valid, hs, w1, w1, w3, w3, w2, w2)


def _expert_residual(tab_ref, ys_ref, lpos_ref, stage, sems, *, tile_rows, cap):
    i = pl.program_id(0)
    slot = lax.rem(i, 2)
    ts = stage.shape[1]
    entry = lambda tile, row, lane: tab_ref[tile * (TAB_ROWS * LANES) + row * LANES + lane]

    def start(tile, s):
        _bucket_run_copies(
            lambda b: (entry(tile, TAB_COUNT, b), entry(tile, TAB_LOCAL, b), b * cap + entry(tile, TAB_GLOBAL, b)),
            tile_rows,
            lambda local, glob, size: pltpu.make_async_copy(
                ys_ref.at[pl.ds(glob, size)], stage.at[s, pl.ds(local, size)], sems.at[s]))

    @pl.when(i == 0)
    def _():
        start(i, slot)

    @pl.when(i + 1 < pl.num_programs(0))
    def _():
        start(i + 1, 1 - slot)

    total = entry(i, TAB_TOTAL, 0)
    _wait_rows(total, ts, lambda size: pltpu.make_async_copy(
        ys_ref.at[pl.ds(0, size)], stage.at[slot, pl.ds(0, size)], sems.at[slot]))
    lpos = lpos_ref[...]
    unperm = jnp.where(lpos == lax.broadcasted_iota(jnp.int32, (tile_rows, ts), 1), 1.0, 0.0).astype(MM_DTYPE)
    filled = lax.broadcasted_iota(jnp.int32, (ts, 1), 0) < total
    return _dot(unperm, jnp.where(filled, stage[slot], 0.0).astype(MM_DTYPE))


def _final_kernel(tab_ref, x_ref, ys_ref, lpos_ref, gt_ref, g_ref, o_ref, stage, sems, *, cap):
    f = _expert_residual(tab_ref, ys_ref, lpos_ref, stage, sems, tile_rows=x_ref.shape[0], cap=cap)
    x = x_ref[...] + gt_ref[0] * f
    o_ref[...] = x * lax.rsqrt(jnp.mean(x * x, axis=-1, keepdims=True) + NORM_EPS) * g_ref[...]


def _final(xs, moe_out, mod, final_g, *, layer, tm, n_rows, n_lat, seq, batch):
    ys, tab, lpos, cap = moe_out
    d = xs.shape[1]
    row = functools.partial(_mod_row, tile=tm, n_lat=n_lat, seq=seq, batch=batch)
    x_spec = pl.BlockSpec((tm, d), lambda i, tab: (i, 0))
    gs = pltpu.PrefetchScalarGridSpec(
        num_scalar_prefetch=1,
        grid=(n_rows // tm,),
        in_specs=[x_spec,
                  pl.BlockSpec(memory_space=pl.ANY),
                  pl.BlockSpec((tm, 1), lambda i, tab: (i, 0)),
                  _layer_spec(layer, (1, 1, d), lambda i, tab: (row(i), 0, 5)),
                  pl.BlockSpec((1, d), lambda i, tab: (0, 0))],
        out_specs=x_spec,
        scratch_shapes=[pltpu.VMEM((2, _sorted_rows(tm), d), F32), pltpu.SemaphoreType.DMA((2,))])
    return pl.pallas_call(
        functools.partial(_final_kernel, cap=cap),
        grid_spec=gs,
        out_shape=jax.ShapeDtypeStruct((n_rows, d), F32),
        compiler_params=_cparams(("arbitrary",)),
    )(tab, xs, ys, lpos, mod, final_g)


def _rope_tables(seq):
    rows = seq // GRID_W
    row = jnp.repeat(jnp.arange(rows, dtype=F32), GRID_W)
    col = jnp.tile(jnp.arange(GRID_W, dtype=F32), rows)
    n_freq = RET_DK // 4
    inv_freq = ROPE_BASE ** (-jnp.arange(n_freq, dtype=F32) / n_freq)
    ang = jnp.concatenate([row[:, None] * inv_freq[None, :], col[:, None] * inv_freq[None, :]], axis=-1)
    cos, sin = jnp.cos(ang), jnp.sin(ang)
    return jnp.concatenate([cos, cos], axis=-1), jnp.concatenate([-sin, sin], axis=-1)


def _work_tables(counts, *, tm, cap, n_work):
    cnt = counts[0, :N_BUCKETS]
    tiles = (cnt + tm - 1) // tm
    ends = jnp.cumsum(tiles)
    starts = ends - tiles
    item = jnp.arange(n_work, dtype=jnp.int32)
    used = item < ends[-1]
    ref_item = jnp.minimum(item, jnp.maximum(ends[-1] - 1, 0))
    member = ((ref_item[:, None] >= starts[None, :]) & (ref_item[:, None] < ends[None, :])).astype(jnp.int32)
    pick = lambda per_bucket: jnp.sum(member * per_bucket[None, :], axis=1)
    buckets = np.arange(N_BUCKETS)
    j = ref_item - pick(starts)
    valid = jnp.where(used, jnp.clip(pick(cnt) - j * tm, 0, tm), 0)
    blk = pick(jnp.asarray(buckets * (cap // tm), jnp.int32)) + j
    first = (buckets // N_PAIRS) * EXP_PER_GROUP
    elo = pick(jnp.asarray(first + np.asarray(PAIR_LO)[buckets % N_PAIRS], jnp.int32))
    ehi = pick(jnp.asarray(first + np.asarray(PAIR_HI)[buckets % N_PAIRS], jnp.int32))
    i32 = lambda a: a.astype(jnp.int32)
    return i32(blk), i32(elo), i32(ehi), i32(valid)


def _router_weights(w_rg, b_rg, w_re, b_re):
    w = jnp.swapaxes(jnp.concatenate([w_rg, w_re], axis=-1).astype(F32), 1, 2)
    depth, n_out, d = w.shape
    assert n_out <= ROUTER_LO_ROW and 2 * ROUTER_LO_ROW <= LANES and EXP_PER_GROUP == 4
    hi = w.astype(jnp.bfloat16)
    lo = (w - hi.astype(F32)).astype(jnp.bfloat16)
    packed = jnp.zeros((depth, LANES, d), jnp.bfloat16)
    packed = packed.at[:, :n_out].set(hi).at[:, ROUTER_LO_ROW:ROUTER_LO_ROW + n_out].set(lo)
    bias = jnp.zeros((depth, ROUTER_LO_ROW, 1), F32)
    bias = bias.at[:, :n_out, 0].set(jnp.concatenate([b_rg, b_re], axis=-1).astype(F32))
    return packed, bias


def _pick_tile(n, want, *also):
    t = want
    while n % t or any(a % t for a in also):
        t //= 2
    return t


def kernel(x, c, ctx, c_ctx, w_ada, b_ada, norm1, norm2, w_in, ret_decay, conv_w, pool_w, pool_scale, w_ret_out,
           w_conv_out, w_pool_out, w_o, w_rg, b_rg, w_re, b_re, w1, w3, w2, final_norm):
    batch, seq, d = x.shape
    n_ctx = ctx.shape[1]
    depth = w_ada.shape[0]
    n_lat, n_c = batch * seq, batch * n_ctx
    n = n_lat + n_c
    assert POOL_WINDOWS == (2, 4, 8, 16) and POOL_HALO >= max(POOL_WINDOWS) // 2
    assert seq % RET_CHUNK == 0 and n_ctx == RET_CHUNK and seq % GRID_W == 0 and n_lat % n_ctx == 0

    tm_out = _pick_tile(seq, 512, n_c)
    tm_moe = 256

    xs = jnp.concatenate([x.reshape(n_lat, d), ctx.reshape(n_c, d)], axis=0)
    mod_rows = -(-(batch + 1) // 8) * 8
    cond = jnp.zeros((mod_rows, d), F32).at[:batch].set(c).at[batch].set(c_ctx)
    mod = _ada(cond, w_ada, b_ada).reshape(depth, mod_rows, 1, N_MOD * d)
    cos, sin = _rope_tables(seq)
    log_gamma = jax.nn.log_sigmoid(ret_decay.astype(F32))

    mm = lambda a: a.astype(MM_DTYPE)
    norm1_s, norm2_s, pool_scale_s = norm1[:, None, :], norm2[:, None, :], pool_scale[:, None, :]
    w_in_b, pool_w_b = mm(w_in), mm(pool_w)
    w_ret_b, w_conv_b, w_pool_b, w_o_b = mm(w_ret_out), mm(w_conv_out), mm(w_pool_out), mm(w_o)
    w1_b, w3_b, w2_b = mm(w1), mm(w3), mm(w2)
    w_router, b_router = _router_weights(w_rg, b_rg, w_re, b_re)
    dims = dict(n_lat=n_lat, seq=seq, batch=batch)
    prev = None
    for l in range(depth):
        last = l == depth - 1
        rows = n_lat if last else n
        p, xs = _inproj(xs, prev, norm1_s, mod, w_in_b, layer=l, tm=tm_out, tn=1024, **dims)
        y_ret = _retention_latent(p, log_gamma[l], cos, sin, batch=batch, seq=seq, n_ctx=n_ctx, n_lat=n_lat)
        y_conv, y_pool = _convpool(p, conv_w, pool_w_b, pool_scale_s, None, layer=l, n_seq=batch, seq=seq, row0=0)
        if not last:
            y_ret = _retention_ctx(p, log_gamma[l], y_ret, batch=batch, n_ctx=n_ctx, n_lat=n_lat)
            y_conv, y_pool = _convpool(p, conv_w, pool_w_b, pool_scale_s, (y_conv, y_pool), layer=l,
                                       n_seq=batch, seq=n_ctx, row0=n_lat // n_ctx)
        n_tiles = rows // tm_out
        cap = -(-(rows + n_tiles * ROW_ALIGN) // tm_moe) * tm_moe
        max_sorted = rows + n_tiles * N_BUCKETS * (ROW_ALIGN - 1)
        xs, hs, counts, tab, lpos = _outproj(xs, p, y_ret, y_conv, y_pool, mod, norm2_s, w_ret_b, w_conv_b, w_pool_b,
                                             w_o_b, w_router, b_router, layer=l, tm=tm_out, cap=cap, n_rows=rows,
                                             **dims)
        blk, elo, ehi, valid = _work_tables(counts, tm=tm_moe, cap=cap, n_work=-(-max_sorted // tm_moe) + N_BUCKETS)
        ys = _moe(blk, elo, ehi, valid, hs, w1_b, w3_b, w2_b, layer=l, tm=tm_moe, d=d)
        prev = (ys, tab[:, :TAB_ROWS, :].reshape(-1), lpos, cap)
    out = _final(xs, prev, mod, final_norm[None], layer=depth - 1, tm=tm_out, n_rows=n_lat, **dims)
    return out.reshape(batch, seq, d)
```

```python
import functools

import numpy as np
import jax
import jax.numpy as jnp
from jax import lax
from jax.experimental import pallas as pl
from jax.experimental.pallas import tpu as pltpu

F32 = jnp.float32
MM_DTYPE = jnp.bfloat16
ACT_DTYPE = jnp.bfloat16

NORM_EPS = 1e-6
GRID_W = 64
ROPE_BASE = 10000.0
N_MOD = 6

RET_HEADS = 4
RET_DK = 128
RET_DV = 256
RET_QK = RET_HEADS * RET_DK
RET_V = RET_HEADS * RET_DV
RET_CHUNK = 256

CONV_W = 512
POOL_WINDOWS = (2, 4, 8, 16)
POOL_GROUPS = 4
POOL_GDIM = 128
POOL_W = POOL_GROUPS * POOL_GDIM
POOL_BLOCK = 256
POOL_HALO = 16

N_GROUPS = 4
EXP_PER_GROUP = 4
N_EXPERTS = N_GROUPS * EXP_PER_GROUP
D_FF = 512
PAIR_LO = (0, 0, 0, 1, 1, 2)
PAIR_HI = (1, 2, 3, 2, 3, 3)
N_PAIRS = len(PAIR_LO)
N_BUCKETS = N_GROUPS * N_PAIRS

LANES = 128
OUT_SUBBLOCKS = 2
ROW_ALIGN = 8
ALIGN_SHIFT = 3
RUN_RARE_ROWS = 64
TAB_COUNT, TAB_LOCAL, TAB_GLOBAL, TAB_TOTAL, TAB_ROWS = 0, 1, 2, 3, 4
ROUTER_LO_ROW = 32

OFF_Q = 0
OFF_K = OFF_Q + RET_QK
OFF_V = OFF_K + RET_QK
OFF_G = OFF_V + RET_V
OFF_CB = OFF_G + RET_V
OFF_CC = OFF_CB + CONV_W
OFF_CX = OFF_CC + CONV_W
OFF_PI = OFF_CX + CONV_W
OFF_GATE = OFF_PI + POOL_W

VMEM_LIMIT = 56 * 1024 * 1024


def _cparams(sem):
    return pltpu.CompilerParams(dimension_semantics=sem, vmem_limit_bytes=VMEM_LIMIT)


def _split_bf16(a):
    hi = a.astype(jnp.bfloat16)
    lo = (a - hi.astype(F32)).astype(jnp.bfloat16)
    return hi, lo


def _dot(a, b):
    return jnp.dot(a, b, preferred_element_type=F32)


def _dot3(a, b):
    ah, al = _split_bf16(a)
    bh, bl = _split_bf16(b)
    return _dot(ah, bh) + _dot(ah, bl) + _dot(al, bh)


def _layer_spec(layer, block, index, **kw):
    return pl.BlockSpec((None,) + tuple(block), lambda *a: (layer,) + tuple(index(*a)), **kw)


def _mod_row(i, tile, n_lat, seq, batch):
    return jnp.where(i < n_lat // tile, (i * tile) // seq, batch)


def _ada_kernel(c_ref, w_ref, b_ref, o_ref):
    cv = c_ref[...]
    s = cv * jax.nn.sigmoid(cv)
    o_ref[0] = _dot3(s, w_ref[0]) + b_ref[0]


def _ada(cond, w_ada, b_ada):
    depth, d, width = w_ada.shape
    rows = cond.shape[0]
    tn = 512
    return pl.pallas_call(
        _ada_kernel,
        grid=(depth, width // tn),
        in_specs=[pl.BlockSpec((rows, d), lambda l, j: (0, 0)),
                  pl.BlockSpec((1, d, tn), lambda l, j: (l, 0, j)),
                  pl.BlockSpec((1, 1, tn), lambda l, j: (l, 0, j))],
        out_specs=pl.BlockSpec((1, rows, tn), lambda l, j: (l, 0, j)),
        out_shape=jax.ShapeDtypeStruct((depth, rows, width), F32),
        compiler_params=_cparams(("parallel", "parallel")),
    )(cond, w_ada, b_ada.reshape(depth, 1, width))


def _rms_mod(x, g, sc, sh):
    ms = jnp.mean(x * x, axis=-1, keepdims=True)
    return x * lax.rsqrt(ms + NORM_EPS) * g * (1.0 + sc) + sh


def _project_columns(h_scr, w_ref, o_ref, tn):
    for j in range(w_ref.shape[1] // tn):
        cols = slice(j * tn, (j + 1) * tn)
        o_ref[:, cols] = _dot(h_scr[...], w_ref[:, cols]).astype(o_ref.dtype)


def _inproj_kernel(x_ref, g_ref, sc_ref, sh_ref, w_ref, o_ref, h_scr, *, tn):
    h_scr[...] = _rms_mod(x_ref[...], g_ref[...], sc_ref[0], sh_ref[0]).astype(h_scr.dtype)
    _project_columns(h_scr, w_ref, o_ref, tn)


def _inproj_residual_kernel(tab_ref, x_ref, ys_ref, lpos_ref, gt_ref, g_ref, sc_ref, sh_ref, w_ref,
                            o_ref, xo_ref, h_scr, stage, sems, *, tn, cap):
    f = _expert_residual(tab_ref, ys_ref, lpos_ref, stage, sems, tile_rows=x_ref.shape[0], cap=cap)
    x = x_ref[...] + gt_ref[0] * f
    xo_ref[...] = x
    h_scr[...] = _rms_mod(x, g_ref[...], sc_ref[0], sh_ref[0]).astype(h_scr.dtype)
    _project_columns(h_scr, w_ref, o_ref, tn)


def _inproj(xs, prev, norm_g, mod, w, *, layer, tm, tn, n_lat, seq, batch):
    n, d = xs.shape
    width = w.shape[-1]
    row = functools.partial(_mod_row, tile=tm, n_lat=n_lat, seq=seq, batch=batch)
    mod_spec = lambda lyr, k: _layer_spec(lyr, (1, 1, d), lambda i, *_: (row(i), 0, k))
    x_spec = pl.BlockSpec((tm, d), lambda i, *_: (i, 0))
    p_spec = pl.BlockSpec((tm, width), lambda i, *_: (i, 0))
    p_shape = jax.ShapeDtypeStruct((n, width), ACT_DTYPE)
    tail_specs = [_layer_spec(layer, (1, d), lambda i, *_: (0, 0)),
                  mod_spec(layer, 1), mod_spec(layer, 0),
                  _layer_spec(layer, (d, width), lambda i, *_: (0, 0), pipeline_mode=pl.Buffered(1))]
    grid = (n // tm,)
    h_scratch = pltpu.VMEM((tm, d), MM_DTYPE)
    if prev is None:
        p = pl.pallas_call(functools.partial(_inproj_kernel, tn=tn), grid=grid, in_specs=[x_spec] + tail_specs,
                           out_specs=p_spec, out_shape=p_shape, scratch_shapes=[h_scratch],
                           compiler_params=_cparams(("parallel",)))(xs, norm_g, mod, mod, w)
        return p, xs
    ys, tab, lpos, cap = prev
    gs = pltpu.PrefetchScalarGridSpec(
        num_scalar_prefetch=1,
        grid=grid,
        in_specs=[x_spec, pl.BlockSpec(memory_space=pl.ANY), pl.BlockSpec((tm, 1), lambda i, *_: (i, 0)),
                  mod_spec(layer - 1, 5)] + tail_specs,
        out_specs=[p_spec, x_spec],
        scratch_shapes=[h_scratch, pltpu.VMEM((2, _sorted_rows(tm), d), F32), pltpu.SemaphoreType.DMA((2,))])
    return pl.pallas_call(
        functools.partial(_inproj_residual_kernel, tn=tn, cap=cap),
        grid_spec=gs,
        out_shape=[p_shape, jax.ShapeDtypeStruct((n, d), F32)],
        input_output_aliases={1: 1},
        compiler_params=_cparams(("arbitrary",)),
    )(tab, xs, ys, lpos, mod, norm_g, mod, mod, w)


def _ret_kernel(lg_ref, *refs, seq, n_ctx, use_rope):
    if n_ctx:
        q_ref, k_ref, v_ref, g_ref, kc_ref, vc_ref, cos_ref, sin_ref, o_ref, kt_scr, sb_scr = refs
    else:
        q_ref, k_ref, v_ref, g_ref, o_ref, kt_scr = refs
    C = RET_CHUNK
    n_chunk = seq // C
    head = pl.program_id(1)
    lgf = lg_ref[0, head]
    lgb = lg_ref[1, head]

    q = q_ref[...].astype(F32)
    k = k_ref[...].astype(F32)
    if use_rope:
        cos = cos_ref[...]
        sin = sin_ref[...]
        q = q * cos + pltpu.roll(q, RET_DK // 2, 1) * sin
        k = k * cos + pltpu.roll(k, RET_DK // 2, 1) * sin
    q = q * (RET_DK ** -0.5)
    kt_scr[...] = k.T

    ri = lax.broadcasted_iota(jnp.int32, (C, C), 0)
    ci = lax.broadcasted_iota(jnp.int32, (C, C), 1)
    rel = (ri - ci).astype(F32)
    dmask = jnp.where(rel > 0.0, jnp.exp(lgf * jnp.maximum(rel, 0.0)),
                      jnp.where(rel < 0.0, jnp.exp(lgb * jnp.maximum(-rel, 0.0)), 2.0))
    icol = lax.broadcasted_iota(jnp.int32, (C, 1), 0).astype(F32)
    jrow = lax.broadcasted_iota(jnp.int32, (1, C), 1).astype(F32)
    qdec_f = jnp.exp(lgf * (icol + 1.0))
    qdec_b = jnp.exp(lgb * (C - icol))
    kdec_f = jnp.exp(lgf * (C - 1.0 - jrow))
    kdec_b = jnp.exp(lgb * jrow)
    zrow = jnp.zeros((1, RET_DV), F32)
    cdec_f = jnp.exp(zrow + lgf * C)
    cdec_b = jnp.exp(zrow + lgb * C)

    def chunk(ref, n):
        return ref[n * C:(n + 1) * C, :]

    def kt_chunk(n):
        return kt_scr[:, n * C:(n + 1) * C]

    if n_ctx:
        kct = kc_ref[...].astype(F32).T
        vcx = vc_ref[...].astype(MM_DTYPE)
        mrow = lax.broadcasted_iota(jnp.int32, (1, n_ctx), 1).astype(F32)
        s_f = _dot((kct * jnp.exp(lgf * (n_ctx - 1.0 - mrow))).astype(MM_DTYPE), vcx)
        s_b = _dot((kct * jnp.exp(lgb * mrow)).astype(MM_DTYPE), vcx)
        sb_scr[n_chunk - 1] = s_b
        for n in range(n_chunk - 1, 0, -1):
            s_b = s_b * cdec_b + _dot((kt_chunk(n) * kdec_b).astype(MM_DTYPE), chunk(v_ref, n).astype(MM_DTYPE))
            sb_scr[n - 1] = s_b

    for n in range(n_chunk):
        qn = q[n * C:(n + 1) * C, :]
        ktn = kt_chunk(n)
        vn = chunk(v_ref, n).astype(MM_DTYPE)
        s = _dot(qn.astype(MM_DTYPE), ktn.astype(MM_DTYPE)) * dmask
        o = _dot(s.astype(MM_DTYPE), vn)
        if n_ctx:
            qcat = jnp.concatenate([qn * qdec_f, qn * qdec_b], axis=1).astype(MM_DTYPE)
            scat = jnp.concatenate([s_f, sb_scr[n]], axis=0).astype(MM_DTYPE)
            o = o + _dot(qcat, scat)
            if n + 1 < n_chunk:
                s_f = s_f * cdec_f + _dot((ktn * kdec_f).astype(MM_DTYPE), vn)
        mu = jnp.mean(o, axis=-1, keepdims=True)
        oc = o - mu
        yn = oc * lax.rsqrt(jnp.mean(oc * oc, axis=-1, keepdims=True) + NORM_EPS)
        gn = chunk(g_ref, n)
        o_ref[n * C:(n + 1) * C, :] = yn.astype(o_ref.dtype) * (gn * jax.nn.sigmoid(gn))


def _retention_latent(p, log_gamma, cos, sin, *, batch, seq, n_ctx, n_lat):
    n = p.shape[0]
    cb = n_lat // n_ctx
    kq, kv = OFF_K // RET_DK, OFF_V // RET_DV
    gs = pltpu.PrefetchScalarGridSpec(
        num_scalar_prefetch=1,
        grid=(batch, RET_HEADS),
        in_specs=[pl.BlockSpec((seq, RET_DK), lambda b, h, lg: (b, h)),
                  pl.BlockSpec((seq, RET_DK), lambda b, h, lg: (b, kq + h)),
                  pl.BlockSpec((seq, RET_DV), lambda b, h, lg: (b, kv + h)),
                  pl.BlockSpec((seq, RET_DV), lambda b, h, lg: (b, OFF_G // RET_DV + h)),
                  pl.BlockSpec((n_ctx, RET_DK), lambda b, h, lg: (cb + b, kq + h)),
                  pl.BlockSpec((n_ctx, RET_DV), lambda b, h, lg: (cb + b, kv + h)),
                  pl.BlockSpec((seq, RET_DK), lambda b, h, lg: (0, 0)),
                  pl.BlockSpec((seq, RET_DK), lambda b, h, lg: (0, 0))],
        out_specs=pl.BlockSpec((seq, RET_DV), lambda b, h, lg: (b, h)),
        scratch_shapes=[pltpu.VMEM((RET_DK, seq), F32),
                        pltpu.VMEM((seq // RET_CHUNK, RET_DK, RET_DV), F32)])
    return pl.pallas_call(
        functools.partial(_ret_kernel, seq=seq, n_ctx=n_ctx, use_rope=True),
        grid_spec=gs,
        out_shape=jax.ShapeDtypeStruct((n, RET_V), ACT_DTYPE),
        compiler_params=_cparams(("parallel", "parallel")),
    )(log_gamma, p, p, p, p, p, p, cos, sin)


def _retention_ctx(p, log_gamma, y_ret, *, batch, n_ctx, n_lat):
    cb = n_lat // n_ctx
    kq, kv = OFF_K // RET_DK, OFF_V // RET_DV
    gs = pltpu.PrefetchScalarGridSpec(
        num_scalar_prefetch=1,
        grid=(batch, RET_HEADS),
        in_specs=[pl.BlockSpec((n_ctx, RET_DK), lambda b, h, lg: (cb + b, h)),
                  pl.BlockSpec((n_ctx, RET_DK), lambda b, h, lg: (cb + b, kq + h)),
                  pl.BlockSpec((n_ctx, RET_DV), lambda b, h, lg: (cb + b, kv + h)),
                  pl.BlockSpec((n_ctx, RET_DV), lambda b, h, lg: (cb + b, OFF_G // RET_DV + h)),
                  pl.BlockSpec(memory_space=pl.ANY)],
        out_specs=pl.BlockSpec((n_ctx, RET_DV), lambda b, h, lg: (cb + b, h)),
        scratch_shapes=[pltpu.VMEM((RET_DK, n_ctx), F32)])

    def body(lg_ref, q_ref, k_ref, v_ref, g_ref, alias_ref, o_ref, kt_scr):
        del alias_ref
        _ret_kernel(lg_ref, q_ref, k_ref, v_ref, g_ref, o_ref, kt_scr, seq=n_ctx, n_ctx=0, use_rope=False)

    return pl.pallas_call(
        body,
        grid_spec=gs,
        out_shape=jax.ShapeDtypeStruct(y_ret.shape, y_ret.dtype),
        input_output_aliases={5: 0},
        compiler_params=_cparams(("parallel", "parallel")),
    )(log_gamma, p, p, p, p, y_ret)


def _convpool_kernel(*refs, seq, aliased):
    if aliased:
        cb_ref, cc_ref, cx_ref, pi_ref, cw_ref, pw_ref, ps_ref, _, _, yc_ref, yp_ref = refs
    else:
        cb_ref, cc_ref, cx_ref, pi_ref, cw_ref, pw_ref, ps_ref, yc_ref, yp_ref = refs
    grp = pl.program_id(1)
    pb, hb = POOL_BLOCK, POOL_HALO
    n_blk = seq // pb

    w = cw_ref[...]
    tl = lax.broadcasted_iota(jnp.int32, (pb, 1), 0)
    cxu = lambda a, b: cc_ref[a:b, :].astype(F32) * cx_ref[a:b, :].astype(F32)
    for blk in range(n_blk):
        r0 = blk * pb
        u = cxu(r0, r0 + pb)
        before = cxu(r0 - hb, r0)[hb - 1:hb, :] if blk > 0 else 0.0
        after = cxu(r0 + pb, r0 + pb + hb)[0:1, :] if blk + 1 < n_blk else 0.0
        u_prev = jnp.where(tl == 0, before, pltpu.roll(u, 1, 0))
        u_next = jnp.where(tl == pb - 1, after, pltpu.roll(u, pb - 1, 0))
        conv = w[0:1, :] * u_prev + w[1:2, :] * u + w[2:3, :] * u_next
        yc_ref[r0:r0 + pb, :] = (cb_ref[r0:r0 + pb, :].astype(F32) * conv).astype(yc_ref.dtype)

    half = jnp.left_shift(1, grp)

    def band(rows, cols, shift):
        dd = (lax.broadcasted_iota(jnp.int32, (rows, cols), 1) + shift
              - lax.broadcasted_iota(jnp.int32, (rows, cols), 0))
        return jnp.where((dd >= -half) & (dd < half), 1.0, 0.0).astype(MM_DTYPE)

    band_self = band(pb, pb, 0)
    band_prev = band(hb, hb, -hb)
    band_next = band(hb, hb, hb)
    for blk in range(n_blk):
        r0 = blk * pb
        p_blk = pi_ref[r0:r0 + pb, :]
        wsum = _dot(band_self, p_blk.astype(MM_DTYPE))
        top, mid, bot = wsum[:hb], wsum[hb:pb - hb], wsum[pb - hb:]
        if blk > 0:
            top = top + _dot(band_prev, pi_ref[r0 - hb:r0, :].astype(MM_DTYPE))
        if blk + 1 < n_blk:
            bot = bot + _dot(band_next, pi_ref[r0 + pb:r0 + pb + hb, :].astype(MM_DTYPE))
        wsum = jnp.concatenate([top, mid, bot], axis=0)
        if 0 < blk < n_blk - 1:
            cnt = (2 * half).astype(F32)
        else:
            tb = r0 + lax.broadcasted_iota(jnp.int32, (pb, 1), 0)
            cnt = (jnp.clip(tb + half, 0, seq) - jnp.clip(tb - half, 0, seq)).astype(F32)
        pooled = wsum / cnt - p_blk.astype(F32)
        mixed = _dot(pooled.astype(MM_DTYPE), pw_ref[0]) * ps_ref[...]
        yp_ref[r0:r0 + pb, :] = mixed.astype(yp_ref.dtype)


def _convpool(p, conv_w, pool_w, pool_scale, prev, *, layer, n_seq, seq, row0):
    n = p.shape[0]
    g128 = lambda off: off // POOL_GDIM
    col = lambda off: (lambda b, g: (row0 + b, g128(off) + g))
    in_specs = [pl.BlockSpec((seq, POOL_GDIM), col(OFF_CB)),
                pl.BlockSpec((seq, POOL_GDIM), col(OFF_CC)),
                pl.BlockSpec((seq, POOL_GDIM), col(OFF_CX)),
                pl.BlockSpec((seq, POOL_GDIM), col(OFF_PI)),
                _layer_spec(layer, (conv_w.shape[1], POOL_GDIM), lambda b, g: (0, g)),
                _layer_spec(layer, (1, POOL_GDIM, POOL_GDIM), lambda b, g: (g, 0, 0)),
                _layer_spec(layer, (1, POOL_GDIM), lambda b, g: (0, g))]
    args = [p, p, p, p, conv_w, pool_w, pool_scale]
    aliases = {}
    if prev is not None:
        in_specs += [pl.BlockSpec(memory_space=pl.ANY), pl.BlockSpec(memory_space=pl.ANY)]
        args += list(prev)
        aliases = {7: 0, 8: 1}
    out_spec = pl.BlockSpec((seq, POOL_GDIM), lambda b, g: (row0 + b, g))
    return pl.pallas_call(
        functools.partial(_convpool_kernel, seq=seq, aliased=prev is not None),
        grid=(n_seq, POOL_GROUPS),
        in_specs=in_specs,
        out_specs=[out_spec, out_spec],
        out_shape=[jax.ShapeDtypeStruct((n, CONV_W), ACT_DTYPE), jax.ShapeDtypeStruct((n, POOL_W), ACT_DTYPE)],
        input_output_aliases=aliases,
        compiler_params=_cparams(("parallel", "parallel")),
    )(*args)


def _first_max(vals):
    top = functools.reduce(jnp.maximum, vals)
    idx = jnp.full(top.shape, len(vals) - 1, jnp.int32)
    for k in range(len(vals) - 2, -1, -1):
        idx = jnp.where(vals[k] == top, k, idx)
    return top, idx


def _route(logits_t):
    row = lambda k: logits_t[k:k + 1, :]
    groups = [row(g) for g in range(N_GROUPS)]
    gmax, gidx = _first_max(groups)
    gtop = 1.0 / sum(jnp.exp(g - gmax) for g in groups)
    experts = []
    for k in range(EXP_PER_GROUP):
        e = row(N_GROUPS + (N_GROUPS - 1) * EXP_PER_GROUP + k)
        for g in range(N_GROUPS - 2, -1, -1):
            e = jnp.where(gidx == g, row(N_GROUPS + g * EXP_PER_GROUP + k), e)
        experts.append(e)
    m1, i1 = _first_max(experts)
    m2, i2 = _first_max([jnp.where(i1 == k, -jnp.inf, e) for k, e in enumerate(experts)])
    e2 = jnp.exp(m2 - m1)
    w1 = gtop / (1.0 + e2)
    w2 = gtop * e2 / (1.0 + e2)
    first_lower = i1 < i2
    lo = jnp.minimum(i1, i2)
    hi = jnp.maximum(i1, i2)
    pair = lo * 3 - jnp.right_shift(lo * (lo - 1), 1) + hi - lo - 1
    bucket = gidx * N_PAIRS + pair
    wlo = jnp.where(first_lower, w1, w2)
    whi = jnp.where(first_lower, w2, w1)
    return bucket, wlo, whi


def _column_to_lanes(col):
    return jnp.broadcast_to(col, (col.shape[0], LANES)).T[0:1, :]


def _row_to_column(row):
    return jnp.broadcast_to(row, (8, row.shape[1])).T[:, 0:1]


def _dot_nt(a, b):
    return lax.dot_general(a, b, (((1,), (1,)), ((), ())), preferred_element_type=F32)


def _bucket_run_copies(read_run, max_rows, make_copy):
    def piece(n, local, glob, size):
        @pl.when((n & size) != 0)
        def _():
            off = n & (-2 * size)
            make_copy(pl.multiple_of(local + off, ROW_ALIGN), pl.multiple_of(glob + off, ROW_ALIGN), size).start()

    top = 1 << (max_rows.bit_length() - 1)
    sizes = [top >> k for k in range(top.bit_length()) if top >> k >= ROW_ALIGN]
    rare = [s for s in sizes if s >= RUN_RARE_ROWS]

    def per_bucket(b, carry_):
        n, local, glob = read_run(b)

        @pl.when(n >= RUN_RARE_ROWS)
        def _():
            for size in rare:
                piece(n, local, glob, size)

        for size in sizes[len(rare):]:
            piece(n, local, glob, size)
        return carry_

    lax.fori_loop(0, N_BUCKETS, per_bucket, 0)


def _wait_rows(total, max_rows, make_copy):
    size = 1 << (max_rows.bit_length() - 1)
    while size >= ROW_ALIGN:
        @pl.when((total & size) != 0)
        def _(size=size):
            make_copy(size).wait()
        size //= 2


def _exact_bf16_pieces(w):
    a = w.astype(jnp.bfloat16).astype(F32)
    b = (w - a).astype(jnp.bfloat16).astype(F32)
    c = (w - a - b).astype(jnp.bfloat16).astype(F32)
    return a, b, c


def _outproj_kernel(x_ref, yr_ref, yc_ref, yp_ref, g0_ref, g1_ref, g2_ref, gt_ref, sc_ref, sh_ref, ng_ref,
                    wr_ref, wc_ref, wp_ref, wo_ref, wrt_ref, brt_ref,
                    xo_ref, hs_ref, cnt_ref, tab_ref, lpos_ref,
                    h_scr, tab_vmem, tab_smem, sent_smem, carry, row_sem, tab_sem, *, cap):
    i = pl.program_id(0)
    n_steps = pl.num_programs(0)
    slot = lax.rem(i, 2)
    tm, d = x_ref.shape
    ts = h_scr.shape[1]

    def wait_rows(s):
        _wait_rows(sent_smem[s], ts, lambda size: pltpu.make_async_copy(
            h_scr.at[s, pl.ds(0, size)], hs_ref.at[pl.ds(0, size)], row_sem.at[s]))

    @pl.when(i == 0)
    def _():
        carry[...] = jnp.zeros_like(carry)

    @pl.when(i >= 2)
    def _():
        wait_rows(slot)

    def rows_to_routing(rows):
        gate = lambda r: jax.nn.sigmoid(r[rows, :].astype(F32))
        merged = (gate(g0_ref) * _dot(yr_ref[rows, :], wr_ref[...])
                  + gate(g1_ref) * _dot(yc_ref[rows, :], wc_ref[...])
                  + gate(g2_ref) * _dot(yp_ref[rows, :], wp_ref[...]))
        y = _dot(merged.astype(MM_DTYPE), wo_ref[...])
        x = x_ref[rows, :] + gt_ref[0] * y
        xo_ref[rows, :] = x
        h = _rms_mod(x, ng_ref[...], sc_ref[0], sh_ref[0])
        hh, hl = _split_bf16(h)
        s2 = _dot_nt(wrt_ref[...], hh) + _dot_nt(wrt_ref[...], hl)
        logits_t = s2[0:ROUTER_LO_ROW, :] + s2[ROUTER_LO_ROW:2 * ROUTER_LO_ROW, :] + brt_ref[...]
        return (h.astype(MM_DTYPE),) + _route(logits_t)

    sub = tm // OUT_SUBBLOCKS
    parts = [rows_to_routing(slice(a, a + sub)) for a in range(0, tm, sub)]
    h = jnp.concatenate([part[0] for part in parts], axis=0)
    bucket, wlo, whi = (jnp.concatenate([part[k] for part in parts], axis=1) for k in (1, 2, 3))

    brow = lax.broadcasted_iota(jnp.int32, (LANES, tm), 0)
    onehot = brow == bucket
    ones = jnp.where(onehot, 1.0, 0.0)
    r = lax.broadcasted_iota(jnp.int32, (tm, tm), 0)
    c = lax.broadcasted_iota(jnp.int32, (tm, tm), 1)
    tri = jnp.where(r <= c, 1.0, 0.0).astype(jnp.bfloat16)
    incl = _dot(ones.astype(jnp.bfloat16), tri)
    cnt_col = incl[:, tm - 1:tm].astype(jnp.int32)
    units_col = jnp.right_shift(cnt_col + (ROW_ALIGN - 1), ALIGN_SHIFT)
    below = jnp.where(lax.broadcasted_iota(jnp.int32, (LANES, LANES), 0)
                      > lax.broadcasted_iota(jnp.int32, (LANES, LANES), 1), 1.0, 0.0).astype(jnp.bfloat16)
    units_b = jnp.broadcast_to(units_col.astype(F32), (LANES, LANES)).astype(jnp.bfloat16)
    start_col = _dot(below, units_b)[:, 0:1] * float(ROW_ALIGN)
    lpos = jnp.sum(jnp.where(onehot, incl - 1.0 + start_col, 0.0), axis=0, keepdims=True)
    lpos_ref[...] = _row_to_column(lpos).astype(jnp.int32)
    run_col = (units_col * ROW_ALIGN).astype(F32)
    run_len = _column_to_lanes(run_col).astype(jnp.int32)
    local_start = _column_to_lanes(start_col).astype(jnp.int32)
    total = (start_col + run_col)[LANES - 1:LANES, :].astype(jnp.int32)
    srow = lax.broadcasted_iota(jnp.int32, (8, LANES), 0)
    tab = jnp.where(srow == TAB_COUNT, run_len,
                    jnp.where(srow == TAB_LOCAL, local_start,
                              jnp.where(srow == TAB_GLOBAL, carry[...], jnp.where(srow == TAB_TOTAL, total, 0))))
    carry[...] = carry[...] + run_len
    tab_ref[0] = tab
    tab_vmem[...] = tab
    to_smem = pltpu.make_async_copy(tab_vmem, tab_smem, tab_sem)
    to_smem.start()

    srt = lax.broadcasted_iota(jnp.int32, (ts, tm), 0)
    perm = jnp.where(srt == lpos.astype(jnp.int32), 1.0, 0.0).astype(MM_DTYPE)
    pieces = _exact_bf16_pieces(wlo) + _exact_bf16_pieces(whi)
    meta_t = jnp.zeros((LANES, tm), F32)
    for k, piece in enumerate(pieces):
        meta_t = jnp.where(brow == k, piece, meta_t)
    h_scr[slot, :, :d] = _dot(perm, h)
    h_scr[slot, :, d:] = _dot_nt(perm, meta_t.astype(MM_DTYPE))
    to_smem.wait()
    sent_smem[slot] = tab_smem[TAB_TOTAL, 0]

    _bucket_run_copies(
        lambda b: (tab_smem[TAB_COUNT, b], tab_smem[TAB_LOCAL, b], b * cap + tab_smem[TAB_GLOBAL, b]), tm,
        lambda local, glob, size: pltpu.make_async_copy(h_scr.at[slot, pl.ds(local, size)],
                                                        hs_ref.at[pl.ds(glob, size)], row_sem.at[slot]))

    @pl.when(i == n_steps - 1)
    def _():
        cnt_ref[...] = jnp.broadcast_to(carry[...].astype(jnp.int32), cnt_ref.shape)
        wait_rows(slot)

    @pl.when((i == n_steps - 1) & (i >= 1))
    def _():
        wait_rows(1 - slot)


def _sorted_rows(tm):
    return tm + N_BUCKETS * ROW_ALIGN


def _outproj(xs, p, y_ret, y_conv, y_pool, mod, norm_g, w_ret, w_conv, w_pool, w_o, w_router, b_router,
             *, layer, tm, cap, n_rows, n_lat, seq, batch):
    n, d = xs.shape
    assert cap >= n_rows + (n_rows // tm) * ROW_ALIGN and cap % ROW_ALIGN == 0
    width = d + LANES
    ts = _sorted_rows(tm)
    row = functools.partial(_mod_row, tile=tm, n_lat=n_lat, seq=seq, batch=batch)
    gate = lambda k: pl.BlockSpec((tm, d), lambda i: (i, OFF_GATE // d + k))
    mod_spec = lambda k: _layer_spec(layer, (1, 1, d), lambda i: (row(i), 0, k))
    full = lambda a: _layer_spec(layer, a.shape[1:], lambda i: (0,) * (a.ndim - 1))
    return pl.pallas_call(
        functools.partial(_outproj_kernel, cap=cap),
        grid=(n_rows // tm,),
        in_specs=[pl.BlockSpec((tm, d), lambda i: (i, 0)),
                  pl.BlockSpec((tm, RET_V), lambda i: (i, 0)),
                  pl.BlockSpec((tm, CONV_W), lambda i: (i, 0)),
                  pl.BlockSpec((tm, POOL_W), lambda i: (i, 0)),
                  gate(0), gate(1), gate(2),
                  mod_spec(2), mod_spec(4), mod_spec(3),
                  full(norm_g), full(w_ret), full(w_conv), full(w_pool), full(w_o), full(w_router), full(b_router)],
        out_specs=[pl.BlockSpec((tm, d), lambda i: (i, 0)),
                   pl.BlockSpec(memory_space=pl.ANY),
                   pl.BlockSpec((8, LANES), lambda i: (0, 0)),
                   pl.BlockSpec((1, 8, LANES), lambda i: (i, 0, 0)),
                   pl.BlockSpec((tm, 1), lambda i: (i, 0))],
        out_shape=[jax.ShapeDtypeStruct((n, d), F32),
                   jax.ShapeDtypeStruct((N_BUCKETS * cap, width), F32),
                   jax.ShapeDtypeStruct((8, LANES), jnp.int32),
                   jax.ShapeDtypeStruct((n_rows // tm, 8, LANES), jnp.int32),
                   jax.ShapeDtypeStruct((n_rows, 1), jnp.int32)],
        scratch_shapes=[pltpu.VMEM((2, ts, width), F32),
                        pltpu.VMEM((8, LANES), jnp.int32),
                        pltpu.SMEM((8, LANES), jnp.int32),
                        pltpu.SMEM((2,), jnp.int32),
                        pltpu.VMEM((1, LANES), jnp.int32),
                        pltpu.SemaphoreType.DMA((2,)),
                        pltpu.SemaphoreType.DMA(())],
        input_output_aliases={0: 0},
        compiler_params=_cparams(("arbitrary",)),
    )(xs, y_ret, y_conv, y_pool, p, p, p, mod, mod, mod, norm_g, w_ret, w_conv, w_pool, w_o, w_router, b_router)


def _moe_kernel(blk_ref, elo_ref, ehi_ref, valid_ref, hs_ref, w1l_ref, w1h_ref, w3l_ref, w3h_ref, w2l_ref, w2h_ref,
                ys_ref):
    del blk_ref, elo_ref, ehi_ref
    valid = valid_ref[pl.program_id(0)]
    tm, d = ys_ref.shape

    @pl.when(valid > 0)
    def _():
        keep = lax.broadcasted_iota(jnp.int32, (tm, 1), 0) < valid
        meta = jnp.where(keep, hs_ref[:, d:], 0.0)
        h = jnp.where(keep, hs_ref[:, :d], 0.0).astype(MM_DTYPE)
        wlo = meta[:, 0:1] + meta[:, 1:2] + meta[:, 2:3]
        whi = meta[:, 3:4] + meta[:, 4:5] + meta[:, 5:6]

        def expert(w1, w3, w2):
            a = _dot(h, w1[0])
            a = a * jax.nn.sigmoid(a) * _dot(h, w3[0])
            return _dot(a.astype(MM_DTYPE), w2[0])

        y = wlo * expert(w1l_ref, w3l_ref, w2l_ref) + whi * expert(w1h_ref, w3h_ref, w2h_ref)
        ys_ref[...] = y.astype(ACT_DTYPE).astype(F32)


def _moe(blk, elo, ehi, valid, hs, w1, w3, w2, *, layer, tm, d):
    n_sorted, width = hs.shape
    n_work = blk.shape[0]
    up = lambda sel: _layer_spec(layer, (1, d, D_FF), lambda s, blk, elo, ehi, valid: ((elo, ehi)[sel][s], 0, 0))
    down = lambda sel: _layer_spec(layer, (1, D_FF, d), lambda s, blk, elo, ehi, valid: ((elo, ehi)[sel][s], 0, 0))
    gs = pltpu.PrefetchScalarGridSpec(
        num_scalar_prefetch=4,
        grid=(n_work,),
        in_specs=[pl.BlockSpec((tm, width), lambda s, blk, elo, ehi, valid: (blk[s], 0)),
                  up(0), up(1), up(0), up(1), down(0), down(1)],
        out_specs=pl.BlockSpec((tm, d), lambda s, blk, elo, ehi, valid: (blk[s], 0)))
    return pl.pallas_call(
        _moe_kernel,
        grid_spec=gs,
        out_shape=jax.ShapeDtypeStruct((n_sorted, d), F32),
        compiler_params=_cparams(("arbitrary",)),
    )(blk, elo, ehi, valid, hs, w1, w1, w3, w3, w2, w2)


def _expert_residual(tab_ref, ys_ref, lpos_ref, stage, sems, *, tile_rows, cap):
    i = pl.program_id(0)
    slot = lax.rem(i, 2)
    ts = stage.shape[1]
    entry = lambda tile, row, lane: tab_ref[tile * (TAB_ROWS * LANES) + row * LANES + lane]

    def start(tile, s):
        _bucket_run_copies(
            lambda b: (entry(tile, TAB_COUNT, b), entry(tile, TAB_LOCAL, b), b * cap + entry(tile, TAB_GLOBAL, b)),
            tile_rows,
            lambda local, glob, size: pltpu.make_async_copy(
                ys_ref.at[pl.ds(glob, size)], stage.at[s, pl.ds(local, size)], sems.at[s]))

    @pl.when(i == 0)
    def _():
        start(i, slot)

    @pl.when(i + 1 < pl.num_programs(0))
    def _():
        start(i + 1, 1 - slot)

    total = entry(i, TAB_TOTAL, 0)
    _wait_rows(total, ts, lambda size: pltpu.make_async_copy(
        ys_ref.at[pl.ds(0, size)], stage.at[slot, pl.ds(0, size)], sems.at[slot]))
    lpos = lpos_ref[...]
    unperm = jnp.where(lpos == lax.broadcasted_iota(jnp.int32, (tile_rows, ts), 1), 1.0, 0.0).astype(MM_DTYPE)
    filled = lax.broadcasted_iota(jnp.int32, (ts, 1), 0) < total
    return _dot(unperm, jnp.where(filled, stage[slot], 0.0).astype(MM_DTYPE))


def _final_kernel(tab_ref, x_ref, ys_ref, lpos_ref, gt_ref, g_ref, o_ref, stage, sems, *, cap):
    f = _expert_residual(tab_ref, ys_ref, lpos_ref, stage, sems, tile_rows=x_ref.shape[0], cap=cap)
    x = x_ref[...] + gt_ref[0] * f
    o_ref[...] = x * lax.rsqrt(jnp.mean(x * x, axis=-1, keepdims=True) + NORM_EPS) * g_ref[...]


def _final(xs, moe_out, mod, final_g, *, layer, tm, n_rows, n_lat, seq, batch):
    ys, tab, lpos, cap = moe_out
    d = xs.shape[1]
    row = functools.partial(_mod_row, tile=tm, n_lat=n_lat, seq=seq, batch=batch)
    x_spec = pl.BlockSpec((tm, d), lambda i, tab: (i, 0))
    gs = pltpu.PrefetchScalarGridSpec(
        num_scalar_prefetch=1,
        grid=(n_rows // tm,),
        in_specs=[x_spec,
                  pl.BlockSpec(memory_space=pl.ANY),
                  pl.BlockSpec((tm, 1), lambda i, tab: (i, 0)),
                  _layer_spec(layer, (1, 1, d), lambda i, tab: (row(i), 0, 5)),
                  pl.BlockSpec((1, d), lambda i, tab: (0, 0))],
        out_specs=x_spec,
        scratch_shapes=[pltpu.VMEM((2, _sorted_rows(tm), d), F32), pltpu.SemaphoreType.DMA((2,))])
    return pl.pallas_call(
        functools.partial(_final_kernel, cap=cap),
        grid_spec=gs,
        out_shape=jax.ShapeDtypeStruct((n_rows, d), F32),
        compiler_params=_cparams(("arbitrary",)),
    )(tab, xs, ys, lpos, mod, final_g)


def _rope_tables(seq):
    rows = seq // GRID_W
    row = jnp.repeat(jnp.arange(rows, dtype=F32), GRID_W)
    col = jnp.tile(jnp.arange(GRID_W, dtype=F32), rows)
    n_freq = RET_DK // 4
    inv_freq = ROPE_BASE ** (-jnp.arange(n_freq, dtype=F32) / n_freq)
    ang = jnp.concatenate([row[:, None] * inv_freq[None, :], col[:, None] * inv_freq[None, :]], axis=-1)
    cos, sin = jnp.cos(ang), jnp.sin(ang)
    return jnp.concatenate([cos, cos], axis=-1), jnp.concatenate([-sin, sin], axis=-1)


def _work_tables(counts, *, tm, cap, n_work):
    cnt = counts[0, :N_BUCKETS]
    tiles = (cnt + tm - 1) // tm
    ends = jnp.cumsum(tiles)
    starts = ends - tiles
    item = jnp.arange(n_work, dtype=jnp.int32)
    used = item < ends[-1]
    ref_item = jnp.minimum(item, jnp.maximum(ends[-1] - 1, 0))
    member = ((ref_item[:, None] >= starts[None, :]) & (ref_item[:, None] < ends[None, :])).astype(jnp.int32)
    pick = lambda per_bucket: jnp.sum(member * per_bucket[None, :], axis=1)
    buckets = np.arange(N_BUCKETS)
    j = ref_item - pick(starts)
    valid = jnp.where(used, jnp.clip(pick(cnt) - j * tm, 0, tm), 0)
    blk = pick(jnp.asarray(buckets * (cap // tm), jnp.int32)) + j
    first = (buckets // N_PAIRS) * EXP_PER_GROUP
    elo = pick(jnp.asarray(first + np.asarray(PAIR_LO)[buckets % N_PAIRS], jnp.int32))
    ehi = pick(jnp.asarray(first + np.asarray(PAIR_HI)[buckets % N_PAIRS], jnp.int32))
    i32 = lambda a: a.astype(jnp.int32)
    return i32(blk), i32(elo), i32(ehi), i32(valid)


def _router_weights(w_rg, b_rg, w_re, b_re):
    w = jnp.swapaxes(jnp.concatenate([w_rg, w_re], axis=-1).astype(F32), 1, 2)
    depth, n_out, d = w.shape
    assert n_out <= ROUTER_LO_ROW and 2 * ROUTER_LO_ROW <= LANES and EXP_PER_GROUP == 4
    hi = w.astype(jnp.bfloat16)
    lo = (w - hi.astype(F32)).astype(jnp.bfloat16)
    packed = jnp.zeros((depth, LANES, d), jnp.bfloat16)
    packed = packed.at[:, :n_out].set(hi).at[:, ROUTER_LO_ROW:ROUTER_LO_ROW + n_out].set(lo)
    bias = jnp.zeros((depth, ROUTER_LO_ROW, 1), F32)
    bias = bias.at[:, :n_out, 0].set(jnp.concatenate([b_rg, b_re], axis=-1).astype(F32))
    return packed, bias


def _pick_tile(n, want, *also):
    t = want
    while n % t or any(a % t for a in also):
        t //= 2
    return t


def kernel(x, c, ctx, c_ctx, w_ada, b_ada, norm1, norm2, w_in, ret_decay, conv_w, pool_w, pool_scale, w_ret_out,
           w_conv_out, w_pool_out, w_o, w_rg, b_rg, w_re, b_re, w1, w3, w2, final_norm):
    batch, seq, d = x.shape
    n_ctx = ctx.shape[1]
    depth = w_ada.shape[0]
    n_lat, n_c = batch * seq, batch * n_ctx
    n = n_lat + n_c
    assert POOL_WINDOWS == (2, 4, 8, 16) and POOL_HALO >= max(POOL_WINDOWS) // 2
    assert seq % RET_CHUNK == 0 and n_ctx == RET_CHUNK and seq % GRID_W == 0 and n_lat % n_ctx == 0

    tm_out = _pick_tile(seq, 512, n_c)
    tm_moe = 256

    xs = jnp.concatenate([x.reshape(n_lat, d), ctx.reshape(n_c, d)], axis=0)
    mod_rows = -(-(batch + 1) // 8) * 8
    cond = jnp.zeros((mod_rows, d), F32).at[:batch].set(c).at[batch].set(c_ctx)
    mod = _ada(cond, w_ada, b_ada).reshape(depth, mod_rows, 1, N_MOD * d)
    cos, sin = _rope_tables(seq)
    log_gamma = jax.nn.log_sigmoid(ret_decay.astype(F32))

    mm = lambda a: a.astype(MM_DTYPE)
    norm1_s, norm2_s, pool_scale_s = norm1[:, None, :], norm2[:, None, :], pool_scale[:, None, :]
    w_in_b, pool_w_b = mm(w_in), mm(pool_w)
    w_ret_b, w_conv_b, w_pool_b, w_o_b = mm(w_ret_out), mm(w_conv_out), mm(w_pool_out), mm(w_o)
    w1_b, w3_b, w2_b = mm(w1), mm(w3), mm(w2)
    w_router, b_router = _router_weights(w_rg, b_rg, w_re, b_re)
    dims = dict(n_lat=n_lat, seq=seq, batch=batch)
    prev = None
    for l in range(depth):
        last = l == depth - 1
        rows = n_lat if last else n
        p, xs = _inproj(xs, prev, norm1_s, mod, w_in_b, layer=l, tm=tm_out, tn=1024, **dims)
        y_ret = _retention_latent(p, log_gamma[l], cos, sin, batch=batch, seq=seq, n_ctx=n_ctx, n_lat=n_lat)
        y_conv, y_pool = _convpool(p, conv_w, pool_w_b, pool_scale_s, None, layer=l, n_seq=batch, seq=seq, row0=0)
        if not last:
            y_ret = _retention_ctx(p, log_gamma[l], y_ret, batch=batch, n_ctx=n_ctx, n_lat=n_lat)
            y_conv, y_pool = _convpool(p, conv_w, pool_w_b, pool_scale_s, (y_conv, y_pool), layer=l,
                                       n_seq=batch, seq=n_ctx, row0=n_lat // n_ctx)
        n_tiles = rows // tm_out
        cap = -(-(rows + n_tiles * ROW_ALIGN) // tm_moe) * tm_moe
        max_sorted = rows + n_tiles * N_BUCKETS * (ROW_ALIGN - 1)
        xs, hs, counts, tab, lpos = _outproj(xs, p, y_ret, y_conv, y_pool, mod, norm2_s, w_ret_b, w_conv_b, w_pool_b,
                                             w_o_b, w_router, b_router, layer=l, tm=tm_out, cap=cap, n_rows=rows,
                                             **dims)
        blk, elo, ehi, valid = _work_tables(counts, tm=tm_moe, cap=cap, n_work=-(-max_sorted // tm_moe) + N_BUCKETS)
        ys = _moe(blk, elo, ehi, valid, hs, w1_b, w3_b, w2_b, layer=l, tm=tm_moe, d=d)
        prev = (ys, tab[:, :TAB_ROWS, :].reshape(-1), lpos, cap)
    out = _final(xs, prev, mod, final_norm[None], layer=depth - 1, tm=tm_out, n_rows=n_lat, **dims)
    return out.reshape(batch, seq, d)
```

```python
import functools

import numpy as np
import jax
import jax.numpy as jnp
from jax import lax
from jax.experimental import pallas as pl
from jax.experimental.pallas import tpu as pltpu

F32 = jnp.float32
MM_DTYPE = jnp.bfloat16
ACT_DTYPE = jnp.bfloat16

NORM_EPS = 1e-6
GRID_W = 64
ROPE_BASE = 10000.0
N_MOD = 6

RET_HEADS = 4
RET_DK = 128
RET_DV = 256
RET_QK = RET_HEADS * RET_DK
RET_V = RET_HEADS * RET_DV
RET_CHUNK = 256

CONV_W = 512
POOL_WINDOWS = (2, 4, 8, 16)
POOL_GROUPS = 4
POOL_GDIM = 128
POOL_W = POOL_GROUPS * POOL_GDIM
POOL_BLOCK = 256
POOL_HALO = 16

N_GROUPS = 4
EXP_PER_GROUP = 4
N_EXPERTS = N_GROUPS * EXP_PER_GROUP
D_FF = 512
PAIR_LO = (0, 0, 0, 1, 1, 2)
PAIR_HI = (1, 2, 3, 2, 3, 3)
N_PAIRS = len(PAIR_LO)
N_BUCKETS = N_GROUPS * N_PAIRS

LANES = 128
OUT_SUBBLOCKS = 4
ROW_ALIGN = 8
ALIGN_SHIFT = 3
RUN_RARE_ROWS = 64
TAB_COUNT, TAB_LOCAL, TAB_GLOBAL, TAB_TOTAL, TAB_ROWS = 0, 1, 2, 3, 4
ROUTER_LO_ROW = 32

OFF_Q = 0
OFF_K = OFF_Q + RET_QK
OFF_V = OFF_K + RET_QK
OFF_G = OFF_V + RET_V
OFF_CB = OFF_G + RET_V
OFF_CC = OFF_CB + CONV_W
OFF_CX = OFF_CC + CONV_W
OFF_PI = OFF_CX + CONV_W
OFF_GATE = OFF_PI + POOL_W

VMEM_LIMIT = 56 * 1024 * 1024


def _cparams(sem):
    return pltpu.CompilerParams(dimension_semantics=sem, vmem_limit_bytes=VMEM_LIMIT)


def _split_bf16(a):
    hi = a.astype(jnp.bfloat16)
    lo = (a - hi.astype(F32)).astype(jnp.bfloat16)
    return hi, lo


def _dot(a, b):
    return jnp.dot(a, b, preferred_element_type=F32)


def _dot3(a, b):
    ah, al = _split_bf16(a)
    bh, bl = _split_bf16(b)
    return _dot(ah, bh) + _dot(ah, bl) + _dot(al, bh)


def _layer_spec(layer, block, index, **kw):
    return pl.BlockSpec((None,) + tuple(block), lambda *a: (layer,) + tuple(index(*a)), **kw)


def _mod_row(i, tile, n_lat, seq, batch):
    return jnp.where(i < n_lat // tile, (i * tile) // seq, batch)


def _ada_kernel(c_ref, w_ref, b_ref, o_ref):
    cv = c_ref[...]
    s = cv * jax.nn.sigmoid(cv)
    o_ref[0] = _dot3(s, w_ref[0]) + b_ref[0]


def _ada(cond, w_ada, b_ada):
    depth, d, width = w_ada.shape
    rows = cond.shape[0]
    tn = 512
    return pl.pallas_call(
        _ada_kernel,
        grid=(depth, width // tn),
        in_specs=[pl.BlockSpec((rows, d), lambda l, j: (0, 0)),
                  pl.BlockSpec((1, d, tn), lambda l, j: (l, 0, j)),
                  pl.BlockSpec((1, 1, tn), lambda l, j: (l, 0, j))],
        out_specs=pl.BlockSpec((1, rows, tn), lambda l, j: (l, 0, j)),
        out_shape=jax.ShapeDtypeStruct((depth, rows, width), F32),
        compiler_params=_cparams(("parallel", "parallel")),
    )(cond, w_ada, b_ada.reshape(depth, 1, width))


def _rms_mod(x, g, sc, sh):
    ms = jnp.mean(x * x, axis=-1, keepdims=True)
    return x * lax.rsqrt(ms + NORM_EPS) * g * (1.0 + sc) + sh


def _project_columns(h_scr, w_ref, o_ref, tn):
    for j in range(w_ref.shape[1] // tn):
        cols = slice(j * tn, (j + 1) * tn)
        o_ref[:, cols] = _dot(h_scr[...], w_ref[:, cols]).astype(o_ref.dtype)


def _inproj_kernel(x_ref, g_ref, sc_ref, sh_ref, w_ref, o_ref, h_scr, *, tn):
    h_scr[...] = _rms_mod(x_ref[...], g_ref[...], sc_ref[0], sh_ref[0]).astype(h_scr.dtype)
    _project_columns(h_scr, w_ref, o_ref, tn)


def _inproj_residual_kernel(tab_ref, x_ref, ys_ref, lpos_ref, gt_ref, g_ref, sc_ref, sh_ref, w_ref,
                            o_ref, xo_ref, h_scr, stage, sems, *, tn, cap):
    f = _expert_residual(tab_ref, ys_ref, lpos_ref, stage, sems, tile_rows=x_ref.shape[0], cap=cap)
    x = x_ref[...] + gt_ref[0] * f
    xo_ref[...] = x
    h_scr[...] = _rms_mod(x, g_ref[...], sc_ref[0], sh_ref[0]).astype(h_scr.dtype)
    _project_columns(h_scr, w_ref, o_ref, tn)


def _inproj(xs, prev, norm_g, mod, w, *, layer, tm, tn, n_lat, seq, batch):
    n, d = xs.shape
    width = w.shape[-1]
    row = functools.partial(_mod_row, tile=tm, n_lat=n_lat, seq=seq, batch=batch)
    mod_spec = lambda lyr, k: _layer_spec(lyr, (1, 1, d), lambda i, *_: (row(i), 0, k))
    x_spec = pl.BlockSpec((tm, d), lambda i, *_: (i, 0))
    p_spec = pl.BlockSpec((tm, width), lambda i, *_: (i, 0))
    p_shape = jax.ShapeDtypeStruct((n, width), ACT_DTYPE)
    tail_specs = [_layer_spec(layer, (1, d), lambda i, *_: (0, 0)),
                  mod_spec(layer, 1), mod_spec(layer, 0),
                  _layer_spec(layer, (d, width), lambda i, *_: (0, 0), pipeline_mode=pl.Buffered(1))]
    grid = (n // tm,)
    h_scratch = pltpu.VMEM((tm, d), MM_DTYPE)
    if prev is None:
        p = pl.pallas_call(functools.partial(_inproj_kernel, tn=tn), grid=grid, in_specs=[x_spec] + tail_specs,
                           out_specs=p_spec, out_shape=p_shape, scratch_shapes=[h_scratch],
                           compiler_params=_cparams(("parallel",)))(xs, norm_g, mod, mod, w)
        return p, xs
    ys, tab, lpos, cap = prev
    gs = pltpu.PrefetchScalarGridSpec(
        num_scalar_prefetch=1,
        grid=grid,
        in_specs=[x_spec, pl.BlockSpec(memory_space=pl.ANY), pl.BlockSpec((tm, 1), lambda i, *_: (i, 0)),
                  mod_spec(layer - 1, 5)] + tail_specs,
        out_specs=[p_spec, x_spec],
        scratch_shapes=[h_scratch, pltpu.VMEM((2, _sorted_rows(tm), d), F32), pltpu.SemaphoreType.DMA((2,))])
    return pl.pallas_call(
        functools.partial(_inproj_residual_kernel, tn=tn, cap=cap),
        grid_spec=gs,
        out_shape=[p_shape, jax.ShapeDtypeStruct((n, d), F32)],
        input_output_aliases={1: 1},
        compiler_params=_cparams(("arbitrary",)),
    )(tab, xs, ys, lpos, mod, norm_g, mod, mod, w)


def _ret_kernel(lg_ref, *refs, seq, n_ctx, use_rope):
    if n_ctx:
        q_ref, k_ref, v_ref, g_ref, kc_ref, vc_ref, cos_ref, sin_ref, o_ref, kt_scr, sb_scr = refs
    else:
        q_ref, k_ref, v_ref, g_ref, o_ref, kt_scr = refs
    C = RET_CHUNK
    n_chunk = seq // C
    head = pl.program_id(1)
    lgf = lg_ref[0, head]
    lgb = lg_ref[1, head]

    q = q_ref[...].astype(F32)
    k = k_ref[...].astype(F32)
    if use_rope:
        cos = cos_ref[...]
        sin = sin_ref[...]
        q = q * cos + pltpu.roll(q, RET_DK // 2, 1) * sin
        k = k * cos + pltpu.roll(k, RET_DK // 2, 1) * sin
    q = q * (RET_DK ** -0.5)
    kt_scr[...] = k.T

    ri = lax.broadcasted_iota(jnp.int32, (C, C), 0)
    ci = lax.broadcasted_iota(jnp.int32, (C, C), 1)
    rel = (ri - ci).astype(F32)
    dmask = jnp.where(rel > 0.0, jnp.exp(lgf * jnp.maximum(rel, 0.0)),
                      jnp.where(rel < 0.0, jnp.exp(lgb * jnp.maximum(-rel, 0.0)), 2.0))
    icol = lax.broadcasted_iota(jnp.int32, (C, 1), 0).astype(F32)
    jrow = lax.broadcasted_iota(jnp.int32, (1, C), 1).astype(F32)
    qdec_f = jnp.exp(lgf * (icol + 1.0))
    qdec_b = jnp.exp(lgb * (C - icol))
    kdec_f = jnp.exp(lgf * (C - 1.0 - jrow))
    kdec_b = jnp.exp(lgb * jrow)
    zrow = jnp.zeros((1, RET_DV), F32)
    cdec_f = jnp.exp(zrow + lgf * C)
    cdec_b = jnp.exp(zrow + lgb * C)

    def chunk(ref, n):
        return ref[n * C:(n + 1) * C, :]

    def kt_chunk(n):
        return kt_scr[:, n * C:(n + 1) * C]

    if n_ctx:
        kct = kc_ref[...].astype(F32).T
        vcx = vc_ref[...].astype(MM_DTYPE)
        mrow = lax.broadcasted_iota(jnp.int32, (1, n_ctx), 1).astype(F32)
        s_f = _dot((kct * jnp.exp(lgf * (n_ctx - 1.0 - mrow))).astype(MM_DTYPE), vcx)
        s_b = _dot((kct * jnp.exp(lgb * mrow)).astype(MM_DTYPE), vcx)
        sb_scr[n_chunk - 1] = s_b
        for n in range(n_chunk - 1, 0, -1):
            s_b = s_b * cdec_b + _dot((kt_chunk(n) * kdec_b).astype(MM_DTYPE), chunk(v_ref, n).astype(MM_DTYPE))
            sb_scr[n - 1] = s_b

    for n in range(n_chunk):
        qn = q[n * C:(n + 1) * C, :]
        ktn = kt_chunk(n)
        vn = chunk(v_ref, n).astype(MM_DTYPE)
        s = _dot(qn.astype(MM_DTYPE), ktn.astype(MM_DTYPE)) * dmask
        o = _dot(s.astype(MM_DTYPE), vn)
        if n_ctx:
            qcat = jnp.concatenate([qn * qdec_f, qn * qdec_b], axis=1).astype(MM_DTYPE)
            scat = jnp.concatenate([s_f, sb_scr[n]], axis=0).astype(MM_DTYPE)
            o = o + _dot(qcat, scat)
            if n + 1 < n_chunk:
                s_f = s_f * cdec_f + _dot((ktn * kdec_f).astype(MM_DTYPE), vn)
        mu = jnp.mean(o, axis=-1, keepdims=True)
        oc = o - mu
        yn = oc * lax.rsqrt(jnp.mean(oc * oc, axis=-1, keepdims=True) + NORM_EPS)
        gn = chunk(g_ref, n)
        o_ref[n * C:(n + 1) * C, :] = yn.astype(o_ref.dtype) * (gn * jax.nn.sigmoid(gn))


def _retention_latent(p, log_gamma, cos, sin, *, batch, seq, n_ctx, n_lat):
    n = p.shape[0]
    cb = n_lat // n_ctx
    kq, kv = OFF_K // RET_DK, OFF_V // RET_DV
    gs = pltpu.PrefetchScalarGridSpec(
        num_scalar_prefetch=1,
        grid=(batch, RET_HEADS),
        in_specs=[pl.BlockSpec((seq, RET_DK), lambda b, h, lg: (b, h)),
                  pl.BlockSpec((seq, RET_DK), lambda b, h, lg: (b, kq + h)),
                  pl.BlockSpec((seq, RET_DV), lambda b, h, lg: (b, kv + h)),
                  pl.BlockSpec((seq, RET_DV), lambda b, h, lg: (b, OFF_G // RET_DV + h)),
                  pl.BlockSpec((n_ctx, RET_DK), lambda b, h, lg: (cb + b, kq + h)),
                  pl.BlockSpec((n_ctx, RET_DV), lambda b, h, lg: (cb + b, kv + h)),
                  pl.BlockSpec((seq, RET_DK), lambda b, h, lg: (0, 0)),
                  pl.BlockSpec((seq, RET_DK), lambda b, h, lg: (0, 0))],
        out_specs=pl.BlockSpec((seq, RET_DV), lambda b, h, lg: (b, h)),
        scratch_shapes=[pltpu.VMEM((RET_DK, seq), F32),
                        pltpu.VMEM((seq // RET_CHUNK, RET_DK, RET_DV), F32)])
    return pl.pallas_call(
        functools.partial(_ret_kernel, seq=seq, n_ctx=n_ctx, use_rope=True),
        grid_spec=gs,
        out_shape=jax.ShapeDtypeStruct((n, RET_V), ACT_DTYPE),
        compiler_params=_cparams(("parallel", "parallel")),
    )(log_gamma, p, p, p, p, p, p, cos, sin)


def _retention_ctx(p, log_gamma, y_ret, *, batch, n_ctx, n_lat):
    cb = n_lat // n_ctx
    kq, kv = OFF_K // RET_DK, OFF_V // RET_DV
    gs = pltpu.PrefetchScalarGridSpec(
        num_scalar_prefetch=1,
        grid=(batch, RET_HEADS),
        in_specs=[pl.BlockSpec((n_ctx, RET_DK), lambda b, h, lg: (cb + b, h)),
                  pl.BlockSpec((n_ctx, RET_DK), lambda b, h, lg: (cb + b, kq + h)),
                  pl.BlockSpec((n_ctx, RET_DV), lambda b, h, lg: (cb + b, kv + h)),
                  pl.BlockSpec((n_ctx, RET_DV), lambda b, h, lg: (cb + b, OFF_G // RET_DV + h)),
                  pl.BlockSpec(memory_space=pl.ANY)],
        out_specs=pl.BlockSpec((n_ctx, RET_DV), lambda b, h, lg: (cb + b, h)),
        scratch_shapes=[pltpu.VMEM((RET_DK, n_ctx), F32)])

    def body(lg_ref, q_ref, k_ref, v_ref, g_ref, alias_ref, o_ref, kt_scr):
        del alias_ref
        _ret_kernel(lg_ref, q_ref, k_ref, v_ref, g_ref, o_ref, kt_scr, seq=n_ctx, n_ctx=0, use_rope=False)

    return pl.pallas_call(
        body,
        grid_spec=gs,
        out_shape=jax.ShapeDtypeStruct(y_ret.shape, y_ret.dtype),
        input_output_aliases={5: 0},
        compiler_params=_cparams(("parallel", "parallel")),
    )(log_gamma, p, p, p, p, y_ret)


def _convpool_kernel(*refs, seq, aliased):
    if aliased:
        cb_ref, cc_ref, cx_ref, pi_ref, cw_ref, pw_ref, ps_ref, _, _, yc_ref, yp_ref = refs
    else:
        cb_ref, cc_ref, cx_ref, pi_ref, cw_ref, pw_ref, ps_ref, yc_ref, yp_ref = refs
    grp = pl.program_id(1)
    pb, hb = POOL_BLOCK, POOL_HALO
    n_blk = seq // pb

    w = cw_ref[...]
    tl = lax.broadcasted_iota(jnp.int32, (pb, 1), 0)
    cxu = lambda a, b: cc_ref[a:b, :].astype(F32) * cx_ref[a:b, :].astype(F32)
    for blk in range(n_blk):
        r0 = blk * pb
        u = cxu(r0, r0 + pb)
        before = cxu(r0 - hb, r0)[hb - 1:hb, :] if blk > 0 else 0.0
        after = cxu(r0 + pb, r0 + pb + hb)[0:1, :] if blk + 1 < n_blk else 0.0
        u_prev = jnp.where(tl == 0, before, pltpu.roll(u, 1, 0))
        u_next = jnp.where(tl == pb - 1, after, pltpu.roll(u, pb - 1, 0))
        conv = w[0:1, :] * u_prev + w[1:2, :] * u + w[2:3, :] * u_next
        yc_ref[r0:r0 + pb, :] = (cb_ref[r0:r0 + pb, :].astype(F32) * conv).astype(yc_ref.dtype)

    half = jnp.left_shift(1, grp)

    def band(rows, cols, shift):
        dd = (lax.broadcasted_iota(jnp.int32, (rows, cols), 1) + shift
              - lax.broadcasted_iota(jnp.int32, (rows, cols), 0))
        return jnp.where((dd >= -half) & (dd < half), 1.0, 0.0).astype(MM_DTYPE)

    band_self = band(pb, pb, 0)
    band_prev = band(hb, hb, -hb)
    band_next = band(hb, hb, hb)
    blocks = range(n_blk)
    p_blks = [pi_ref[blk * pb:(blk + 1) * pb, :] for blk in blocks]
    wsums = [_dot(band_self, p_blk.astype(MM_DTYPE)) for p_blk in p_blks]
    pooled = []
    for blk in blocks:
        r0 = blk * pb
        top, mid, bot = wsums[blk][:hb], wsums[blk][hb:pb - hb], wsums[blk][pb - hb:]
        if blk > 0:
            top = top + _dot(band_prev, pi_ref[r0 - hb:r0, :].astype(MM_DTYPE))
        if blk + 1 < n_blk:
            bot = bot + _dot(band_next, pi_ref[r0 + pb:r0 + pb + hb, :].astype(MM_DTYPE))
        wsum = jnp.concatenate([top, mid, bot], axis=0)
        if 0 < blk < n_blk - 1:
            cnt = (2 * half).astype(F32)
        else:
            tb = r0 + lax.broadcasted_iota(jnp.int32, (pb, 1), 0)
            cnt = (jnp.clip(tb + half, 0, seq) - jnp.clip(tb - half, 0, seq)).astype(F32)
        pooled.append((wsum / cnt - p_blks[blk].astype(F32)).astype(MM_DTYPE))
    mixed = [_dot(pooled[blk], pw_ref[0]) for blk in blocks]
    for blk in blocks:
        yp_ref[blk * pb:(blk + 1) * pb, :] = (mixed[blk] * ps_ref[...]).astype(yp_ref.dtype)


def _convpool(p, conv_w, pool_w, pool_scale, prev, *, layer, n_seq, seq, row0):
    n = p.shape[0]
    g128 = lambda off: off // POOL_GDIM
    col = lambda off: (lambda b, g: (row0 + b, g128(off) + g))
    in_specs = [pl.BlockSpec((seq, POOL_GDIM), col(OFF_CB)),
                pl.BlockSpec((seq, POOL_GDIM), col(OFF_CC)),
                pl.BlockSpec((seq, POOL_GDIM), col(OFF_CX)),
                pl.BlockSpec((seq, POOL_GDIM), col(OFF_PI)),
                _layer_spec(layer, (conv_w.shape[1], POOL_GDIM), lambda b, g: (0, g)),
                _layer_spec(layer, (1, POOL_GDIM, POOL_GDIM), lambda b, g: (g, 0, 0)),
                _layer_spec(layer, (1, POOL_GDIM), lambda b, g: (0, g))]
    args = [p, p, p, p, conv_w, pool_w, pool_scale]
    aliases = {}
    if prev is not None:
        in_specs += [pl.BlockSpec(memory_space=pl.ANY), pl.BlockSpec(memory_space=pl.ANY)]
        args += list(prev)
        aliases = {7: 0, 8: 1}
    out_spec = pl.BlockSpec((seq, POOL_GDIM), lambda b, g: (row0 + b, g))
    return pl.pallas_call(
        functools.partial(_convpool_kernel, seq=seq, aliased=prev is not None),
        grid=(n_seq, POOL_GROUPS),
        in_specs=in_specs,
        out_specs=[out_spec, out_spec],
        out_shape=[jax.ShapeDtypeStruct((n, CONV_W), ACT_DTYPE), jax.ShapeDtypeStruct((n, POOL_W), ACT_DTYPE)],
        input_output_aliases=aliases,
        compiler_params=_cparams(("parallel", "parallel")),
    )(*args)


def _first_max(vals):
    top = functools.reduce(jnp.maximum, vals)
    idx = jnp.full(top.shape, len(vals) - 1, jnp.int32)
    for k in range(len(vals) - 2, -1, -1):
        idx = jnp.where(vals[k] == top, k, idx)
    return top, idx


def _route(logits_t):
    row = lambda k: logits_t[k:k + 1, :]
    groups = [row(g) for g in range(N_GROUPS)]
    gmax, gidx = _first_max(groups)
    gtop = 1.0 / sum(jnp.exp(g - gmax) for g in groups)
    experts = []
    for k in range(EXP_PER_GROUP):
        e = row(N_GROUPS + (N_GROUPS - 1) * EXP_PER_GROUP + k)
        for g in range(N_GROUPS - 2, -1, -1):
            e = jnp.where(gidx == g, row(N_GROUPS + g * EXP_PER_GROUP + k), e)
        experts.append(e)
    m1, i1 = _first_max(experts)
    m2, i2 = _first_max([jnp.where(i1 == k, -jnp.inf, e) for k, e in enumerate(experts)])
    e2 = jnp.exp(m2 - m1)
    w1 = gtop / (1.0 + e2)
    w2 = gtop * e2 / (1.0 + e2)
    first_lower = i1 < i2
    lo = jnp.minimum(i1, i2)
    hi = jnp.maximum(i1, i2)
    pair = lo * 3 - jnp.right_shift(lo * (lo - 1), 1) + hi - lo - 1
    bucket = gidx * N_PAIRS + pair
    wlo = jnp.where(first_lower, w1, w2)
    whi = jnp.where(first_lower, w2, w1)
    return bucket, wlo, whi


def _column_to_lanes(col):
    return jnp.broadcast_to(col, (col.shape[0], LANES)).T[0:1, :]


def _row_to_column(row):
    return jnp.broadcast_to(row, (8, row.shape[1])).T[:, 0:1]


def _dot_nt(a, b):
    return lax.dot_general(a, b, (((1,), (1,)), ((), ())), preferred_element_type=F32)


def _bucket_run_copies(read_run, max_rows, make_copy):
    def piece(n, local, glob, size):
        @pl.when((n & size) != 0)
        def _():
            off = n & (-2 * size)
            make_copy(pl.multiple_of(local + off, ROW_ALIGN), pl.multiple_of(glob + off, ROW_ALIGN), size).start()

    top = 1 << (max_rows.bit_length() - 1)
    sizes = [top >> k for k in range(top.bit_length()) if top >> k >= ROW_ALIGN]
    rare = [s for s in sizes if s >= RUN_RARE_ROWS]

    def per_bucket(b, carry_):
        n, local, glob = read_run(b)

        @pl.when(n >= RUN_RARE_ROWS)
        def _():
            for size in rare:
                piece(n, local, glob, size)

        for size in sizes[len(rare):]:
            piece(n, local, glob, size)
        return carry_

    lax.fori_loop(0, N_BUCKETS, per_bucket, 0)


def _wait_rows(total, max_rows, make_copy):
    size = 1 << (max_rows.bit_length() - 1)
    while size >= ROW_ALIGN:
        @pl.when((total & size) != 0)
        def _(size=size):
            make_copy(size).wait()
        size //= 2


def _exact_bf16_pieces(w):
    a = w.astype(jnp.bfloat16).astype(F32)
    b = (w - a).astype(jnp.bfloat16).astype(F32)
    c = (w - a - b).astype(jnp.bfloat16).astype(F32)
    return a, b, c


def _outproj_kernel(x_ref, yr_ref, yc_ref, yp_ref, g0_ref, g1_ref, g2_ref, gt_ref, sc_ref, sh_ref, ng_ref,
                    wr_ref, wc_ref, wp_ref, wo_ref, wrt_ref, brt_ref,
                    xo_ref, hs_ref, cnt_ref, tab_ref, lpos_ref,
                    h_scr, tab_vmem, tab_smem, sent_smem, carry, row_sem, tab_sem, *, cap):
    i = pl.program_id(0)
    n_steps = pl.num_programs(0)
    slot = lax.rem(i, 2)
    tm, d = x_ref.shape
    ts = h_scr.shape[1]

    def wait_rows(s):
        _wait_rows(sent_smem[s], ts, lambda size: pltpu.make_async_copy(
            h_scr.at[s, pl.ds(0, size)], hs_ref.at[pl.ds(0, size)], row_sem.at[s]))

    @pl.when(i == 0)
    def _():
        carry[...] = jnp.zeros_like(carry)

    @pl.when(i >= 2)
    def _():
        wait_rows(slot)

    sub = tm // OUT_SUBBLOCKS
    blocks = [slice(a, a + sub) for a in range(0, tm, sub)]
    gate = lambda r, rows: jax.nn.sigmoid(r[rows, :].astype(F32))
    merged = [(gate(g0_ref, rows) * _dot(yr_ref[rows, :], wr_ref[...])
               + gate(g1_ref, rows) * _dot(yc_ref[rows, :], wc_ref[...])
               + gate(g2_ref, rows) * _dot(yp_ref[rows, :], wp_ref[...])).astype(MM_DTYPE) for rows in blocks]
    ys = [_dot(m, wo_ref[...]) for m in merged]
    xs_new = [x_ref[rows, :] + gt_ref[0] * y for rows, y in zip(blocks, ys)]
    for rows, x in zip(blocks, xs_new):
        xo_ref[rows, :] = x
    hs = [_rms_mod(x, ng_ref[...], sc_ref[0], sh_ref[0]) for x in xs_new]
    splits = [_split_bf16(hb) for hb in hs]
    s2s = [_dot_nt(wrt_ref[...], hh) + _dot_nt(wrt_ref[...], hl) for hh, hl in splits]
    routes = [_route(s2[0:ROUTER_LO_ROW, :] + s2[ROUTER_LO_ROW:2 * ROUTER_LO_ROW, :] + brt_ref[...]) for s2 in s2s]
    h = jnp.concatenate([hb.astype(MM_DTYPE) for hb in hs], axis=0)
    bucket, wlo, whi = (jnp.concatenate([route[k] for route in routes], axis=1) for k in (0, 1, 2))

    brow = lax.broadcasted_iota(jnp.int32, (LANES, tm), 0)
    onehot = brow == bucket
    ones = jnp.where(onehot, 1.0, 0.0)
    r = lax.broadcasted_iota(jnp.int32, (tm, tm), 0)
    c = lax.broadcasted_iota(jnp.int32, (tm, tm), 1)
    tri = jnp.where(r <= c, 1.0, 0.0).astype(jnp.bfloat16)
    incl = _dot(ones.astype(jnp.bfloat16), tri)
    cnt_col = incl[:, tm - 1:tm].astype(jnp.int32)
    units_col = jnp.right_shift(cnt_col + (ROW_ALIGN - 1), ALIGN_SHIFT)
    below = jnp.where(lax.broadcasted_iota(jnp.int32, (LANES, LANES), 0)
                      > lax.broadcasted_iota(jnp.int32, (LANES, LANES), 1), 1.0, 0.0).astype(jnp.bfloat16)
    units_b = jnp.broadcast_to(units_col.astype(F32), (LANES, LANES)).astype(jnp.bfloat16)
    start_col = _dot(below, units_b)[:, 0:1] * float(ROW_ALIGN)
    lpos = jnp.sum(jnp.where(onehot, incl - 1.0 + start_col, 0.0), axis=0, keepdims=True)
    lpos_ref[...] = _row_to_column(lpos).astype(jnp.int32)
    run_col = (units_col * ROW_ALIGN).astype(F32)
    run_len = _column_to_lanes(run_col).astype(jnp.int32)
    local_start = _column_to_lanes(start_col).astype(jnp.int32)
    total = (start_col + run_col)[LANES - 1:LANES, :].astype(jnp.int32)
    srow = lax.broadcasted_iota(jnp.int32, (8, LANES), 0)
    tab = jnp.where(srow == TAB_COUNT, run_len,
                    jnp.where(srow == TAB_LOCAL, local_start,
                              jnp.where(srow == TAB_GLOBAL, carry[...], jnp.where(srow == TAB_TOTAL, total, 0))))
    carry[...] = carry[...] + run_len
    tab_ref[0] = tab
    tab_vmem[...] = tab
    to_smem = pltpu.make_async_copy(tab_vmem, tab_smem, tab_sem)
    to_smem.start()

    srt = lax.broadcasted_iota(jnp.int32, (ts, tm), 0)
    perm = jnp.where(srt == lpos.astype(jnp.int32), 1.0, 0.0).astype(MM_DTYPE)
    pieces = _exact_bf16_pieces(wlo) + _exact_bf16_pieces(whi)
    meta_t = jnp.zeros((LANES, tm), F32)
    for k, piece in enumerate(pieces):
        meta_t = jnp.where(brow == k, piece, meta_t)
    h_scr[slot, :, :d] = _dot(perm, h)
    h_scr[slot, :, d:] = _dot_nt(perm, meta_t.astype(MM_DTYPE))
    to_smem.wait()
    sent_smem[slot] = tab_smem[TAB_TOTAL, 0]

    _bucket_run_copies(
        lambda b: (tab_smem[TAB_COUNT, b], tab_smem[TAB_LOCAL, b], b * cap + tab_smem[TAB_GLOBAL, b]), tm,
        lambda local, glob, size: pltpu.make_async_copy(h_scr.at[slot, pl.ds(local, size)],
                                                        hs_ref.at[pl.ds(glob, size)], row_sem.at[slot]))

    @pl.when(i == n_steps - 1)
    def _():
        cnt_ref[...] = jnp.broadcast_to(carry[...].astype(jnp.int32), cnt_ref.shape)
        wait_rows(slot)

    @pl.when((i == n_steps - 1) & (i >= 1))
    def _():
        wait_rows(1 - slot)


def _sorted_rows(tm):
    return tm + N_BUCKETS * ROW_ALIGN


def _outproj(xs, p, y_ret, y_conv, y_pool, mod, norm_g, w_ret, w_conv, w_pool, w_o, w_router, b_router,
             *, layer, tm, cap, n_rows, n_lat, seq, batch):
    n, d = xs.shape
    assert cap >= n_rows + (n_rows // tm) * ROW_ALIGN and cap % ROW_ALIGN == 0
    width = d + LANES
    ts = _sorted_rows(tm)
    row = functools.partial(_mod_row, tile=tm, n_lat=n_lat, seq=seq, batch=batch)
    gate = lambda k: pl.BlockSpec((tm, d), lambda i: (i, OFF_GATE // d + k))
    mod_spec = lambda k: _layer_spec(layer, (1, 1, d), lambda i: (row(i), 0, k))
    full = lambda a: _layer_spec(layer, a.shape[1:], lambda i: (0,) * (a.ndim - 1))
    return pl.pallas_call(
        functools.partial(_outproj_kernel, cap=cap),
        grid=(n_rows // tm,),
        in_specs=[pl.BlockSpec((tm, d), lambda i: (i, 0)),
                  pl.BlockSpec((tm, RET_V), lambda i: (i, 0)),
                  pl.BlockSpec((tm, CONV_W), lambda i: (i, 0)),
                  pl.BlockSpec((tm, POOL_W), lambda i: (i, 0)),
                  gate(0), gate(1), gate(2),
                  mod_spec(2), mod_spec(4), mod_spec(3),
                  full(norm_g), full(w_ret), full(w_conv), full(w_pool), full(w_o), full(w_router), full(b_router)],
        out_specs=[pl.BlockSpec((tm, d), lambda i: (i, 0)),
                   pl.BlockSpec(memory_space=pl.ANY),
                   pl.BlockSpec((8, LANES), lambda i: (0, 0)),
                   pl.BlockSpec((1, 8, LANES), lambda i: (i, 0, 0)),
                   pl.BlockSpec((tm, 1), lambda i: (i, 0))],
        out_shape=[jax.ShapeDtypeStruct((n, d), F32),
                   jax.ShapeDtypeStruct((N_BUCKETS * cap, width), F32),
                   jax.ShapeDtypeStruct((8, LANES), jnp.int32),
                   jax.ShapeDtypeStruct((n_rows // tm, 8, LANES), jnp.int32),
                   jax.ShapeDtypeStruct((n_rows, 1), jnp.int32)],
        scratch_shapes=[pltpu.VMEM((2, ts, width), F32),
                        pltpu.VMEM((8, LANES), jnp.int32),
                        pltpu.SMEM((8, LANES), jnp.int32),
                        pltpu.SMEM((2,), jnp.int32),
                        pltpu.VMEM((1, LANES), jnp.int32),
                        pltpu.SemaphoreType.DMA((2,)),
                        pltpu.SemaphoreType.DMA(())],
        input_output_aliases={0: 0},
        compiler_params=_cparams(("arbitrary",)),
    )(xs, y_ret, y_conv, y_pool, p, p, p, mod, mod, mod, norm_g, w_ret, w_conv, w_pool, w_o, w_router, b_router)


def _moe_kernel(blk_ref, elo_ref, ehi_ref, valid_ref, hs_ref, w1l_ref, w1h_ref, w3l_ref, w3h_ref, w2l_ref, w2h_ref,
                ys_ref):
    del blk_ref, elo_ref, ehi_ref
    valid = valid_ref[pl.program_id(0)]
    tm, d = ys_ref.shape

    @pl.when(valid > 0)
    def _():
        keep = lax.broadcasted_iota(jnp.int32, (tm, 1), 0) < valid
        meta = jnp.where(keep, hs_ref[:, d:], 0.0)
        h = jnp.where(keep, hs_ref[:, :d], 0.0).astype(MM_DTYPE)
        wlo = meta[:, 0:1] + meta[:, 1:2] + meta[:, 2:3]
        whi = meta[:, 3:4] + meta[:, 4:5] + meta[:, 5:6]

        ups = [(_dot(h, w1[0]), _dot(h, w3[0])) for w1, w3 in ((w1l_ref, w3l_ref), (w1h_ref, w3h_ref))]
        acts = [(a * jax.nn.sigmoid(a) * b).astype(MM_DTYPE) for a, b in ups]
        y_lo, y_hi = (_dot(act, w2[0]) for act, w2 in zip(acts, (w2l_ref, w2h_ref)))
        y = wlo * y_lo + whi * y_hi
        ys_ref[...] = y.astype(ACT_DTYPE).astype(F32)


def _moe(blk, elo, ehi, valid, hs, w1, w3, w2, *, layer, tm, d):
    n_sorted, width = hs.shape
    n_work = blk.shape[0]
    up = lambda sel: _layer_spec(layer, (1, d, D_FF), lambda s, blk, elo, ehi, valid: ((elo, ehi)[sel][s], 0, 0))
    down = lambda sel: _layer_spec(layer, (1, D_FF, d), lambda s, blk, elo, ehi, valid: ((elo, ehi)[sel][s], 0, 0))
    gs = pltpu.PrefetchScalarGridSpec(
        num_scalar_prefetch=4,
        grid=(n_work,),
        in_specs=[pl.BlockSpec((tm, width), lambda s, blk, elo, ehi, valid: (blk[s], 0)),
                  up(0), up(1), up(0), up(1), down(0), down(1)],
        out_specs=pl.BlockSpec((tm, d), lambda s, blk, elo, ehi, valid: (blk[s], 0)))
    return pl.pallas_call(
        _moe_kernel,
        grid_spec=gs,
        out_shape=jax.ShapeDtypeStruct((n_sorted, d), F32),
        compiler_params=_cparams(("arbitrary",)),
    )(blk, elo, ehi, valid, hs, w1, w1, w3, w3, w2, w2)


def _expert_residual(tab_ref, ys_ref, lpos_ref, stage, sems, *, tile_rows, cap):
    i = pl.program_id(0)
    slot = lax.rem(i, 2)
    ts = stage.shape[1]
    entry = lambda tile, row, lane: tab_ref[tile * (TAB_ROWS * LANES) + row * LANES + lane]

    def start(tile, s):
        _bucket_run_copies(
            lambda b: (entry(tile, TAB_COUNT, b), entry(tile, TAB_LOCAL, b), b * cap + entry(tile, TAB_GLOBAL, b)),
            tile_rows,
            lambda local, glob, size: pltpu.make_async_copy(
                ys_ref.at[pl.ds(glob, size)], stage.at[s, pl.ds(local, size)], sems.at[s]))

    @pl.when(i == 0)
    def _():
        start(i, slot)

    @pl.when(i + 1 < pl.num_programs(0))
    def _():
        start(i + 1, 1 - slot)

    total = entry(i, TAB_TOTAL, 0)
    _wait_rows(total, ts, lambda size: pltpu.make_async_copy(
        ys_ref.at[pl.ds(0, size)], stage.at[slot, pl.ds(0, size)], sems.at[slot]))
    lpos = lpos_ref[...]
    unperm = jnp.where(lpos == lax.broadcasted_iota(jnp.int32, (tile_rows, ts), 1), 1.0, 0.0).astype(MM_DTYPE)
    filled = lax.broadcasted_iota(jnp.int32, (ts, 1), 0) < total
    return _dot(unperm, jnp.where(filled, stage[slot], 0.0).astype(MM_DTYPE))


def _final_kernel(tab_ref, x_ref, ys_ref, lpos_ref, gt_ref, g_ref, o_ref, stage, sems, *, cap):
    f = _expert_residual(tab_ref, ys_ref, lpos_ref, stage, sems, tile_rows=x_ref.shape[0], cap=cap)
    x = x_ref[...] + gt_ref[0] * f
    o_ref[...] = x * lax.rsqrt(jnp.mean(x * x, axis=-1, keepdims=True) + NORM_EPS) * g_ref[...]


def _final(xs, moe_out, mod, final_g, *, layer, tm, n_rows, n_lat, seq, batch):
    ys, tab, lpos, cap = moe_out
    d = xs.shape[1]
    row = functools.partial(_mod_row, tile=tm, n_lat=n_lat, seq=seq, batch=batch)
    x_spec = pl.BlockSpec((tm, d), lambda i, tab: (i, 0))
    gs = pltpu.PrefetchScalarGridSpec(
        num_scalar_prefetch=1,
        grid=(n_rows // tm,),
        in_specs=[x_spec,
                  pl.BlockSpec(memory_space=pl.ANY),
                  pl.BlockSpec((tm, 1), lambda i, tab: (i, 0)),
                  _layer_spec(layer, (1, 1, d), lambda i, tab: (row(i), 0, 5)),
                  pl.BlockSpec((1, d), lambda i, tab: (0, 0))],
        out_specs=x_spec,
        scratch_shapes=[pltpu.VMEM((2, _sorted_rows(tm), d), F32), pltpu.SemaphoreType.DMA((2,))])
    return pl.pallas_call(
        functools.partial(_final_kernel, cap=cap),
        grid_spec=gs,
        out_shape=jax.ShapeDtypeStruct((n_rows, d), F32),
        compiler_params=_cparams(("arbitrary",)),
    )(tab, xs, ys, lpos, mod, final_g)


def _rope_tables(seq):
    rows = seq // GRID_W
    row = jnp.repeat(jnp.arange(rows, dtype=F32), GRID_W)
    col = jnp.tile(jnp.arange(GRID_W, dtype=F32), rows)
    n_freq = RET_DK // 4
    inv_freq = ROPE_BASE ** (-jnp.arange(n_freq, dtype=F32) / n_freq)
    ang = jnp.concatenate([row[:, None] * inv_freq[None, :], col[:, None] * inv_freq[None, :]], axis=-1)
    cos, sin = jnp.cos(ang), jnp.sin(ang)
    return jnp.concatenate([cos, cos], axis=-1), jnp.concatenate([-sin, sin], axis=-1)


def _work_tables(counts, *, tm, cap, n_work):
    cnt = counts[0, :N_BUCKETS]
    tiles = (cnt + tm - 1) // tm
    ends = jnp.cumsum(tiles)
    starts = ends - tiles
    item = jnp.arange(n_work, dtype=jnp.int32)
    used = item < ends[-1]
    ref_item = jnp.minimum(item, jnp.maximum(ends[-1] - 1, 0))
    member = ((ref_item[:, None] >= starts[None, :]) & (ref_item[:, None] < ends[None, :])).astype(jnp.int32)
    pick = lambda per_bucket: jnp.sum(member * per_bucket[None, :], axis=1)
    buckets = np.arange(N_BUCKETS)
    j = ref_item - pick(starts)
    valid = jnp.where(used, jnp.clip(pick(cnt) - j * tm, 0, tm), 0)
    blk = pick(jnp.asarray(buckets * (cap // tm), jnp.int32)) + j
    first = (buckets // N_PAIRS) * EXP_PER_GROUP
    elo = pick(jnp.asarray(first + np.asarray(PAIR_LO)[buckets % N_PAIRS], jnp.int32))
    ehi = pick(jnp.asarray(first + np.asarray(PAIR_HI)[buckets % N_PAIRS], jnp.int32))
    i32 = lambda a: a.astype(jnp.int32)
    return i32(blk), i32(elo), i32(ehi), i32(valid)


def _router_weights(w_rg, b_rg, w_re, b_re):
    w = jnp.swapaxes(jnp.concatenate([w_rg, w_re], axis=-1).astype(F32), 1, 2)
    depth, n_out, d = w.shape
    assert n_out <= ROUTER_LO_ROW and 2 * ROUTER_LO_ROW <= LANES and EXP_PER_GROUP == 4
    hi = w.astype(jnp.bfloat16)
    lo = (w - hi.astype(F32)).astype(jnp.bfloat16)
    packed = jnp.zeros((depth, LANES, d), jnp.bfloat16)
    packed = packed.at[:, :n_out].set(hi).at[:, ROUTER_LO_ROW:ROUTER_LO_ROW + n_out].set(lo)
    bias = jnp.zeros((depth, ROUTER_LO_ROW, 1), F32)
    bias = bias.at[:, :n_out, 0].set(jnp.concatenate([b_rg, b_re], axis=-1).astype(F32))
    return packed, bias


def _pick_tile(n, want, *also):
    t = want
    while n % t or any(a % t for a in also):
        t //= 2
    return t


def kernel(x, c, ctx, c_ctx, w_ada, b_ada, norm1, norm2, w_in, ret_decay, conv_w, pool_w, pool_scale, w_ret_out,
           w_conv_out, w_pool_out, w_o, w_rg, b_rg, w_re, b_re, w1, w3, w2, final_norm):
    batch, seq, d = x.shape
    n_ctx = ctx.shape[1]
    depth = w_ada.shape[0]
    n_lat, n_c = batch * seq, batch * n_ctx
    n = n_lat + n_c
    assert POOL_WINDOWS == (2, 4, 8, 16) and POOL_HALO >= max(POOL_WINDOWS) // 2
    assert seq % RET_CHUNK == 0 and n_ctx == RET_CHUNK and seq % GRID_W == 0 and n_lat % n_ctx == 0

    tm_out = _pick_tile(seq, 512, n_c)
    tm_moe = 256

    xs = jnp.concatenate([x.reshape(n_lat, d), ctx.reshape(n_c, d)], axis=0)
    mod_rows = -(-(batch + 1) // 8) * 8
    cond = jnp.zeros((mod_rows, d), F32).at[:batch].set(c).at[batch].set(c_ctx)
    mod = _ada(cond, w_ada, b_ada).reshape(depth, mod_rows, 1, N_MOD * d)
    cos, sin = _rope_tables(seq)
    log_gamma = jax.nn.log_sigmoid(ret_decay.astype(F32))

    mm = lambda a: a.astype(MM_DTYPE)
    norm1_s, norm2_s, pool_scale_s = norm1[:, None, :], norm2[:, None, :], pool_scale[:, None, :]
    w_in_b, pool_w_b = mm(w_in), mm(pool_w)
    w_ret_b, w_conv_b, w_pool_b, w_o_b = mm(w_ret_out), mm(w_conv_out), mm(w_pool_out), mm(w_o)
    w1_b, w3_b, w2_b = mm(w1), mm(w3), mm(w2)
    w_router, b_router = _router_weights(w_rg, b_rg, w_re, b_re)
    dims = dict(n_lat=n_lat, seq=seq, batch=batch)
    prev = None
    for l in range(depth):
        last = l == depth - 1
        rows = n_lat if last else n
        p, xs = _inproj(xs, prev, norm1_s, mod, w_in_b, layer=l, tm=tm_out, tn=1024, **dims)
        y_ret = _retention_latent(p, log_gamma[l], cos, sin, batch=batch, seq=seq, n_ctx=n_ctx, n_lat=n_lat)
        y_conv, y_pool = _convpool(p, conv_w, pool_w_b, pool_scale_s, None, layer=l, n_seq=batch, seq=seq, row0=0)
        if not last:
            y_ret = _retention_ctx(p, log_gamma[l], y_ret, batch=batch, n_ctx=n_ctx, n_lat=n_lat)
            y_conv, y_pool = _convpool(p, conv_w, pool_w_b, pool_scale_s, (y_conv, y_pool), layer=l,
                                       n_seq=batch, seq=n_ctx, row0=n_lat // n_ctx)
        n_tiles = rows // tm_out
        cap = -(-(rows + n_tiles * ROW_ALIGN) // tm_moe) * tm_moe
        max_sorted = rows + n_tiles * N_BUCKETS * (ROW_ALIGN - 1)
        xs, hs, counts, tab, lpos = _outproj(xs, p, y_ret, y_conv, y_pool, mod, norm2_s, w_ret_b, w_conv_b, w_pool_b,
                                             w_o_b, w_router, b_router, layer=l, tm=tm_out, cap=cap, n_rows=rows,
                                             **dims)
        blk, elo, ehi, valid = _work_tables(counts, tm=tm_moe, cap=cap, n_work=-(-max_sorted // tm_moe) + N_BUCKETS)
        ys = _moe(blk, elo, ehi, valid, hs, w1_b, w3_b, w2_b, layer=l, tm=tm_moe, d=d)
        prev = (ys, tab[:, :TAB_ROWS, :].reshape(-1), lpos, cap)
    out = _final(xs, prev, mod, final_norm[None], layer=depth - 1, tm=tm_out, n_rows=n_lat, **dims)
    return out.reshape(batch, seq, d)
```

```python
import functools

import numpy as np
import jax
import jax.numpy as jnp
from jax import lax
from jax.experimental import pallas as pl
from jax.experimental.pallas import tpu as pltpu

F32 = jnp.float32
MM_DTYPE = jnp.bfloat16
ACT_DTYPE = jnp.bfloat16

NORM_EPS = 1e-6
GRID_W = 64
ROPE_BASE = 10000.0
N_MOD = 6

RET_HEADS = 4
RET_DK = 128
RET_DV = 256
RET_QK = RET_HEADS * RET_DK
RET_V = RET_HEADS * RET_DV
RET_CHUNK = 256
RET_GROUP = 2

CONV_W = 512
CONV_GROUP = 2
POOL_WINDOWS = (2, 4, 8, 16)
POOL_GROUPS = 4
POOL_GDIM = 128
POOL_W = POOL_GROUPS * POOL_GDIM
POOL_BLOCK = 256
POOL_HALO = 16

N_GROUPS = 4
EXP_PER_GROUP = 4
N_EXPERTS = N_GROUPS * EXP_PER_GROUP
D_FF = 512
PAIR_LO = (0, 0, 0, 1, 1, 2)
PAIR_HI = (1, 2, 3, 2, 3, 3)
N_PAIRS = len(PAIR_LO)
N_BUCKETS = N_GROUPS * N_PAIRS

LANES = 128
OUT_SUBBLOCKS = 4
ROW_ALIGN = 8
ALIGN_SHIFT = 3
RUN_RARE_ROWS = 64
TAB_COUNT, TAB_LOCAL, TAB_GLOBAL, TAB_TOTAL, TAB_ROWS = 0, 1, 2, 3, 4
ROUTER_LO_ROW = 32

OFF_Q = 0
OFF_K = OFF_Q + RET_QK
OFF_V = OFF_K + RET_QK
OFF_G = OFF_V + RET_V
OFF_CB = OFF_G + RET_V
OFF_CC = OFF_CB + CONV_W
OFF_CX = OFF_CC + CONV_W
OFF_PI = OFF_CX + CONV_W
OFF_GATE = OFF_PI + POOL_W

VMEM_LIMIT = 56 * 1024 * 1024


def _cparams(sem):
    return pltpu.CompilerParams(dimension_semantics=sem, vmem_limit_bytes=VMEM_LIMIT)


def _split_bf16(a):
    hi = a.astype(jnp.bfloat16)
    lo = (a - hi.astype(F32)).astype(jnp.bfloat16)
    return hi, lo


def _dot(a, b):
    return jnp.dot(a, b, preferred_element_type=F32)


def _dot3(a, b):
    ah, al = _split_bf16(a)
    bh, bl = _split_bf16(b)
    return _dot(ah, bh) + _dot(ah, bl) + _dot(al, bh)


def _layer_spec(layer, block, index, **kw):
    return pl.BlockSpec((None,) + tuple(block), lambda *a: (layer,) + tuple(index(*a)), **kw)


def _mod_row(i, tile, n_lat, seq, batch):
    return jnp.where(i < n_lat // tile, (i * tile) // seq, batch)


def _ada_kernel(c_ref, w_ref, b_ref, o_ref):
    cv = c_ref[...]
    s = cv * jax.nn.sigmoid(cv)
    o_ref[0] = _dot3(s, w_ref[0]) + b_ref[0]


def _ada(cond, w_ada, b_ada):
    depth, d, width = w_ada.shape
    rows = cond.shape[0]
    tn = 512
    return pl.pallas_call(
        _ada_kernel,
        grid=(depth, width // tn),
        in_specs=[pl.BlockSpec((rows, d), lambda l, j: (0, 0)),
                  pl.BlockSpec((1, d, tn), lambda l, j: (l, 0, j)),
                  pl.BlockSpec((1, 1, tn), lambda l, j: (l, 0, j))],
        out_specs=pl.BlockSpec((1, rows, tn), lambda l, j: (l, 0, j)),
        out_shape=jax.ShapeDtypeStruct((depth, rows, width), F32),
        compiler_params=_cparams(("parallel", "parallel")),
    )(cond, w_ada, b_ada.reshape(depth, 1, width))


def _rms_mod(x, g, sc, sh):
    ms = jnp.mean(x * x, axis=-1, keepdims=True)
    return x * lax.rsqrt(ms + NORM_EPS) * g * (1.0 + sc) + sh


def _project_columns(h_scr, w_ref, o_ref, tn):
    for j in range(w_ref.shape[1] // tn):
        cols = slice(j * tn, (j + 1) * tn)
        o_ref[:, cols] = _dot(h_scr[...], w_ref[:, cols]).astype(o_ref.dtype)


def _inproj_kernel(x_ref, g_ref, sc_ref, sh_ref, w_ref, o_ref, h_scr, *, tn):
    h_scr[...] = _rms_mod(x_ref[...], g_ref[...], sc_ref[0], sh_ref[0]).astype(h_scr.dtype)
    _project_columns(h_scr, w_ref, o_ref, tn)


def _inproj_residual_kernel(tab_ref, x_ref, ys_ref, lpos_ref, gt_ref, g_ref, sc_ref, sh_ref, w_ref,
                            o_ref, xo_ref, h_scr, stage, sems, *, tn, cap):
    f = _expert_residual(tab_ref, ys_ref, lpos_ref, stage, sems, tile_rows=x_ref.shape[0], cap=cap)
    x = x_ref[...] + gt_ref[0] * f
    xo_ref[...] = x
    h_scr[...] = _rms_mod(x, g_ref[...], sc_ref[0], sh_ref[0]).astype(h_scr.dtype)
    _project_columns(h_scr, w_ref, o_ref, tn)


def _inproj(xs, prev, norm_g, mod, w, *, layer, tm, tn, n_lat, seq, batch):
    n, d = xs.shape
    width = w.shape[-1]
    row = functools.partial(_mod_row, tile=tm, n_lat=n_lat, seq=seq, batch=batch)
    mod_spec = lambda lyr, k: _layer_spec(lyr, (1, 1, d), lambda i, *_: (row(i), 0, k))
    x_spec = pl.BlockSpec((tm, d), lambda i, *_: (i, 0))
    p_spec = pl.BlockSpec((tm, width), lambda i, *_: (i, 0))
    p_shape = jax.ShapeDtypeStruct((n, width), ACT_DTYPE)
    tail_specs = [_layer_spec(layer, (1, d), lambda i, *_: (0, 0)),
                  mod_spec(layer, 1), mod_spec(layer, 0),
                  _layer_spec(layer, (d, width), lambda i, *_: (0, 0), pipeline_mode=pl.Buffered(1))]
    grid = (n // tm,)
    h_scratch = pltpu.VMEM((tm, d), MM_DTYPE)
    if prev is None:
        p = pl.pallas_call(functools.partial(_inproj_kernel, tn=tn), grid=grid, in_specs=[x_spec] + tail_specs,
                           out_specs=p_spec, out_shape=p_shape, scratch_shapes=[h_scratch],
                           compiler_params=_cparams(("parallel",)))(xs, norm_g, mod, mod, w)
        return p, xs
    ys, tab, lpos, cap = prev
    gs = pltpu.PrefetchScalarGridSpec(
        num_scalar_prefetch=1,
        grid=grid,
        in_specs=[x_spec, pl.BlockSpec(memory_space=pl.ANY), pl.BlockSpec((tm, 1), lambda i, *_: (i, 0)),
                  mod_spec(layer - 1, 5)] + tail_specs,
        out_specs=[p_spec, x_spec],
        scratch_shapes=[h_scratch, pltpu.VMEM((2, _sorted_rows(tm), d), F32), pltpu.SemaphoreType.DMA((2,))])
    return pl.pallas_call(
        functools.partial(_inproj_residual_kernel, tn=tn, cap=cap),
        grid_spec=gs,
        out_shape=[p_shape, jax.ShapeDtypeStruct((n, d), F32)],
        input_output_aliases={1: 1},
        compiler_params=_cparams(("arbitrary",)),
    )(tab, xs, ys, lpos, mod, norm_g, mod, mod, w)


def _ret_kernel(lg_ref, *refs, seq, n_ctx, use_rope):
    if n_ctx:
        q_ref, k_ref, v_ref, g_ref, kc_ref, vc_ref, cos_ref, sin_ref, o_ref, q_scr, kt_scr, sb_scr = refs
    else:
        q_ref, k_ref, v_ref, g_ref, o_ref, q_scr, kt_scr = refs
    C = RET_CHUNK
    n_chunk = seq // C
    head = pl.program_id(1)
    lgf = lg_ref[0, head]
    lgb = lg_ref[1, head]

    for n in range(n_chunk):
        rows = slice(n * C, (n + 1) * C)
        qn = q_ref[rows, :].astype(F32)
        kn = k_ref[rows, :].astype(F32)
        if use_rope:
            cos = cos_ref[rows, :]
            sin = sin_ref[rows, :]
            qn = qn * cos + pltpu.roll(qn, RET_DK // 2, 1) * sin
            kn = kn * cos + pltpu.roll(kn, RET_DK // 2, 1) * sin
        q_scr[rows, :] = (qn * (RET_DK ** -0.5)).astype(q_scr.dtype)
        kt_scr[:, rows] = kn.T.astype(kt_scr.dtype)

    ri = lax.broadcasted_iota(jnp.int32, (C, C), 0)
    ci = lax.broadcasted_iota(jnp.int32, (C, C), 1)
    rel = (ri - ci).astype(F32)
    dmask = jnp.where(rel > 0.0, jnp.exp(lgf * jnp.maximum(rel, 0.0)),
                      jnp.where(rel < 0.0, jnp.exp(lgb * jnp.maximum(-rel, 0.0)), 2.0))
    icol = lax.broadcasted_iota(jnp.int32, (C, 1), 0).astype(F32)
    jrow = lax.broadcasted_iota(jnp.int32, (1, C), 1).astype(F32)
    qdec_f = jnp.exp(lgf * (icol + 1.0))
    qdec_b = jnp.exp(lgb * (C - icol))
    kdec_f = jnp.exp(lgf * (C - 1.0 - jrow))
    kdec_b = jnp.exp(lgb * jrow)
    zrow = jnp.zeros((1, RET_DV), F32)
    cdec_f = jnp.exp(zrow + lgf * C)
    cdec_b = jnp.exp(zrow + lgb * C)

    def chunk(ref, n):
        return ref[n * C:(n + 1) * C, :]

    def kt_chunk(n):
        return kt_scr[:, n * C:(n + 1) * C]

    if n_ctx:
        kct = kc_ref[...].astype(F32).T
        vcx = vc_ref[...].astype(MM_DTYPE)
        mrow = lax.broadcasted_iota(jnp.int32, (1, n_ctx), 1).astype(F32)
        s_f = _dot((kct * jnp.exp(lgf * (n_ctx - 1.0 - mrow))).astype(MM_DTYPE), vcx)
        s_b = _dot((kct * jnp.exp(lgb * mrow)).astype(MM_DTYPE), vcx)
        sb_scr[n_chunk - 1] = s_b
        for n in range(n_chunk - 1, 0, -1):
            s_b = s_b * cdec_b + _dot((kt_chunk(n).astype(F32) * kdec_b).astype(MM_DTYPE),
                                      chunk(v_ref, n).astype(MM_DTYPE))
            sb_scr[n - 1] = s_b

    for n0 in range(0, n_chunk, RET_GROUP):
        group = range(n0, min(n0 + RET_GROUP, n_chunk))
        qs = {n: chunk(q_scr, n) for n in group}
        kts = {n: kt_chunk(n) for n in group}
        vs = {n: chunk(v_ref, n).astype(MM_DTYPE) for n in group}
        scores = {n: _dot(qs[n], kts[n]) for n in group}
        if n_ctx:
            incs = {n: _dot((kts[n].astype(F32) * kdec_f).astype(MM_DTYPE), vs[n]) for n in group if n + 1 < n_chunk}
            qcats = {n: jnp.concatenate([qs[n].astype(F32) * qdec_f, qs[n].astype(F32) * qdec_b], axis=1
                                        ).astype(MM_DTYPE) for n in group}
        probs = {n: (scores[n] * dmask).astype(MM_DTYPE) for n in group}
        outs = {n: _dot(probs[n], vs[n]) for n in group}
        if n_ctx:
            for n in group:
                scat = jnp.concatenate([s_f, sb_scr[n]], axis=0).astype(MM_DTYPE)
                outs[n] = outs[n] + _dot(qcats[n], scat)
                if n + 1 < n_chunk:
                    s_f = s_f * cdec_f + incs[n]
        for n in group:
            o = outs[n]
            mu = jnp.mean(o, axis=-1, keepdims=True)
            oc = o - mu
            yn = oc * lax.rsqrt(jnp.mean(oc * oc, axis=-1, keepdims=True) + NORM_EPS)
            gn = chunk(g_ref, n)
            o_ref[n * C:(n + 1) * C, :] = yn.astype(o_ref.dtype) * (gn * jax.nn.sigmoid(gn))


def _retention_latent(p, log_gamma, cos, sin, *, batch, seq, n_ctx, n_lat):
    n = p.shape[0]
    cb = n_lat // n_ctx
    kq, kv = OFF_K // RET_DK, OFF_V // RET_DV
    gs = pltpu.PrefetchScalarGridSpec(
        num_scalar_prefetch=1,
        grid=(batch, RET_HEADS),
        in_specs=[pl.BlockSpec((seq, RET_DK), lambda b, h, lg: (b, h)),
                  pl.BlockSpec((seq, RET_DK), lambda b, h, lg: (b, kq + h)),
                  pl.BlockSpec((seq, RET_DV), lambda b, h, lg: (b, kv + h)),
                  pl.BlockSpec((seq, RET_DV), lambda b, h, lg: (b, OFF_G // RET_DV + h)),
                  pl.BlockSpec((n_ctx, RET_DK), lambda b, h, lg: (cb + b, kq + h)),
                  pl.BlockSpec((n_ctx, RET_DV), lambda b, h, lg: (cb + b, kv + h)),
                  pl.BlockSpec((seq, RET_DK), lambda b, h, lg: (0, 0)),
                  pl.BlockSpec((seq, RET_DK), lambda b, h, lg: (0, 0))],
        out_specs=pl.BlockSpec((seq, RET_DV), lambda b, h, lg: (b, h)),
        scratch_shapes=[pltpu.VMEM((seq, RET_DK), MM_DTYPE),
                        pltpu.VMEM((RET_DK, seq), MM_DTYPE),
                        pltpu.VMEM((seq // RET_CHUNK, RET_DK, RET_DV), F32)])
    return pl.pallas_call(
        functools.partial(_ret_kernel, seq=seq, n_ctx=n_ctx, use_rope=True),
        grid_spec=gs,
        out_shape=jax.ShapeDtypeStruct((n, RET_V), ACT_DTYPE),
        compiler_params=_cparams(("parallel", "parallel")),
    )(log_gamma, p, p, p, p, p, p, cos, sin)


def _retention_ctx(p, log_gamma, y_ret, *, batch, n_ctx, n_lat):
    cb = n_lat // n_ctx
    kq, kv = OFF_K // RET_DK, OFF_V // RET_DV
    gs = pltpu.PrefetchScalarGridSpec(
        num_scalar_prefetch=1,
        grid=(batch, RET_HEADS),
        in_specs=[pl.BlockSpec((n_ctx, RET_DK), lambda b, h, lg: (cb + b, h)),
                  pl.BlockSpec((n_ctx, RET_DK), lambda b, h, lg: (cb + b, kq + h)),
                  pl.BlockSpec((n_ctx, RET_DV), lambda b, h, lg: (cb + b, kv + h)),
                  pl.BlockSpec((n_ctx, RET_DV), lambda b, h, lg: (cb + b, OFF_G // RET_DV + h)),
                  pl.BlockSpec(memory_space=pl.ANY)],
        out_specs=pl.BlockSpec((n_ctx, RET_DV), lambda b, h, lg: (cb + b, h)),
        scratch_shapes=[pltpu.VMEM((n_ctx, RET_DK), MM_DTYPE), pltpu.VMEM((RET_DK, n_ctx), MM_DTYPE)])

    def body(lg_ref, q_ref, k_ref, v_ref, g_ref, alias_ref, o_ref, q_scr, kt_scr):
        del alias_ref
        _ret_kernel(lg_ref, q_ref, k_ref, v_ref, g_ref, o_ref, q_scr, kt_scr, seq=n_ctx, n_ctx=0, use_rope=False)

    return pl.pallas_call(
        body,
        grid_spec=gs,
        out_shape=jax.ShapeDtypeStruct(y_ret.shape, y_ret.dtype),
        input_output_aliases={5: 0},
        compiler_params=_cparams(("parallel", "parallel")),
    )(log_gamma, p, p, p, p, y_ret)


def _convpool_kernel(*refs, seq, aliased):
    if aliased:
        cb_ref, cc_ref, cx_ref, pi_ref, cw_ref, pw_ref, ps_ref, _, _, yc_ref, yp_ref = refs
    else:
        cb_ref, cc_ref, cx_ref, pi_ref, cw_ref, pw_ref, ps_ref, yc_ref, yp_ref = refs
    grp = pl.program_id(1)
    pb, hb = POOL_BLOCK, POOL_HALO
    n_blk = seq // pb

    w = cw_ref[...]
    tl = lax.broadcasted_iota(jnp.int32, (pb, 1), 0)
    cxu = lambda a, b: cc_ref[a:b, :].astype(F32) * cx_ref[a:b, :].astype(F32)
    for b0 in range(0, n_blk, CONV_GROUP):
        group = range(b0, min(b0 + CONV_GROUP, n_blk))
        us = {blk: cxu(blk * pb, (blk + 1) * pb) for blk in group}
        before = {blk: cxu(blk * pb - hb, blk * pb)[hb - 1:hb, :] if blk > 0 else 0.0 for blk in group}
        after = {blk: cxu((blk + 1) * pb, (blk + 1) * pb + hb)[0:1, :] if blk + 1 < n_blk else 0.0 for blk in group}
        prevs = {blk: jnp.where(tl == 0, before[blk], pltpu.roll(us[blk], 1, 0)) for blk in group}
        nexts = {blk: jnp.where(tl == pb - 1, after[blk], pltpu.roll(us[blk], pb - 1, 0)) for blk in group}
        for blk in group:
            conv = w[0:1, :] * prevs[blk] + w[1:2, :] * us[blk] + w[2:3, :] * nexts[blk]
            yc_ref[blk * pb:(blk + 1) * pb, :] = (cb_ref[blk * pb:(blk + 1) * pb, :].astype(F32) * conv
                                                  ).astype(yc_ref.dtype)

    half = jnp.left_shift(1, grp)

    def band(rows, cols, shift):
        dd = (lax.broadcasted_iota(jnp.int32, (rows, cols), 1) + shift
              - lax.broadcasted_iota(jnp.int32, (rows, cols), 0))
        return jnp.where((dd >= -half) & (dd < half), 1.0, 0.0).astype(MM_DTYPE)

    band_self = band(pb, pb, 0)
    band_prev = band(hb, hb, -hb)
    band_next = band(hb, hb, hb)
    blocks = range(n_blk)
    p_blks = [pi_ref[blk * pb:(blk + 1) * pb, :] for blk in blocks]
    wsums = [_dot(band_self, p_blk.astype(MM_DTYPE)) for p_blk in p_blks]
    pooled = []
    for blk in blocks:
        r0 = blk * pb
        top, mid, bot = wsums[blk][:hb], wsums[blk][hb:pb - hb], wsums[blk][pb - hb:]
        if blk > 0:
            top = top + _dot(band_prev, pi_ref[r0 - hb:r0, :].astype(MM_DTYPE))
        if blk + 1 < n_blk:
            bot = bot + _dot(band_next, pi_ref[r0 + pb:r0 + pb + hb, :].astype(MM_DTYPE))
        wsum = jnp.concatenate([top, mid, bot], axis=0)
        if 0 < blk < n_blk - 1:
            cnt = (2 * half).astype(F32)
        else:
            tb = r0 + lax.broadcasted_iota(jnp.int32, (pb, 1), 0)
            cnt = (jnp.clip(tb + half, 0, seq) - jnp.clip(tb - half, 0, seq)).astype(F32)
        pooled.append((wsum / cnt - p_blks[blk].astype(F32)).astype(MM_DTYPE))
    mixed = [_dot(pooled[blk], pw_ref[0]) for blk in blocks]
    for blk in blocks:
        yp_ref[blk * pb:(blk + 1) * pb, :] = (mixed[blk] * ps_ref[...]).astype(yp_ref.dtype)


def _convpool(p, conv_w, pool_w, pool_scale, prev, *, layer, n_seq, seq, row0):
    n = p.shape[0]
    g128 = lambda off: off // POOL_GDIM
    col = lambda off: (lambda b, g: (row0 + b, g128(off) + g))
    in_specs = [pl.BlockSpec((seq, POOL_GDIM), col(OFF_CB)),
                pl.BlockSpec((seq, POOL_GDIM), col(OFF_CC)),
                pl.BlockSpec((seq, POOL_GDIM), col(OFF_CX)),
                pl.BlockSpec((seq, POOL_GDIM), col(OFF_PI)),
                _layer_spec(layer, (conv_w.shape[1], POOL_GDIM), lambda b, g: (0, g)),
                _layer_spec(layer, (1, POOL_GDIM, POOL_GDIM), lambda b, g: (g, 0, 0)),
                _layer_spec(layer, (1, POOL_GDIM), lambda b, g: (0, g))]
    args = [p, p, p, p, conv_w, pool_w, pool_scale]
    aliases = {}
    if prev is not None:
        in_specs += [pl.BlockSpec(memory_space=pl.ANY), pl.BlockSpec(memory_space=pl.ANY)]
        args += list(prev)
        aliases = {7: 0, 8: 1}
    out_spec = pl.BlockSpec((seq, POOL_GDIM), lambda b, g: (row0 + b, g))
    return pl.pallas_call(
        functools.partial(_convpool_kernel, seq=seq, aliased=prev is not None),
        grid=(n_seq, POOL_GROUPS),
        in_specs=in_specs,
        out_specs=[out_spec, out_spec],
        out_shape=[jax.ShapeDtypeStruct((n, CONV_W), ACT_DTYPE), jax.ShapeDtypeStruct((n, POOL_W), ACT_DTYPE)],
        input_output_aliases=aliases,
        compiler_params=_cparams(("parallel", "parallel")),
    )(*args)


def _first_max(vals):
    top = functools.reduce(jnp.maximum, vals)
    idx = jnp.full(top.shape, len(vals) - 1, jnp.int32)
    for k in range(len(vals) - 2, -1, -1):
        idx = jnp.where(vals[k] == top, k, idx)
    return top, idx


def _route(logits_t):
    row = lambda k: logits_t[k:k + 1, :]
    groups = [row(g) for g in range(N_GROUPS)]
    gmax, gidx = _first_max(groups)
    gtop = 1.0 / sum(jnp.exp(g - gmax) for g in groups)
    experts = []
    for k in range(EXP_PER_GROUP):
        e = row(N_GROUPS + (N_GROUPS - 1) * EXP_PER_GROUP + k)
        for g in range(N_GROUPS - 2, -1, -1):
            e = jnp.where(gidx == g, row(N_GROUPS + g * EXP_PER_GROUP + k), e)
        experts.append(e)
    m1, i1 = _first_max(experts)
    m2, i2 = _first_max([jnp.where(i1 == k, -jnp.inf, e) for k, e in enumerate(experts)])
    e2 = jnp.exp(m2 - m1)
    w1 = gtop / (1.0 + e2)
    w2 = gtop * e2 / (1.0 + e2)
    first_lower = i1 < i2
    lo = jnp.minimum(i1, i2)
    hi = jnp.maximum(i1, i2)
    pair = lo * 3 - jnp.right_shift(lo * (lo - 1), 1) + hi - lo - 1
    bucket = gidx * N_PAIRS + pair
    wlo = jnp.where(first_lower, w1, w2)
    whi = jnp.where(first_lower, w2, w1)
    return bucket, wlo, whi


def _column_to_lanes(col):
    return jnp.broadcast_to(col, (col.shape[0], LANES)).T[0:1, :]


def _row_to_column(row):
    return jnp.broadcast_to(row, (8, row.shape[1])).T[:, 0:1]


def _dot_nt(a, b):
    return lax.dot_general(a, b, (((1,), (1,)), ((), ())), preferred_element_type=F32)


def _bucket_run_copies(read_run, max_rows, make_copy):
    def piece(n, local, glob, size):
        @pl.when((n & size) != 0)
        def _():
            off = n & (-2 * size)
            make_copy(pl.multiple_of(local + off, ROW_ALIGN), pl.multiple_of(glob + off, ROW_ALIGN), size).start()

    top = 1 << (max_rows.bit_length() - 1)
    sizes = [top >> k for k in range(top.bit_length()) if top >> k >= ROW_ALIGN]
    rare = [s for s in sizes if s >= RUN_RARE_ROWS]

    def per_bucket(b, carry_):
        n, local, glob = read_run(b)

        @pl.when(n >= RUN_RARE_ROWS)
        def _():
            for size in rare:
                piece(n, local, glob, size)

        for size in sizes[len(rare):]:
            piece(n, local, glob, size)
        return carry_

    lax.fori_loop(0, N_BUCKETS, per_bucket, 0)


def _wait_rows(total, max_rows, make_copy):
    size = 1 << (max_rows.bit_length() - 1)
    while size >= ROW_ALIGN:
        @pl.when((total & size) != 0)
        def _(size=size):
            make_copy(size).wait()
        size //= 2


def _exact_bf16_pieces(w):
    a = w.astype(jnp.bfloat16).astype(F32)
    b = (w - a).astype(jnp.bfloat16).astype(F32)
    c = (w - a - b).astype(jnp.bfloat16).astype(F32)
    return a, b, c


def _outproj_kernel(x_ref, yr_ref, yc_ref, yp_ref, g0_ref, g1_ref, g2_ref, gt_ref, sc_ref, sh_ref, ng_ref,
                    wr_ref, wc_ref, wp_ref, wo_ref, wrt_ref, brt_ref,
                    xo_ref, hs_ref, cnt_ref, tab_ref, lpos_ref,
                    h_scr, tab_vmem, tab_smem, sent_smem, carry, row_sem, tab_sem, *, cap):
    i = pl.program_id(0)
    n_steps = pl.num_programs(0)
    slot = lax.rem(i, 2)
    tm, d = x_ref.shape
    ts = h_scr.shape[1]

    def wait_rows(s):
        _wait_rows(sent_smem[s], ts, lambda size: pltpu.make_async_copy(
            h_scr.at[s, pl.ds(0, size)], hs_ref.at[pl.ds(0, size)], row_sem.at[s]))

    @pl.when(i == 0)
    def _():
        carry[...] = jnp.zeros_like(carry)

    @pl.when(i >= 2)
    def _():
        wait_rows(slot)

    sub = tm // OUT_SUBBLOCKS
    blocks = [slice(a, a + sub) for a in range(0, tm, sub)]
    gate = lambda r, rows: jax.nn.sigmoid(r[rows, :].astype(F32))
    merged = [(gate(g0_ref, rows) * _dot(yr_ref[rows, :], wr_ref[...])
               + gate(g1_ref, rows) * _dot(yc_ref[rows, :], wc_ref[...])
               + gate(g2_ref, rows) * _dot(yp_ref[rows, :], wp_ref[...])).astype(MM_DTYPE) for rows in blocks]
    ys = [_dot(m, wo_ref[...]) for m in merged]
    xs_new = [x_ref[rows, :] + gt_ref[0] * y for rows, y in zip(blocks, ys)]
    for rows, x in zip(blocks, xs_new):
        xo_ref[rows, :] = x
    hs = [_rms_mod(x, ng_ref[...], sc_ref[0], sh_ref[0]) for x in xs_new]
    splits = [_split_bf16(hb) for hb in hs]
    s2s = [_dot_nt(wrt_ref[...], hh) + _dot_nt(wrt_ref[...], hl) for hh, hl in splits]
    routes = [_route(s2[0:ROUTER_LO_ROW, :] + s2[ROUTER_LO_ROW:2 * ROUTER_LO_ROW, :] + brt_ref[...]) for s2 in s2s]
    h = jnp.concatenate([hb.astype(MM_DTYPE) for hb in hs], axis=0)
    bucket, wlo, whi = (jnp.concatenate([route[k] for route in routes], axis=1) for k in (0, 1, 2))

    brow = lax.broadcasted_iota(jnp.int32, (LANES, tm), 0)
    onehot = brow == bucket
    ones = jnp.where(onehot, 1.0, 0.0)
    r = lax.broadcasted_iota(jnp.int32, (tm, tm), 0)
    c = lax.broadcasted_iota(jnp.int32, (tm, tm), 1)
    tri = jnp.where(r <= c, 1.0, 0.0).astype(jnp.bfloat16)
    incl = _dot(ones.astype(jnp.bfloat16), tri)
    cnt_col = incl[:, tm - 1:tm].astype(jnp.int32)
    units_col = jnp.right_shift(cnt_col + (ROW_ALIGN - 1), ALIGN_SHIFT)
    below = jnp.where(lax.broadcasted_iota(jnp.int32, (LANES, LANES), 0)
                      > lax.broadcasted_iota(jnp.int32, (LANES, LANES), 1), 1.0, 0.0).astype(jnp.bfloat16)
    units_b = jnp.broadcast_to(units_col.astype(F32), (LANES, LANES)).astype(jnp.bfloat16)
    start_col = _dot(below, units_b)[:, 0:1] * float(ROW_ALIGN)
    lpos = jnp.sum(jnp.where(onehot, incl - 1.0 + start_col, 0.0), axis=0, keepdims=True)
    lpos_ref[...] = _row_to_column(lpos).astype(jnp.int32)
    run_col = (units_col * ROW_ALIGN).astype(F32)
    run_len = _column_to_lanes(run_col).astype(jnp.int32)
    local_start = _column_to_lanes(start_col).astype(jnp.int32)
    total = (start_col + run_col)[LANES - 1:LANES, :].astype(jnp.int32)
    srow = lax.broadcasted_iota(jnp.int32, (8, LANES), 0)
    tab = jnp.where(srow == TAB_COUNT, run_len,
                    jnp.where(srow == TAB_LOCAL, local_start,
                              jnp.where(srow == TAB_GLOBAL, carry[...], jnp.where(srow == TAB_TOTAL, total, 0))))
    carry[...] = carry[...] + run_len
    tab_ref[0] = tab
    tab_vmem[...] = tab
    to_smem = pltpu.make_async_copy(tab_vmem, tab_smem, tab_sem)
    to_smem.start()

    srt = lax.broadcasted_iota(jnp.int32, (ts, tm), 0)
    perm = jnp.where(srt == lpos.astype(jnp.int32), 1.0, 0.0).astype(MM_DTYPE)
    pieces = _exact_bf16_pieces(wlo) + _exact_bf16_pieces(whi)
    meta_t = jnp.zeros((LANES, tm), F32)
    for k, piece in enumerate(pieces):
        meta_t = jnp.where(brow == k, piece, meta_t)
    h_scr[slot, :, :d] = _dot(perm, h)
    h_scr[slot, :, d:] = _dot_nt(perm, meta_t.astype(MM_DTYPE))
    to_smem.wait()
    sent_smem[slot] = tab_smem[TAB_TOTAL, 0]

    _bucket_run_copies(
        lambda b: (tab_smem[TAB_COUNT, b], tab_smem[TAB_LOCAL, b], b * cap + tab_smem[TAB_GLOBAL, b]), tm,
        lambda local, glob, size: pltpu.make_async_copy(h_scr.at[slot, pl.ds(local, size)],
                                                        hs_ref.at[pl.ds(glob, size)], row_sem.at[slot]))

    @pl.when(i == n_steps - 1)
    def _():
        cnt_ref[...] = jnp.broadcast_to(carry[...].astype(jnp.int32), cnt_ref.shape)
        wait_rows(slot)

    @pl.when((i == n_steps - 1) & (i >= 1))
    def _():
        wait_rows(1 - slot)


def _sorted_rows(tm):
    return tm + N_BUCKETS * ROW_ALIGN


def _outproj(xs, p, y_ret, y_conv, y_pool, mod, norm_g, w_ret, w_conv, w_pool, w_o, w_router, b_router,
             *, layer, tm, cap, n_rows, n_lat, seq, batch):
    n, d = xs.shape
    assert cap >= n_rows + (n_rows // tm) * ROW_ALIGN and cap % ROW_ALIGN == 0
    width = d + LANES
    ts = _sorted_rows(tm)
    row = functools.partial(_mod_row, tile=tm, n_lat=n_lat, seq=seq, batch=batch)
    gate = lambda k: pl.BlockSpec((tm, d), lambda i: (i, OFF_GATE // d + k))
    mod_spec = lambda k: _layer_spec(layer, (1, 1, d), lambda i: (row(i), 0, k))
    full = lambda a: _layer_spec(layer, a.shape[1:], lambda i: (0,) * (a.ndim - 1))
    return pl.pallas_call(
        functools.partial(_outproj_kernel, cap=cap),
        grid=(n_rows // tm,),
        in_specs=[pl.BlockSpec((tm, d), lambda i: (i, 0)),
                  pl.BlockSpec((tm, RET_V), lambda i: (i, 0)),
                  pl.BlockSpec((tm, CONV_W), lambda i: (i, 0)),
                  pl.BlockSpec((tm, POOL_W), lambda i: (i, 0)),
                  gate(0), gate(1), gate(2),
                  mod_spec(2), mod_spec(4), mod_spec(3),
                  full(norm_g), full(w_ret), full(w_conv), full(w_pool), full(w_o), full(w_router), full(b_router)],
        out_specs=[pl.BlockSpec((tm, d), lambda i: (i, 0)),
                   pl.BlockSpec(memory_space=pl.ANY),
                   pl.BlockSpec((8, LANES), lambda i: (0, 0)),
                   pl.BlockSpec((1, 8, LANES), lambda i: (i, 0, 0)),
                   pl.BlockSpec((tm, 1), lambda i: (i, 0))],
        out_shape=[jax.ShapeDtypeStruct((n, d), F32),
                   jax.ShapeDtypeStruct((N_BUCKETS * cap, width), F32),
                   jax.ShapeDtypeStruct((8, LANES), jnp.int32),
                   jax.ShapeDtypeStruct((n_rows // tm, 8, LANES), jnp.int32),
                   jax.ShapeDtypeStruct((n_rows, 1), jnp.int32)],
        scratch_shapes=[pltpu.VMEM((2, ts, width), F32),
                        pltpu.VMEM((8, LANES), jnp.int32),
                        pltpu.SMEM((8, LANES), jnp.int32),
                        pltpu.SMEM((2,), jnp.int32),
                        pltpu.VMEM((1, LANES), jnp.int32),
                        pltpu.SemaphoreType.DMA((2,)),
                        pltpu.SemaphoreType.DMA(())],
        input_output_aliases={0: 0},
        compiler_params=_cparams(("arbitrary",)),
    )(xs, y_ret, y_conv, y_pool, p, p, p, mod, mod, mod, norm_g, w_ret, w_conv, w_pool, w_o, w_router, b_router)


def _moe_kernel(blk_ref, elo_ref, ehi_ref, valid_ref, hs_ref, w1l_ref, w1h_ref, w3l_ref, w3h_ref, w2l_ref, w2h_ref,
                ys_ref):
    del blk_ref, elo_ref, ehi_ref
    valid = valid_ref[pl.program_id(0)]
    tm, d = ys_ref.shape

    @pl.when(valid > 0)
    def _():
        keep = lax.broadcasted_iota(jnp.int32, (tm, 1), 0) < valid
        meta = jnp.where(keep, hs_ref[:, d:], 0.0)
        h = jnp.where(keep, hs_ref[:, :d], 0.0).astype(MM_DTYPE)
        wlo = meta[:, 0:1] + meta[:, 1:2] + meta[:, 2:3]
        whi = meta[:, 3:4] + meta[:, 4:5] + meta[:, 5:6]

        ups = [(_dot(h, w1[0]), _dot(h, w3[0])) for w1, w3 in ((w1l_ref, w3l_ref), (w1h_ref, w3h_ref))]
        acts = [(a * jax.nn.sigmoid(a) * b).astype(MM_DTYPE) for a, b in ups]
        y_lo, y_hi = (_dot(act, w2[0]) for act, w2 in zip(acts, (w2l_ref, w2h_ref)))
        y = wlo * y_lo + whi * y_hi
        ys_ref[...] = y.astype(ACT_DTYPE).astype(F32)


def _moe(blk, elo, ehi, valid, hs, w1, w3, w2, *, layer, tm, d):
    n_sorted, width = hs.shape
    n_work = blk.shape[0]
    up = lambda sel: _layer_spec(layer, (1, d, D_FF), lambda s, blk, elo, ehi, valid: ((elo, ehi)[sel][s], 0, 0))
    down = lambda sel: _layer_spec(layer, (1, D_FF, d), lambda s, blk, elo, ehi, valid: ((elo, ehi)[sel][s], 0, 0))
    gs = pltpu.PrefetchScalarGridSpec(
        num_scalar_prefetch=4,
        grid=(n_work,),
        in_specs=[pl.BlockSpec((tm, width), lambda s, blk, elo, ehi, valid: (blk[s], 0)),
                  up(0), up(1), up(0), up(1), down(0), down(1)],
        out_specs=pl.BlockSpec((tm, d), lambda s, blk, elo, ehi, valid: (blk[s], 0)))
    return pl.pallas_call(
        _moe_kernel,
        grid_spec=gs,
        out_shape=jax.ShapeDtypeStruct((n_sorted, d), F32),
        compiler_params=_cparams(("arbitrary",)),
    )(blk, elo, ehi, valid, hs, w1, w1, w3, w3, w2, w2)


def _expert_residual(tab_ref, ys_ref, lpos_ref, stage, sems, *, tile_rows, cap):
    i = pl.program_id(0)
    slot = lax.rem(i, 2)
    ts = stage.shape[1]
    entry = lambda tile, row, lane: tab_ref[tile * (TAB_ROWS * LANES) + row * LANES + lane]

    def start(tile, s):
        _bucket_run_copies(
            lambda b: (entry(tile, TAB_COUNT, b), entry(tile, TAB_LOCAL, b), b * cap + entry(tile, TAB_GLOBAL, b)),
            tile_rows,
            lambda local, glob, size: pltpu.make_async_copy(
                ys_ref.at[pl.ds(glob, size)], stage.at[s, pl.ds(local, size)], sems.at[s]))

    @pl.when(i == 0)
    def _():
        start(i, slot)

    @pl.when(i + 1 < pl.num_programs(0))
    def _():
        start(i + 1, 1 - slot)

    total = entry(i, TAB_TOTAL, 0)
    _wait_rows(total, ts, lambda size: pltpu.make_async_copy(
        ys_ref.at[pl.ds(0, size)], stage.at[slot, pl.ds(0, size)], sems.at[slot]))
    lpos = lpos_ref[...]
    unperm = jnp.where(lpos == lax.broadcasted_iota(jnp.int32, (tile_rows, ts), 1), 1.0, 0.0).astype(MM_DTYPE)
    filled = lax.broadcasted_iota(jnp.int32, (ts, 1), 0) < total
    return _dot(unperm, jnp.where(filled, stage[slot], 0.0).astype(MM_DTYPE))


def _final_kernel(tab_ref, x_ref, ys_ref, lpos_ref, gt_ref, g_ref, o_ref, stage, sems, *, cap):
    f = _expert_residual(tab_ref, ys_ref, lpos_ref, stage, sems, tile_rows=x_ref.shape[0], cap=cap)
    x = x_ref[...] + gt_ref[0] * f
    o_ref[...] = x * lax.rsqrt(jnp.mean(x * x, axis=-1, keepdims=True) + NORM_EPS) * g_ref[...]


def _final(xs, moe_out, mod, final_g, *, layer, tm, n_rows, n_lat, seq, batch):
    ys, tab, lpos, cap = moe_out
    d = xs.shape[1]
    row = functools.partial(_mod_row, tile=tm, n_lat=n_lat, seq=seq, batch=batch)
    x_spec = pl.BlockSpec((tm, d), lambda i, tab: (i, 0))
    gs = pltpu.PrefetchScalarGridSpec(
        num_scalar_prefetch=1,
        grid=(n_rows // tm,),
        in_specs=[x_spec,
                  pl.BlockSpec(memory_space=pl.ANY),
                  pl.BlockSpec((tm, 1), lambda i, tab: (i, 0)),
                  _layer_spec(layer, (1, 1, d), lambda i, tab: (row(i), 0, 5)),
                  pl.BlockSpec((1, d), lambda i, tab: (0, 0))],
        out_specs=x_spec,
        scratch_shapes=[pltpu.VMEM((2, _sorted_rows(tm), d), F32), pltpu.SemaphoreType.DMA((2,))])
    return pl.pallas_call(
        functools.partial(_final_kernel, cap=cap),
        grid_spec=gs,
        out_shape=jax.ShapeDtypeStruct((n_rows, d), F32),
        compiler_params=_cparams(("arbitrary",)),
    )(tab, xs, ys, lpos, mod, final_g)


def _rope_tables(seq):
    rows = seq // GRID_W
    row = jnp.repeat(jnp.arange(rows, dtype=F32), GRID_W)
    col = jnp.tile(jnp.arange(GRID_W, dtype=F32), rows)
    n_freq = RET_DK // 4
    inv_freq = ROPE_BASE ** (-jnp.arange(n_freq, dtype=F32) / n_freq)
    ang = jnp.concatenate([row[:, None] * inv_freq[None, :], col[:, None] * inv_freq[None, :]], axis=-1)
    cos, sin = jnp.cos(ang), jnp.sin(ang)
    return jnp.concatenate([cos, cos], axis=-1), jnp.concatenate([-sin, sin], axis=-1)


def _work_tables(counts, *, tm, cap, n_work):
    cnt = counts[0, :N_BUCKETS]
    tiles = (cnt + tm - 1) // tm
    ends = jnp.cumsum(tiles)
    starts = ends - tiles
    item = jnp.arange(n_work, dtype=jnp.int32)
    used = item < ends[-1]
    ref_item = jnp.minimum(item, jnp.maximum(ends[-1] - 1, 0))
    member = ((ref_item[:, None] >= starts[None, :]) & (ref_item[:, None] < ends[None, :])).astype(jnp.int32)
    pick = lambda per_bucket: jnp.sum(member * per_bucket[None, :], axis=1)
    buckets = np.arange(N_BUCKETS)
    j = ref_item - pick(starts)
    valid = jnp.where(used, jnp.clip(pick(cnt) - j * tm, 0, tm), 0)
    blk = pick(jnp.asarray(buckets * (cap // tm), jnp.int32)) + j
    first = (buckets // N_PAIRS) * EXP_PER_GROUP
    elo = pick(jnp.asarray(first + np.asarray(PAIR_LO)[buckets % N_PAIRS], jnp.int32))
    ehi = pick(jnp.asarray(first + np.asarray(PAIR_HI)[buckets % N_PAIRS], jnp.int32))
    i32 = lambda a: a.astype(jnp.int32)
    return i32(blk), i32(elo), i32(ehi), i32(valid)


def _router_weights(w_rg, b_rg, w_re, b_re):
    w = jnp.swapaxes(jnp.concatenate([w_rg, w_re], axis=-1).astype(F32), 1, 2)
    depth, n_out, d = w.shape
    assert n_out <= ROUTER_LO_ROW and 2 * ROUTER_LO_ROW <= LANES and EXP_PER_GROUP == 4
    hi = w.astype(jnp.bfloat16)
    lo = (w - hi.astype(F32)).astype(jnp.bfloat16)
    packed = jnp.zeros((depth, LANES, d), jnp.bfloat16)
    packed = packed.at[:, :n_out].set(hi).at[:, ROUTER_LO_ROW:ROUTER_LO_ROW + n_out].set(lo)
    bias = jnp.zeros((depth, ROUTER_LO_ROW, 1), F32)
    bias = bias.at[:, :n_out, 0].set(jnp.concatenate([b_rg, b_re], axis=-1).astype(F32))
    return packed, bias


def _pick_tile(n, want, *also):
    t = want
    while n % t or any(a % t for a in also):
        t //= 2
    return t


def kernel(x, c, ctx, c_ctx, w_ada, b_ada, norm1, norm2, w_in, ret_decay, conv_w, pool_w, pool_scale, w_ret_out,
           w_conv_out, w_pool_out, w_o, w_rg, b_rg, w_re, b_re, w1, w3, w2, final_norm):
    batch, seq, d = x.shape
    n_ctx = ctx.shape[1]
    depth = w_ada.shape[0]
    n_lat, n_c = batch * seq, batch * n_ctx
    n = n_lat + n_c
    assert POOL_WINDOWS == (2, 4, 8, 16) and POOL_HALO >= max(POOL_WINDOWS) // 2
    assert seq % RET_CHUNK == 0 and n_ctx == RET_CHUNK and seq % GRID_W == 0 and n_lat % n_ctx == 0

    tm_out = _pick_tile(seq, 512, n_c)
    tm_moe = 256

    xs = jnp.concatenate([x.reshape(n_lat, d), ctx.reshape(n_c, d)], axis=0)
    mod_rows = -(-(batch + 1) // 8) * 8
    cond = jnp.zeros((mod_rows, d), F32).at[:batch].set(c).at[batch].set(c_ctx)
    mod = _ada(cond, w_ada, b_ada).reshape(depth, mod_rows, 1, N_MOD * d)
    cos, sin = _rope_tables(seq)
    log_gamma = jax.nn.log_sigmoid(ret_decay.astype(F32))

    mm = lambda a: a.astype(MM_DTYPE)
    norm1_s, norm2_s, pool_scale_s = norm1[:, None, :], norm2[:, None, :], pool_scale[:, None, :]
    w_in_b, pool_w_b = mm(w_in), mm(pool_w)
    w_ret_b, w_conv_b, w_pool_b, w_o_b = mm(w_ret_out), mm(w_conv_out), mm(w_pool_out), mm(w_o)
    w1_b, w3_b, w2_b = mm(w1), mm(w3), mm(w2)
    w_router, b_router = _router_weights(w_rg, b_rg, w_re, b_re)
    dims = dict(n_lat=n_lat, seq=seq, batch=batch)
    prev = None
    for l in range(depth):
        last = l == depth - 1
        rows = n_lat if last else n
        p, xs = _inproj(xs, prev, norm1_s, mod, w_in_b, layer=l, tm=tm_out, tn=1024, **dims)
        y_ret = _retention_latent(p, log_gamma[l], cos, sin, batch=batch, seq=seq, n_ctx=n_ctx, n_lat=n_lat)
        y_conv, y_pool = _convpool(p, conv_w, pool_w_b, pool_scale_s, None, layer=l, n_seq=batch, seq=seq, row0=0)
        if not last:
            y_ret = _retention_ctx(p, log_gamma[l], y_ret, batch=batch, n_ctx=n_ctx, n_lat=n_lat)
            y_conv, y_pool = _convpool(p, conv_w, pool_w_b, pool_scale_s, (y_conv, y_pool), layer=l,
                                       n_seq=batch, seq=n_ctx, row0=n_lat // n_ctx)
        n_tiles = rows // tm_out
        cap = -(-(rows + n_tiles * ROW_ALIGN) // tm_moe) * tm_moe
        max_sorted = rows + n_tiles * N_BUCKETS * (ROW_ALIGN - 1)
        xs, hs, counts, tab, lpos = _outproj(xs, p, y_ret, y_conv, y_pool, mod, norm2_s, w_ret_b, w_conv_b, w_pool_b,
                                             w_o_b, w_router, b_router, layer=l, tm=tm_out, cap=cap, n_rows=rows,
                                             **dims)
        blk, elo, ehi, valid = _work_tables(counts, tm=tm_moe, cap=cap, n_work=-(-max_sorted // tm_moe) + N_BUCKETS)
        ys = _moe(blk, elo, ehi, valid, hs, w1_b, w3_b, w2_b, layer=l, tm=tm_moe, d=d)
        prev = (ys, tab[:, :TAB_ROWS, :].reshape(-1), lpos, cap)
    out = _final(xs, prev, mod, final_norm[None], layer=depth - 1, tm=tm_out, n_rows=n_lat, **dims)
    return out.reshape(batch, seq, d)
```

```python
import functools

import numpy as np
import jax
import jax.numpy as jnp
from jax import lax
from jax.experimental import pallas as pl
from jax.experimental.pallas import tpu as pltpu

F32 = jnp.float32
MM_DTYPE = jnp.bfloat16
ACT_DTYPE = jnp.bfloat16

NORM_EPS = 1e-6
GRID_W = 64
ROPE_BASE = 10000.0
N_MOD = 6

RET_HEADS = 4
RET_DK = 128
RET_DV = 256
RET_QK = RET_HEADS * RET_DK
RET_V = RET_HEADS * RET_DV
RET_CHUNK = 256
RET_GROUP = 2

CONV_W = 512
CONV_GROUP = 2
POOL_WINDOWS = (2, 4, 8, 16)
POOL_GROUPS = 4
POOL_GDIM = 128
POOL_W = POOL_GROUPS * POOL_GDIM
POOL_BLOCK = 256
POOL_HALO = 16

N_GROUPS = 4
EXP_PER_GROUP = 4
N_EXPERTS = N_GROUPS * EXP_PER_GROUP
D_FF = 512
PAIR_LO = (0, 0, 0, 1, 1, 2)
PAIR_HI = (1, 2, 3, 2, 3, 3)
N_PAIRS = len(PAIR_LO)
N_BUCKETS = N_GROUPS * N_PAIRS

LANES = 128
OUT_SUBBLOCKS = 4
ROW_ALIGN = 8
ALIGN_SHIFT = 3
RUN_RARE_ROWS = 64
TAB_COUNT, TAB_LOCAL, TAB_GLOBAL, TAB_TOTAL, TAB_ROWS = 0, 1, 2, 3, 4
ROUTER_LO_ROW = 32

OFF_Q = 0
OFF_K = OFF_Q + RET_QK
OFF_V = OFF_K + RET_QK
OFF_G = OFF_V + RET_V
OFF_CB = OFF_G + RET_V
OFF_CC = OFF_CB + CONV_W
OFF_CX = OFF_CC + CONV_W
OFF_PI = OFF_CX + CONV_W
OFF_GATE = OFF_PI + POOL_W

VMEM_LIMIT = 56 * 1024 * 1024


def _cparams(sem):
    return pltpu.CompilerParams(dimension_semantics=sem, vmem_limit_bytes=VMEM_LIMIT)


def _split_bf16(a):
    hi = a.astype(jnp.bfloat16)
    lo = (a - hi.astype(F32)).astype(jnp.bfloat16)
    return hi, lo


def _dot(a, b):
    return jnp.dot(a, b, preferred_element_type=F32)


def _dot3(a, b):
    ah, al = _split_bf16(a)
    bh, bl = _split_bf16(b)
    return _dot(ah, bh) + _dot(ah, bl) + _dot(al, bh)


def _layer_spec(layer, block, index, **kw):
    return pl.BlockSpec((None,) + tuple(block), lambda *a: (layer,) + tuple(index(*a)), **kw)


def _mod_row(i, tile, n_lat, seq, batch):
    return jnp.where(i < n_lat // tile, (i * tile) // seq, batch)


def _ada_kernel(c_ref, w_ref, b_ref, o_ref):
    cv = c_ref[...]
    s = cv * jax.nn.sigmoid(cv)
    o_ref[0] = _dot3(s, w_ref[0]) + b_ref[0]


def _ada(cond, w_ada, b_ada):
    depth, d, width = w_ada.shape
    rows = cond.shape[0]
    tn = 512
    return pl.pallas_call(
        _ada_kernel,
        grid=(depth, width // tn),
        in_specs=[pl.BlockSpec((rows, d), lambda l, j: (0, 0)),
                  pl.BlockSpec((1, d, tn), lambda l, j: (l, 0, j)),
                  pl.BlockSpec((1, 1, tn), lambda l, j: (l, 0, j))],
        out_specs=pl.BlockSpec((1, rows, tn), lambda l, j: (l, 0, j)),
        out_shape=jax.ShapeDtypeStruct((depth, rows, width), F32),
        compiler_params=_cparams(("parallel", "parallel")),
    )(cond, w_ada, b_ada.reshape(depth, 1, width))


def _rms_mod(x, g, sc, sh):
    ms = jnp.mean(x * x, axis=-1, keepdims=True)
    return x * lax.rsqrt(ms + NORM_EPS) * g * (1.0 + sc) + sh


def _project_columns(h_scr, w_ref, o_ref, tn):
    for j in range(w_ref.shape[1] // tn):
        cols = slice(j * tn, (j + 1) * tn)
        o_ref[:, cols] = _dot(h_scr[...], w_ref[:, cols]).astype(o_ref.dtype)


def _pick_stream(lat_tiles, lat_ref, ctx_ref, rows=slice(None)):
    return jnp.where(pl.program_id(0) < lat_tiles, lat_ref[rows, :], ctx_ref[rows, :])


def _inproj_kernel(xl_ref, xc_ref, g_ref, sc_ref, sh_ref, w_ref, o_ref, h_scr, *, tn, lat_tiles):
    x = _pick_stream(lat_tiles, xl_ref, xc_ref)
    h_scr[...] = _rms_mod(x, g_ref[...], sc_ref[0], sh_ref[0]).astype(h_scr.dtype)
    _project_columns(h_scr, w_ref, o_ref, tn)


def _stream_specs(tm, d, lat_tiles):
    return [pl.BlockSpec((tm, d), lambda i, *_: (jnp.minimum(i, lat_tiles - 1), 0)),
            pl.BlockSpec((tm, d), lambda i, *_: (jnp.maximum(i - lat_tiles, 0), 0))]


def _inproj_residual_kernel(tab_ref, x_ref, ys_ref, lpos_ref, gt_ref, g_ref, sc_ref, sh_ref, w_ref,
                            o_ref, xo_ref, h_scr, stage, sems, *, tn, cap):
    f = _expert_residual(tab_ref, ys_ref, lpos_ref, stage, sems, tile_rows=x_ref.shape[0], cap=cap)
    x = x_ref[...] + gt_ref[0] * f
    xo_ref[...] = x
    h_scr[...] = _rms_mod(x, g_ref[...], sc_ref[0], sh_ref[0]).astype(h_scr.dtype)
    _project_columns(h_scr, w_ref, o_ref, tn)


def _inproj(xs, prev, norm_g, mod, w, *, layer, tm, tn, n_lat, seq, batch):
    n, d = (xs[0].shape[0] + xs[1].shape[0], xs[0].shape[1]) if prev is None else xs.shape
    width = w.shape[-1]
    row = functools.partial(_mod_row, tile=tm, n_lat=n_lat, seq=seq, batch=batch)
    mod_spec = lambda lyr, k: _layer_spec(lyr, (1, 1, d), lambda i, *_: (row(i), 0, k))
    x_spec = pl.BlockSpec((tm, d), lambda i, *_: (i, 0))
    p_spec = pl.BlockSpec((tm, width), lambda i, *_: (i, 0))
    p_shape = jax.ShapeDtypeStruct((n, width), ACT_DTYPE)
    tail_specs = [_layer_spec(layer, (1, d), lambda i, *_: (0, 0)),
                  mod_spec(layer, 1), mod_spec(layer, 0),
                  _layer_spec(layer, (d, width), lambda i, *_: (0, 0), pipeline_mode=pl.Buffered(1))]
    grid = (n // tm,)
    h_scratch = pltpu.VMEM((tm, d), MM_DTYPE)
    if prev is None:
        lat_tiles = n_lat // tm
        p = pl.pallas_call(functools.partial(_inproj_kernel, tn=tn, lat_tiles=lat_tiles), grid=grid,
                           in_specs=_stream_specs(tm, d, lat_tiles) + tail_specs,
                           out_specs=p_spec, out_shape=p_shape, scratch_shapes=[h_scratch],
                           compiler_params=_cparams(("parallel",)))(*xs, norm_g, mod, mod, w)
        return p, xs
    ys, tab, lpos, cap = prev
    gs = pltpu.PrefetchScalarGridSpec(
        num_scalar_prefetch=1,
        grid=grid,
        in_specs=[x_spec, pl.BlockSpec(memory_space=pl.ANY), pl.BlockSpec((tm, 1), lambda i, *_: (i, 0)),
                  mod_spec(layer - 1, 5)] + tail_specs,
        out_specs=[p_spec, x_spec],
        scratch_shapes=[h_scratch, pltpu.VMEM((2, _sorted_rows(tm), d), F32), pltpu.SemaphoreType.DMA((2,))])
    return pl.pallas_call(
        functools.partial(_inproj_residual_kernel, tn=tn, cap=cap),
        grid_spec=gs,
        out_shape=[p_shape, jax.ShapeDtypeStruct((n, d), F32)],
        input_output_aliases={1: 1},
        compiler_params=_cparams(("arbitrary",)),
    )(tab, xs, ys, lpos, mod, norm_g, mod, mod, w)


def _ret_kernel(lg_ref, *refs, seq, n_ctx, use_rope):
    if n_ctx:
        q_ref, k_ref, v_ref, g_ref, kc_ref, vc_ref, cos_ref, sin_ref, o_ref, q_scr, kt_scr, sb_scr = refs
    else:
        q_ref, k_ref, v_ref, g_ref, o_ref, q_scr, kt_scr = refs
    C = RET_CHUNK
    n_chunk = seq // C
    head = pl.program_id(1)
    lgf = lg_ref[0, head]
    lgb = lg_ref[1, head]

    for n in range(n_chunk):
        rows = slice(n * C, (n + 1) * C)
        qn = q_ref[rows, :].astype(F32)
        kn = k_ref[rows, :].astype(F32)
        if use_rope:
            cos = cos_ref[rows, :]
            sin = sin_ref[rows, :]
            qn = qn * cos + pltpu.roll(qn, RET_DK // 2, 1) * sin
            kn = kn * cos + pltpu.roll(kn, RET_DK // 2, 1) * sin
        q_scr[rows, :] = (qn * (RET_DK ** -0.5)).astype(q_scr.dtype)
        kt_scr[:, rows] = kn.T.astype(kt_scr.dtype)

    ri = lax.broadcasted_iota(jnp.int32, (C, C), 0)
    ci = lax.broadcasted_iota(jnp.int32, (C, C), 1)
    rel = (ri - ci).astype(F32)
    dmask = jnp.where(rel > 0.0, jnp.exp(lgf * jnp.maximum(rel, 0.0)),
                      jnp.where(rel < 0.0, jnp.exp(lgb * jnp.maximum(-rel, 0.0)), 2.0))
    icol = lax.broadcasted_iota(jnp.int32, (C, 1), 0).astype(F32)
    jrow = lax.broadcasted_iota(jnp.int32, (1, C), 1).astype(F32)
    qdec_f = jnp.exp(lgf * (icol + 1.0))
    qdec_b = jnp.exp(lgb * (C - icol))
    kdec_f = jnp.exp(lgf * (C - 1.0 - jrow))
    kdec_b = jnp.exp(lgb * jrow)
    zrow = jnp.zeros((1, RET_DV), F32)
    cdec_f = jnp.exp(zrow + lgf * C)
    cdec_b = jnp.exp(zrow + lgb * C)

    def chunk(ref, n):
        return ref[n * C:(n + 1) * C, :]

    def kt_chunk(n):
        return kt_scr[:, n * C:(n + 1) * C]

    if n_ctx:
        kct = kc_ref[...].astype(F32).T
        vcx = vc_ref[...].astype(MM_DTYPE)
        mrow = lax.broadcasted_iota(jnp.int32, (1, n_ctx), 1).astype(F32)
        s_f = _dot((kct * jnp.exp(lgf * (n_ctx - 1.0 - mrow))).astype(MM_DTYPE), vcx)
        s_b = _dot((kct * jnp.exp(lgb * mrow)).astype(MM_DTYPE), vcx)
        sb_scr[n_chunk - 1] = s_b
        for n in range(n_chunk - 1, 0, -1):
            s_b = s_b * cdec_b + _dot((kt_chunk(n).astype(F32) * kdec_b).astype(MM_DTYPE),
                                      chunk(v_ref, n).astype(MM_DTYPE))
            sb_scr[n - 1] = s_b

    for n0 in range(0, n_chunk, RET_GROUP):
        group = range(n0, min(n0 + RET_GROUP, n_chunk))
        qs = {n: chunk(q_scr, n) for n in group}
        kts = {n: kt_chunk(n) for n in group}
        vs = {n: chunk(v_ref, n).astype(MM_DTYPE) for n in group}
        scores = {n: _dot(qs[n], kts[n]) for n in group}
        if n_ctx:
            incs = {n: _dot((kts[n].astype(F32) * kdec_f).astype(MM_DTYPE), vs[n]) for n in group if n + 1 < n_chunk}
            qcats = {n: jnp.concatenate([qs[n].astype(F32) * qdec_f, qs[n].astype(F32) * qdec_b], axis=1
                                        ).astype(MM_DTYPE) for n in group}
        probs = {n: (scores[n] * dmask).astype(MM_DTYPE) for n in group}
        outs = {n: _dot(probs[n], vs[n]) for n in group}
        if n_ctx:
            for n in group:
                scat = jnp.concatenate([s_f, sb_scr[n]], axis=0).astype(MM_DTYPE)
                outs[n] = outs[n] + _dot(qcats[n], scat)
                if n + 1 < n_chunk:
                    s_f = s_f * cdec_f + incs[n]
        for n in group:
            o = outs[n]
            mu = jnp.mean(o, axis=-1, keepdims=True)
            oc = o - mu
            yn = oc * lax.rsqrt(jnp.mean(oc * oc, axis=-1, keepdims=True) + NORM_EPS)
            gn = chunk(g_ref, n)
            o_ref[n * C:(n + 1) * C, :] = yn.astype(o_ref.dtype) * (gn * jax.nn.sigmoid(gn))


def _retention_latent(p, log_gamma, cos, sin, *, batch, seq, n_ctx, n_lat):
    n = p.shape[0]
    cb = n_lat // n_ctx
    kq, kv = OFF_K // RET_DK, OFF_V // RET_DV
    gs = pltpu.PrefetchScalarGridSpec(
        num_scalar_prefetch=1,
        grid=(batch, RET_HEADS),
        in_specs=[pl.BlockSpec((seq, RET_DK), lambda b, h, lg: (b, h)),
                  pl.BlockSpec((seq, RET_DK), lambda b, h, lg: (b, kq + h)),
                  pl.BlockSpec((seq, RET_DV), lambda b, h, lg: (b, kv + h)),
                  pl.BlockSpec((seq, RET_DV), lambda b, h, lg: (b, OFF_G // RET_DV + h)),
                  pl.BlockSpec((n_ctx, RET_DK), lambda b, h, lg: (cb + b, kq + h)),
                  pl.BlockSpec((n_ctx, RET_DV), lambda b, h, lg: (cb + b, kv + h)),
                  pl.BlockSpec((seq, RET_DK), lambda b, h, lg: (0, 0)),
                  pl.BlockSpec((seq, RET_DK), lambda b, h, lg: (0, 0))],
        out_specs=pl.BlockSpec((seq, RET_DV), lambda b, h, lg: (b, h)),
        scratch_shapes=[pltpu.VMEM((seq, RET_DK), MM_DTYPE),
                        pltpu.VMEM((RET_DK, seq), MM_DTYPE),
                        pltpu.VMEM((seq // RET_CHUNK, RET_DK, RET_DV), F32)])
    return pl.pallas_call(
        functools.partial(_ret_kernel, seq=seq, n_ctx=n_ctx, use_rope=True),
        grid_spec=gs,
        out_shape=jax.ShapeDtypeStruct((n, RET_V), ACT_DTYPE),
        compiler_params=_cparams(("parallel", "parallel")),
    )(log_gamma, p, p, p, p, p, p, cos, sin)


def _retention_ctx(p, log_gamma, y_ret, *, batch, n_ctx, n_lat):
    cb = n_lat // n_ctx
    kq, kv = OFF_K // RET_DK, OFF_V // RET_DV
    gs = pltpu.PrefetchScalarGridSpec(
        num_scalar_prefetch=1,
        grid=(batch, RET_HEADS),
        in_specs=[pl.BlockSpec((n_ctx, RET_DK), lambda b, h, lg: (cb + b, h)),
                  pl.BlockSpec((n_ctx, RET_DK), lambda b, h, lg: (cb + b, kq + h)),
                  pl.BlockSpec((n_ctx, RET_DV), lambda b, h, lg: (cb + b, kv + h)),
                  pl.BlockSpec((n_ctx, RET_DV), lambda b, h, lg: (cb + b, OFF_G // RET_DV + h)),
                  pl.BlockSpec(memory_space=pl.ANY)],
        out_specs=pl.BlockSpec((n_ctx, RET_DV), lambda b, h, lg: (cb + b, h)),
        scratch_shapes=[pltpu.VMEM((n_ctx, RET_DK), MM_DTYPE), pltpu.VMEM((RET_DK, n_ctx), MM_DTYPE)])

    def body(lg_ref, q_ref, k_ref, v_ref, g_ref, alias_ref, o_ref, q_scr, kt_scr):
        del alias_ref
        _ret_kernel(lg_ref, q_ref, k_ref, v_ref, g_ref, o_ref, q_scr, kt_scr, seq=n_ctx, n_ctx=0, use_rope=False)

    return pl.pallas_call(
        body,
        grid_spec=gs,
        out_shape=jax.ShapeDtypeStruct(y_ret.shape, y_ret.dtype),
        input_output_aliases={5: 0},
        compiler_params=_cparams(("parallel", "parallel")),
    )(log_gamma, p, p, p, p, y_ret)


def _convpool_kernel(*refs, seq, aliased):
    if aliased:
        cb_ref, cc_ref, cx_ref, pi_ref, cw_ref, pw_ref, ps_ref, _, _, yc_ref, yp_ref = refs
    else:
        cb_ref, cc_ref, cx_ref, pi_ref, cw_ref, pw_ref, ps_ref, yc_ref, yp_ref = refs
    grp = pl.program_id(1)
    pb, hb = POOL_BLOCK, POOL_HALO
    n_blk = seq // pb

    w = cw_ref[...]
    tl = lax.broadcasted_iota(jnp.int32, (pb, 1), 0)
    cxu = lambda a, b: cc_ref[a:b, :].astype(F32) * cx_ref[a:b, :].astype(F32)
    for b0 in range(0, n_blk, CONV_GROUP):
        group = range(b0, min(b0 + CONV_GROUP, n_blk))
        us = {blk: cxu(blk * pb, (blk + 1) * pb) for blk in group}
        before = {blk: cxu(blk * pb - hb, blk * pb)[hb - 1:hb, :] if blk > 0 else 0.0 for blk in group}
        after = {blk: cxu((blk + 1) * pb, (blk + 1) * pb + hb)[0:1, :] if blk + 1 < n_blk else 0.0 for blk in group}
        prevs = {blk: jnp.where(tl == 0, before[blk], pltpu.roll(us[blk], 1, 0)) for blk in group}
        nexts = {blk: jnp.where(tl == pb - 1, after[blk], pltpu.roll(us[blk], pb - 1, 0)) for blk in group}
        for blk in group:
            conv = w[0:1, :] * prevs[blk] + w[1:2, :] * us[blk] + w[2:3, :] * nexts[blk]
            yc_ref[blk * pb:(blk + 1) * pb, :] = (cb_ref[blk * pb:(blk + 1) * pb, :].astype(F32) * conv
                                                  ).astype(yc_ref.dtype)

    half = jnp.left_shift(1, grp)

    def band(rows, cols, shift):
        dd = (lax.broadcasted_iota(jnp.int32, (rows, cols), 1) + shift
              - lax.broadcasted_iota(jnp.int32, (rows, cols), 0))
        return jnp.where((dd >= -half) & (dd < half), 1.0, 0.0).astype(MM_DTYPE)

    band_self = band(pb, pb, 0)
    band_prev = band(hb, hb, -hb)
    band_next = band(hb, hb, hb)
    blocks = range(n_blk)
    p_blks = [pi_ref[blk * pb:(blk + 1) * pb, :] for blk in blocks]
    wsums = [_dot(band_self, p_blk.astype(MM_DTYPE)) for p_blk in p_blks]
    pooled = []
    for blk in blocks:
        r0 = blk * pb
        top, mid, bot = wsums[blk][:hb], wsums[blk][hb:pb - hb], wsums[blk][pb - hb:]
        if blk > 0:
            top = top + _dot(band_prev, pi_ref[r0 - hb:r0, :].astype(MM_DTYPE))
        if blk + 1 < n_blk:
            bot = bot + _dot(band_next, pi_ref[r0 + pb:r0 + pb + hb, :].astype(MM_DTYPE))
        wsum = jnp.concatenate([top, mid, bot], axis=0)
        if 0 < blk < n_blk - 1:
            cnt = (2 * half).astype(F32)
        else:
            tb = r0 + lax.broadcasted_iota(jnp.int32, (pb, 1), 0)
            cnt = (jnp.clip(tb + half, 0, seq) - jnp.clip(tb - half, 0, seq)).astype(F32)
        pooled.append((wsum / cnt - p_blks[blk].astype(F32)).astype(MM_DTYPE))
    mixed = [_dot(pooled[blk], pw_ref[0]) for blk in blocks]
    for blk in blocks:
        yp_ref[blk * pb:(blk + 1) * pb, :] = (mixed[blk] * ps_ref[...]).astype(yp_ref.dtype)


def _convpool(p, conv_w, pool_w, pool_scale, prev, *, layer, n_seq, seq, row0):
    n = p.shape[0]
    g128 = lambda off: off // POOL_GDIM
    col = lambda off: (lambda b, g: (row0 + b, g128(off) + g))
    in_specs = [pl.BlockSpec((seq, POOL_GDIM), col(OFF_CB)),
                pl.BlockSpec((seq, POOL_GDIM), col(OFF_CC)),
                pl.BlockSpec((seq, POOL_GDIM), col(OFF_CX)),
                pl.BlockSpec((seq, POOL_GDIM), col(OFF_PI)),
                _layer_spec(layer, (conv_w.shape[1], POOL_GDIM), lambda b, g: (0, g)),
                _layer_spec(layer, (1, POOL_GDIM, POOL_GDIM), lambda b, g: (g, 0, 0)),
                _layer_spec(layer, (1, POOL_GDIM), lambda b, g: (0, g))]
    args = [p, p, p, p, conv_w, pool_w, pool_scale]
    aliases = {}
    if prev is not None:
        in_specs += [pl.BlockSpec(memory_space=pl.ANY), pl.BlockSpec(memory_space=pl.ANY)]
        args += list(prev)
        aliases = {7: 0, 8: 1}
    out_spec = pl.BlockSpec((seq, POOL_GDIM), lambda b, g: (row0 + b, g))
    return pl.pallas_call(
        functools.partial(_convpool_kernel, seq=seq, aliased=prev is not None),
        grid=(n_seq, POOL_GROUPS),
        in_specs=in_specs,
        out_specs=[out_spec, out_spec],
        out_shape=[jax.ShapeDtypeStruct((n, CONV_W), ACT_DTYPE), jax.ShapeDtypeStruct((n, POOL_W), ACT_DTYPE)],
        input_output_aliases=aliases,
        compiler_params=_cparams(("parallel", "parallel")),
    )(*args)


def _first_max(vals):
    top = functools.reduce(jnp.maximum, vals)
    idx = jnp.full(top.shape, len(vals) - 1, jnp.int32)
    for k in range(len(vals) - 2, -1, -1):
        idx = jnp.where(vals[k] == top, k, idx)
    return top, idx


def _route(logits_t):
    row = lambda k: logits_t[k:k + 1, :]
    groups = [row(g) for g in range(N_GROUPS)]
    gmax, gidx = _first_max(groups)
    gtop = 1.0 / sum(jnp.exp(g - gmax) for g in groups)
    experts = []
    for k in range(EXP_PER_GROUP):
        e = row(N_GROUPS + (N_GROUPS - 1) * EXP_PER_GROUP + k)
        for g in range(N_GROUPS - 2, -1, -1):
            e = jnp.where(gidx == g, row(N_GROUPS + g * EXP_PER_GROUP + k), e)
        experts.append(e)
    m1, i1 = _first_max(experts)
    m2, i2 = _first_max([jnp.where(i1 == k, -jnp.inf, e) for k, e in enumerate(experts)])
    e2 = jnp.exp(m2 - m1)
    w1 = gtop / (1.0 + e2)
    w2 = gtop * e2 / (1.0 + e2)
    first_lower = i1 < i2
    lo = jnp.minimum(i1, i2)
    hi = jnp.maximum(i1, i2)
    pair = lo * 3 - jnp.right_shift(lo * (lo - 1), 1) + hi - lo - 1
    bucket = gidx * N_PAIRS + pair
    wlo = jnp.where(first_lower, w1, w2)
    whi = jnp.where(first_lower, w2, w1)
    return bucket, wlo, whi


def _column_to_lanes(col):
    return jnp.broadcast_to(col, (col.shape[0], LANES)).T[0:1, :]


def _row_to_column(row):
    return jnp.broadcast_to(row, (8, row.shape[1])).T[:, 0:1]


def _dot_nt(a, b):
    return lax.dot_general(a, b, (((1,), (1,)), ((), ())), preferred_element_type=F32)


def _bucket_run_copies(read_run, max_rows, make_copy):
    def piece(n, local, glob, size):
        @pl.when((n & size) != 0)
        def _():
            off = n & (-2 * size)
            make_copy(pl.multiple_of(local + off, ROW_ALIGN), pl.multiple_of(glob + off, ROW_ALIGN), size).start()

    top = 1 << (max_rows.bit_length() - 1)
    sizes = [top >> k for k in range(top.bit_length()) if top >> k >= ROW_ALIGN]
    rare = [s for s in sizes if s >= RUN_RARE_ROWS]

    def per_bucket(b, carry_):
        n, local, glob = read_run(b)

        @pl.when(n >= RUN_RARE_ROWS)
        def _():
            for size in rare:
                piece(n, local, glob, size)

        for size in sizes[len(rare):]:
            piece(n, local, glob, size)
        return carry_

    lax.fori_loop(0, N_BUCKETS, per_bucket, 0)


def _wait_rows(total, max_rows, make_copy):
    size = 1 << (max_rows.bit_length() - 1)
    while size >= ROW_ALIGN:
        @pl.when((total & size) != 0)
        def _(size=size):
            make_copy(size).wait()
        size //= 2


def _exact_bf16_pieces(w):
    a = w.astype(jnp.bfloat16).astype(F32)
    b = (w - a).astype(jnp.bfloat16).astype(F32)
    c = (w - a - b).astype(jnp.bfloat16).astype(F32)
    return a, b, c


def _outproj_kernel(*refs, cap, lat_tiles):
    if lat_tiles is None:
        x_ref, *refs = refs
        read_x = lambda rows: x_ref[rows, :]
    else:
        x_ref, xc_ref, *refs = refs
        read_x = functools.partial(_pick_stream, lat_tiles, x_ref, xc_ref)
    (yr_ref, yc_ref, yp_ref, g0_ref, g1_ref, g2_ref, gt_ref, sc_ref, sh_ref, ng_ref,
     wr_ref, wc_ref, wp_ref, wo_ref, wrt_ref, brt_ref,
     xo_ref, hs_ref, cnt_ref, tab_ref, lpos_ref,
     h_scr, tab_vmem, tab_smem, sent_smem, carry, row_sem, tab_sem) = refs
    i = pl.program_id(0)
    n_steps = pl.num_programs(0)
    slot = lax.rem(i, 2)
    tm, d = x_ref.shape
    ts = h_scr.shape[1]

    def wait_rows(s):
        _wait_rows(sent_smem[s], ts, lambda size: pltpu.make_async_copy(
            h_scr.at[s, pl.ds(0, size)], hs_ref.at[pl.ds(0, size)], row_sem.at[s]))

    @pl.when(i == 0)
    def _():
        carry[...] = jnp.zeros_like(carry)

    @pl.when(i >= 2)
    def _():
        wait_rows(slot)

    sub = tm // OUT_SUBBLOCKS
    blocks = [slice(a, a + sub) for a in range(0, tm, sub)]
    gate = lambda r, rows: jax.nn.sigmoid(r[rows, :].astype(F32))
    merged = [(gate(g0_ref, rows) * _dot(yr_ref[rows, :], wr_ref[...])
               + gate(g1_ref, rows) * _dot(yc_ref[rows, :], wc_ref[...])
               + gate(g2_ref, rows) * _dot(yp_ref[rows, :], wp_ref[...])).astype(MM_DTYPE) for rows in blocks]
    ys = [_dot(m, wo_ref[...]) for m in merged]
    xs_new = [read_x(rows) + gt_ref[0] * y for rows, y in zip(blocks, ys)]
    for rows, x in zip(blocks, xs_new):
        xo_ref[rows, :] = x
    hs = [_rms_mod(x, ng_ref[...], sc_ref[0], sh_ref[0]) for x in xs_new]
    splits = [_split_bf16(hb) for hb in hs]
    s2s = [_dot_nt(wrt_ref[...], hh) + _dot_nt(wrt_ref[...], hl) for hh, hl in splits]
    routes = [_route(s2[0:ROUTER_LO_ROW, :] + s2[ROUTER_LO_ROW:2 * ROUTER_LO_ROW, :] + brt_ref[...]) for s2 in s2s]
    h = jnp.concatenate([hb.astype(MM_DTYPE) for hb in hs], axis=0)
    bucket, wlo, whi = (jnp.concatenate([route[k] for route in routes], axis=1) for k in (0, 1, 2))

    brow = lax.broadcasted_iota(jnp.int32, (LANES, tm), 0)
    onehot = brow == bucket
    ones = jnp.where(onehot, 1.0, 0.0)
    r = lax.broadcasted_iota(jnp.int32, (tm, tm), 0)
    c = lax.broadcasted_iota(jnp.int32, (tm, tm), 1)
    tri = jnp.where(r <= c, 1.0, 0.0).astype(jnp.bfloat16)
    incl = _dot(ones.astype(jnp.bfloat16), tri)
    cnt_col = incl[:, tm - 1:tm].astype(jnp.int32)
    units_col = jnp.right_shift(cnt_col + (ROW_ALIGN - 1), ALIGN_SHIFT)
    below = jnp.where(lax.broadcasted_iota(jnp.int32, (LANES, LANES), 0)
                      > lax.broadcasted_iota(jnp.int32, (LANES, LANES), 1), 1.0, 0.0).astype(jnp.bfloat16)
    units_b = jnp.broadcast_to(units_col.astype(F32), (LANES, LANES)).astype(jnp.bfloat16)
    start_col = _dot(below, units_b)[:, 0:1] * float(ROW_ALIGN)
    lpos = jnp.sum(jnp.where(onehot, incl - 1.0 + start_col, 0.0), axis=0, keepdims=True)
    lpos_ref[...] = _row_to_column(lpos).astype(jnp.int32)
    run_col = (units_col * ROW_ALIGN).astype(F32)
    run_len = _column_to_lanes(run_col).astype(jnp.int32)
    local_start = _column_to_lanes(start_col).astype(jnp.int32)
    total = (start_col + run_col)[LANES - 1:LANES, :].astype(jnp.int32)
    srow = lax.broadcasted_iota(jnp.int32, (8, LANES), 0)
    tab = jnp.where(srow == TAB_COUNT, run_len,
                    jnp.where(srow == TAB_LOCAL, local_start,
                              jnp.where(srow == TAB_GLOBAL, carry[...], jnp.where(srow == TAB_TOTAL, total, 0))))
    carry[...] = carry[...] + run_len
    tab_ref[0] = tab
    tab_vmem[...] = tab
    to_smem = pltpu.make_async_copy(tab_vmem, tab_smem, tab_sem)
    to_smem.start()

    srt = lax.broadcasted_iota(jnp.int32, (ts, tm), 0)
    perm = jnp.where(srt == lpos.astype(jnp.int32), 1.0, 0.0).astype(MM_DTYPE)
    pieces = _exact_bf16_pieces(wlo) + _exact_bf16_pieces(whi)
    meta_t = jnp.zeros((LANES, tm), F32)
    for k, piece in enumerate(pieces):
        meta_t = jnp.where(brow == k, piece, meta_t)
    h_scr[slot, :, :d] = _dot(perm, h)
    h_scr[slot, :, d:] = _dot_nt(perm, meta_t.astype(MM_DTYPE))
    to_smem.wait()
    sent_smem[slot] = tab_smem[TAB_TOTAL, 0]

    _bucket_run_copies(
        lambda b: (tab_smem[TAB_COUNT, b], tab_smem[TAB_LOCAL, b], b * cap + tab_smem[TAB_GLOBAL, b]), tm,
        lambda local, glob, size: pltpu.make_async_copy(h_scr.at[slot, pl.ds(local, size)],
                                                        hs_ref.at[pl.ds(glob, size)], row_sem.at[slot]))

    @pl.when(i == n_steps - 1)
    def _():
        cnt_ref[...] = jnp.broadcast_to(carry[...], cnt_ref.shape)
        wait_rows(slot)

    @pl.when((i == n_steps - 1) & (i >= 1))
    def _():
        wait_rows(1 - slot)


def _sorted_rows(tm):
    return tm + N_BUCKETS * ROW_ALIGN


def _outproj(xs, p, y_ret, y_conv, y_pool, mod, norm_g, w_ret, w_conv, w_pool, w_o, w_router, b_router,
             *, layer, tm, cap, n_rows, n_lat, seq, batch):
    split = isinstance(xs, tuple)
    n, d = (xs[0].shape[0] + xs[1].shape[0], xs[0].shape[1]) if split else xs.shape
    lat_tiles = n_lat // tm if split else None
    x_specs = _stream_specs(tm, d, lat_tiles) if split else [pl.BlockSpec((tm, d), lambda i: (i, 0))]
    x_args = list(xs) if split else [xs]
    assert cap >= n_rows + (n_rows // tm) * ROW_ALIGN and cap % ROW_ALIGN == 0
    width = d + LANES
    ts = _sorted_rows(tm)
    row = functools.partial(_mod_row, tile=tm, n_lat=n_lat, seq=seq, batch=batch)
    gate = lambda k: pl.BlockSpec((tm, d), lambda i: (i, OFF_GATE // d + k))
    mod_spec = lambda k: _layer_spec(layer, (1, 1, d), lambda i: (row(i), 0, k))
    full = lambda a: _layer_spec(layer, a.shape[1:], lambda i: (0,) * (a.ndim - 1))
    return pl.pallas_call(
        functools.partial(_outproj_kernel, cap=cap, lat_tiles=lat_tiles),
        grid=(n_rows // tm,),
        in_specs=x_specs + [
                  pl.BlockSpec((tm, RET_V), lambda i: (i, 0)),
                  pl.BlockSpec((tm, CONV_W), lambda i: (i, 0)),
                  pl.BlockSpec((tm, POOL_W), lambda i: (i, 0)),
                  gate(0), gate(1), gate(2),
                  mod_spec(2), mod_spec(4), mod_spec(3),
                  full(norm_g), full(w_ret), full(w_conv), full(w_pool), full(w_o), full(w_router), full(b_router)],
        out_specs=[pl.BlockSpec((tm, d), lambda i: (i, 0)),
                   pl.BlockSpec(memory_space=pl.ANY),
                   pl.BlockSpec((8, LANES), lambda i: (0, 0)),
                   pl.BlockSpec((1, 8, LANES), lambda i: (i, 0, 0)),
                   pl.BlockSpec((tm, 1), lambda i: (i, 0))],
        out_shape=[jax.ShapeDtypeStruct((n, d), F32),
                   jax.ShapeDtypeStruct((N_BUCKETS * cap, width), F32),
                   jax.ShapeDtypeStruct((8, LANES), jnp.int32),
                   jax.ShapeDtypeStruct((n_rows // tm, 8, LANES), jnp.int32),
                   jax.ShapeDtypeStruct((n_rows, 1), jnp.int32)],
        scratch_shapes=[pltpu.VMEM((2, ts, width), F32),
                        pltpu.VMEM((8, LANES), jnp.int32),
                        pltpu.SMEM((8, LANES), jnp.int32),
                        pltpu.SMEM((2,), jnp.int32),
                        pltpu.VMEM((1, LANES), jnp.int32),
                        pltpu.SemaphoreType.DMA((2,)),
                        pltpu.SemaphoreType.DMA(())],
        input_output_aliases={} if split else {0: 0},
        compiler_params=_cparams(("arbitrary",)),
    )(*x_args, y_ret, y_conv, y_pool, p, p, p, mod, mod, mod, norm_g, w_ret, w_conv, w_pool, w_o, w_router, b_router)


def _moe_kernel(blk_ref, elo_ref, ehi_ref, valid_ref, hs_ref, w1l_ref, w1h_ref, w3l_ref, w3h_ref, w2l_ref, w2h_ref,
                ys_ref):
    del blk_ref, elo_ref, ehi_ref
    valid = valid_ref[pl.program_id(0)]
    tm, d = ys_ref.shape

    @pl.when(valid > 0)
    def _():
        keep = lax.broadcasted_iota(jnp.int32, (tm, 1), 0) < valid
        meta = jnp.where(keep, hs_ref[:, d:], 0.0)
        h = jnp.where(keep, hs_ref[:, :d], 0.0).astype(MM_DTYPE)
        wlo = meta[:, 0:1] + meta[:, 1:2] + meta[:, 2:3]
        whi = meta[:, 3:4] + meta[:, 4:5] + meta[:, 5:6]

        ups = [(_dot(h, w1[0]), _dot(h, w3[0])) for w1, w3 in ((w1l_ref, w3l_ref), (w1h_ref, w3h_ref))]
        acts = [(a * jax.nn.sigmoid(a) * b).astype(MM_DTYPE) for a, b in ups]
        y_lo, y_hi = (_dot(act, w2[0]) for act, w2 in zip(acts, (w2l_ref, w2h_ref)))
        y = wlo * y_lo + whi * y_hi
        ys_ref[...] = y.astype(ACT_DTYPE).astype(F32)


def _moe(blk, elo, ehi, valid, hs, w1, w3, w2, *, layer, tm, d):
    n_sorted, width = hs.shape
    n_work = blk.shape[0]
    up = lambda sel: _layer_spec(layer, (1, d, D_FF), lambda s, blk, elo, ehi, valid: ((elo, ehi)[sel][s], 0, 0))
    down = lambda sel: _layer_spec(layer, (1, D_FF, d), lambda s, blk, elo, ehi, valid: ((elo, ehi)[sel][s], 0, 0))
    gs = pltpu.PrefetchScalarGridSpec(
        num_scalar_prefetch=4,
        grid=(n_work,),
        in_specs=[pl.BlockSpec((tm, width), lambda s, blk, elo, ehi, valid: (blk[s], 0)),
                  up(0), up(1), up(0), up(1), down(0), down(1)],
        out_specs=pl.BlockSpec((tm, d), lambda s, blk, elo, ehi, valid: (blk[s], 0)))
    return pl.pallas_call(
        _moe_kernel,
        grid_spec=gs,
        out_shape=jax.ShapeDtypeStruct((n_sorted, d), F32),
        compiler_params=_cparams(("arbitrary",)),
    )(blk, elo, ehi, valid, hs, w1, w1, w3, w3, w2, w2)


def _expert_residual(tab_ref, ys_ref, lpos_ref, stage, sems, *, tile_rows, cap):
    i = pl.program_id(0)
    slot = lax.rem(i, 2)
    ts = stage.shape[1]
    entry = lambda tile, row, lane: tab_ref[tile * (TAB_ROWS * LANES) + row * LANES + lane]

    def start(tile, s):
        _bucket_run_copies(
            lambda b: (entry(tile, TAB_COUNT, b), entry(tile, TAB_LOCAL, b), b * cap + entry(tile, TAB_GLOBAL, b)),
            tile_rows,
            lambda local, glob, size: pltpu.make_async_copy(
                ys_ref.at[pl.ds(glob, size)], stage.at[s, pl.ds(local, size)], sems.at[s]))

    @pl.when(i == 0)
    def _():
        start(i, slot)

    @pl.when(i + 1 < pl.num_programs(0))
    def _():
        start(i + 1, 1 - slot)

    total = entry(i, TAB_TOTAL, 0)
    _wait_rows(total, ts, lambda size: pltpu.make_async_copy(
        ys_ref.at[pl.ds(0, size)], stage.at[slot, pl.ds(0, size)], sems.at[slot]))
    lpos = lpos_ref[...]
    unperm = jnp.where(lpos == lax.broadcasted_iota(jnp.int32, (tile_rows, ts), 1), 1.0, 0.0).astype(MM_DTYPE)
    filled = lax.broadcasted_iota(jnp.int32, (ts, 1), 0) < total
    return _dot(unperm, jnp.where(filled, stage[slot], 0.0).astype(MM_DTYPE))


def _final_kernel(tab_ref, x_ref, ys_ref, lpos_ref, gt_ref, g_ref, o_ref, stage, sems, *, cap):
    f = _expert_residual(tab_ref, ys_ref, lpos_ref, stage, sems, tile_rows=x_ref.shape[0], cap=cap)
    x = x_ref[...] + gt_ref[0] * f
    o_ref[...] = x * lax.rsqrt(jnp.mean(x * x, axis=-1, keepdims=True) + NORM_EPS) * g_ref[...]


def _final(xs, moe_out, mod, final_g, *, layer, tm, n_rows, n_lat, seq, batch):
    ys, tab, lpos, cap = moe_out
    d = xs.shape[1]
    row = functools.partial(_mod_row, tile=tm, n_lat=n_lat, seq=seq, batch=batch)
    x_spec = pl.BlockSpec((tm, d), lambda i, tab: (i, 0))
    gs = pltpu.PrefetchScalarGridSpec(
        num_scalar_prefetch=1,
        grid=(n_rows // tm,),
        in_specs=[x_spec,
                  pl.BlockSpec(memory_space=pl.ANY),
                  pl.BlockSpec((tm, 1), lambda i, tab: (i, 0)),
                  _layer_spec(layer, (1, 1, d), lambda i, tab: (row(i), 0, 5)),
                  pl.BlockSpec((1, d), lambda i, tab: (0, 0))],
        out_specs=x_spec,
        scratch_shapes=[pltpu.VMEM((2, _sorted_rows(tm), d), F32), pltpu.SemaphoreType.DMA((2,))])
    return pl.pallas_call(
        functools.partial(_final_kernel, cap=cap),
        grid_spec=gs,
        out_shape=jax.ShapeDtypeStruct((n_rows, d), F32),
        compiler_params=_cparams(("arbitrary",)),
    )(tab, xs, ys, lpos, mod, final_g)


def _rope_tables(seq):
    rows = seq // GRID_W
    row = jnp.repeat(jnp.arange(rows, dtype=F32), GRID_W)
    col = jnp.tile(jnp.arange(GRID_W, dtype=F32), rows)
    n_freq = RET_DK // 4
    inv_freq = ROPE_BASE ** (-jnp.arange(n_freq, dtype=F32) / n_freq)
    ang = jnp.concatenate([row[:, None] * inv_freq[None, :], col[:, None] * inv_freq[None, :]], axis=-1)
    cos, sin = jnp.cos(ang), jnp.sin(ang)
    return jnp.concatenate([cos, cos], axis=-1), jnp.concatenate([-sin, sin], axis=-1)


def _work_tables(counts, *, tm, cap, n_work):
    cnt = counts[0, :N_BUCKETS]
    tiles = (cnt + tm - 1) // tm
    ends = jnp.cumsum(tiles)
    starts = ends - tiles
    item = jnp.arange(n_work, dtype=jnp.int32)
    used = item < ends[-1]
    ref_item = jnp.minimum(item, jnp.maximum(ends[-1] - 1, 0))
    member = ((ref_item[:, None] >= starts[None, :]) & (ref_item[:, None] < ends[None, :])).astype(jnp.int32)
    pick = lambda per_bucket: jnp.sum(member * per_bucket[None, :], axis=1)
    buckets = np.arange(N_BUCKETS)
    j = ref_item - pick(starts)
    valid = jnp.where(used, jnp.clip(pick(cnt) - j * tm, 0, tm), 0)
    blk = pick(jnp.asarray(buckets * (cap // tm), jnp.int32)) + j
    first = (buckets // N_PAIRS) * EXP_PER_GROUP
    elo = pick(jnp.asarray(first + np.asarray(PAIR_LO)[buckets % N_PAIRS], jnp.int32))
    ehi = pick(jnp.asarray(first + np.asarray(PAIR_HI)[buckets % N_PAIRS], jnp.int32))
    i32 = lambda a: a.astype(jnp.int32)
    return i32(blk), i32(elo), i32(ehi), i32(valid)


def _router_weights(w_rg, b_rg, w_re, b_re):
    w = jnp.swapaxes(jnp.concatenate([w_rg, w_re], axis=-1).astype(F32), 1, 2)
    depth, n_out, d = w.shape
    assert n_out <= ROUTER_LO_ROW and 2 * ROUTER_LO_ROW <= LANES and EXP_PER_GROUP == 4
    hi = w.astype(jnp.bfloat16)
    lo = (w - hi.astype(F32)).astype(jnp.bfloat16)
    packed = jnp.zeros((depth, LANES, d), jnp.bfloat16)
    packed = packed.at[:, :n_out].set(hi).at[:, ROUTER_LO_ROW:ROUTER_LO_ROW + n_out].set(lo)
    bias = jnp.zeros((depth, ROUTER_LO_ROW, 1), F32)
    bias = bias.at[:, :n_out, 0].set(jnp.concatenate([b_rg, b_re], axis=-1).astype(F32))
    return packed, bias


def _pick_tile(n, want, *also):
    t = want
    while n % t or any(a % t for a in also):
        t //= 2
    return t


def kernel(x, c, ctx, c_ctx, w_ada, b_ada, norm1, norm2, w_in, ret_decay, conv_w, pool_w, pool_scale, w_ret_out,
           w_conv_out, w_pool_out, w_o, w_rg, b_rg, w_re, b_re, w1, w3, w2, final_norm):
    batch, seq, d = x.shape
    n_ctx = ctx.shape[1]
    depth = w_ada.shape[0]
    n_lat, n_c = batch * seq, batch * n_ctx
    n = n_lat + n_c
    assert POOL_WINDOWS == (2, 4, 8, 16) and POOL_HALO >= max(POOL_WINDOWS) // 2
    assert seq % RET_CHUNK == 0 and n_ctx == RET_CHUNK and seq % GRID_W == 0 and n_lat % n_ctx == 0

    tm_out = _pick_tile(seq, 512, n_c)
    tm_moe = 256

    xs = (x.reshape(n_lat, d), ctx.reshape(n_c, d))
    mod_rows = -(-(batch + 1) // 8) * 8
    cond = jnp.zeros((mod_rows, d), F32).at[:batch].set(c).at[batch].set(c_ctx)
    mod = _ada(cond, w_ada, b_ada).reshape(depth, mod_rows, 1, N_MOD * d)
    cos, sin = _rope_tables(seq)
    log_gamma = jax.nn.log_sigmoid(ret_decay.astype(F32))

    mm = lambda a: a.astype(MM_DTYPE)
    norm1_s, norm2_s, pool_scale_s = norm1[:, None, :], norm2[:, None, :], pool_scale[:, None, :]
    w_in_b, pool_w_b = mm(w_in), mm(pool_w)
    w_ret_b, w_conv_b, w_pool_b, w_o_b = mm(w_ret_out), mm(w_conv_out), mm(w_pool_out), mm(w_o)
    w1_b, w3_b, w2_b = mm(w1), mm(w3), mm(w2)
    w_router, b_router = _router_weights(w_rg, b_rg, w_re, b_re)
    dims = dict(n_lat=n_lat, seq=seq, batch=batch)
    prev = None
    for l in range(depth):
        last = l == depth - 1
        rows = n_lat if last else n
        p, xs = _inproj(xs, prev, norm1_s, mod, w_in_b, layer=l, tm=tm_out, tn=1024, **dims)
        y_ret = _retention_latent(p, log_gamma[l], cos, sin, batch=batch, seq=seq, n_ctx=n_ctx, n_lat=n_lat)
        y_conv, y_pool = _convpool(p, conv_w, pool_w_b, pool_scale_s, None, layer=l, n_seq=batch, seq=seq, row0=0)
        if not last:
            y_ret = _retention_ctx(p, log_gamma[l], y_ret, batch=batch, n_ctx=n_ctx, n_lat=n_lat)
            y_conv, y_pool = _convpool(p, conv_w, pool_w_b, pool_scale_s, (y_conv, y_pool), layer=l,
                                       n_seq=batch, seq=n_ctx, row0=n_lat // n_ctx)
        n_tiles = rows // tm_out
        cap = -(-(rows + n_tiles * ROW_ALIGN) // tm_moe) * tm_moe
        max_sorted = rows + n_tiles * N_BUCKETS * (ROW_ALIGN - 1)
        xs, hs, counts, tab, lpos = _outproj(xs, p, y_ret, y_conv, y_pool, mod, norm2_s, w_ret_b, w_conv_b, w_pool_b,
                                             w_o_b, w_router, b_router, layer=l, tm=tm_out, cap=cap, n_rows=rows,
                                             **dims)
        blk, elo, ehi, valid = _work_tables(counts, tm=tm_moe, cap=cap, n_work=-(-max_sorted // tm_moe) + N_BUCKETS)
        ys = _moe(blk, elo, ehi, valid, hs, w1_b, w3_b, w2_b, layer=l, tm=tm_moe, d=d)
        prev = (ys, tab[:, :TAB_ROWS, :].reshape(-1), lpos, cap)
    out = _final(xs, prev, mod, final_norm[None], layer=depth - 1, tm=tm_out, n_rows=n_lat, **dims)
    return out.reshape(batch, seq, d)
```

```python
import functools

import numpy as np
import jax
import jax.numpy as jnp
from jax import lax
from jax.experimental import pallas as pl
from jax.experimental.pallas import tpu as pltpu

F32 = jnp.float32
MM_DTYPE = jnp.bfloat16
ACT_DTYPE = jnp.bfloat16

NORM_EPS = 1e-6
GRID_W = 64
ROPE_BASE = 10000.0
N_MOD = 6

RET_HEADS = 4
RET_DK = 128
RET_DV = 256
RET_QK = RET_HEADS * RET_DK
RET_V = RET_HEADS * RET_DV
RET_CHUNK = 256
RET_GROUP = 2

CONV_W = 512
CONV_GROUP = 2
POOL_WINDOWS = (2, 4, 8, 16)
POOL_GROUPS = 4
POOL_GDIM = 128
POOL_W = POOL_GROUPS * POOL_GDIM
POOL_BLOCK = 256
POOL_HALO = 16

N_GROUPS = 4
EXP_PER_GROUP = 4
N_EXPERTS = N_GROUPS * EXP_PER_GROUP
D_FF = 512
PAIR_LO = (0, 0, 0, 1, 1, 2)
PAIR_HI = (1, 2, 3, 2, 3, 3)
N_PAIRS = len(PAIR_LO)
N_BUCKETS = N_GROUPS * N_PAIRS

LANES = 128
OUT_SUBBLOCKS = 4
ROW_ALIGN = 8
ALIGN_SHIFT = 3
RUN_RARE_ROWS = 64
TAB_COUNT, TAB_LOCAL, TAB_GLOBAL, TAB_TOTAL, TAB_ROWS = 0, 1, 2, 3, 4
ROUTER_LO_ROW = 32

OFF_Q = 0
OFF_K = OFF_Q + RET_QK
OFF_V = OFF_K + RET_QK
OFF_G = OFF_V + RET_V
OFF_CB = OFF_G + RET_V
OFF_CC = OFF_CB + CONV_W
OFF_CX = OFF_CC + CONV_W
OFF_PI = OFF_CX + CONV_W
OFF_GATE = OFF_PI + POOL_W

VMEM_LIMIT = 56 * 1024 * 1024


def _cparams(sem):
    return pltpu.CompilerParams(dimension_semantics=sem, vmem_limit_bytes=VMEM_LIMIT)


def _split_bf16(a):
    hi = a.astype(jnp.bfloat16)
    lo = (a - hi.astype(F32)).astype(jnp.bfloat16)
    return hi, lo


def _dot(a, b):
    return jnp.dot(a, b, preferred_element_type=F32)


def _dot3(a, b):
    ah, al = _split_bf16(a)
    bh, bl = _split_bf16(b)
    return _dot(ah, bh) + _dot(ah, bl) + _dot(al, bh)


def _layer_spec(layer, block, index, **kw):
    return pl.BlockSpec((None,) + tuple(block), lambda *a: (layer,) + tuple(index(*a)), **kw)


def _mod_row(i, tile, n_lat, seq, batch):
    return jnp.where(i < n_lat // tile, (i * tile) // seq, batch)


def _ada_kernel(c_ref, w_ref, b_ref, o_ref):
    cv = c_ref[...]
    s = cv * jax.nn.sigmoid(cv)
    o_ref[0] = _dot3(s, w_ref[0]) + b_ref[0]


def _ada(cond, w_ada, b_ada):
    depth, d, width = w_ada.shape
    rows = cond.shape[0]
    tn = 512
    return pl.pallas_call(
        _ada_kernel,
        grid=(depth, width // tn),
        in_specs=[pl.BlockSpec((rows, d), lambda l, j: (0, 0)),
                  pl.BlockSpec((1, d, tn), lambda l, j: (l, 0, j)),
                  pl.BlockSpec((1, 1, tn), lambda l, j: (l, 0, j))],
        out_specs=pl.BlockSpec((1, rows, tn), lambda l, j: (l, 0, j)),
        out_shape=jax.ShapeDtypeStruct((depth, rows, width), F32),
        compiler_params=_cparams(("parallel", "parallel")),
    )(cond, w_ada, b_ada.reshape(depth, 1, width))


def _rms_mod(x, g, sc, sh):
    ms = jnp.mean(x * x, axis=-1, keepdims=True)
    return x * lax.rsqrt(ms + NORM_EPS) * g * (1.0 + sc) + sh


def _project_columns(h_scr, w_ref, o_ref, tn):
    for j in range(w_ref.shape[1] // tn):
        cols = slice(j * tn, (j + 1) * tn)
        o_ref[:, cols] = _dot(h_scr[...], w_ref[:, cols]).astype(o_ref.dtype)


def _pick_stream(lat_tiles, lat_ref, ctx_ref, rows=slice(None)):
    return jnp.where(pl.program_id(0) < lat_tiles, lat_ref[rows, :], ctx_ref[rows, :])


def _inproj_kernel(xl_ref, xc_ref, g_ref, sc_ref, sh_ref, w_ref, o_ref, h_scr, *, tn, lat_tiles):
    x = _pick_stream(lat_tiles, xl_ref, xc_ref)
    h_scr[...] = _rms_mod(x, g_ref[...], sc_ref[0], sh_ref[0]).astype(h_scr.dtype)
    _project_columns(h_scr, w_ref, o_ref, tn)


def _stream_specs(tm, d, lat_tiles):
    return [pl.BlockSpec((tm, d), lambda i, *_: (jnp.minimum(i, lat_tiles - 1), 0)),
            pl.BlockSpec((tm, d), lambda i, *_: (jnp.maximum(i - lat_tiles, 0), 0))]


def _inproj_residual_kernel(tab_ref, x_ref, ys_ref, lpos_ref, gt_ref, g_ref, sc_ref, sh_ref, w_ref,
                            o_ref, xo_ref, h_scr, stage, sems, *, tn, cap):
    f = _expert_residual(tab_ref, ys_ref, lpos_ref, stage, sems, tile_rows=x_ref.shape[0], cap=cap)
    x = x_ref[...] + gt_ref[0] * f
    xo_ref[...] = x
    h_scr[...] = _rms_mod(x, g_ref[...], sc_ref[0], sh_ref[0]).astype(h_scr.dtype)
    _project_columns(h_scr, w_ref, o_ref, tn)


def _inproj(xs, prev, norm_g, mod, w, *, layer, tm, tn, n_lat, seq, batch):
    n, d = (xs[0].shape[0] + xs[1].shape[0], xs[0].shape[1]) if prev is None else xs.shape
    width = w.shape[-1]
    row = functools.partial(_mod_row, tile=tm, n_lat=n_lat, seq=seq, batch=batch)
    mod_spec = lambda lyr, k: _layer_spec(lyr, (1, 1, d), lambda i, *_: (row(i), 0, k))
    x_spec = pl.BlockSpec((tm, d), lambda i, *_: (i, 0))
    p_spec = pl.BlockSpec((tm, width), lambda i, *_: (i, 0))
    p_shape = jax.ShapeDtypeStruct((n, width), ACT_DTYPE)
    tail_specs = [_layer_spec(layer, (1, d), lambda i, *_: (0, 0)),
                  mod_spec(layer, 1), mod_spec(layer, 0),
                  _layer_spec(layer, (d, width), lambda i, *_: (0, 0), pipeline_mode=pl.Buffered(1))]
    grid = (n // tm,)
    h_scratch = pltpu.VMEM((tm, d), MM_DTYPE)
    if prev is None:
        lat_tiles = n_lat // tm
        p = pl.pallas_call(functools.partial(_inproj_kernel, tn=tn, lat_tiles=lat_tiles), grid=grid,
                           in_specs=_stream_specs(tm, d, lat_tiles) + tail_specs,
                           out_specs=p_spec, out_shape=p_shape, scratch_shapes=[h_scratch],
                           compiler_params=_cparams(("parallel",)))(*xs, norm_g, mod, mod, w)
        return p, xs
    ys, tab, lpos, cap = prev
    gs = pltpu.PrefetchScalarGridSpec(
        num_scalar_prefetch=1,
        grid=grid,
        in_specs=[x_spec, pl.BlockSpec(memory_space=pl.ANY), pl.BlockSpec((tm, 1), lambda i, *_: (i, 0)),
                  mod_spec(layer - 1, 5)] + tail_specs,
        out_specs=[p_spec, x_spec],
        scratch_shapes=[h_scratch, pltpu.VMEM((2, _sorted_rows(tm), d), F32), pltpu.SemaphoreType.DMA((2,))])
    return pl.pallas_call(
        functools.partial(_inproj_residual_kernel, tn=tn, cap=cap),
        grid_spec=gs,
        out_shape=[p_shape, jax.ShapeDtypeStruct((n, d), F32)],
        input_output_aliases={1: 1},
        compiler_params=_cparams(("arbitrary",)),
    )(tab, xs, ys, lpos, mod, norm_g, mod, mod, w)


def _ret_kernel(lg_ref, *refs, seq, n_ctx, use_rope):
    if n_ctx:
        q_ref, k_ref, v_ref, g_ref, kc_ref, vc_ref, cos_ref, sin_ref, o_ref, q_scr, kt_scr, sb_scr = refs
    else:
        q_ref, k_ref, v_ref, g_ref, o_ref, q_scr, kt_scr = refs
    C = RET_CHUNK
    n_chunk = seq // C
    head = pl.program_id(1)
    lgf = lg_ref[0, head]
    lgb = lg_ref[1, head]

    for n in range(n_chunk):
        rows = slice(n * C, (n + 1) * C)
        qn = q_ref[rows, :].astype(F32)
        kn = k_ref[rows, :].astype(F32)
        if use_rope:
            cos = cos_ref[rows, :]
            sin = sin_ref[rows, :]
            qn = qn * cos + pltpu.roll(qn, RET_DK // 2, 1) * sin
            kn = kn * cos + pltpu.roll(kn, RET_DK // 2, 1) * sin
        q_scr[rows, :] = (qn * (RET_DK ** -0.5)).astype(q_scr.dtype)
        kt_scr[:, rows] = kn.T.astype(kt_scr.dtype)

    ri = lax.broadcasted_iota(jnp.int32, (C, C), 0)
    ci = lax.broadcasted_iota(jnp.int32, (C, C), 1)
    rel = (ri - ci).astype(F32)
    dmask = jnp.where(rel > 0.0, jnp.exp(lgf * jnp.maximum(rel, 0.0)),
                      jnp.where(rel < 0.0, jnp.exp(lgb * jnp.maximum(-rel, 0.0)), 2.0))
    icol = lax.broadcasted_iota(jnp.int32, (C, 1), 0).astype(F32)
    jrow = lax.broadcasted_iota(jnp.int32, (1, C), 1).astype(F32)
    qdec_f = jnp.exp(lgf * (icol + 1.0))
    qdec_b = jnp.exp(lgb * (C - icol))
    kdec_f = jnp.exp(lgf * (C - 1.0 - jrow))
    kdec_b = jnp.exp(lgb * jrow)
    zrow = jnp.zeros((1, RET_DV), F32)
    cdec_f = jnp.exp(zrow + lgf * C)
    cdec_b = jnp.exp(zrow + lgb * C)

    def chunk(ref, n):
        return ref[n * C:(n + 1) * C, :]

    def kt_chunk(n):
        return kt_scr[:, n * C:(n + 1) * C]

    if n_ctx:
        kct = kc_ref[...].astype(F32).T
        vcx = vc_ref[...].astype(MM_DTYPE)
        mrow = lax.broadcasted_iota(jnp.int32, (1, n_ctx), 1).astype(F32)
        s_f = _dot((kct * jnp.exp(lgf * (n_ctx - 1.0 - mrow))).astype(MM_DTYPE), vcx)
        s_b = _dot((kct * jnp.exp(lgb * mrow)).astype(MM_DTYPE), vcx)
        sb_scr[n_chunk - 1] = s_b
        for n in range(n_chunk - 1, 0, -1):
            s_b = s_b * cdec_b + _dot((kt_chunk(n).astype(F32) * kdec_b).astype(MM_DTYPE),
                                      chunk(v_ref, n).astype(MM_DTYPE))
            sb_scr[n - 1] = s_b

    for n0 in range(0, n_chunk, RET_GROUP):
        group = range(n0, min(n0 + RET_GROUP, n_chunk))
        qs = {n: chunk(q_scr, n) for n in group}
        kts = {n: kt_chunk(n) for n in group}
        vs = {n: chunk(v_ref, n).astype(MM_DTYPE) for n in group}
        scores = {n: _dot(qs[n], kts[n]) for n in group}
        if n_ctx:
            incs = {n: _dot((kts[n].astype(F32) * kdec_f).astype(MM_DTYPE), vs[n]) for n in group if n + 1 < n_chunk}
            qcats = {n: jnp.concatenate([qs[n].astype(F32) * qdec_f, qs[n].astype(F32) * qdec_b], axis=1
                                        ).astype(MM_DTYPE) for n in group}
        probs = {n: (scores[n] * dmask).astype(MM_DTYPE) for n in group}
        outs = {n: _dot(probs[n], vs[n]) for n in group}
        if n_ctx:
            for n in group:
                scat = jnp.concatenate([s_f, sb_scr[n]], axis=0).astype(MM_DTYPE)
                outs[n] = outs[n] + _dot(qcats[n], scat)
                if n + 1 < n_chunk:
                    s_f = s_f * cdec_f + incs[n]
        for n in group:
            o = outs[n]
            mu = jnp.mean(o, axis=-1, keepdims=True)
            oc = o - mu
            yn = oc * lax.rsqrt(jnp.mean(oc * oc, axis=-1, keepdims=True) + NORM_EPS)
            gn = chunk(g_ref, n)
            o_ref[n * C:(n + 1) * C, :] = yn.astype(o_ref.dtype) * (gn * jax.nn.sigmoid(gn))


def _retention_latent(p, log_gamma, cos, sin, *, batch, seq, n_ctx, n_lat):
    n = p.shape[0]
    cb = n_lat // n_ctx
    kq, kv = OFF_K // RET_DK, OFF_V // RET_DV
    gs = pltpu.PrefetchScalarGridSpec(
        num_scalar_prefetch=1,
        grid=(batch, RET_HEADS),
        in_specs=[pl.BlockSpec((seq, RET_DK), lambda b, h, lg: (b, h)),
                  pl.BlockSpec((seq, RET_DK), lambda b, h, lg: (b, kq + h)),
                  pl.BlockSpec((seq, RET_DV), lambda b, h, lg: (b, kv + h)),
                  pl.BlockSpec((seq, RET_DV), lambda b, h, lg: (b, OFF_G // RET_DV + h)),
                  pl.BlockSpec((n_ctx, RET_DK), lambda b, h, lg: (cb + b, kq + h)),
                  pl.BlockSpec((n_ctx, RET_DV), lambda b, h, lg: (cb + b, kv + h)),
                  pl.BlockSpec((seq, RET_DK), lambda b, h, lg: (0, 0)),
                  pl.BlockSpec((seq, RET_DK), lambda b, h, lg: (0, 0))],
        out_specs=pl.BlockSpec((seq, RET_DV), lambda b, h, lg: (b, h)),
        scratch_shapes=[pltpu.VMEM((seq, RET_DK), MM_DTYPE),
                        pltpu.VMEM((RET_DK, seq), MM_DTYPE),
                        pltpu.VMEM((seq // RET_CHUNK, RET_DK, RET_DV), F32)])
    return pl.pallas_call(
        functools.partial(_ret_kernel, seq=seq, n_ctx=n_ctx, use_rope=True),
        grid_spec=gs,
        out_shape=jax.ShapeDtypeStruct((n, RET_V), ACT_DTYPE),
        compiler_params=_cparams(("parallel", "parallel")),
    )(log_gamma, p, p, p, p, p, p, cos, sin)


def _retention_ctx(p, log_gamma, y_ret, *, batch, n_ctx, n_lat):
    cb = n_lat // n_ctx
    kq, kv = OFF_K // RET_DK, OFF_V // RET_DV
    gs = pltpu.PrefetchScalarGridSpec(
        num_scalar_prefetch=1,
        grid=(batch, RET_HEADS),
        in_specs=[pl.BlockSpec((n_ctx, RET_DK), lambda b, h, lg: (cb + b, h)),
                  pl.BlockSpec((n_ctx, RET_DK), lambda b, h, lg: (cb + b, kq + h)),
                  pl.BlockSpec((n_ctx, RET_DV), lambda b, h, lg: (cb + b, kv + h)),
                  pl.BlockSpec((n_ctx, RET_DV), lambda b, h, lg: (cb + b, OFF_G // RET_DV + h)),
                  pl.BlockSpec(memory_space=pl.ANY)],
        out_specs=pl.BlockSpec((n_ctx, RET_DV), lambda b, h, lg: (cb + b, h)),
        scratch_shapes=[pltpu.VMEM((n_ctx, RET_DK), MM_DTYPE), pltpu.VMEM((RET_DK, n_ctx), MM_DTYPE)])

    def body(lg_ref, q_ref, k_ref, v_ref, g_ref, alias_ref, o_ref, q_scr, kt_scr):
        del alias_ref
        _ret_kernel(lg_ref, q_ref, k_ref, v_ref, g_ref, o_ref, q_scr, kt_scr, seq=n_ctx, n_ctx=0, use_rope=False)

    return pl.pallas_call(
        body,
        grid_spec=gs,
        out_shape=jax.ShapeDtypeStruct(y_ret.shape, y_ret.dtype),
        input_output_aliases={5: 0},
        compiler_params=_cparams(("parallel", "parallel")),
    )(log_gamma, p, p, p, p, y_ret)


def _convpool_kernel(*refs, seq, aliased):
    if aliased:
        cb_ref, cc_ref, cx_ref, pi_ref, cw_ref, pw_ref, ps_ref, _, _, yc_ref, yp_ref = refs
    else:
        cb_ref, cc_ref, cx_ref, pi_ref, cw_ref, pw_ref, ps_ref, yc_ref, yp_ref = refs
    grp = pl.program_id(1)
    pb, hb = POOL_BLOCK, POOL_HALO
    n_blk = seq // pb

    w = cw_ref[...]
    tl = lax.broadcasted_iota(jnp.int32, (pb, 1), 0)
    cxu = lambda a, b: cc_ref[a:b, :].astype(F32) * cx_ref[a:b, :].astype(F32)
    for b0 in range(0, n_blk, CONV_GROUP):
        group = range(b0, min(b0 + CONV_GROUP, n_blk))
        us = {blk: cxu(blk * pb, (blk + 1) * pb) for blk in group}
        before = {blk: cxu(blk * pb - hb, blk * pb)[hb - 1:hb, :] if blk > 0 else 0.0 for blk in group}
        after = {blk: cxu((blk + 1) * pb, (blk + 1) * pb + hb)[0:1, :] if blk + 1 < n_blk else 0.0 for blk in group}
        prevs = {blk: jnp.where(tl == 0, before[blk], pltpu.roll(us[blk], 1, 0)) for blk in group}
        nexts = {blk: jnp.where(tl == pb - 1, after[blk], pltpu.roll(us[blk], pb - 1, 0)) for blk in group}
        for blk in group:
            conv = w[0:1, :] * prevs[blk] + w[1:2, :] * us[blk] + w[2:3, :] * nexts[blk]
            yc_ref[blk * pb:(blk + 1) * pb, :] = (cb_ref[blk * pb:(blk + 1) * pb, :].astype(F32) * conv
                                                  ).astype(yc_ref.dtype)

    half = jnp.left_shift(1, grp)

    def band(rows, cols, shift):
        dd = (lax.broadcasted_iota(jnp.int32, (rows, cols), 1) + shift
              - lax.broadcasted_iota(jnp.int32, (rows, cols), 0))
        return jnp.where((dd >= -half) & (dd < half), 1.0, 0.0).astype(MM_DTYPE)

    band_self = band(pb, pb, 0)
    band_prev = band(hb, hb, -hb)
    band_next = band(hb, hb, hb)
    blocks = range(n_blk)
    p_blks = [pi_ref[blk * pb:(blk + 1) * pb, :] for blk in blocks]
    wsums = [_dot(band_self, p_blk.astype(MM_DTYPE)) for p_blk in p_blks]
    pooled = []
    for blk in blocks:
        r0 = blk * pb
        top, mid, bot = wsums[blk][:hb], wsums[blk][hb:pb - hb], wsums[blk][pb - hb:]
        if blk > 0:
            top = top + _dot(band_prev, pi_ref[r0 - hb:r0, :].astype(MM_DTYPE))
        if blk + 1 < n_blk:
            bot = bot + _dot(band_next, pi_ref[r0 + pb:r0 + pb + hb, :].astype(MM_DTYPE))
        wsum = jnp.concatenate([top, mid, bot], axis=0)
        if 0 < blk < n_blk - 1:
            cnt = (2 * half).astype(F32)
        else:
            tb = r0 + lax.broadcasted_iota(jnp.int32, (pb, 1), 0)
            cnt = (jnp.clip(tb + half, 0, seq) - jnp.clip(tb - half, 0, seq)).astype(F32)
        pooled.append((wsum / cnt - p_blks[blk].astype(F32)).astype(MM_DTYPE))
    mixed = [_dot(pooled[blk], pw_ref[0]) for blk in blocks]
    for blk in blocks:
        yp_ref[blk * pb:(blk + 1) * pb, :] = (mixed[blk] * ps_ref[...]).astype(yp_ref.dtype)


def _convpool(p, conv_w, pool_w, pool_scale, prev, *, layer, n_seq, seq, row0):
    n = p.shape[0]
    g128 = lambda off: off // POOL_GDIM
    col = lambda off: (lambda b, g: (row0 + b, g128(off) + g))
    in_specs = [pl.BlockSpec((seq, POOL_GDIM), col(OFF_CB)),
                pl.BlockSpec((seq, POOL_GDIM), col(OFF_CC)),
                pl.BlockSpec((seq, POOL_GDIM), col(OFF_CX)),
                pl.BlockSpec((seq, POOL_GDIM), col(OFF_PI)),
                _layer_spec(layer, (conv_w.shape[1], POOL_GDIM), lambda b, g: (0, g)),
                _layer_spec(layer, (1, POOL_GDIM, POOL_GDIM), lambda b, g: (g, 0, 0)),
                _layer_spec(layer, (1, POOL_GDIM), lambda b, g: (0, g))]
    args = [p, p, p, p, conv_w, pool_w, pool_scale]
    aliases = {}
    if prev is not None:
        in_specs += [pl.BlockSpec(memory_space=pl.ANY), pl.BlockSpec(memory_space=pl.ANY)]
        args += list(prev)
        aliases = {7: 0, 8: 1}
    out_spec = pl.BlockSpec((seq, POOL_GDIM), lambda b, g: (row0 + b, g))
    return pl.pallas_call(
        functools.partial(_convpool_kernel, seq=seq, aliased=prev is not None),
        grid=(n_seq, POOL_GROUPS),
        in_specs=in_specs,
        out_specs=[out_spec, out_spec],
        out_shape=[jax.ShapeDtypeStruct((n, CONV_W), ACT_DTYPE), jax.ShapeDtypeStruct((n, POOL_W), ACT_DTYPE)],
        input_output_aliases=aliases,
        compiler_params=_cparams(("parallel", "parallel")),
    )(*args)


def _first_max(vals):
    top = functools.reduce(jnp.maximum, vals)
    idx = jnp.full(top.shape, len(vals) - 1, jnp.int32)
    for k in range(len(vals) - 2, -1, -1):
        idx = jnp.where(vals[k] == top, k, idx)
    return top, idx


def _route(logits_t):
    row = lambda k: logits_t[k:k + 1, :]
    groups = [row(g) for g in range(N_GROUPS)]
    gmax, gidx = _first_max(groups)
    gtop = 1.0 / sum(jnp.exp(g - gmax) for g in groups)
    experts = []
    for k in range(EXP_PER_GROUP):
        e = row(N_GROUPS + (N_GROUPS - 1) * EXP_PER_GROUP + k)
        for g in range(N_GROUPS - 2, -1, -1):
            e = jnp.where(gidx == g, row(N_GROUPS + g * EXP_PER_GROUP + k), e)
        experts.append(e)
    m1, i1 = _first_max(experts)
    m2, i2 = _first_max([jnp.where(i1 == k, -jnp.inf, e) for k, e in enumerate(experts)])
    e2 = jnp.exp(m2 - m1)
    w1 = gtop / (1.0 + e2)
    w2 = gtop * e2 / (1.0 + e2)
    first_lower = i1 < i2
    lo = jnp.minimum(i1, i2)
    hi = jnp.maximum(i1, i2)
    pair = lo * 3 - jnp.right_shift(lo * (lo - 1), 1) + hi - lo - 1
    bucket = gidx * N_PAIRS + pair
    wlo = jnp.where(first_lower, w1, w2)
    whi = jnp.where(first_lower, w2, w1)
    return bucket, wlo, whi


def _column_to_lanes(col):
    return jnp.broadcast_to(col, (col.shape[0], LANES)).T[0:1, :]


def _row_to_column(row):
    return jnp.broadcast_to(row, (8, row.shape[1])).T[:, 0:1]


def _dot_nt(a, b):
    return lax.dot_general(a, b, (((1,), (1,)), ((), ())), preferred_element_type=F32)


def _bucket_run_copies(read_run, max_rows, make_copy):
    def piece(n, local, glob, size):
        @pl.when((n & size) != 0)
        def _():
            off = n & (-2 * size)
            make_copy(pl.multiple_of(local + off, ROW_ALIGN), pl.multiple_of(glob + off, ROW_ALIGN), size).start()

    top = 1 << (max_rows.bit_length() - 1)
    sizes = [top >> k for k in range(top.bit_length()) if top >> k >= ROW_ALIGN]
    rare = [s for s in sizes if s >= RUN_RARE_ROWS]

    def per_bucket(b, carry_):
        n, local, glob = read_run(b)

        @pl.when(n >= RUN_RARE_ROWS)
        def _():
            for size in rare:
                piece(n, local, glob, size)

        for size in sizes[len(rare):]:
            piece(n, local, glob, size)
        return carry_

    lax.fori_loop(0, N_BUCKETS, per_bucket, 0)


def _wait_rows(total, max_rows, make_copy):
    size = 1 << (max_rows.bit_length() - 1)
    while size >= ROW_ALIGN:
        @pl.when((total & size) != 0)
        def _(size=size):
            make_copy(size).wait()
        size //= 2


def _exact_bf16_pieces(w):
    a = w.astype(jnp.bfloat16).astype(F32)
    b = (w - a).astype(jnp.bfloat16).astype(F32)
    c = (w - a - b).astype(jnp.bfloat16).astype(F32)
    return a, b, c


def _outproj_kernel(*refs, cap, lat_tiles):
    if lat_tiles is None:
        x_ref, *refs = refs
        read_x = lambda rows: x_ref[rows, :]
    else:
        x_ref, xc_ref, *refs = refs
        read_x = functools.partial(_pick_stream, lat_tiles, x_ref, xc_ref)
    (yr_ref, yc_ref, yp_ref, g0_ref, g1_ref, g2_ref, gt_ref, sc_ref, sh_ref, ng_ref,
     wr_ref, wc_ref, wp_ref, wo_ref, wrt_ref, brt_ref,
     xo_ref, hs_ref, cnt_ref, tab_ref, lpos_ref,
     h_scr, tab_vmem, tab_smem, sent_smem, carry, row_sem, tab_sem) = refs
    i = pl.program_id(0)
    n_steps = pl.num_programs(0)
    slot = lax.rem(i, 2)
    tm, d = x_ref.shape
    ts = h_scr.shape[1]

    def wait_rows(s):
        _wait_rows(sent_smem[s], ts, lambda size: pltpu.make_async_copy(
            h_scr.at[s, pl.ds(0, size)], hs_ref.at[pl.ds(0, size)], row_sem.at[s]))

    @pl.when(i == 0)
    def _():
        carry[...] = jnp.zeros_like(carry)

    @pl.when(i >= 2)
    def _():
        wait_rows(slot)

    sub = tm // OUT_SUBBLOCKS
    blocks = [slice(a, a + sub) for a in range(0, tm, sub)]
    gate = lambda r, rows: jax.nn.sigmoid(r[rows, :].astype(F32))
    merged = [(gate(g0_ref, rows) * _dot(yr_ref[rows, :], wr_ref[...])
               + gate(g1_ref, rows) * _dot(yc_ref[rows, :], wc_ref[...])
               + gate(g2_ref, rows) * _dot(yp_ref[rows, :], wp_ref[...])).astype(MM_DTYPE) for rows in blocks]
    ys = [_dot(m, wo_ref[...]) for m in merged]
    xs_new = [read_x(rows) + gt_ref[0] * y for rows, y in zip(blocks, ys)]
    for rows, x in zip(blocks, xs_new):
        xo_ref[rows, :] = x
    hs = [_rms_mod(x, ng_ref[...], sc_ref[0], sh_ref[0]) for x in xs_new]
    splits = [_split_bf16(hb) for hb in hs]
    s2s = [_dot_nt(wrt_ref[...], hh) + _dot_nt(wrt_ref[...], hl) for hh, hl in splits]
    routes = [_route(s2[0:ROUTER_LO_ROW, :] + s2[ROUTER_LO_ROW:2 * ROUTER_LO_ROW, :] + brt_ref[...]) for s2 in s2s]
    h = jnp.concatenate([hb.astype(MM_DTYPE) for hb in hs], axis=0)
    bucket, wlo, whi = (jnp.concatenate([route[k] for route in routes], axis=1) for k in (0, 1, 2))

    brow = lax.broadcasted_iota(jnp.int32, (LANES, tm), 0)
    onehot = brow == bucket
    ones = jnp.where(onehot, 1.0, 0.0)
    r = lax.broadcasted_iota(jnp.int32, (tm, tm), 0)
    c = lax.broadcasted_iota(jnp.int32, (tm, tm), 1)
    tri = jnp.where(r <= c, 1.0, 0.0).astype(jnp.bfloat16)
    incl = _dot(ones.astype(jnp.bfloat16), tri)
    cnt_col = incl[:, tm - 1:tm].astype(jnp.int32)
    units_col = jnp.right_shift(cnt_col + (ROW_ALIGN - 1), ALIGN_SHIFT)
    below = jnp.where(lax.broadcasted_iota(jnp.int32, (LANES, LANES), 0)
                      > lax.broadcasted_iota(jnp.int32, (LANES, LANES), 1), 1.0, 0.0).astype(jnp.bfloat16)
    units_b = jnp.broadcast_to(units_col.astype(F32), (LANES, LANES)).astype(jnp.bfloat16)
    start_col = _dot(below, units_b)[:, 0:1] * float(ROW_ALIGN)
    lpos = jnp.sum(jnp.where(onehot, incl - 1.0 + start_col, 0.0), axis=0, keepdims=True)
    lpos_ref[...] = _row_to_column(lpos).astype(jnp.int32)
    run_col = (units_col * ROW_ALIGN).astype(F32)
    run_len = _column_to_lanes(run_col).astype(jnp.int32)
    local_start = _column_to_lanes(start_col).astype(jnp.int32)
    total = (start_col + run_col)[LANES - 1:LANES, :].astype(jnp.int32)
    srow = lax.broadcasted_iota(jnp.int32, (8, LANES), 0)
    tab = jnp.where(srow == TAB_COUNT, run_len,
                    jnp.where(srow == TAB_LOCAL, local_start,
                              jnp.where(srow == TAB_GLOBAL, carry[...], jnp.where(srow == TAB_TOTAL, total, 0))))
    carry[...] = carry[...] + run_len
    tab_ref[0] = tab
    tab_vmem[...] = tab
    to_smem = pltpu.make_async_copy(tab_vmem, tab_smem, tab_sem)
    to_smem.start()

    srt = lax.broadcasted_iota(jnp.int32, (ts, tm), 0)
    perm = jnp.where(srt == lpos.astype(jnp.int32), 1.0, 0.0).astype(MM_DTYPE)
    pieces = _exact_bf16_pieces(wlo) + _exact_bf16_pieces(whi)
    meta_t = jnp.zeros((LANES, tm), F32)
    for k, piece in enumerate(pieces):
        meta_t = jnp.where(brow == k, piece, meta_t)
    h_scr[slot, :, :d] = _dot(perm, h)
    h_scr[slot, :, d:] = _dot_nt(perm, meta_t.astype(MM_DTYPE))
    to_smem.wait()
    sent_smem[slot] = tab_smem[TAB_TOTAL, 0]

    _bucket_run_copies(
        lambda b: (tab_smem[TAB_COUNT, b], tab_smem[TAB_LOCAL, b], b * cap + tab_smem[TAB_GLOBAL, b]), tm,
        lambda local, glob, size: pltpu.make_async_copy(h_scr.at[slot, pl.ds(local, size)],
                                                        hs_ref.at[pl.ds(glob, size)], row_sem.at[slot]))

    @pl.when(i == n_steps - 1)
    def _():
        cnt_ref[...] = jnp.broadcast_to(carry[...], cnt_ref.shape)
        wait_rows(slot)

    @pl.when((i == n_steps - 1) & (i >= 1))
    def _():
        wait_rows(1 - slot)


def _sorted_rows(tm):
    return tm + N_BUCKETS * ROW_ALIGN


def _outproj(xs, p, y_ret, y_conv, y_pool, mod, norm_g, w_ret, w_conv, w_pool, w_o, w_router, b_router,
             *, layer, tm, cap, n_rows, n_lat, seq, batch):
    split = isinstance(xs, tuple)
    n, d = (xs[0].shape[0] + xs[1].shape[0], xs[0].shape[1]) if split else xs.shape
    lat_tiles = n_lat // tm if split else None
    x_specs = _stream_specs(tm, d, lat_tiles) if split else [pl.BlockSpec((tm, d), lambda i: (i, 0))]
    x_args = list(xs) if split else [xs]
    assert cap >= n_rows + (n_rows // tm) * ROW_ALIGN and cap % ROW_ALIGN == 0
    width = d + LANES
    ts = _sorted_rows(tm)
    row = functools.partial(_mod_row, tile=tm, n_lat=n_lat, seq=seq, batch=batch)
    gate = lambda k: pl.BlockSpec((tm, d), lambda i: (i, OFF_GATE // d + k))
    mod_spec = lambda k: _layer_spec(layer, (1, 1, d), lambda i: (row(i), 0, k))
    full = lambda a: _layer_spec(layer, a.shape[1:], lambda i: (0,) * (a.ndim - 1))
    return pl.pallas_call(
        functools.partial(_outproj_kernel, cap=cap, lat_tiles=lat_tiles),
        grid=(n_rows // tm,),
        in_specs=x_specs + [
                  pl.BlockSpec((tm, RET_V), lambda i: (i, 0)),
                  pl.BlockSpec((tm, CONV_W), lambda i: (i, 0)),
                  pl.BlockSpec((tm, POOL_W), lambda i: (i, 0)),
                  gate(0), gate(1), gate(2),
                  mod_spec(2), mod_spec(4), mod_spec(3),
                  full(norm_g), full(w_ret), full(w_conv), full(w_pool), full(w_o), full(w_router), full(b_router)],
        out_specs=[pl.BlockSpec((tm, d), lambda i: (i, 0)),
                   pl.BlockSpec(memory_space=pl.ANY),
                   pl.BlockSpec((8, LANES), lambda i: (0, 0)),
                   pl.BlockSpec((1, 8, LANES), lambda i: (i, 0, 0)),
                   pl.BlockSpec((tm, 1), lambda i: (i, 0))],
        out_shape=[jax.ShapeDtypeStruct((n, d), F32),
                   jax.ShapeDtypeStruct((N_BUCKETS * cap, width), F32),
                   jax.ShapeDtypeStruct((8, LANES), jnp.int32),
                   jax.ShapeDtypeStruct((n_rows // tm, 8, LANES), jnp.int32),
                   jax.ShapeDtypeStruct((n_rows, 1), jnp.int32)],
        scratch_shapes=[pltpu.VMEM((2, ts, width), F32),
                        pltpu.VMEM((8, LANES), jnp.int32),
                        pltpu.SMEM((8, LANES), jnp.int32),
                        pltpu.SMEM((2,), jnp.int32),
                        pltpu.VMEM((1, LANES), jnp.int32),
                        pltpu.SemaphoreType.DMA((2,)),
                        pltpu.SemaphoreType.DMA(())],
        input_output_aliases={} if split else {0: 0},
        compiler_params=_cparams(("arbitrary",)),
    )(*x_args, y_ret, y_conv, y_pool, p, p, p, mod, mod, mod, norm_g, w_ret, w_conv, w_pool, w_o, w_router, b_router)


def _moe_kernel(blk_ref, elo_ref, ehi_ref, valid_ref, hs_ref, w1l_ref, w1h_ref, w3l_ref, w3h_ref, w2l_ref, w2h_ref,
                ys_ref):
    del blk_ref, elo_ref, ehi_ref
    valid = valid_ref[pl.program_id(0)]
    tm, d = ys_ref.shape

    @pl.when(valid > 0)
    def _():
        keep = lax.broadcasted_iota(jnp.int32, (tm, 1), 0) < valid
        meta = jnp.where(keep, hs_ref[:, d:], 0.0)
        h = jnp.where(keep, hs_ref[:, :d], 0.0).astype(MM_DTYPE)
        wlo = meta[:, 0:1] + meta[:, 1:2] + meta[:, 2:3]
        whi = meta[:, 3:4] + meta[:, 4:5] + meta[:, 5:6]

        ups = [(_dot(h, w1[0]), _dot(h, w3[0])) for w1, w3 in ((w1l_ref, w3l_ref), (w1h_ref, w3h_ref))]
        acts = [(a * jax.nn.sigmoid(a) * b).astype(MM_DTYPE) for a, b in ups]
        y_lo, y_hi = (_dot(act, w2[0]) for act, w2 in zip(acts, (w2l_ref, w2h_ref)))
        y = wlo * y_lo + whi * y_hi
        ys_ref[...] = y.astype(ACT_DTYPE).astype(F32)


def _moe(blk, elo, ehi, valid, hs, w1, w3, w2, *, layer, tm, d):
    n_sorted, width = hs.shape
    n_work = blk.shape[0]
    up = lambda sel: _layer_spec(layer, (1, d, D_FF), lambda s, blk, elo, ehi, valid: ((elo, ehi)[sel][s], 0, 0))
    down = lambda sel: _layer_spec(layer, (1, D_FF, d), lambda s, blk, elo, ehi, valid: ((elo, ehi)[sel][s], 0, 0))
    gs = pltpu.PrefetchScalarGridSpec(
        num_scalar_prefetch=4,
        grid=(n_work,),
        in_specs=[pl.BlockSpec((tm, width), lambda s, blk, elo, ehi, valid: (blk[s], 0)),
                  up(0), up(1), up(0), up(1), down(0), down(1)],
        out_specs=pl.BlockSpec((tm, d), lambda s, blk, elo, ehi, valid: (blk[s], 0)))
    return pl.pallas_call(
        _moe_kernel,
        grid_spec=gs,
        out_shape=jax.ShapeDtypeStruct((n_sorted, d), F32),
        compiler_params=_cparams(("arbitrary",)),
    )(blk, elo, ehi, valid, hs, w1, w1, w3, w3, w2, w2)


def _expert_residual(tab_ref, ys_ref, lpos_ref, stage, sems, *, tile_rows, cap):
    i = pl.program_id(0)
    slot = lax.rem(i, 2)
    ts = stage.shape[1]
    entry = lambda tile, row, lane: tab_ref[tile * (TAB_ROWS * LANES) + row * LANES + lane]

    def start(tile, s):
        _bucket_run_copies(
            lambda b: (entry(tile, TAB_COUNT, b), entry(tile, TAB_LOCAL, b), b * cap + entry(tile, TAB_GLOBAL, b)),
            tile_rows,
            lambda local, glob, size: pltpu.make_async_copy(
                ys_ref.at[pl.ds(glob, size)], stage.at[s, pl.ds(local, size)], sems.at[s]))

    @pl.when(i == 0)
    def _():
        start(i, slot)

    @pl.when(i + 1 < pl.num_programs(0))
    def _():
        start(i + 1, 1 - slot)

    total = entry(i, TAB_TOTAL, 0)
    _wait_rows(total, ts, lambda size: pltpu.make_async_copy(
        ys_ref.at[pl.ds(0, size)], stage.at[slot, pl.ds(0, size)], sems.at[slot]))
    lpos = lpos_ref[...]
    unperm = jnp.where(lpos == lax.broadcasted_iota(jnp.int32, (tile_rows, ts), 1), 1.0, 0.0).astype(MM_DTYPE)
    filled = lax.broadcasted_iota(jnp.int32, (ts, 1), 0) < total
    return _dot(unperm, jnp.where(filled, stage[slot], 0.0).astype(MM_DTYPE))


def _final_kernel(tab_ref, x_ref, ys_ref, lpos_ref, gt_ref, g_ref, o_ref, stage, sems, *, cap):
    f = _expert_residual(tab_ref, ys_ref, lpos_ref, stage, sems, tile_rows=x_ref.shape[0], cap=cap)
    x = x_ref[...] + gt_ref[0] * f
    o_ref[...] = x * lax.rsqrt(jnp.mean(x * x, axis=-1, keepdims=True) + NORM_EPS) * g_ref[...]


def _final(xs, moe_out, mod, final_g, *, layer, tm, n_rows, n_lat, seq, batch):
    ys, tab, lpos, cap = moe_out
    d = xs.shape[1]
    row = functools.partial(_mod_row, tile=tm, n_lat=n_lat, seq=seq, batch=batch)
    x_spec = pl.BlockSpec((tm, d), lambda i, tab: (i, 0))
    gs = pltpu.PrefetchScalarGridSpec(
        num_scalar_prefetch=1,
        grid=(n_rows // tm,),
        in_specs=[x_spec,
                  pl.BlockSpec(memory_space=pl.ANY),
                  pl.BlockSpec((tm, 1), lambda i, tab: (i, 0)),
                  _layer_spec(layer, (1, 1, d), lambda i, tab: (row(i), 0, 5)),
                  pl.BlockSpec((1, d), lambda i, tab: (0, 0))],
        out_specs=x_spec,
        scratch_shapes=[pltpu.VMEM((2, _sorted_rows(tm), d), F32), pltpu.SemaphoreType.DMA((2,))])
    return pl.pallas_call(
        functools.partial(_final_kernel, cap=cap),
        grid_spec=gs,
        out_shape=jax.ShapeDtypeStruct((n_rows, d), F32),
        compiler_params=_cparams(("arbitrary",)),
    )(tab, xs, ys, lpos, mod, final_g)


def _rope_tables(seq):
    rows = seq // GRID_W
    row = jnp.repeat(jnp.arange(rows, dtype=F32), GRID_W)
    col = jnp.tile(jnp.arange(GRID_W, dtype=F32), rows)
    n_freq = RET_DK // 4
    inv_freq = ROPE_BASE ** (-jnp.arange(n_freq, dtype=F32) / n_freq)
    ang = jnp.concatenate([row[:, None] * inv_freq[None, :], col[:, None] * inv_freq[None, :]], axis=-1)
    cos, sin = jnp.cos(ang), jnp.sin(ang)
    return jnp.concatenate([cos, cos], axis=-1), jnp.concatenate([-sin, sin], axis=-1)


def _work_tables(counts, *, tm, cap, n_work):
    cnt = counts[0, :N_BUCKETS]
    tiles = (cnt + tm - 1) // tm
    ends = jnp.cumsum(tiles)
    starts = ends - tiles
    item = jnp.arange(n_work, dtype=jnp.int32)
    used = item < ends[-1]
    ref_item = jnp.minimum(item, jnp.maximum(ends[-1] - 1, 0))
    member = ((ref_item[:, None] >= starts[None, :]) & (ref_item[:, None] < ends[None, :])).astype(jnp.int32)
    pick = lambda per_bucket: jnp.sum(member * per_bucket[None, :], axis=1)
    buckets = np.arange(N_BUCKETS)
    j = ref_item - pick(starts)
    valid = jnp.where(used, jnp.clip(pick(cnt) - j * tm, 0, tm), 0)
    blk = pick(jnp.asarray(buckets * (cap // tm), jnp.int32)) + j
    first = (buckets // N_PAIRS) * EXP_PER_GROUP
    elo = pick(jnp.asarray(first + np.asarray(PAIR_LO)[buckets % N_PAIRS], jnp.int32))
    ehi = pick(jnp.asarray(first + np.asarray(PAIR_HI)[buckets % N_PAIRS], jnp.int32))
    i32 = lambda a: a.astype(jnp.int32)
    return i32(blk), i32(elo), i32(ehi), i32(valid)


def _router_weights(w_rg, b_rg, w_re, b_re):
    w = jnp.swapaxes(jnp.concatenate([w_rg, w_re], axis=-1).astype(F32), 1, 2)
    depth, n_out, d = w.shape
    assert n_out <= ROUTER_LO_ROW and 2 * ROUTER_LO_ROW <= LANES and EXP_PER_GROUP == 4
    hi = w.astype(jnp.bfloat16)
    lo = (w - hi.astype(F32)).astype(jnp.bfloat16)
    packed = jnp.zeros((depth, LANES, d), jnp.bfloat16)
    packed = packed.at[:, :n_out].set(hi).at[:, ROUTER_LO_ROW:ROUTER_LO_ROW + n_out].set(lo)
    bias = jnp.zeros((depth, ROUTER_LO_ROW, 1), F32)
    bias = bias.at[:, :n_out, 0].set(jnp.concatenate([b_rg, b_re], axis=-1).astype(F32))
    return packed, bias


def _pick_tile(n, want, *also):
    t = want
    while n % t or any(a % t for a in also):
        t //= 2
    return t


def kernel(x, c, ctx, c_ctx, w_ada, b_ada, norm1, norm2, w_in, ret_decay, conv_w, pool_w, pool_scale, w_ret_out,
           w_conv_out, w_pool_out, w_o, w_rg, b_rg, w_re, b_re, w1, w3, w2, final_norm):
    batch, seq, d = x.shape
    n_ctx = ctx.shape[1]
    depth = w_ada.shape[0]
    n_lat, n_c = batch * seq, batch * n_ctx
    n = n_lat + n_c
    assert POOL_WINDOWS == (2, 4, 8, 16) and POOL_HALO >= max(POOL_WINDOWS) // 2
    assert seq % RET_CHUNK == 0 and n_ctx == RET_CHUNK and seq % GRID_W == 0 and n_lat % n_ctx == 0

    tm_out = _pick_tile(seq, 512, n_c)
    tm_moe = 512

    xs = (x.reshape(n_lat, d), ctx.reshape(n_c, d))
    mod_rows = -(-(batch + 1) // 8) * 8
    cond = jnp.zeros((mod_rows, d), F32).at[:batch].set(c).at[batch].set(c_ctx)
    mod = _ada(cond, w_ada, b_ada).reshape(depth, mod_rows, 1, N_MOD * d)
    cos, sin = _rope_tables(seq)
    log_gamma = jax.nn.log_sigmoid(ret_decay.astype(F32))

    mm = lambda a: a.astype(MM_DTYPE)
    norm1_s, norm2_s, pool_scale_s = norm1[:, None, :], norm2[:, None, :], pool_scale[:, None, :]
    w_in_b, pool_w_b = mm(w_in), mm(pool_w)
    w_ret_b, w_conv_b, w_pool_b, w_o_b = mm(w_ret_out), mm(w_conv_out), mm(w_pool_out), mm(w_o)
    w1_b, w3_b, w2_b = mm(w1), mm(w3), mm(w2)
    w_router, b_router = _router_weights(w_rg, b_rg, w_re, b_re)
    dims = dict(n_lat=n_lat, seq=seq, batch=batch)
    prev = None
    for l in range(depth):
        last = l == depth - 1
        rows = n_lat if last else n
        p, xs = _inproj(xs, prev, norm1_s, mod, w_in_b, layer=l, tm=tm_out, tn=1024, **dims)
        y_ret = _retention_latent(p, log_gamma[l], cos, sin, batch=batch, seq=seq, n_ctx=n_ctx, n_lat=n_lat)
        y_conv, y_pool = _convpool(p, conv_w, pool_w_b, pool_scale_s, None, layer=l, n_seq=batch, seq=seq, row0=0)
        if not last:
            y_ret = _retention_ctx(p, log_gamma[l], y_ret, batch=batch, n_ctx=n_ctx, n_lat=n_lat)
            y_conv, y_pool = _convpool(p, conv_w, pool_w_b, pool_scale_s, (y_conv, y_pool), layer=l,
                                       n_seq=batch, seq=n_ctx, row0=n_lat // n_ctx)
        n_tiles = rows // tm_out
        cap = -(-(rows + n_tiles * ROW_ALIGN) // tm_moe) * tm_moe
        max_sorted = rows + n_tiles * N_BUCKETS * (ROW_ALIGN - 1)
        xs, hs, counts, tab, lpos = _outproj(xs, p, y_ret, y_conv, y_pool, mod, norm2_s, w_ret_b, w_conv_b, w_pool_b,
                                             w_o_b, w_router, b_router, layer=l, tm=tm_out, cap=cap, n_rows=rows,
                                             **dims)
        blk, elo, ehi, valid = _work_tables(counts, tm=tm_moe, cap=cap, n_work=-(-max_sorted // tm_moe) + N_BUCKETS)
        ys = _moe(blk, elo, ehi, valid, hs, w1_b, w3_b, w2_b, layer=l, tm=tm_moe, d=d)
        prev = (ys, tab[:, :TAB_ROWS, :].reshape(-1), lpos, cap)
    out = _final(xs, prev, mod, final_norm[None], layer=depth - 1, tm=tm_out, n_rows=n_lat, **dims)
    return out.reshape(batch, seq, d)
```

```python
import functools

import numpy as np
import jax
import jax.numpy as jnp
from jax import lax
from jax.experimental import pallas as pl
from jax.experimental.pallas import tpu as pltpu

F32 = jnp.float32
MM_DTYPE = jnp.bfloat16
ACT_DTYPE = jnp.bfloat16

NORM_EPS = 1e-6
GRID_W = 64
ROPE_BASE = 10000.0
N_MOD = 6

RET_HEADS = 4
RET_DK = 128
RET_DV = 256
RET_QK = RET_HEADS * RET_DK
RET_V = RET_HEADS * RET_DV
RET_CHUNK = 256
RET_GROUP = 2

CONV_W = 512
CONV_GROUP = 2
POOL_WINDOWS = (2, 4, 8, 16)
POOL_GROUPS = 4
POOL_GDIM = 128
POOL_W = POOL_GROUPS * POOL_GDIM
POOL_BLOCK = 256
POOL_HALO = 16

N_GROUPS = 4
EXP_PER_GROUP = 4
N_EXPERTS = N_GROUPS * EXP_PER_GROUP
D_FF = 512
PAIR_LO = (0, 0, 0, 1, 1, 2)
PAIR_HI = (1, 2, 3, 2, 3, 3)
N_PAIRS = len(PAIR_LO)
N_BUCKETS = N_GROUPS * N_PAIRS

LANES = 128
OUT_SUBBLOCKS = 4
ROW_ALIGN = 8
ALIGN_SHIFT = 3
RUN_RARE_ROWS = 64
TAB_COUNT, TAB_LOCAL, TAB_GLOBAL, TAB_TOTAL, TAB_ROWS = 0, 1, 2, 3, 4
ROUTER_LO_ROW = 32

OFF_Q = 0
OFF_K = OFF_Q + RET_QK
OFF_V = OFF_K + RET_QK
OFF_G = OFF_V + RET_V
OFF_CB = OFF_G + RET_V
OFF_CC = OFF_CB + CONV_W
OFF_CX = OFF_CC + CONV_W
OFF_PI = OFF_CX + CONV_W
OFF_GATE = OFF_PI + POOL_W

VMEM_LIMIT = 56 * 1024 * 1024


def _cparams(sem):
    return pltpu.CompilerParams(dimension_semantics=sem, vmem_limit_bytes=VMEM_LIMIT)


def _split_bf16(a):
    hi = a.astype(jnp.bfloat16)
    lo = (a - hi.astype(F32)).astype(jnp.bfloat16)
    return hi, lo


def _dot(a, b):
    return jnp.dot(a, b, preferred_element_type=F32)


def _dot3(a, b):
    ah, al = _split_bf16(a)
    bh, bl = _split_bf16(b)
    return _dot(ah, bh) + _dot(ah, bl) + _dot(al, bh)


def _layer_spec(layer, block, index, **kw):
    return pl.BlockSpec((None,) + tuple(block), lambda *a: (layer,) + tuple(index(*a)), **kw)


def _mod_row(i, tile, n_lat, seq, batch):
    return jnp.where(i < n_lat // tile, (i * tile) // seq, batch)


def _ada_kernel(c_ref, w_ref, b_ref, o_ref):
    cv = c_ref[...]
    s = cv * jax.nn.sigmoid(cv)
    o_ref[0] = _dot3(s, w_ref[0]) + b_ref[0]


def _ada(cond, w_ada, b_ada):
    depth, d, width = w_ada.shape
    rows = cond.shape[0]
    tn = 512
    return pl.pallas_call(
        _ada_kernel,
        grid=(depth, width // tn),
        in_specs=[pl.BlockSpec((rows, d), lambda l, j: (0, 0)),
                  pl.BlockSpec((1, d, tn), lambda l, j: (l, 0, j)),
                  pl.BlockSpec((1, 1, tn), lambda l, j: (l, 0, j))],
        out_specs=pl.BlockSpec((1, rows, tn), lambda l, j: (l, 0, j)),
        out_shape=jax.ShapeDtypeStruct((depth, rows, width), F32),
        compiler_params=_cparams(("parallel", "parallel")),
    )(cond, w_ada, b_ada.reshape(depth, 1, width))


def _rms_mod(x, g, sc, sh):
    ms = jnp.mean(x * x, axis=-1, keepdims=True)
    return x * lax.rsqrt(ms + NORM_EPS) * g * (1.0 + sc) + sh


def _project_columns(h_scr, w_ref, o_ref, tn):
    for j in range(w_ref.shape[1] // tn):
        cols = slice(j * tn, (j + 1) * tn)
        o_ref[:, cols] = _dot(h_scr[...], w_ref[:, cols]).astype(o_ref.dtype)


def _pick_stream(lat_tiles, lat_ref, ctx_ref, rows=slice(None)):
    return jnp.where(pl.program_id(0) < lat_tiles, lat_ref[rows, :], ctx_ref[rows, :])


def _inproj_kernel(xl_ref, xc_ref, g_ref, sc_ref, sh_ref, w_ref, o_ref, h_scr, *, tn, lat_tiles):
    x = _pick_stream(lat_tiles, xl_ref, xc_ref)
    h_scr[...] = _rms_mod(x, g_ref[...], sc_ref[0], sh_ref[0]).astype(h_scr.dtype)
    _project_columns(h_scr, w_ref, o_ref, tn)


def _stream_specs(tm, d, lat_tiles):
    return [pl.BlockSpec((tm, d), lambda i, *_: (jnp.minimum(i, lat_tiles - 1), 0)),
            pl.BlockSpec((tm, d), lambda i, *_: (jnp.maximum(i - lat_tiles, 0), 0))]


def _inproj_residual_kernel(tab_ref, x_ref, ys_ref, lpos_ref, gt_ref, g_ref, sc_ref, sh_ref, w_ref,
                            o_ref, xo_ref, h_scr, stage, sems, *, tn, cap):
    f = _expert_residual(tab_ref, ys_ref, lpos_ref, stage, sems, tile_rows=x_ref.shape[0], cap=cap)
    x = x_ref[...] + gt_ref[0] * f
    xo_ref[...] = x
    h_scr[...] = _rms_mod(x, g_ref[...], sc_ref[0], sh_ref[0]).astype(h_scr.dtype)
    _project_columns(h_scr, w_ref, o_ref, tn)


def _inproj(xs, prev, norm_g, mod, w, *, layer, tm, tn, n_lat, seq, batch):
    n, d = (xs[0].shape[0] + xs[1].shape[0], xs[0].shape[1]) if prev is None else xs.shape
    width = w.shape[-1]
    row = functools.partial(_mod_row, tile=tm, n_lat=n_lat, seq=seq, batch=batch)
    mod_spec = lambda lyr, k: _layer_spec(lyr, (1, 1, d), lambda i, *_: (row(i), 0, k))
    x_spec = pl.BlockSpec((tm, d), lambda i, *_: (i, 0))
    p_spec = pl.BlockSpec((tm, width), lambda i, *_: (i, 0))
    p_shape = jax.ShapeDtypeStruct((n, width), ACT_DTYPE)
    tail_specs = [_layer_spec(layer, (1, d), lambda i, *_: (0, 0)),
                  mod_spec(layer, 1), mod_spec(layer, 0),
                  _layer_spec(layer, (d, width), lambda i, *_: (0, 0), pipeline_mode=pl.Buffered(1))]
    grid = (n // tm,)
    h_scratch = pltpu.VMEM((tm, d), MM_DTYPE)
    if prev is None:
        lat_tiles = n_lat // tm
        p = pl.pallas_call(functools.partial(_inproj_kernel, tn=tn, lat_tiles=lat_tiles), grid=grid,
                           in_specs=_stream_specs(tm, d, lat_tiles) + tail_specs,
                           out_specs=p_spec, out_shape=p_shape, scratch_shapes=[h_scratch],
                           compiler_params=_cparams(("parallel",)))(*xs, norm_g, mod, mod, w)
        return p, xs
    ys, tab, lpos, cap = prev
    gs = pltpu.PrefetchScalarGridSpec(
        num_scalar_prefetch=1,
        grid=grid,
        in_specs=[x_spec, pl.BlockSpec(memory_space=pl.ANY), pl.BlockSpec((tm, 1), lambda i, *_: (i, 0)),
                  mod_spec(layer - 1, 5)] + tail_specs,
        out_specs=[p_spec, x_spec],
        scratch_shapes=[h_scratch, pltpu.VMEM((2, _sorted_rows(tm), d), F32), pltpu.SemaphoreType.DMA((2,))])
    return pl.pallas_call(
        functools.partial(_inproj_residual_kernel, tn=tn, cap=cap),
        grid_spec=gs,
        out_shape=[p_shape, jax.ShapeDtypeStruct((n, d), F32)],
        input_output_aliases={1: 1},
        compiler_params=_cparams(("arbitrary",)),
    )(tab, xs, ys, lpos, mod, norm_g, mod, mod, w)


def _ret_kernel(lg_ref, *refs, seq, n_ctx, use_rope, n_plain=None):
    if n_ctx:
        q_ref, k_ref, v_ref, g_ref, kc_ref, vc_ref, cos_ref, sin_ref, o_ref, q_scr, kt_scr, sb_scr = refs
    else:
        q_ref, k_ref, v_ref, g_ref, o_ref, q_scr, kt_scr = refs
    C = RET_CHUNK
    n_chunk = seq // C
    head = pl.program_id(1)
    lgf = lg_ref[0, head]
    lgb = lg_ref[1, head]

    for n in range(n_chunk):
        rows = slice(n * C, (n + 1) * C)
        qn = q_ref[rows, :].astype(F32)
        kn = k_ref[rows, :].astype(F32)
        if use_rope:
            cos = cos_ref[rows, :]
            sin = sin_ref[rows, :]
            qn = qn * cos + pltpu.roll(qn, RET_DK // 2, 1) * sin
            kn = kn * cos + pltpu.roll(kn, RET_DK // 2, 1) * sin
        q_scr[rows, :] = (qn * (RET_DK ** -0.5)).astype(q_scr.dtype)
        kt_scr[:, rows] = kn.T.astype(kt_scr.dtype)

    ri = lax.broadcasted_iota(jnp.int32, (C, C), 0)
    ci = lax.broadcasted_iota(jnp.int32, (C, C), 1)
    rel = (ri - ci).astype(F32)
    dmask = jnp.where(rel > 0.0, jnp.exp(lgf * jnp.maximum(rel, 0.0)),
                      jnp.where(rel < 0.0, jnp.exp(lgb * jnp.maximum(-rel, 0.0)), 2.0))
    icol = lax.broadcasted_iota(jnp.int32, (C, 1), 0).astype(F32)
    jrow = lax.broadcasted_iota(jnp.int32, (1, C), 1).astype(F32)
    qdec_f = jnp.exp(lgf * (icol + 1.0))
    qdec_b = jnp.exp(lgb * (C - icol))
    kdec_f = jnp.exp(lgf * (C - 1.0 - jrow))
    kdec_b = jnp.exp(lgb * jrow)
    zrow = jnp.zeros((1, RET_DV), F32)
    cdec_f = jnp.exp(zrow + lgf * C)
    cdec_b = jnp.exp(zrow + lgb * C)
    if n_plain is not None:
        live = jnp.where(pl.program_id(0) < n_plain, 1.0, 0.0)
        kdec_f = kdec_f * live
        kdec_b = kdec_b * live
    else:
        live = 1.0

    def chunk(ref, n):
        return ref[n * C:(n + 1) * C, :]

    def kt_chunk(n):
        return kt_scr[:, n * C:(n + 1) * C]

    if n_ctx:
        kct = kc_ref[...].astype(F32).T
        vcx = vc_ref[...].astype(MM_DTYPE)
        mrow = lax.broadcasted_iota(jnp.int32, (1, n_ctx), 1).astype(F32)
        s_f = _dot((kct * (jnp.exp(lgf * (n_ctx - 1.0 - mrow)) * live)).astype(MM_DTYPE), vcx)
        s_b = _dot((kct * (jnp.exp(lgb * mrow) * live)).astype(MM_DTYPE), vcx)
        sb_scr[n_chunk - 1] = s_b
        for n in range(n_chunk - 1, 0, -1):
            s_b = s_b * cdec_b + _dot((kt_chunk(n).astype(F32) * kdec_b).astype(MM_DTYPE),
                                      chunk(v_ref, n).astype(MM_DTYPE))
            sb_scr[n - 1] = s_b

    for n0 in range(0, n_chunk, RET_GROUP):
        group = range(n0, min(n0 + RET_GROUP, n_chunk))
        qs = {n: chunk(q_scr, n) for n in group}
        kts = {n: kt_chunk(n) for n in group}
        vs = {n: chunk(v_ref, n).astype(MM_DTYPE) for n in group}
        scores = {n: _dot(qs[n], kts[n]) for n in group}
        if n_ctx:
            incs = {n: _dot((kts[n].astype(F32) * kdec_f).astype(MM_DTYPE), vs[n]) for n in group if n + 1 < n_chunk}
            qcats = {n: jnp.concatenate([qs[n].astype(F32) * qdec_f, qs[n].astype(F32) * qdec_b], axis=1
                                        ).astype(MM_DTYPE) for n in group}
        probs = {n: (scores[n] * dmask).astype(MM_DTYPE) for n in group}
        outs = {n: _dot(probs[n], vs[n]) for n in group}
        if n_ctx:
            for n in group:
                scat = jnp.concatenate([s_f, sb_scr[n]], axis=0).astype(MM_DTYPE)
                outs[n] = outs[n] + _dot(qcats[n], scat)
                if n + 1 < n_chunk:
                    s_f = s_f * cdec_f + incs[n]
        for n in group:
            o = outs[n]
            mu = jnp.mean(o, axis=-1, keepdims=True)
            oc = o - mu
            yn = oc * lax.rsqrt(jnp.mean(oc * oc, axis=-1, keepdims=True) + NORM_EPS)
            gn = chunk(g_ref, n)
            o_ref[n * C:(n + 1) * C, :] = yn.astype(o_ref.dtype) * (gn * jax.nn.sigmoid(gn))


def _retention_latent(p, log_gamma, cos, sin, *, batch, seq, n_ctx, n_lat, n_packed):
    n = p.shape[0]
    cb = n_lat // n_ctx
    kq, kv = OFF_K // RET_DK, OFF_V // RET_DV
    own = lambda b: jnp.minimum(b, batch - 1)
    table = lambda b: jnp.where(b < batch, 0, 1)
    gs = pltpu.PrefetchScalarGridSpec(
        num_scalar_prefetch=1,
        grid=(batch + n_packed, RET_HEADS),
        in_specs=[pl.BlockSpec((seq, RET_DK), lambda b, h, lg: (b, h)),
                  pl.BlockSpec((seq, RET_DK), lambda b, h, lg: (b, kq + h)),
                  pl.BlockSpec((seq, RET_DV), lambda b, h, lg: (b, kv + h)),
                  pl.BlockSpec((seq, RET_DV), lambda b, h, lg: (b, OFF_G // RET_DV + h)),
                  pl.BlockSpec((n_ctx, RET_DK), lambda b, h, lg: (cb + own(b), kq + h)),
                  pl.BlockSpec((n_ctx, RET_DV), lambda b, h, lg: (cb + own(b), kv + h)),
                  pl.BlockSpec((None, seq, RET_DK), lambda b, h, lg: (table(b), 0, 0)),
                  pl.BlockSpec((None, seq, RET_DK), lambda b, h, lg: (table(b), 0, 0))],
        out_specs=pl.BlockSpec((seq, RET_DV), lambda b, h, lg: (b, h)),
        scratch_shapes=[pltpu.VMEM((seq, RET_DK), MM_DTYPE),
                        pltpu.VMEM((RET_DK, seq), MM_DTYPE),
                        pltpu.VMEM((seq // RET_CHUNK, RET_DK, RET_DV), F32)])
    return pl.pallas_call(
        functools.partial(_ret_kernel, seq=seq, n_ctx=n_ctx, use_rope=True, n_plain=batch),
        grid_spec=gs,
        out_shape=jax.ShapeDtypeStruct((n, RET_V), ACT_DTYPE),
        compiler_params=_cparams(("parallel", "parallel")),
    )(log_gamma, p, p, p, p, p, p, cos, sin)


def _retention_ctx(p, log_gamma, y_ret, *, batch, n_ctx, n_lat):
    cb = n_lat // n_ctx
    kq, kv = OFF_K // RET_DK, OFF_V // RET_DV
    gs = pltpu.PrefetchScalarGridSpec(
        num_scalar_prefetch=1,
        grid=(batch, RET_HEADS),
        in_specs=[pl.BlockSpec((n_ctx, RET_DK), lambda b, h, lg: (cb + b, h)),
                  pl.BlockSpec((n_ctx, RET_DK), lambda b, h, lg: (cb + b, kq + h)),
                  pl.BlockSpec((n_ctx, RET_DV), lambda b, h, lg: (cb + b, kv + h)),
                  pl.BlockSpec((n_ctx, RET_DV), lambda b, h, lg: (cb + b, OFF_G // RET_DV + h)),
                  pl.BlockSpec(memory_space=pl.ANY)],
        out_specs=pl.BlockSpec((n_ctx, RET_DV), lambda b, h, lg: (cb + b, h)),
        scratch_shapes=[pltpu.VMEM((n_ctx, RET_DK), MM_DTYPE), pltpu.VMEM((RET_DK, n_ctx), MM_DTYPE)])

    def body(lg_ref, q_ref, k_ref, v_ref, g_ref, alias_ref, o_ref, q_scr, kt_scr):
        del alias_ref
        _ret_kernel(lg_ref, q_ref, k_ref, v_ref, g_ref, o_ref, q_scr, kt_scr, seq=n_ctx, n_ctx=0, use_rope=False)

    return pl.pallas_call(
        body,
        grid_spec=gs,
        out_shape=jax.ShapeDtypeStruct(y_ret.shape, y_ret.dtype),
        input_output_aliases={5: 0},
        compiler_params=_cparams(("parallel", "parallel")),
    )(log_gamma, p, p, p, p, y_ret)


def _convpool_kernel(*refs, seq, aliased):
    if aliased:
        cb_ref, cc_ref, cx_ref, pi_ref, cw_ref, pw_ref, ps_ref, _, _, yc_ref, yp_ref = refs
    else:
        cb_ref, cc_ref, cx_ref, pi_ref, cw_ref, pw_ref, ps_ref, yc_ref, yp_ref = refs
    grp = pl.program_id(1)
    pb, hb = POOL_BLOCK, POOL_HALO
    n_blk = seq // pb

    w = cw_ref[...]
    tl = lax.broadcasted_iota(jnp.int32, (pb, 1), 0)
    cxu = lambda a, b: cc_ref[a:b, :].astype(F32) * cx_ref[a:b, :].astype(F32)
    for b0 in range(0, n_blk, CONV_GROUP):
        group = range(b0, min(b0 + CONV_GROUP, n_blk))
        us = {blk: cxu(blk * pb, (blk + 1) * pb) for blk in group}
        before = {blk: cxu(blk * pb - hb, blk * pb)[hb - 1:hb, :] if blk > 0 else 0.0 for blk in group}
        after = {blk: cxu((blk + 1) * pb, (blk + 1) * pb + hb)[0:1, :] if blk + 1 < n_blk else 0.0 for blk in group}
        prevs = {blk: jnp.where(tl == 0, before[blk], pltpu.roll(us[blk], 1, 0)) for blk in group}
        nexts = {blk: jnp.where(tl == pb - 1, after[blk], pltpu.roll(us[blk], pb - 1, 0)) for blk in group}
        for blk in group:
            conv = w[0:1, :] * prevs[blk] + w[1:2, :] * us[blk] + w[2:3, :] * nexts[blk]
            yc_ref[blk * pb:(blk + 1) * pb, :] = (cb_ref[blk * pb:(blk + 1) * pb, :].astype(F32) * conv
                                                  ).astype(yc_ref.dtype)

    half = jnp.left_shift(1, grp)

    def band(rows, cols, shift):
        dd = (lax.broadcasted_iota(jnp.int32, (rows, cols), 1) + shift
              - lax.broadcasted_iota(jnp.int32, (rows, cols), 0))
        return jnp.where((dd >= -half) & (dd < half), 1.0, 0.0).astype(MM_DTYPE)

    band_self = band(pb, pb, 0)
    band_prev = band(hb, hb, -hb)
    band_next = band(hb, hb, hb)
    blocks = range(n_blk)
    p_blks = [pi_ref[blk * pb:(blk + 1) * pb, :] for blk in blocks]
    wsums = [_dot(band_self, p_blk.astype(MM_DTYPE)) for p_blk in p_blks]
    pooled = []
    for blk in blocks:
        r0 = blk * pb
        top, mid, bot = wsums[blk][:hb], wsums[blk][hb:pb - hb], wsums[blk][pb - hb:]
        if blk > 0:
            top = top + _dot(band_prev, pi_ref[r0 - hb:r0, :].astype(MM_DTYPE))
        if blk + 1 < n_blk:
            bot = bot + _dot(band_next, pi_ref[r0 + pb:r0 + pb + hb, :].astype(MM_DTYPE))
        wsum = jnp.concatenate([top, mid, bot], axis=0)
        if 0 < blk < n_blk - 1:
            cnt = (2 * half).astype(F32)
        else:
            tb = r0 + lax.broadcasted_iota(jnp.int32, (pb, 1), 0)
            cnt = (jnp.clip(tb + half, 0, seq) - jnp.clip(tb - half, 0, seq)).astype(F32)
        pooled.append((wsum / cnt - p_blks[blk].astype(F32)).astype(MM_DTYPE))
    mixed = [_dot(pooled[blk], pw_ref[0]) for blk in blocks]
    for blk in blocks:
        yp_ref[blk * pb:(blk + 1) * pb, :] = (mixed[blk] * ps_ref[...]).astype(yp_ref.dtype)


def _convpool(p, conv_w, pool_w, pool_scale, prev, *, layer, n_seq, seq, row0):
    n = p.shape[0]
    g128 = lambda off: off // POOL_GDIM
    col = lambda off: (lambda b, g: (row0 + b, g128(off) + g))
    in_specs = [pl.BlockSpec((seq, POOL_GDIM), col(OFF_CB)),
                pl.BlockSpec((seq, POOL_GDIM), col(OFF_CC)),
                pl.BlockSpec((seq, POOL_GDIM), col(OFF_CX)),
                pl.BlockSpec((seq, POOL_GDIM), col(OFF_PI)),
                _layer_spec(layer, (conv_w.shape[1], POOL_GDIM), lambda b, g: (0, g)),
                _layer_spec(layer, (1, POOL_GDIM, POOL_GDIM), lambda b, g: (g, 0, 0)),
                _layer_spec(layer, (1, POOL_GDIM), lambda b, g: (0, g))]
    args = [p, p, p, p, conv_w, pool_w, pool_scale]
    aliases = {}
    if prev is not None:
        in_specs += [pl.BlockSpec(memory_space=pl.ANY), pl.BlockSpec(memory_space=pl.ANY)]
        args += list(prev)
        aliases = {7: 0, 8: 1}
    out_spec = pl.BlockSpec((seq, POOL_GDIM), lambda b, g: (row0 + b, g))
    return pl.pallas_call(
        functools.partial(_convpool_kernel, seq=seq, aliased=prev is not None),
        grid=(n_seq, POOL_GROUPS),
        in_specs=in_specs,
        out_specs=[out_spec, out_spec],
        out_shape=[jax.ShapeDtypeStruct((n, CONV_W), ACT_DTYPE), jax.ShapeDtypeStruct((n, POOL_W), ACT_DTYPE)],
        input_output_aliases=aliases,
        compiler_params=_cparams(("parallel", "parallel")),
    )(*args)


def _first_max(vals):
    top = functools.reduce(jnp.maximum, vals)
    idx = jnp.full(top.shape, len(vals) - 1, jnp.int32)
    for k in range(len(vals) - 2, -1, -1):
        idx = jnp.where(vals[k] == top, k, idx)
    return top, idx


def _route(logits_t):
    row = lambda k: logits_t[k:k + 1, :]
    groups = [row(g) for g in range(N_GROUPS)]
    gmax, gidx = _first_max(groups)
    gtop = 1.0 / sum(jnp.exp(g - gmax) for g in groups)
    experts = []
    for k in range(EXP_PER_GROUP):
        e = row(N_GROUPS + (N_GROUPS - 1) * EXP_PER_GROUP + k)
        for g in range(N_GROUPS - 2, -1, -1):
            e = jnp.where(gidx == g, row(N_GROUPS + g * EXP_PER_GROUP + k), e)
        experts.append(e)
    m1, i1 = _first_max(experts)
    m2, i2 = _first_max([jnp.where(i1 == k, -jnp.inf, e) for k, e in enumerate(experts)])
    e2 = jnp.exp(m2 - m1)
    w1 = gtop / (1.0 + e2)
    w2 = gtop * e2 / (1.0 + e2)
    first_lower = i1 < i2
    lo = jnp.minimum(i1, i2)
    hi = jnp.maximum(i1, i2)
    pair = lo * 3 - jnp.right_shift(lo * (lo - 1), 1) + hi - lo - 1
    bucket = gidx * N_PAIRS + pair
    wlo = jnp.where(first_lower, w1, w2)
    whi = jnp.where(first_lower, w2, w1)
    return bucket, wlo, whi


def _column_to_lanes(col):
    return jnp.broadcast_to(col, (col.shape[0], LANES)).T[0:1, :]


def _row_to_column(row):
    return jnp.broadcast_to(row, (8, row.shape[1])).T[:, 0:1]


def _dot_nt(a, b):
    return lax.dot_general(a, b, (((1,), (1,)), ((), ())), preferred_element_type=F32)


def _bucket_run_copies(read_run, max_rows, make_copy):
    def piece(n, local, glob, size):
        @pl.when((n & size) != 0)
        def _():
            off = n & (-2 * size)
            make_copy(pl.multiple_of(local + off, ROW_ALIGN), pl.multiple_of(glob + off, ROW_ALIGN), size).start()

    top = 1 << (max_rows.bit_length() - 1)
    sizes = [top >> k for k in range(top.bit_length()) if top >> k >= ROW_ALIGN]
    rare = [s for s in sizes if s >= RUN_RARE_ROWS]

    def per_bucket(b, carry_):
        n, local, glob = read_run(b)

        @pl.when(n >= RUN_RARE_ROWS)
        def _():
            for size in rare:
                piece(n, local, glob, size)

        for size in sizes[len(rare):]:
            piece(n, local, glob, size)
        return carry_

    lax.fori_loop(0, N_BUCKETS, per_bucket, 0)


def _wait_rows(total, max_rows, make_copy):
    size = 1 << (max_rows.bit_length() - 1)
    while size >= ROW_ALIGN:
        @pl.when((total & size) != 0)
        def _(size=size):
            make_copy(size).wait()
        size //= 2


def _exact_bf16_pieces(w):
    a = w.astype(jnp.bfloat16).astype(F32)
    b = (w - a).astype(jnp.bfloat16).astype(F32)
    c = (w - a - b).astype(jnp.bfloat16).astype(F32)
    return a, b, c


def _outproj_kernel(*refs, cap, lat_tiles):
    if lat_tiles is None:
        x_ref, *refs = refs
        read_x = lambda rows: x_ref[rows, :]
    else:
        x_ref, xc_ref, *refs = refs
        read_x = functools.partial(_pick_stream, lat_tiles, x_ref, xc_ref)
    (yr_ref, yc_ref, yp_ref, g0_ref, g1_ref, g2_ref, gt_ref, sc_ref, sh_ref, ng_ref,
     wr_ref, wc_ref, wp_ref, wo_ref, wrt_ref, brt_ref,
     xo_ref, hs_ref, cnt_ref, tab_ref, lpos_ref,
     h_scr, tab_vmem, tab_smem, sent_smem, carry, row_sem, tab_sem) = refs
    i = pl.program_id(0)
    n_steps = pl.num_programs(0)
    slot = lax.rem(i, 2)
    tm, d = x_ref.shape
    ts = h_scr.shape[1]

    def wait_rows(s):
        _wait_rows(sent_smem[s], ts, lambda size: pltpu.make_async_copy(
            h_scr.at[s, pl.ds(0, size)], hs_ref.at[pl.ds(0, size)], row_sem.at[s]))

    @pl.when(i == 0)
    def _():
        carry[...] = jnp.zeros_like(carry)

    @pl.when(i >= 2)
    def _():
        wait_rows(slot)

    sub = tm // OUT_SUBBLOCKS
    blocks = [slice(a, a + sub) for a in range(0, tm, sub)]
    gate = lambda r, rows: jax.nn.sigmoid(r[rows, :].astype(F32))
    merged = [(gate(g0_ref, rows) * _dot(yr_ref[rows, :], wr_ref[...])
               + gate(g1_ref, rows) * _dot(yc_ref[rows, :], wc_ref[...])
               + gate(g2_ref, rows) * _dot(yp_ref[rows, :], wp_ref[...])).astype(MM_DTYPE) for rows in blocks]
    ys = [_dot(m, wo_ref[...]) for m in merged]
    xs_new = [read_x(rows) + gt_ref[0] * y for rows, y in zip(blocks, ys)]
    for rows, x in zip(blocks, xs_new):
        xo_ref[rows, :] = x
    hs = [_rms_mod(x, ng_ref[...], sc_ref[0], sh_ref[0]) for x in xs_new]
    splits = [_split_bf16(hb) for hb in hs]
    s2s = [_dot_nt(wrt_ref[...], hh) + _dot_nt(wrt_ref[...], hl) for hh, hl in splits]
    routes = [_route(s2[0:ROUTER_LO_ROW, :] + s2[ROUTER_LO_ROW:2 * ROUTER_LO_ROW, :] + brt_ref[...]) for s2 in s2s]
    h = jnp.concatenate([hb.astype(MM_DTYPE) for hb in hs], axis=0)
    bucket, wlo, whi = (jnp.concatenate([route[k] for route in routes], axis=1) for k in (0, 1, 2))

    brow = lax.broadcasted_iota(jnp.int32, (LANES, tm), 0)
    onehot = brow == bucket
    ones = jnp.where(onehot, 1.0, 0.0)
    r = lax.broadcasted_iota(jnp.int32, (tm, tm), 0)
    c = lax.broadcasted_iota(jnp.int32, (tm, tm), 1)
    tri = jnp.where(r <= c, 1.0, 0.0).astype(jnp.bfloat16)
    incl = _dot(ones.astype(jnp.bfloat16), tri)
    cnt_col = incl[:, tm - 1:tm].astype(jnp.int32)
    units_col = jnp.right_shift(cnt_col + (ROW_ALIGN - 1), ALIGN_SHIFT)
    below = jnp.where(lax.broadcasted_iota(jnp.int32, (LANES, LANES), 0)
                      > lax.broadcasted_iota(jnp.int32, (LANES, LANES), 1), 1.0, 0.0).astype(jnp.bfloat16)
    units_b = jnp.broadcast_to(units_col.astype(F32), (LANES, LANES)).astype(jnp.bfloat16)
    start_col = _dot(below, units_b)[:, 0:1] * float(ROW_ALIGN)
    lpos = jnp.sum(jnp.where(onehot, incl - 1.0 + start_col, 0.0), axis=0, keepdims=True)
    lpos_ref[...] = _row_to_column(lpos).astype(jnp.int32)
    run_col = (units_col * ROW_ALIGN).astype(F32)
    run_len = _column_to_lanes(run_col).astype(jnp.int32)
    local_start = _column_to_lanes(start_col).astype(jnp.int32)
    total = (start_col + run_col)[LANES - 1:LANES, :].astype(jnp.int32)
    srow = lax.broadcasted_iota(jnp.int32, (8, LANES), 0)
    tab = jnp.where(srow == TAB_COUNT, run_len,
                    jnp.where(srow == TAB_LOCAL, local_start,
                              jnp.where(srow == TAB_GLOBAL, carry[...], jnp.where(srow == TAB_TOTAL, total, 0))))
    carry[...] = carry[...] + run_len
    tab_ref[0] = tab
    tab_vmem[...] = tab
    to_smem = pltpu.make_async_copy(tab_vmem, tab_smem, tab_sem)
    to_smem.start()

    srt = lax.broadcasted_iota(jnp.int32, (ts, tm), 0)
    perm = jnp.where(srt == lpos.astype(jnp.int32), 1.0, 0.0).astype(MM_DTYPE)
    pieces = _exact_bf16_pieces(wlo) + _exact_bf16_pieces(whi)
    meta_t = jnp.zeros((LANES, tm), F32)
    for k, piece in enumerate(pieces):
        meta_t = jnp.where(brow == k, piece, meta_t)
    h_scr[slot, :, :d] = _dot(perm, h)
    h_scr[slot, :, d:] = _dot_nt(perm, meta_t.astype(MM_DTYPE))
    to_smem.wait()
    sent_smem[slot] = tab_smem[TAB_TOTAL, 0]

    _bucket_run_copies(
        lambda b: (tab_smem[TAB_COUNT, b], tab_smem[TAB_LOCAL, b], b * cap + tab_smem[TAB_GLOBAL, b]), tm,
        lambda local, glob, size: pltpu.make_async_copy(h_scr.at[slot, pl.ds(local, size)],
                                                        hs_ref.at[pl.ds(glob, size)], row_sem.at[slot]))

    @pl.when(i == n_steps - 1)
    def _():
        cnt_ref[...] = jnp.broadcast_to(carry[...], cnt_ref.shape)
        wait_rows(slot)

    @pl.when((i == n_steps - 1) & (i >= 1))
    def _():
        wait_rows(1 - slot)


def _sorted_rows(tm):
    return tm + N_BUCKETS * ROW_ALIGN


def _outproj(xs, p, y_ret, y_conv, y_pool, mod, norm_g, w_ret, w_conv, w_pool, w_o, w_router, b_router,
             *, layer, tm, cap, n_rows, n_lat, seq, batch):
    split = isinstance(xs, tuple)
    n, d = (xs[0].shape[0] + xs[1].shape[0], xs[0].shape[1]) if split else xs.shape
    lat_tiles = n_lat // tm if split else None
    x_specs = _stream_specs(tm, d, lat_tiles) if split else [pl.BlockSpec((tm, d), lambda i: (i, 0))]
    x_args = list(xs) if split else [xs]
    assert cap >= n_rows + (n_rows // tm) * ROW_ALIGN and cap % ROW_ALIGN == 0
    width = d + LANES
    ts = _sorted_rows(tm)
    row = functools.partial(_mod_row, tile=tm, n_lat=n_lat, seq=seq, batch=batch)
    gate = lambda k: pl.BlockSpec((tm, d), lambda i: (i, OFF_GATE // d + k))
    mod_spec = lambda k: _layer_spec(layer, (1, 1, d), lambda i: (row(i), 0, k))
    full = lambda a: _layer_spec(layer, a.shape[1:], lambda i: (0,) * (a.ndim - 1))
    return pl.pallas_call(
        functools.partial(_outproj_kernel, cap=cap, lat_tiles=lat_tiles),
        grid=(n_rows // tm,),
        in_specs=x_specs + [
                  pl.BlockSpec((tm, RET_V), lambda i: (i, 0)),
                  pl.BlockSpec((tm, CONV_W), lambda i: (i, 0)),
                  pl.BlockSpec((tm, POOL_W), lambda i: (i, 0)),
                  gate(0), gate(1), gate(2),
                  mod_spec(2), mod_spec(4), mod_spec(3),
                  full(norm_g), full(w_ret), full(w_conv), full(w_pool), full(w_o), full(w_router), full(b_router)],
        out_specs=[pl.BlockSpec((tm, d), lambda i: (i, 0)),
                   pl.BlockSpec(memory_space=pl.ANY),
                   pl.BlockSpec((8, LANES), lambda i: (0, 0)),
                   pl.BlockSpec((1, 8, LANES), lambda i: (i, 0, 0)),
                   pl.BlockSpec((tm, 1), lambda i: (i, 0))],
        out_shape=[jax.ShapeDtypeStruct((n, d), F32),
                   jax.ShapeDtypeStruct((N_BUCKETS * cap, width), F32),
                   jax.ShapeDtypeStruct((8, LANES), jnp.int32),
                   jax.ShapeDtypeStruct((n_rows // tm, 8, LANES), jnp.int32),
                   jax.ShapeDtypeStruct((n_rows, 1), jnp.int32)],
        scratch_shapes=[pltpu.VMEM((2, ts, width), F32),
                        pltpu.VMEM((8, LANES), jnp.int32),
                        pltpu.SMEM((8, LANES), jnp.int32),
                        pltpu.SMEM((2,), jnp.int32),
                        pltpu.VMEM((1, LANES), jnp.int32),
                        pltpu.SemaphoreType.DMA((2,)),
                        pltpu.SemaphoreType.DMA(())],
        input_output_aliases={} if split else {0: 0},
        compiler_params=_cparams(("arbitrary",)),
    )(*x_args, y_ret, y_conv, y_pool, p, p, p, mod, mod, mod, norm_g, w_ret, w_conv, w_pool, w_o, w_router, b_router)


def _moe_kernel(blk_ref, elo_ref, ehi_ref, valid_ref, hs_ref, w1l_ref, w1h_ref, w3l_ref, w3h_ref, w2l_ref, w2h_ref,
                ys_ref):
    del blk_ref, elo_ref, ehi_ref
    valid = valid_ref[pl.program_id(0)]
    tm, d = ys_ref.shape

    @pl.when(valid > 0)
    def _():
        keep = lax.broadcasted_iota(jnp.int32, (tm, 1), 0) < valid
        meta = jnp.where(keep, hs_ref[:, d:], 0.0)
        h = jnp.where(keep, hs_ref[:, :d], 0.0).astype(MM_DTYPE)
        wlo = meta[:, 0:1] + meta[:, 1:2] + meta[:, 2:3]
        whi = meta[:, 3:4] + meta[:, 4:5] + meta[:, 5:6]

        ups = [(_dot(h, w1[0]), _dot(h, w3[0])) for w1, w3 in ((w1l_ref, w3l_ref), (w1h_ref, w3h_ref))]
        acts = [(a * jax.nn.sigmoid(a) * b).astype(MM_DTYPE) for a, b in ups]
        y_lo, y_hi = (_dot(act, w2[0]) for act, w2 in zip(acts, (w2l_ref, w2h_ref)))
        y = wlo * y_lo + whi * y_hi
        ys_ref[...] = y.astype(ACT_DTYPE).astype(F32)


def _moe(blk, elo, ehi, valid, hs, w1, w3, w2, *, layer, tm, d):
    n_sorted, width = hs.shape
    n_work = blk.shape[0]
    up = lambda sel: _layer_spec(layer, (1, d, D_FF), lambda s, blk, elo, ehi, valid: ((elo, ehi)[sel][s], 0, 0))
    down = lambda sel: _layer_spec(layer, (1, D_FF, d), lambda s, blk, elo, ehi, valid: ((elo, ehi)[sel][s], 0, 0))
    gs = pltpu.PrefetchScalarGridSpec(
        num_scalar_prefetch=4,
        grid=(n_work,),
        in_specs=[pl.BlockSpec((tm, width), lambda s, blk, elo, ehi, valid: (blk[s], 0)),
                  up(0), up(1), up(0), up(1), down(0), down(1)],
        out_specs=pl.BlockSpec((tm, d), lambda s, blk, elo, ehi, valid: (blk[s], 0)))
    return pl.pallas_call(
        _moe_kernel,
        grid_spec=gs,
        out_shape=jax.ShapeDtypeStruct((n_sorted, d), F32),
        compiler_params=_cparams(("arbitrary",)),
    )(blk, elo, ehi, valid, hs, w1, w1, w3, w3, w2, w2)


def _expert_residual(tab_ref, ys_ref, lpos_ref, stage, sems, *, tile_rows, cap):
    i = pl.program_id(0)
    slot = lax.rem(i, 2)
    ts = stage.shape[1]
    entry = lambda tile, row, lane: tab_ref[tile * (TAB_ROWS * LANES) + row * LANES + lane]

    def start(tile, s):
        _bucket_run_copies(
            lambda b: (entry(tile, TAB_COUNT, b), entry(tile, TAB_LOCAL, b), b * cap + entry(tile, TAB_GLOBAL, b)),
            tile_rows,
            lambda local, glob, size: pltpu.make_async_copy(
                ys_ref.at[pl.ds(glob, size)], stage.at[s, pl.ds(local, size)], sems.at[s]))

    @pl.when(i == 0)
    def _():
        start(i, slot)

    @pl.when(i + 1 < pl.num_programs(0))
    def _():
        start(i + 1, 1 - slot)

    total = entry(i, TAB_TOTAL, 0)
    _wait_rows(total, ts, lambda size: pltpu.make_async_copy(
        ys_ref.at[pl.ds(0, size)], stage.at[slot, pl.ds(0, size)], sems.at[slot]))
    lpos = lpos_ref[...]
    unperm = jnp.where(lpos == lax.broadcasted_iota(jnp.int32, (tile_rows, ts), 1), 1.0, 0.0).astype(MM_DTYPE)
    filled = lax.broadcasted_iota(jnp.int32, (ts, 1), 0) < total
    return _dot(unperm, jnp.where(filled, stage[slot], 0.0).astype(MM_DTYPE))


def _final_kernel(tab_ref, x_ref, ys_ref, lpos_ref, gt_ref, g_ref, o_ref, stage, sems, *, cap):
    f = _expert_residual(tab_ref, ys_ref, lpos_ref, stage, sems, tile_rows=x_ref.shape[0], cap=cap)
    x = x_ref[...] + gt_ref[0] * f
    o_ref[...] = x * lax.rsqrt(jnp.mean(x * x, axis=-1, keepdims=True) + NORM_EPS) * g_ref[...]


def _final(xs, moe_out, mod, final_g, *, layer, tm, n_rows, n_lat, seq, batch):
    ys, tab, lpos, cap = moe_out
    d = xs.shape[1]
    row = functools.partial(_mod_row, tile=tm, n_lat=n_lat, seq=seq, batch=batch)
    x_spec = pl.BlockSpec((tm, d), lambda i, tab: (i, 0))
    gs = pltpu.PrefetchScalarGridSpec(
        num_scalar_prefetch=1,
        grid=(n_rows // tm,),
        in_specs=[x_spec,
                  pl.BlockSpec(memory_space=pl.ANY),
                  pl.BlockSpec((tm, 1), lambda i, tab: (i, 0)),
                  _layer_spec(layer, (1, 1, d), lambda i, tab: (row(i), 0, 5)),
                  pl.BlockSpec((1, d), lambda i, tab: (0, 0))],
        out_specs=x_spec,
        scratch_shapes=[pltpu.VMEM((2, _sorted_rows(tm), d), F32), pltpu.SemaphoreType.DMA((2,))])
    return pl.pallas_call(
        functools.partial(_final_kernel, cap=cap),
        grid_spec=gs,
        out_shape=jax.ShapeDtypeStruct((n_rows, d), F32),
        compiler_params=_cparams(("arbitrary",)),
    )(tab, xs, ys, lpos, mod, final_g)


def _rope_tables(seq):
    rows = seq // GRID_W
    row = jnp.repeat(jnp.arange(rows, dtype=F32), GRID_W)
    col = jnp.tile(jnp.arange(GRID_W, dtype=F32), rows)
    n_freq = RET_DK // 4
    inv_freq = ROPE_BASE ** (-jnp.arange(n_freq, dtype=F32) / n_freq)
    ang = jnp.concatenate([row[:, None] * inv_freq[None, :], col[:, None] * inv_freq[None, :]], axis=-1)
    cos, sin = jnp.cos(ang), jnp.sin(ang)
    cos2, sin2 = jnp.concatenate([cos, cos], axis=-1), jnp.concatenate([-sin, sin], axis=-1)
    return jnp.stack([cos2, jnp.ones_like(cos2)]), jnp.stack([sin2, jnp.zeros_like(sin2)])


def _work_tables(counts, *, tm, cap, n_work):
    cnt = counts[0, :N_BUCKETS]
    tiles = (cnt + tm - 1) // tm
    ends = jnp.cumsum(tiles)
    starts = ends - tiles
    item = jnp.arange(n_work, dtype=jnp.int32)
    used = item < ends[-1]
    ref_item = jnp.minimum(item, jnp.maximum(ends[-1] - 1, 0))
    member = ((ref_item[:, None] >= starts[None, :]) & (ref_item[:, None] < ends[None, :])).astype(jnp.int32)
    pick = lambda per_bucket: jnp.sum(member * per_bucket[None, :], axis=1)
    buckets = np.arange(N_BUCKETS)
    j = ref_item - pick(starts)
    valid = jnp.where(used, jnp.clip(pick(cnt) - j * tm, 0, tm), 0)
    blk = pick(jnp.asarray(buckets * (cap // tm), jnp.int32)) + j
    first = (buckets // N_PAIRS) * EXP_PER_GROUP
    elo = pick(jnp.asarray(first + np.asarray(PAIR_LO)[buckets % N_PAIRS], jnp.int32))
    ehi = pick(jnp.asarray(first + np.asarray(PAIR_HI)[buckets % N_PAIRS], jnp.int32))
    i32 = lambda a: a.astype(jnp.int32)
    return i32(blk), i32(elo), i32(ehi), i32(valid)


def _router_weights(w_rg, b_rg, w_re, b_re):
    w = jnp.swapaxes(jnp.concatenate([w_rg, w_re], axis=-1).astype(F32), 1, 2)
    depth, n_out, d = w.shape
    assert n_out <= ROUTER_LO_ROW and 2 * ROUTER_LO_ROW <= LANES and EXP_PER_GROUP == 4
    hi = w.astype(jnp.bfloat16)
    lo = (w - hi.astype(F32)).astype(jnp.bfloat16)
    packed = jnp.zeros((depth, LANES, d), jnp.bfloat16)
    packed = packed.at[:, :n_out].set(hi).at[:, ROUTER_LO_ROW:ROUTER_LO_ROW + n_out].set(lo)
    bias = jnp.zeros((depth, ROUTER_LO_ROW, 1), F32)
    bias = bias.at[:, :n_out, 0].set(jnp.concatenate([b_rg, b_re], axis=-1).astype(F32))
    return packed, bias


def _pick_tile(n, want, *also):
    t = want
    while n % t or any(a % t for a in also):
        t //= 2
    return t


def kernel(x, c, ctx, c_ctx, w_ada, b_ada, norm1, norm2, w_in, ret_decay, conv_w, pool_w, pool_scale, w_ret_out,
           w_conv_out, w_pool_out, w_o, w_rg, b_rg, w_re, b_re, w1, w3, w2, final_norm):
    batch, seq, d = x.shape
    n_ctx = ctx.shape[1]
    depth = w_ada.shape[0]
    n_lat, n_c = batch * seq, batch * n_ctx
    n = n_lat + n_c
    assert POOL_WINDOWS == (2, 4, 8, 16) and POOL_HALO >= max(POOL_WINDOWS) // 2
    assert seq % RET_CHUNK == 0 and n_ctx == RET_CHUNK and seq % GRID_W == 0 and n_lat % n_ctx == 0

    tm_out = _pick_tile(seq, 512, n_c)
    tm_moe = 512

    xs = (x.reshape(n_lat, d), ctx.reshape(n_c, d))
    mod_rows = -(-(batch + 1) // 8) * 8
    cond = jnp.zeros((mod_rows, d), F32).at[:batch].set(c).at[batch].set(c_ctx)
    mod = _ada(cond, w_ada, b_ada).reshape(depth, mod_rows, 1, N_MOD * d)
    cos, sin = _rope_tables(seq)
    log_gamma = jax.nn.log_sigmoid(ret_decay.astype(F32))

    mm = lambda a: a.astype(MM_DTYPE)
    norm1_s, norm2_s, pool_scale_s = norm1[:, None, :], norm2[:, None, :], pool_scale[:, None, :]
    w_in_b, pool_w_b = mm(w_in), mm(pool_w)
    w_ret_b, w_conv_b, w_pool_b, w_o_b = mm(w_ret_out), mm(w_conv_out), mm(w_pool_out), mm(w_o)
    w1_b, w3_b, w2_b = mm(w1), mm(w3), mm(w2)
    w_router, b_router = _router_weights(w_rg, b_rg, w_re, b_re)
    dims = dict(n_lat=n_lat, seq=seq, batch=batch)
    prev = None
    for l in range(depth):
        last = l == depth - 1
        rows = n_lat if last else n
        p, xs = _inproj(xs, prev, norm1_s, mod, w_in_b, layer=l, tm=tm_out, tn=1024, **dims)
        pack_ctx = not last and n_c % seq == 0
        y_ret = _retention_latent(p, log_gamma[l], cos, sin, batch=batch, seq=seq, n_ctx=n_ctx, n_lat=n_lat,
                                  n_packed=n_c // seq if pack_ctx else 0)
        y_conv, y_pool = _convpool(p, conv_w, pool_w_b, pool_scale_s, None, layer=l, n_seq=batch, seq=seq, row0=0)
        if not last and not pack_ctx:
            y_ret = _retention_ctx(p, log_gamma[l], y_ret, batch=batch, n_ctx=n_ctx, n_lat=n_lat)
        if not last:
            y_conv, y_pool = _convpool(p, conv_w, pool_w_b, pool_scale_s, (y_conv, y_pool), layer=l,
                                       n_seq=batch, seq=n_ctx, row0=n_lat // n_ctx)
        n_tiles = rows // tm_out
        cap = -(-(rows + n_tiles * ROW_ALIGN) // tm_moe) * tm_moe
        max_sorted = rows + n_tiles * N_BUCKETS * (ROW_ALIGN - 1)
        xs, hs, counts, tab, lpos = _outproj(xs, p, y_ret, y_conv, y_pool, mod, norm2_s, w_ret_b, w_conv_b, w_pool_b,
                                             w_o_b, w_router, b_router, layer=l, tm=tm_out, cap=cap, n_rows=rows,
                                             **dims)
        blk, elo, ehi, valid = _work_tables(counts, tm=tm_moe, cap=cap, n_work=-(-max_sorted // tm_moe) + N_BUCKETS)
        ys = _moe(blk, elo, ehi, valid, hs, w1_b, w3_b, w2_b, layer=l, tm=tm_moe, d=d)
        prev = (ys, tab[:, :TAB_ROWS, :].reshape(-1), lpos, cap)
    out = _final(xs, prev, mod, final_norm[None], layer=depth - 1, tm=tm_out, n_rows=n_lat, **dims)
    return out.reshape(batch, seq, d)
```

```python
import functools

import numpy as np
import jax
import jax.numpy as jnp
from jax import lax
from jax.experimental import pallas as pl
from jax.experimental.pallas import tpu as pltpu

F32 = jnp.float32
MM_DTYPE = jnp.bfloat16
ACT_DTYPE = jnp.bfloat16

NORM_EPS = 1e-6
GRID_W = 64
ROPE_BASE = 10000.0
N_MOD = 6

RET_HEADS = 4
RET_DK = 128
RET_DV = 256
RET_QK = RET_HEADS * RET_DK
RET_V = RET_HEADS * RET_DV
RET_CHUNK = 256
RET_GROUP = 2

CONV_W = 512
CONV_GROUP = 2
POOL_WINDOWS = (2, 4, 8, 16)
POOL_GROUPS = 4
POOL_GDIM = 128
POOL_W = POOL_GROUPS * POOL_GDIM
POOL_BLOCK = 256
POOL_HALO = 16

N_GROUPS = 4
EXP_PER_GROUP = 4
N_EXPERTS = N_GROUPS * EXP_PER_GROUP
D_FF = 512
PAIR_LO = (0, 0, 0, 1, 1, 2)
PAIR_HI = (1, 2, 3, 2, 3, 3)
N_PAIRS = len(PAIR_LO)
N_BUCKETS = N_GROUPS * N_PAIRS

LANES = 128
OUT_SUBBLOCKS = 4
ROW_ALIGN = 8
ALIGN_SHIFT = 3
RUN_RARE_ROWS = 64
TAB_COUNT, TAB_LOCAL, TAB_GLOBAL, TAB_TOTAL, TAB_ROWS = 0, 1, 2, 3, 4
ROUTER_LO_ROW = 32

OFF_Q = 0
OFF_K = OFF_Q + RET_QK
OFF_V = OFF_K + RET_QK
OFF_G = OFF_V + RET_V
OFF_CB = OFF_G + RET_V
OFF_CC = OFF_CB + CONV_W
OFF_CX = OFF_CC + CONV_W
OFF_PI = OFF_CX + CONV_W
OFF_GATE = OFF_PI + POOL_W

VMEM_LIMIT = 56 * 1024 * 1024


def _cparams(sem):
    return pltpu.CompilerParams(dimension_semantics=sem, vmem_limit_bytes=VMEM_LIMIT)


def _split_bf16(a):
    hi = a.astype(jnp.bfloat16)
    lo = (a - hi.astype(F32)).astype(jnp.bfloat16)
    return hi, lo


def _dot(a, b):
    return jnp.dot(a, b, preferred_element_type=F32)


def _dot3(a, b):
    ah, al = _split_bf16(a)
    bh, bl = _split_bf16(b)
    return _dot(ah, bh) + _dot(ah, bl) + _dot(al, bh)


def _layer_spec(layer, block, index, **kw):
    return pl.BlockSpec((None,) + tuple(block), lambda *a: (layer,) + tuple(index(*a)), **kw)


def _mod_row(i, tile, n_lat, seq, batch):
    return jnp.where(i < n_lat // tile, (i * tile) // seq, batch)


def _ada_kernel(c_ref, w_ref, b_ref, o_ref):
    cv = c_ref[...]
    s = cv * jax.nn.sigmoid(cv)
    o_ref[0] = _dot3(s, w_ref[0]) + b_ref[0]


def _ada(cond, w_ada, b_ada):
    depth, d, width = w_ada.shape
    rows = cond.shape[0]
    tn = 512
    return pl.pallas_call(
        _ada_kernel,
        grid=(depth, width // tn),
        in_specs=[pl.BlockSpec((rows, d), lambda l, j: (0, 0)),
                  pl.BlockSpec((1, d, tn), lambda l, j: (l, 0, j)),
                  pl.BlockSpec((1, 1, tn), lambda l, j: (l, 0, j))],
        out_specs=pl.BlockSpec((1, rows, tn), lambda l, j: (l, 0, j)),
        out_shape=jax.ShapeDtypeStruct((depth, rows, width), F32),
        compiler_params=_cparams(("parallel", "parallel")),
    )(cond, w_ada, b_ada.reshape(depth, 1, width))


def _rms_mod(x, g, sc, sh):
    ms = jnp.mean(x * x, axis=-1, keepdims=True)
    return x * lax.rsqrt(ms + NORM_EPS) * g * (1.0 + sc) + sh


def _project_columns(h_scr, w_ref, o_ref, tn, full_tiles=None, part_cols=0):
    for j in range(w_ref.shape[1] // tn):
        cols = slice(j * tn, (j + 1) * tn)

        def chunk(cols=cols):
            o_ref[:, cols] = _dot(h_scr[...], w_ref[:, cols]).astype(o_ref.dtype)

        if full_tiles is None or (j + 1) * tn <= part_cols:
            chunk()
        else:
            pl.when(pl.program_id(0) < full_tiles)(chunk)


def _pick_stream(lat_tiles, lat_ref, ctx_ref, rows=slice(None)):
    return jnp.where(pl.program_id(0) < lat_tiles, lat_ref[rows, :], ctx_ref[rows, :])


def _inproj_kernel(xl_ref, xc_ref, g_ref, sc_ref, sh_ref, w_ref, o_ref, h_scr, *, tn, lat_tiles):
    x = _pick_stream(lat_tiles, xl_ref, xc_ref)
    h_scr[...] = _rms_mod(x, g_ref[...], sc_ref[0], sh_ref[0]).astype(h_scr.dtype)
    _project_columns(h_scr, w_ref, o_ref, tn)


def _stream_specs(tm, d, lat_tiles):
    return [pl.BlockSpec((tm, d), lambda i, *_: (jnp.minimum(i, lat_tiles - 1), 0)),
            pl.BlockSpec((tm, d), lambda i, *_: (jnp.maximum(i - lat_tiles, 0), 0))]


def _inproj_residual_kernel(tab_ref, x_ref, ys_ref, lpos_ref, gt_ref, g_ref, sc_ref, sh_ref, w_ref,
                            o_ref, xo_ref, h_scr, stage, sems, *, tn, cap, full_tiles, part_cols):
    f = _expert_residual(tab_ref, ys_ref, lpos_ref, stage, sems, tile_rows=x_ref.shape[0], cap=cap)
    x = x_ref[...] + gt_ref[0] * f
    xo_ref[...] = x
    h_scr[...] = _rms_mod(x, g_ref[...], sc_ref[0], sh_ref[0]).astype(h_scr.dtype)
    _project_columns(h_scr, w_ref, o_ref, tn, full_tiles, part_cols)


def _inproj(xs, prev, norm_g, mod, w, *, layer, tm, tn, n_lat, seq, batch, ctx_cols=None):
    n, d = (xs[0].shape[0] + xs[1].shape[0], xs[0].shape[1]) if prev is None else xs.shape
    width = w.shape[-1]
    row = functools.partial(_mod_row, tile=tm, n_lat=n_lat, seq=seq, batch=batch)
    mod_spec = lambda lyr, k: _layer_spec(lyr, (1, 1, d), lambda i, *_: (row(i), 0, k))
    x_spec = pl.BlockSpec((tm, d), lambda i, *_: (i, 0))
    p_spec = pl.BlockSpec((tm, width), lambda i, *_: (i, 0))
    p_shape = jax.ShapeDtypeStruct((n, width), ACT_DTYPE)
    tail_specs = [_layer_spec(layer, (1, d), lambda i, *_: (0, 0)),
                  mod_spec(layer, 1), mod_spec(layer, 0),
                  _layer_spec(layer, (d, width), lambda i, *_: (0, 0), pipeline_mode=pl.Buffered(1))]
    grid = (n // tm,)
    h_scratch = pltpu.VMEM((tm, d), MM_DTYPE)
    if prev is None:
        lat_tiles = n_lat // tm
        p = pl.pallas_call(functools.partial(_inproj_kernel, tn=tn, lat_tiles=lat_tiles), grid=grid,
                           in_specs=_stream_specs(tm, d, lat_tiles) + tail_specs,
                           out_specs=p_spec, out_shape=p_shape, scratch_shapes=[h_scratch],
                           compiler_params=_cparams(("parallel",)))(*xs, norm_g, mod, mod, w)
        return p, xs
    ys, tab, lpos, cap = prev
    gs = pltpu.PrefetchScalarGridSpec(
        num_scalar_prefetch=1,
        grid=grid,
        in_specs=[x_spec, pl.BlockSpec(memory_space=pl.ANY), pl.BlockSpec((tm, 1), lambda i, *_: (i, 0)),
                  mod_spec(layer - 1, 5)] + tail_specs,
        out_specs=[p_spec, x_spec],
        scratch_shapes=[h_scratch, pltpu.VMEM((2, _sorted_rows(tm), d), F32), pltpu.SemaphoreType.DMA((2,))])
    return pl.pallas_call(
        functools.partial(_inproj_residual_kernel, tn=tn, cap=cap,
                          full_tiles=None if ctx_cols is None else n_lat // tm, part_cols=ctx_cols or 0),
        grid_spec=gs,
        out_shape=[p_shape, jax.ShapeDtypeStruct((n, d), F32)],
        input_output_aliases={1: 1},
        compiler_params=_cparams(("arbitrary",)),
    )(tab, xs, ys, lpos, mod, norm_g, mod, mod, w)


def _ret_kernel(lg_ref, q_ref, k_ref, v_ref, g_ref, kc_ref, vc_ref, cos_ref, sin_ref, o_ref, q_scr, kt_scr, sb_scr,
                *, seq, n_ctx, n_plain):
    C = RET_CHUNK
    n_chunk = seq // C
    head = pl.program_id(1)
    lgf = lg_ref[0, head]
    lgb = lg_ref[1, head]

    for n in range(n_chunk):
        rows = slice(n * C, (n + 1) * C)
        qn = q_ref[rows, :].astype(F32)
        kn = k_ref[rows, :].astype(F32)
        cos = cos_ref[rows, :]
        sin = sin_ref[rows, :]
        qn = qn * cos + pltpu.roll(qn, RET_DK // 2, 1) * sin
        kn = kn * cos + pltpu.roll(kn, RET_DK // 2, 1) * sin
        q_scr[rows, :] = (qn * (RET_DK ** -0.5)).astype(q_scr.dtype)
        kt_scr[:, rows] = kn.T.astype(kt_scr.dtype)

    ri = lax.broadcasted_iota(jnp.int32, (C, C), 0)
    ci = lax.broadcasted_iota(jnp.int32, (C, C), 1)
    rel = (ri - ci).astype(F32)
    dmask = jnp.where(rel > 0.0, jnp.exp(lgf * jnp.maximum(rel, 0.0)),
                      jnp.where(rel < 0.0, jnp.exp(lgb * jnp.maximum(-rel, 0.0)), 2.0))
    icol = lax.broadcasted_iota(jnp.int32, (C, 1), 0).astype(F32)
    jrow = lax.broadcasted_iota(jnp.int32, (1, C), 1).astype(F32)
    qdec_f = jnp.exp(lgf * (icol + 1.0))
    qdec_b = jnp.exp(lgb * (C - icol))
    kdec_f = jnp.exp(lgf * (C - 1.0 - jrow))
    kdec_b = jnp.exp(lgb * jrow)
    zrow = jnp.zeros((1, RET_DV), F32)
    cdec_f = jnp.exp(zrow + lgf * C)
    cdec_b = jnp.exp(zrow + lgb * C)
    live = jnp.where(pl.program_id(0) < n_plain, 1.0, 0.0)
    kdec_f = kdec_f * live
    kdec_b = kdec_b * live

    def chunk(ref, n):
        return ref[n * C:(n + 1) * C, :]

    def kt_chunk(n):
        return kt_scr[:, n * C:(n + 1) * C]

    if n_ctx:
        kct = kc_ref[...].astype(F32).T
        vcx = vc_ref[...].astype(MM_DTYPE)
        mrow = lax.broadcasted_iota(jnp.int32, (1, n_ctx), 1).astype(F32)
        s_f = _dot((kct * (jnp.exp(lgf * (n_ctx - 1.0 - mrow)) * live)).astype(MM_DTYPE), vcx)
        s_b = _dot((kct * (jnp.exp(lgb * mrow) * live)).astype(MM_DTYPE), vcx)
        sb_scr[n_chunk - 1] = s_b
        for n in range(n_chunk - 1, 0, -1):
            s_b = s_b * cdec_b + _dot((kt_chunk(n).astype(F32) * kdec_b).astype(MM_DTYPE),
                                      chunk(v_ref, n).astype(MM_DTYPE))
            sb_scr[n - 1] = s_b

    for n0 in range(0, n_chunk, RET_GROUP):
        group = range(n0, min(n0 + RET_GROUP, n_chunk))
        qs = {n: chunk(q_scr, n) for n in group}
        kts = {n: kt_chunk(n) for n in group}
        vs = {n: chunk(v_ref, n).astype(MM_DTYPE) for n in group}
        scores = {n: _dot(qs[n], kts[n]) for n in group}
        if n_ctx:
            incs = {n: _dot((kts[n].astype(F32) * kdec_f).astype(MM_DTYPE), vs[n]) for n in group if n + 1 < n_chunk}
            qcats = {n: jnp.concatenate([qs[n].astype(F32) * qdec_f, qs[n].astype(F32) * qdec_b], axis=1
                                        ).astype(MM_DTYPE) for n in group}
        probs = {n: (scores[n] * dmask).astype(MM_DTYPE) for n in group}
        outs = {n: _dot(probs[n], vs[n]) for n in group}
        if n_ctx:
            for n in group:
                scat = jnp.concatenate([s_f, sb_scr[n]], axis=0).astype(MM_DTYPE)
                outs[n] = outs[n] + _dot(qcats[n], scat)
                if n + 1 < n_chunk:
                    s_f = s_f * cdec_f + incs[n]
        for n in group:
            o = outs[n]
            mu = jnp.mean(o, axis=-1, keepdims=True)
            oc = o - mu
            yn = oc * lax.rsqrt(jnp.mean(oc * oc, axis=-1, keepdims=True) + NORM_EPS)
            gn = chunk(g_ref, n)
            o_ref[n * C:(n + 1) * C, :] = yn.astype(o_ref.dtype) * (gn * jax.nn.sigmoid(gn))


def _retention_latent(p, log_gamma, cos, sin, *, batch, seq, n_ctx, n_lat, n_packed):
    n = p.shape[0]
    cb = n_lat // n_ctx
    kq, kv = OFF_K // RET_DK, OFF_V // RET_DV
    own = lambda b: jnp.minimum(b, batch - 1)
    table = lambda b: jnp.where(b < batch, 0, 1)
    gs = pltpu.PrefetchScalarGridSpec(
        num_scalar_prefetch=1,
        grid=(batch + n_packed, RET_HEADS),
        in_specs=[pl.BlockSpec((seq, RET_DK), lambda b, h, lg: (b, h)),
                  pl.BlockSpec((seq, RET_DK), lambda b, h, lg: (b, kq + h)),
                  pl.BlockSpec((seq, RET_DV), lambda b, h, lg: (b, kv + h)),
                  pl.BlockSpec((seq, RET_DV), lambda b, h, lg: (b, OFF_G // RET_DV + h)),
                  pl.BlockSpec((n_ctx, RET_DK), lambda b, h, lg: (cb + own(b), kq + h)),
                  pl.BlockSpec((n_ctx, RET_DV), lambda b, h, lg: (cb + own(b), kv + h)),
                  pl.BlockSpec((None, seq, RET_DK), lambda b, h, lg: (table(b), 0, 0)),
                  pl.BlockSpec((None, seq, RET_DK), lambda b, h, lg: (table(b), 0, 0))],
        out_specs=pl.BlockSpec((seq, RET_DV), lambda b, h, lg: (b, h)),
        scratch_shapes=[pltpu.VMEM((seq, RET_DK), MM_DTYPE),
                        pltpu.VMEM((RET_DK, seq), MM_DTYPE),
                        pltpu.VMEM((seq // RET_CHUNK, RET_DK, RET_DV), F32)])
    return pl.pallas_call(
        functools.partial(_ret_kernel, seq=seq, n_ctx=n_ctx, n_plain=batch),
        grid_spec=gs,
        out_shape=jax.ShapeDtypeStruct((n, RET_V), ACT_DTYPE),
        compiler_params=_cparams(("parallel", "parallel")),
    )(log_gamma, p, p, p, p, p, p, cos, sin)


def _convpool_kernel(cb_ref, cc_ref, cx_ref, pi_ref, cw_ref, pw_ref, ps_ref, yc_ref, yp_ref, *, seq, n_plain):
    grp = pl.program_id(1)
    pb, hb = POOL_BLOCK, POOL_HALO
    n_blk = seq // pb
    packed = pl.program_id(0) >= n_plain
    live = jnp.where(packed, 0.0, 1.0)

    w = cw_ref[...]
    tl = lax.broadcasted_iota(jnp.int32, (pb, 1), 0)
    cxu = lambda a, b: cc_ref[a:b, :].astype(F32) * cx_ref[a:b, :].astype(F32)
    for b0 in range(0, n_blk, CONV_GROUP):
        group = range(b0, min(b0 + CONV_GROUP, n_blk))
        us = {blk: cxu(blk * pb, (blk + 1) * pb) for blk in group}
        before = {blk: cxu(blk * pb - hb, blk * pb)[hb - 1:hb, :] * live if blk > 0 else 0.0 for blk in group}
        after = {blk: cxu((blk + 1) * pb, (blk + 1) * pb + hb)[0:1, :] * live if blk + 1 < n_blk else 0.0
                 for blk in group}
        prevs = {blk: jnp.where(tl == 0, before[blk], pltpu.roll(us[blk], 1, 0)) for blk in group}
        nexts = {blk: jnp.where(tl == pb - 1, after[blk], pltpu.roll(us[blk], pb - 1, 0)) for blk in group}
        for blk in group:
            conv = w[0:1, :] * prevs[blk] + w[1:2, :] * us[blk] + w[2:3, :] * nexts[blk]
            yc_ref[blk * pb:(blk + 1) * pb, :] = (cb_ref[blk * pb:(blk + 1) * pb, :].astype(F32) * conv
                                                  ).astype(yc_ref.dtype)

    half = jnp.left_shift(1, grp)

    def band(rows, cols, shift):
        dd = (lax.broadcasted_iota(jnp.int32, (rows, cols), 1) + shift
              - lax.broadcasted_iota(jnp.int32, (rows, cols), 0))
        return jnp.where((dd >= -half) & (dd < half), 1.0, 0.0).astype(MM_DTYPE)

    band_self = band(pb, pb, 0)
    band_prev = band(hb, hb, -hb)
    band_next = band(hb, hb, hb)
    blocks = range(n_blk)
    lo, hi = tl - half, tl + half
    cnt_first = (hi - jnp.maximum(lo, 0)).astype(F32)
    cnt_last = (jnp.minimum(hi, pb) - lo).astype(F32)
    cnt_both = (jnp.minimum(hi, pb) - jnp.maximum(lo, 0)).astype(F32)
    p_blks = [pi_ref[blk * pb:(blk + 1) * pb, :] for blk in blocks]
    wsums = [_dot(band_self, p_blk.astype(MM_DTYPE)) for p_blk in p_blks]
    pooled = []
    for blk in blocks:
        r0 = blk * pb
        top, mid, bot = wsums[blk][:hb], wsums[blk][hb:pb - hb], wsums[blk][pb - hb:]
        if blk > 0:
            top = top + _dot(band_prev, pi_ref[r0 - hb:r0, :].astype(MM_DTYPE)) * live
        if blk + 1 < n_blk:
            bot = bot + _dot(band_next, pi_ref[r0 + pb:r0 + pb + hb, :].astype(MM_DTYPE)) * live
        wsum = jnp.concatenate([top, mid, bot], axis=0)
        if n_blk == 1:
            cnt = cnt_both
        elif blk == 0:
            cnt = cnt_first
        elif blk == n_blk - 1:
            cnt = cnt_last
        else:
            cnt = (2 * half).astype(F32)
        cnt = jnp.where(packed, cnt_both, cnt)
        pooled.append((wsum / cnt - p_blks[blk].astype(F32)).astype(MM_DTYPE))
    mixed = [_dot(pooled[blk], pw_ref[0]) for blk in blocks]
    for blk in blocks:
        yp_ref[blk * pb:(blk + 1) * pb, :] = (mixed[blk] * ps_ref[...]).astype(yp_ref.dtype)


def _convpool(p, conv_w, pool_w, pool_scale, *, layer, n_seq, seq, n_packed):
    n = p.shape[0]
    g128 = lambda off: off // POOL_GDIM
    col = lambda off: (lambda b, g: (b, g128(off) + g))
    in_specs = [pl.BlockSpec((seq, POOL_GDIM), col(OFF_CB)),
                pl.BlockSpec((seq, POOL_GDIM), col(OFF_CC)),
                pl.BlockSpec((seq, POOL_GDIM), col(OFF_CX)),
                pl.BlockSpec((seq, POOL_GDIM), col(OFF_PI)),
                _layer_spec(layer, (conv_w.shape[1], POOL_GDIM), lambda b, g: (0, g)),
                _layer_spec(layer, (1, POOL_GDIM, POOL_GDIM), lambda b, g: (g, 0, 0)),
                _layer_spec(layer, (1, POOL_GDIM), lambda b, g: (0, g))]
    out_spec = pl.BlockSpec((seq, POOL_GDIM), lambda b, g: (b, g))
    return pl.pallas_call(
        functools.partial(_convpool_kernel, seq=seq, n_plain=n_seq),
        grid=(n_seq + n_packed, POOL_GROUPS),
        in_specs=in_specs,
        out_specs=[out_spec, out_spec],
        out_shape=[jax.ShapeDtypeStruct((n, CONV_W), ACT_DTYPE), jax.ShapeDtypeStruct((n, POOL_W), ACT_DTYPE)],
        compiler_params=_cparams(("parallel", "parallel")),
    )(p, p, p, p, conv_w, pool_w, pool_scale)


def _first_max(vals):
    top = functools.reduce(jnp.maximum, vals)
    idx = jnp.full(top.shape, len(vals) - 1, jnp.int32)
    for k in range(len(vals) - 2, -1, -1):
        idx = jnp.where(vals[k] == top, k, idx)
    return top, idx


def _route(logits_t):
    row = lambda k: logits_t[k:k + 1, :]
    groups = [row(g) for g in range(N_GROUPS)]
    gmax, gidx = _first_max(groups)
    gtop = 1.0 / sum(jnp.exp(g - gmax) for g in groups)
    experts = []
    for k in range(EXP_PER_GROUP):
        e = row(N_GROUPS + (N_GROUPS - 1) * EXP_PER_GROUP + k)
        for g in range(N_GROUPS - 2, -1, -1):
            e = jnp.where(gidx == g, row(N_GROUPS + g * EXP_PER_GROUP + k), e)
        experts.append(e)
    m1, i1 = _first_max(experts)
    m2, i2 = _first_max([jnp.where(i1 == k, -jnp.inf, e) for k, e in enumerate(experts)])
    e2 = jnp.exp(m2 - m1)
    w1 = gtop / (1.0 + e2)
    w2 = gtop * e2 / (1.0 + e2)
    first_lower = i1 < i2
    lo = jnp.minimum(i1, i2)
    hi = jnp.maximum(i1, i2)
    pair = lo * 3 - jnp.right_shift(lo * (lo - 1), 1) + hi - lo - 1
    bucket = gidx * N_PAIRS + pair
    wlo = jnp.where(first_lower, w1, w2)
    whi = jnp.where(first_lower, w2, w1)
    return bucket, wlo, whi


def _column_to_lanes(col):
    return jnp.broadcast_to(col, (col.shape[0], LANES)).T[0:1, :]


def _row_to_column(row):
    return jnp.broadcast_to(row, (8, row.shape[1])).T[:, 0:1]


def _dot_nt(a, b):
    return lax.dot_general(a, b, (((1,), (1,)), ((), ())), preferred_element_type=F32)


def _bucket_run_copies(read_run, max_rows, make_copy):
    def piece(n, local, glob, size):
        @pl.when((n & size) != 0)
        def _():
            off = n & (-2 * size)
            make_copy(pl.multiple_of(local + off, ROW_ALIGN), pl.multiple_of(glob + off, ROW_ALIGN), size).start()

    top = 1 << (max_rows.bit_length() - 1)
    sizes = [top >> k for k in range(top.bit_length()) if top >> k >= ROW_ALIGN]
    rare = [s for s in sizes if s >= RUN_RARE_ROWS]

    def per_bucket(b, carry_):
        n, local, glob = read_run(b)

        @pl.when(n >= RUN_RARE_ROWS)
        def _():
            for size in rare:
                piece(n, local, glob, size)

        for size in sizes[len(rare):]:
            piece(n, local, glob, size)
        return carry_

    lax.fori_loop(0, N_BUCKETS, per_bucket, 0)


def _wait_rows(total, max_rows, make_copy):
    size = 1 << (max_rows.bit_length() - 1)
    while size >= ROW_ALIGN:
        @pl.when((total & size) != 0)
        def _(size=size):
            make_copy(size).wait()
        size //= 2


def _exact_bf16_pieces(w):
    a = w.astype(jnp.bfloat16).astype(F32)
    b = (w - a).astype(jnp.bfloat16).astype(F32)
    c = (w - a - b).astype(jnp.bfloat16).astype(F32)
    return a, b, c


def _outproj_kernel(*refs, cap, lat_tiles):
    if lat_tiles is None:
        x_ref, *refs = refs
        read_x = lambda rows: x_ref[rows, :]
    else:
        x_ref, xc_ref, *refs = refs
        read_x = functools.partial(_pick_stream, lat_tiles, x_ref, xc_ref)
    (yr_ref, yc_ref, yp_ref, g0_ref, g1_ref, g2_ref, gt_ref, sc_ref, sh_ref, ng_ref,
     wr_ref, wc_ref, wp_ref, wo_ref, wrt_ref, brt_ref,
     xo_ref, hs_ref, cnt_ref, tab_ref, lpos_ref,
     h_scr, tab_vmem, tab_smem, sent_smem, carry, row_sem, tab_sem) = refs
    i = pl.program_id(0)
    n_steps = pl.num_programs(0)
    slot = lax.rem(i, 2)
    tm, d = x_ref.shape
    ts = h_scr.shape[1]

    def wait_rows(s):
        _wait_rows(sent_smem[s], ts, lambda size: pltpu.make_async_copy(
            h_scr.at[s, pl.ds(0, size)], hs_ref.at[pl.ds(0, size)], row_sem.at[s]))

    @pl.when(i == 0)
    def _():
        carry[...] = jnp.zeros_like(carry)

    @pl.when(i >= 2)
    def _():
        wait_rows(slot)

    sub = tm // OUT_SUBBLOCKS
    blocks = [slice(a, a + sub) for a in range(0, tm, sub)]
    gate = lambda r, rows: jax.nn.sigmoid(r[rows, :].astype(F32))
    merged = [(gate(g0_ref, rows) * _dot(yr_ref[rows, :], wr_ref[...])
               + gate(g1_ref, rows) * _dot(yc_ref[rows, :], wc_ref[...])
               + gate(g2_ref, rows) * _dot(yp_ref[rows, :], wp_ref[...])).astype(MM_DTYPE) for rows in blocks]
    ys = [_dot(m, wo_ref[...]) for m in merged]
    xs_new = [read_x(rows) + gt_ref[0] * y for rows, y in zip(blocks, ys)]
    for rows, x in zip(blocks, xs_new):
        xo_ref[rows, :] = x
    hs = [_rms_mod(x, ng_ref[...], sc_ref[0], sh_ref[0]) for x in xs_new]
    splits = [_split_bf16(hb) for hb in hs]
    s2s = [_dot_nt(wrt_ref[...], hh) + _dot_nt(wrt_ref[...], hl) for hh, hl in splits]
    routes = [_route(s2[0:ROUTER_LO_ROW, :] + s2[ROUTER_LO_ROW:2 * ROUTER_LO_ROW, :] + brt_ref[...]) for s2 in s2s]
    h = jnp.concatenate([hb.astype(MM_DTYPE) for hb in hs], axis=0)
    bucket, wlo, whi = (jnp.concatenate([route[k] for route in routes], axis=1) for k in (0, 1, 2))

    brow = lax.broadcasted_iota(jnp.int32, (LANES, tm), 0)
    onehot = brow == bucket
    ones = jnp.where(onehot, 1.0, 0.0)
    r = lax.broadcasted_iota(jnp.int32, (tm, tm), 0)
    c = lax.broadcasted_iota(jnp.int32, (tm, tm), 1)
    tri = jnp.where(r <= c, 1.0, 0.0).astype(jnp.bfloat16)
    incl = _dot(ones.astype(jnp.bfloat16), tri)
    cnt_col = incl[:, tm - 1:tm].astype(jnp.int32)
    units_col = jnp.right_shift(cnt_col + (ROW_ALIGN - 1), ALIGN_SHIFT)
    below = jnp.where(lax.broadcasted_iota(jnp.int32, (LANES, LANES), 0)
                      > lax.broadcasted_iota(jnp.int32, (LANES, LANES), 1), 1.0, 0.0).astype(jnp.bfloat16)
    units_b = jnp.broadcast_to(units_col.astype(F32), (LANES, LANES)).astype(jnp.bfloat16)
    start_col = _dot(below, units_b)[:, 0:1] * float(ROW_ALIGN)
    lpos = jnp.sum(jnp.where(onehot, incl - 1.0 + start_col, 0.0), axis=0, keepdims=True)
    lpos_ref[...] = _row_to_column(lpos).astype(jnp.int32)
    run_col = (units_col * ROW_ALIGN).astype(F32)
    run_len = _column_to_lanes(run_col).astype(jnp.int32)
    local_start = _column_to_lanes(start_col).astype(jnp.int32)
    total = (start_col + run_col)[LANES - 1:LANES, :].astype(jnp.int32)
    srow = lax.broadcasted_iota(jnp.int32, (8, LANES), 0)
    tab = jnp.where(srow == TAB_COUNT, run_len,
                    jnp.where(srow == TAB_LOCAL, local_start,
                              jnp.where(srow == TAB_GLOBAL, carry[...], jnp.where(srow == TAB_TOTAL, total, 0))))
    carry[...] = carry[...] + run_len
    tab_ref[0] = tab
    tab_vmem[...] = tab
    to_smem = pltpu.make_async_copy(tab_vmem, tab_smem, tab_sem)
    to_smem.start()

    srt = lax.broadcasted_iota(jnp.int32, (ts, tm), 0)
    perm = jnp.where(srt == lpos.astype(jnp.int32), 1.0, 0.0).astype(MM_DTYPE)
    pieces = _exact_bf16_pieces(wlo) + _exact_bf16_pieces(whi)
    meta_t = jnp.zeros((LANES, tm), F32)
    for k, piece in enumerate(pieces):
        meta_t = jnp.where(brow == k, piece, meta_t)
    h_scr[slot, :, :d] = _dot(perm, h)
    h_scr[slot, :, d:] = _dot_nt(perm, meta_t.astype(MM_DTYPE))
    to_smem.wait()
    sent_smem[slot] = tab_smem[TAB_TOTAL, 0]

    _bucket_run_copies(
        lambda b: (tab_smem[TAB_COUNT, b], tab_smem[TAB_LOCAL, b], b * cap + tab_smem[TAB_GLOBAL, b]), tm,
        lambda local, glob, size: pltpu.make_async_copy(h_scr.at[slot, pl.ds(local, size)],
                                                        hs_ref.at[pl.ds(glob, size)], row_sem.at[slot]))

    @pl.when(i == n_steps - 1)
    def _():
        cnt_ref[...] = jnp.broadcast_to(carry[...], cnt_ref.shape)
        wait_rows(slot)

    @pl.when((i == n_steps - 1) & (i >= 1))
    def _():
        wait_rows(1 - slot)


def _sorted_rows(tm):
    return tm + N_BUCKETS * ROW_ALIGN


def _outproj(xs, p, y_ret, y_conv, y_pool, mod, norm_g, w_ret, w_conv, w_pool, w_o, w_router, b_router,
             *, layer, tm, cap, n_rows, n_lat, seq, batch):
    split = isinstance(xs, tuple)
    n, d = (xs[0].shape[0] + xs[1].shape[0], xs[0].shape[1]) if split else xs.shape
    lat_tiles = n_lat // tm if split else None
    x_specs = _stream_specs(tm, d, lat_tiles) if split else [pl.BlockSpec((tm, d), lambda i: (i, 0))]
    x_args = list(xs) if split else [xs]
    assert cap >= n_rows + (n_rows // tm) * ROW_ALIGN and cap % ROW_ALIGN == 0
    width = d + LANES
    ts = _sorted_rows(tm)
    row = functools.partial(_mod_row, tile=tm, n_lat=n_lat, seq=seq, batch=batch)
    gate = lambda k: pl.BlockSpec((tm, d), lambda i: (i, OFF_GATE // d + k))
    mod_spec = lambda k: _layer_spec(layer, (1, 1, d), lambda i: (row(i), 0, k))
    full = lambda a: _layer_spec(layer, a.shape[1:], lambda i: (0,) * (a.ndim - 1))
    return pl.pallas_call(
        functools.partial(_outproj_kernel, cap=cap, lat_tiles=lat_tiles),
        grid=(n_rows // tm,),
        in_specs=x_specs + [
                  pl.BlockSpec((tm, RET_V), lambda i: (i, 0)),
                  pl.BlockSpec((tm, CONV_W), lambda i: (i, 0)),
                  pl.BlockSpec((tm, POOL_W), lambda i: (i, 0)),
                  gate(0), gate(1), gate(2),
                  mod_spec(2), mod_spec(4), mod_spec(3),
                  full(norm_g), full(w_ret), full(w_conv), full(w_pool), full(w_o), full(w_router), full(b_router)],
        out_specs=[pl.BlockSpec((tm, d), lambda i: (i, 0)),
                   pl.BlockSpec(memory_space=pl.ANY),
                   pl.BlockSpec((8, LANES), lambda i: (0, 0)),
                   pl.BlockSpec((1, 8, LANES), lambda i: (i, 0, 0)),
                   pl.BlockSpec((tm, 1), lambda i: (i, 0))],
        out_shape=[jax.ShapeDtypeStruct((n, d), F32),
                   jax.ShapeDtypeStruct((N_BUCKETS * cap, width), F32),
                   jax.ShapeDtypeStruct((8, LANES), jnp.int32),
                   jax.ShapeDtypeStruct((n_rows // tm, 8, LANES), jnp.int32),
                   jax.ShapeDtypeStruct((n_rows, 1), jnp.int32)],
        scratch_shapes=[pltpu.VMEM((2, ts, width), F32),
                        pltpu.VMEM((8, LANES), jnp.int32),
                        pltpu.SMEM((8, LANES), jnp.int32),
                        pltpu.SMEM((2,), jnp.int32),
                        pltpu.VMEM((1, LANES), jnp.int32),
                        pltpu.SemaphoreType.DMA((2,)),
                        pltpu.SemaphoreType.DMA(())],
        input_output_aliases={} if split else {0: 0},
        compiler_params=_cparams(("arbitrary",)),
    )(*x_args, y_ret, y_conv, y_pool, p, p, p, mod, mod, mod, norm_g, w_ret, w_conv, w_pool, w_o, w_router, b_router)


def _moe_kernel(blk_ref, elo_ref, ehi_ref, valid_ref, hs_ref, w1l_ref, w1h_ref, w3l_ref, w3h_ref, w2l_ref, w2h_ref,
                ys_ref):
    del blk_ref, elo_ref, ehi_ref
    valid = valid_ref[pl.program_id(0)]
    tm, d = ys_ref.shape

    @pl.when(valid > 0)
    def _():
        keep = lax.broadcasted_iota(jnp.int32, (tm, 1), 0) < valid
        meta = jnp.where(keep, hs_ref[:, d:], 0.0)
        h = jnp.where(keep, hs_ref[:, :d], 0.0).astype(MM_DTYPE)
        wlo = meta[:, 0:1] + meta[:, 1:2] + meta[:, 2:3]
        whi = meta[:, 3:4] + meta[:, 4:5] + meta[:, 5:6]

        ups = [(_dot(h, w1[0]), _dot(h, w3[0])) for w1, w3 in ((w1l_ref, w3l_ref), (w1h_ref, w3h_ref))]
        acts = [(a * jax.nn.sigmoid(a) * b).astype(MM_DTYPE) for a, b in ups]
        y_lo, y_hi = (_dot(act, w2[0]) for act, w2 in zip(acts, (w2l_ref, w2h_ref)))
        y = wlo * y_lo + whi * y_hi
        ys_ref[...] = y.astype(ACT_DTYPE).astype(F32)


def _moe(blk, elo, ehi, valid, hs, w1, w3, w2, *, layer, tm, d):
    n_sorted, width = hs.shape
    n_work = blk.shape[0]
    up = lambda sel: _layer_spec(layer, (1, d, D_FF), lambda s, blk, elo, ehi, valid: ((elo, ehi)[sel][s], 0, 0))
    down = lambda sel: _layer_spec(layer, (1, D_FF, d), lambda s, blk, elo, ehi, valid: ((elo, ehi)[sel][s], 0, 0))
    gs = pltpu.PrefetchScalarGridSpec(
        num_scalar_prefetch=4,
        grid=(n_work,),
        in_specs=[pl.BlockSpec((tm, width), lambda s, blk, elo, ehi, valid: (blk[s], 0)),
                  up(0), up(1), up(0), up(1), down(0), down(1)],
        out_specs=pl.BlockSpec((tm, d), lambda s, blk, elo, ehi, valid: (blk[s], 0)))
    return pl.pallas_call(
        _moe_kernel,
        grid_spec=gs,
        out_shape=jax.ShapeDtypeStruct((n_sorted, d), F32),
        compiler_params=_cparams(("arbitrary",)),
    )(blk, elo, ehi, valid, hs, w1, w1, w3, w3, w2, w2)


def _expert_residual(tab_ref, ys_ref, lpos_ref, stage, sems, *, tile_rows, cap):
    i = pl.program_id(0)
    slot = lax.rem(i, 2)
    ts = stage.shape[1]
    entry = lambda tile, row, lane: tab_ref[tile * (TAB_ROWS * LANES) + row * LANES + lane]

    def start(tile, s):
        _bucket_run_copies(
            lambda b: (entry(tile, TAB_COUNT, b), entry(tile, TAB_LOCAL, b), b * cap + entry(tile, TAB_GLOBAL, b)),
            tile_rows,
            lambda local, glob, size: pltpu.make_async_copy(
                ys_ref.at[pl.ds(glob, size)], stage.at[s, pl.ds(local, size)], sems.at[s]))

    @pl.when(i == 0)
    def _():
        start(i, slot)

    @pl.when(i + 1 < pl.num_programs(0))
    def _():
        start(i + 1, 1 - slot)

    total = entry(i, TAB_TOTAL, 0)
    _wait_rows(total, ts, lambda size: pltpu.make_async_copy(
        ys_ref.at[pl.ds(0, size)], stage.at[slot, pl.ds(0, size)], sems.at[slot]))
    lpos = lpos_ref[...]
    unperm = jnp.where(lpos == lax.broadcasted_iota(jnp.int32, (tile_rows, ts), 1), 1.0, 0.0).astype(MM_DTYPE)
    filled = lax.broadcasted_iota(jnp.int32, (ts, 1), 0) < total
    return _dot(unperm, jnp.where(filled, stage[slot], 0.0).astype(MM_DTYPE))


def _final_kernel(tab_ref, x_ref, ys_ref, lpos_ref, gt_ref, g_ref, o_ref, stage, sems, *, cap):
    f = _expert_residual(tab_ref, ys_ref, lpos_ref, stage, sems, tile_rows=x_ref.shape[0], cap=cap)
    x = x_ref[...] + gt_ref[0] * f
    o_ref[...] = x * lax.rsqrt(jnp.mean(x * x, axis=-1, keepdims=True) + NORM_EPS) * g_ref[...]


def _final(xs, moe_out, mod, final_g, *, layer, tm, n_rows, n_lat, seq, batch):
    ys, tab, lpos, cap = moe_out
    d = xs.shape[1]
    row = functools.partial(_mod_row, tile=tm, n_lat=n_lat, seq=seq, batch=batch)
    x_spec = pl.BlockSpec((tm, d), lambda i, tab: (i, 0))
    gs = pltpu.PrefetchScalarGridSpec(
        num_scalar_prefetch=1,
        grid=(n_rows // tm,),
        in_specs=[x_spec,
                  pl.BlockSpec(memory_space=pl.ANY),
                  pl.BlockSpec((tm, 1), lambda i, tab: (i, 0)),
                  _layer_spec(layer, (1, 1, d), lambda i, tab: (row(i), 0, 5)),
                  pl.BlockSpec((1, d), lambda i, tab: (0, 0))],
        out_specs=x_spec,
        scratch_shapes=[pltpu.VMEM((2, _sorted_rows(tm), d), F32), pltpu.SemaphoreType.DMA((2,))])
    return pl.pallas_call(
        functools.partial(_final_kernel, cap=cap),
        grid_spec=gs,
        out_shape=jax.ShapeDtypeStruct((n_rows, d), F32),
        compiler_params=_cparams(("arbitrary",)),
    )(tab, xs, ys, lpos, mod, final_g)


def _rope_tables(seq):
    rows = seq // GRID_W
    row = jnp.repeat(jnp.arange(rows, dtype=F32), GRID_W)
    col = jnp.tile(jnp.arange(GRID_W, dtype=F32), rows)
    n_freq = RET_DK // 4
    inv_freq = ROPE_BASE ** (-jnp.arange(n_freq, dtype=F32) / n_freq)
    ang = jnp.concatenate([row[:, None] * inv_freq[None, :], col[:, None] * inv_freq[None, :]], axis=-1)
    cos, sin = jnp.cos(ang), jnp.sin(ang)
    cos2, sin2 = jnp.concatenate([cos, cos], axis=-1), jnp.concatenate([-sin, sin], axis=-1)
    return jnp.stack([cos2, jnp.ones_like(cos2)]), jnp.stack([sin2, jnp.zeros_like(sin2)])


def _work_tables(counts, *, tm, cap, n_work):
    cnt = counts[0, :N_BUCKETS]
    tiles = (cnt + tm - 1) // tm
    ends = jnp.cumsum(tiles)
    starts = ends - tiles
    item = jnp.arange(n_work, dtype=jnp.int32)
    used = item < ends[-1]
    ref_item = jnp.minimum(item, jnp.maximum(ends[-1] - 1, 0))
    member = ((ref_item[:, None] >= starts[None, :]) & (ref_item[:, None] < ends[None, :])).astype(jnp.int32)
    pick = lambda per_bucket: jnp.sum(member * per_bucket[None, :], axis=1)
    buckets = np.arange(N_BUCKETS)
    j = ref_item - pick(starts)
    valid = jnp.where(used, jnp.clip(pick(cnt) - j * tm, 0, tm), 0)
    blk = pick(jnp.asarray(buckets * (cap // tm), jnp.int32)) + j
    first = (buckets // N_PAIRS) * EXP_PER_GROUP
    elo = pick(jnp.asarray(first + np.asarray(PAIR_LO)[buckets % N_PAIRS], jnp.int32))
    ehi = pick(jnp.asarray(first + np.asarray(PAIR_HI)[buckets % N_PAIRS], jnp.int32))
    i32 = lambda a: a.astype(jnp.int32)
    return i32(blk), i32(elo), i32(ehi), i32(valid)


def _router_weights(w_rg, b_rg, w_re, b_re):
    w = jnp.swapaxes(jnp.concatenate([w_rg, w_re], axis=-1).astype(F32), 1, 2)
    depth, n_out, d = w.shape
    assert n_out <= ROUTER_LO_ROW and 2 * ROUTER_LO_ROW <= LANES and EXP_PER_GROUP == 4
    hi = w.astype(jnp.bfloat16)
    lo = (w - hi.astype(F32)).astype(jnp.bfloat16)
    packed = jnp.zeros((depth, LANES, d), jnp.bfloat16)
    packed = packed.at[:, :n_out].set(hi).at[:, ROUTER_LO_ROW:ROUTER_LO_ROW + n_out].set(lo)
    bias = jnp.zeros((depth, ROUTER_LO_ROW, 1), F32)
    bias = bias.at[:, :n_out, 0].set(jnp.concatenate([b_rg, b_re], axis=-1).astype(F32))
    return packed, bias


def _pick_tile(n, want, *also):
    t = want
    while n % t or any(a % t for a in also):
        t //= 2
    return t


def kernel(x, c, ctx, c_ctx, w_ada, b_ada, norm1, norm2, w_in, ret_decay, conv_w, pool_w, pool_scale, w_ret_out,
           w_conv_out, w_pool_out, w_o, w_rg, b_rg, w_re, b_re, w1, w3, w2, final_norm):
    batch, seq, d = x.shape
    n_ctx = ctx.shape[1]
    depth = w_ada.shape[0]
    n_lat, n_c = batch * seq, batch * n_ctx
    n = n_lat + n_c
    assert POOL_WINDOWS == (2, 4, 8, 16) and POOL_HALO >= max(POOL_WINDOWS) // 2
    assert seq % RET_CHUNK == 0 and seq % GRID_W == 0 and n_lat % n_ctx == 0
    assert n_ctx == RET_CHUNK == POOL_BLOCK and n_c % seq == 0

    tm_out = _pick_tile(seq, 512, n_c)
    tm_moe = 512

    xs = (x.reshape(n_lat, d), ctx.reshape(n_c, d))
    mod_rows = -(-(batch + 1) // 8) * 8
    cond = jnp.zeros((mod_rows, d), F32).at[:batch].set(c).at[batch].set(c_ctx)
    mod = _ada(cond, w_ada, b_ada).reshape(depth, mod_rows, 1, N_MOD * d)
    cos, sin = _rope_tables(seq)
    log_gamma = jax.nn.log_sigmoid(ret_decay.astype(F32))

    mm = lambda a: a.astype(MM_DTYPE)
    norm1_s, norm2_s, pool_scale_s = norm1[:, None, :], norm2[:, None, :], pool_scale[:, None, :]
    w_in_b, pool_w_b = mm(w_in), mm(pool_w)
    w_ret_b, w_conv_b, w_pool_b, w_o_b = mm(w_ret_out), mm(w_conv_out), mm(w_pool_out), mm(w_o)
    w1_b, w3_b, w2_b = mm(w1), mm(w3), mm(w2)
    w_router, b_router = _router_weights(w_rg, b_rg, w_re, b_re)
    dims = dict(n_lat=n_lat, seq=seq, batch=batch)
    prev = None
    for l in range(depth):
        last = l == depth - 1
        rows = n_lat if last else n
        p, xs = _inproj(xs, prev, norm1_s, mod, w_in_b, layer=l, tm=tm_out, tn=1024,
                        ctx_cols=OFF_G if last and l > 0 else None, **dims)
        n_packed = 0 if last else n_c // seq
        y_ret = _retention_latent(p, log_gamma[l], cos, sin, batch=batch, seq=seq, n_ctx=n_ctx, n_lat=n_lat,
                                  n_packed=n_packed)
        y_conv, y_pool = _convpool(p, conv_w, pool_w_b, pool_scale_s, layer=l, n_seq=batch, seq=seq, n_packed=n_packed)
        n_tiles = rows // tm_out
        cap = -(-(rows + n_tiles * ROW_ALIGN) // tm_moe) * tm_moe
        max_sorted = rows + n_tiles * N_BUCKETS * (ROW_ALIGN - 1)
        xs, hs, counts, tab, lpos = _outproj(xs, p, y_ret, y_conv, y_pool, mod, norm2_s, w_ret_b, w_conv_b, w_pool_b,
                                             w_o_b, w_router, b_router, layer=l, tm=tm_out, cap=cap, n_rows=rows,
                                             **dims)
        blk, elo, ehi, valid = _work_tables(counts, tm=tm_moe, cap=cap, n_work=-(-max_sorted // tm_moe) + N_BUCKETS)
        ys = _moe(blk, elo, ehi, valid, hs, w1_b, w3_b, w2_b, layer=l, tm=tm_moe, d=d)
        prev = (ys, tab[:, :TAB_ROWS, :].reshape(-1), lpos, cap)
    out = _final(xs, prev, mod, final_norm[None], layer=depth - 1, tm=tm_out, n_rows=n_lat, **dims)
    return out.reshape(batch, seq, d)
```
